```python
import jax
import jax.numpy as jnp
from jax import lax
import numpy as np

D_MODEL = 1024
BATCH = 8
SEQ = 2048
DEPTH = 2
DEC_BATCH = 128
DEC_SEQ = 8
PAST_LEN = 16384
PAGE_SIZE = 128

MIX_WIDTH = D_MODEL
N_MIXERS = 4
W_GROUP = MIX_WIDTH // N_MIXERS
HEAD_CH = 64
CONV_A = 3
POOL_WINDOWS = (2, 4, 8, 16)
POOL_CH = W_GROUP // len(POOL_WINDOWS)
POOL_STATE = max(POOL_WINDOWS) - 1
CONV_C = 31
CHUNK = 128
N_HEADS_D = W_GROUP // HEAD_CH
HEAD_D = HEAD_CH
IN_COLS = 8 * W_GROUP
N_EXPERT_GROUPS = 4
EXPERTS_PER_GROUP = 8
N_EXPERTS = N_EXPERT_GROUPS * EXPERTS_PER_GROUP
TOP_K_IN_GROUP = 2
D_FF_EXPERT = 128
EPS = 1e-6

kernel_name = 'hybrid_conv_pool_sgu_hmoe_step'


def rmsnorm(x, g):
    xf = x.astype(jnp.float32)
    y = xf * lax.rsqrt(jnp.mean(xf * xf, axis=-1, keepdims=True) + EPS)
    return (y * g.astype(jnp.float32)).astype(x.dtype)


def layernorm(x, g, b):
    xf = x.astype(jnp.float32)
    mu = jnp.mean(xf, axis=-1, keepdims=True)
    var = jnp.mean(jnp.square(xf - mu), axis=-1, keepdims=True)
    y = (xf - mu) * lax.rsqrt(var + EPS) * g.astype(jnp.float32) + b.astype(jnp.float32)
    return y.astype(x.dtype)


def causal_dwconv(x, prev, w):
    k_w, ch = w.shape
    xp = jnp.concatenate([prev.astype(x.dtype), x], axis=1)
    y = lax.conv_general_dilated(xp, w[:, None, :].astype(x.dtype), window_strides=(1,), padding='VALID',
                                 dimension_numbers=('NWC', 'WIO', 'NWC'), feature_group_count=ch)
    return y, xp[:, xp.shape[1] - (k_w - 1):]


def multiscale_pool(x, prev, w_pool, scale, start_pos):
    bsz, t_len, _ = x.shape
    xp = jnp.concatenate([prev.astype(x.dtype), x], axis=1).astype(jnp.float32)
    csum = jnp.concatenate([jnp.zeros_like(xp[:, :1]), jnp.cumsum(xp, axis=1)], axis=1)
    pos = start_pos + jnp.arange(t_len, dtype=jnp.int32)
    end = POOL_STATE + 1
    means = []
    for g, win in enumerate(POOL_WINDOWS):
        ch = slice(g * POOL_CH, (g + 1) * POOL_CH)
        s = csum[:, end:end + t_len, ch] - csum[:, end - win:end - win + t_len, ch]
        cnt = jnp.minimum(pos + 1, win).astype(jnp.float32)[None, :, None]
        means.append(s / cnt)
    d = jnp.concatenate(means, axis=-1) - x.astype(jnp.float32)
    d = d.reshape(bsz, t_len, len(POOL_WINDOWS), POOL_CH).astype(x.dtype)
    y = jnp.einsum('btgc,gce->btge', d, w_pool.astype(x.dtype)).reshape(bsz, t_len, W_GROUP)
    return y * scale.astype(x.dtype), xp[:, xp.shape[1] - POOL_STATE:].astype(x.dtype)


def chunk_spatial_mix(v, w_s, b_s):
    bsz, t_len, _ = v.shape
    L = min(t_len, CHUNK)
    n_chunks = t_len // L
    ws = jnp.where(jnp.tril(jnp.ones((L, L), dtype=bool)), w_s[:, :L, :L], 0.0).astype(v.dtype)
    vh = v.reshape(bsz, n_chunks, L, N_HEADS_D, HEAD_D)
    bias = b_s[:, :L].T[None, None, :, :, None].astype(v.dtype)
    mixed = jnp.einsum('hij,bnjhc->bnihc', ws, vh) + bias
    return mixed.reshape(bsz, t_len, W_GROUP)


def hier_moe(x, rg_w, rg_b, re_w, re_b, w_gate, w_up, w_down):
    bsz, t_len, d = x.shape
    xt = x.reshape(-1, d)
    n_tok = xt.shape[0]
    lg = jnp.matmul(xt, rg_w).astype(jnp.float32) + rg_b.astype(jnp.float32)
    pg = jax.nn.softmax(lg, axis=-1)
    p_top, g_idx = lax.top_k(pg, 1)
    le = (jnp.matmul(xt, re_w).astype(jnp.float32) + re_b.astype(jnp.float32))
    le = le.reshape(n_tok, N_EXPERT_GROUPS, EXPERTS_PER_GROUP)
    le_sel = jnp.einsum('nge,ng->ne', le, jax.nn.one_hot(g_idx[:, 0], N_EXPERT_GROUPS, dtype=jnp.float32))
    v2, e_idx = lax.top_k(le_sel, TOP_K_IN_GROUP)
    w2 = jax.nn.softmax(v2, axis=-1) * p_top
    expert = g_idx * EXPERTS_PER_GROUP + e_idx
    gates = jnp.einsum('nk,nke->ne', w2, jax.nn.one_hot(expert, N_EXPERTS, dtype=jnp.float32)).astype(x.dtype)
    hg = jnp.einsum('nd,edf->nef', xt, w_gate.astype(x.dtype))
    hu = jnp.einsum('nd,edf->nef', xt, w_up.astype(x.dtype))
    act = jax.nn.silu(hg) * hu * gates[:, :, None]
    out = jnp.einsum('nef,efd->nd', act, w_down.astype(x.dtype))
    return out.reshape(bsz, t_len, d)


def _trunk(x, st_a, st_p, st_c, start_pos, weights):
    (g_mix, w_in, conv_a_w, pool_w, pool_scale, conv_c_w, conv_c_b, ln_c_g, ln_c_b, ln_d_g, ln_d_b,
     sgu_w, sgu_b, w_out, g_ffn, router_group_w, router_group_b, router_expert_w, router_expert_b,
     expert_w_gate, expert_w_up, expert_w_down, g_final) = weights
    new_a, new_p, new_c, v_rows = [], [], [], []
    for l in range(DEPTH):
        h = rmsnorm(x, g_mix[l])
        z = jnp.einsum('btd,de->bte', h, w_in[l].astype(x.dtype))
        a_b, a_c, a_x, p_x, c_a, c_g, d_u, d_v = jnp.split(z, 8, axis=-1)
        y_a, sa = causal_dwconv(a_c * a_x, st_a[l], conv_a_w[l])
        y_a = a_b * y_a
        y_p, sp = multiscale_pool(p_x, st_p[l], pool_w[l], pool_scale[l], start_pos)
        y_c, sc = causal_dwconv(c_a * jax.nn.sigmoid(c_g), st_c[l], conv_c_w[l])
        y_c = jax.nn.silu(layernorm(y_c + conv_c_b[l].astype(x.dtype), ln_c_g[l], ln_c_b[l]))
        v_n = layernorm(d_v, ln_d_g[l], ln_d_b[l])
        y_d = d_u * chunk_spatial_mix(v_n, sgu_w[l], sgu_b[l])
        mix = jnp.concatenate([y_a, y_p, y_c, y_d], axis=-1)
        x = x + jnp.einsum('bte,ed->btd', mix, w_out[l].astype(x.dtype))
        h = rmsnorm(x, g_ffn[l])
        x = x + hier_moe(h, router_group_w[l], router_group_b[l], router_expert_w[l], router_expert_b[l],
                         expert_w_gate[l], expert_w_up[l], expert_w_down[l])
        new_a.append(sa)
        new_p.append(sp)
        new_c.append(sc)
        v_rows.append(v_n)
    y = rmsnorm(x, g_final)
    return y, jnp.stack(new_a), jnp.stack(new_p), jnp.stack(new_c), v_rows


def setup_inputs(seed: int = 0) -> dict:
    key = jax.random.key(seed)
    ks = iter(jax.random.split(key, 40))

    def nrm(shape, scale):
        return jax.random.normal(next(ks), shape, jnp.float32) * scale

    def gain(shape):
        return 1.0 + nrm(shape, 0.05)

    return {
        'x_prompt': nrm((BATCH, SEQ, D_MODEL), 1.0),
        'x_sample': nrm((DEC_BATCH, DEC_SEQ, D_MODEL), 1.0),
        'state_conv_a': nrm((DEPTH, DEC_BATCH, CONV_A - 1, W_GROUP), 1.0),
        'state_pool': nrm((DEPTH, DEC_BATCH, POOL_STATE, W_GROUP), 1.0),
        'state_conv_c': nrm((DEPTH, DEC_BATCH, CONV_C - 1, W_GROUP), 0.5),
        'g_mix': gain((DEPTH, D_MODEL)),
        'w_in': nrm((DEPTH, D_MODEL, IN_COLS), D_MODEL ** -0.5),
        'conv_a_w': nrm((DEPTH, CONV_A, W_GROUP), CONV_A ** -0.5),
        'pool_w': nrm((DEPTH, len(POOL_WINDOWS), POOL_CH, POOL_CH), POOL_CH ** -0.5),
        'pool_scale': 1.0 + nrm((DEPTH, W_GROUP), 0.1),
        'conv_c_w': nrm((DEPTH, CONV_C, W_GROUP), CONV_C ** -0.5),
        'conv_c_b': nrm((DEPTH, W_GROUP), 0.02),
        'ln_c_g': gain((DEPTH, W_GROUP)),
        'ln_c_b': nrm((DEPTH, W_GROUP), 0.02),
        'ln_d_g': gain((DEPTH, W_GROUP)),
        'ln_d_b': nrm((DEPTH, W_GROUP), 0.02),
        'sgu_w': nrm((DEPTH, N_HEADS_D, CHUNK, CHUNK), 0.5 * CHUNK ** -0.5),
        'sgu_b': 1.0 + nrm((DEPTH, N_HEADS_D, CHUNK), 0.1),
        'w_out': nrm((DEPTH, MIX_WIDTH, D_MODEL), MIX_WIDTH ** -0.5),
        'g_ffn': gain((DEPTH, D_MODEL)),
        'router_group_w': nrm((DEPTH, D_MODEL, N_EXPERT_GROUPS), D_MODEL ** -0.5),
        'router_group_b': nrm((DEPTH, N_EXPERT_GROUPS), 0.01),
        'router_expert_w': nrm((DEPTH, D_MODEL, N_EXPERTS), D_MODEL ** -0.5),
        'router_expert_b': nrm((DEPTH, N_EXPERTS), 0.01),
        'expert_w_gate': nrm((DEPTH, N_EXPERTS, D_MODEL, D_FF_EXPERT), D_MODEL ** -0.5),
        'expert_w_up': nrm((DEPTH, N_EXPERTS, D_MODEL, D_FF_EXPERT), D_MODEL ** -0.5),
        'expert_w_down': nrm((DEPTH, N_EXPERTS, D_FF_EXPERT, D_MODEL), D_FF_EXPERT ** -0.5),
        'g_final': gain((D_MODEL,)),
    }


def reference(x_prompt, x_sample, state_conv_a, state_pool, state_conv_c, g_mix, w_in, conv_a_w, pool_w,
              pool_scale, conv_c_w, conv_c_b, ln_c_g, ln_c_b, ln_d_g, ln_d_b, sgu_w, sgu_b, w_out, g_ffn,
              router_group_w, router_group_b, router_expert_w, router_expert_b, expert_w_gate, expert_w_up,
              expert_w_down, g_final):
    weights = (g_mix, w_in, conv_a_w, pool_w, pool_scale, conv_c_w, conv_c_b, ln_c_g, ln_c_b, ln_d_g, ln_d_b,
               sgu_w, sgu_b, w_out, g_ffn, router_group_w, router_group_b, router_expert_w, router_expert_b,
               expert_w_gate, expert_w_up, expert_w_down, g_final)
    bsz = x_prompt.shape[0]
    dt = x_prompt.dtype
    zero_a = jnp.zeros((DEPTH, bsz, CONV_A - 1, W_GROUP), dt)
    zero_p = jnp.zeros((DEPTH, bsz, POOL_STATE, W_GROUP), dt)
    zero_c = jnp.zeros((DEPTH, bsz, CONV_C - 1, W_GROUP), dt)
    y_prompt, new_conv_a_prompt, new_pool_prompt, new_conv_c_prompt, _ = _trunk(
        x_prompt, zero_a, zero_p, zero_c, 0, weights)
    y_sample, new_conv_a_sample, new_pool_sample, new_conv_c_sample, v_rows = _trunk(
        x_sample, state_conv_a, state_pool, state_conv_c, PAST_LEN, weights)
    chunk_v_sample = jnp.stack(v_rows)
    return (y_prompt, y_sample, new_conv_a_prompt, new_pool_prompt, new_conv_c_prompt,
            new_conv_a_sample, new_pool_sample, new_conv_c_sample, chunk_v_sample)
```

```python
import functools

import jax
import jax.numpy as jnp
from jax import lax
from jax.experimental import pallas as pl
from jax.experimental.pallas import tpu as pltpu

D_MODEL = 1024
W_GROUP = 256
IN_COLS = 8 * W_GROUP
CONV_A = 3
POOL_WINDOWS = (2, 4, 8, 16)
POOL_CH = W_GROUP // len(POOL_WINDOWS)
POOL_STATE = max(POOL_WINDOWS) - 1
CONV_C = 31
CHUNK = 128
N_HEADS_D = 4
HEAD_D = W_GROUP // N_HEADS_D
N_EXPERT_GROUPS = 4
EXPERTS_PER_GROUP = 8
N_EXPERTS = N_EXPERT_GROUPS * EXPERTS_PER_GROUP
D_FF_EXPERT = 128
EPS = 1e-6
PAST_LEN = 16384

SAMPLE_SEQ_BLK = 64
LANES = 128
HALO = 32
T_TILE = 512
ROW_BLK = 64
MOE_TILE = 512
EXPERT_BLK = 8
VMEM_LIMIT = 56 * 1024 * 1024

_F32 = jnp.float32
_BF16 = jnp.bfloat16


def _rmsnorm(x, g):
    return x * lax.rsqrt(jnp.mean(x * x, axis=-1, keepdims=True) + EPS) * g


def _layernorm(x, g, b):
    mu = jnp.mean(x, axis=-1, keepdims=True)
    xc = x - mu
    var = jnp.mean(xc * xc, axis=-1, keepdims=True)
    return xc * lax.rsqrt(var + EPS) * g + b


def _silu(x):
    return x * jax.nn.sigmoid(x)


def _pool_windows(shape):
    lane = lax.broadcasted_iota(jnp.int32, shape, 1)
    return jnp.left_shift(2, lane // POOL_CH)


def _prompt_mixer_kernel(x_ref, gmix_ref, win_ref, caw_ref, pw_ref, ps_ref, ccw_ref, ccb_ref, lncg_ref, lncb_ref,
                         lndg_ref, lndb_ref, sgw_ref, sgb_ref, wout_ref,
                         x1_ref, sa_ref, sp_ref, sc_ref,
                         z_ref, exta_ref, extp_ref, extc_ref, dpool_ref, vn_ref, mix_ref):
    t = pl.program_id(1)
    n_t = pl.num_programs(1)
    tt = x_ref.shape[1]

    @pl.when(t == 0)
    def _():
        zeros = jnp.zeros((HALO, W_GROUP), _F32)
        exta_ref[0:HALO, :] = zeros
        extp_ref[0:HALO, :] = zeros
        extc_ref[0:HALO, :] = zeros

    x = x_ref[0]
    h = _rmsnorm(x, gmix_ref[...]).astype(_BF16)
    z_ref[...] = jnp.dot(h, win_ref[...], preferred_element_type=_F32)

    def col(k):
        return slice(k * W_GROUP, (k + 1) * W_GROUP)

    for rb in range(tt // ROW_BLK):
        rows = slice(rb * ROW_BLK, (rb + 1) * ROW_BLK)
        ext_rows = slice(HALO + rb * ROW_BLK, HALO + (rb + 1) * ROW_BLK)
        exta_ref[ext_rows, :] = z_ref[rows, col(1)] * z_ref[rows, col(2)]
        extp_ref[ext_rows, :] = z_ref[rows, col(3)]
        extc_ref[ext_rows, :] = z_ref[rows, col(4)] * jax.nn.sigmoid(z_ref[rows, col(5)])

    win = _pool_windows((ROW_BLK, W_GROUP))
    row_iota = lax.broadcasted_iota(jnp.int32, (ROW_BLK, W_GROUP), 0)

    for rb in range(tt // ROW_BLK):
        r0 = rb * ROW_BLK
        rows = slice(r0, r0 + ROW_BLK)

        conv_a = caw_ref[CONV_A - 1:CONV_A, :] * exta_ref[HALO + r0:HALO + r0 + ROW_BLK, :]
        for k in range(CONV_A - 1):
            s = HALO + r0 - (CONV_A - 1) + k
            conv_a = conv_a + caw_ref[k:k + 1, :] * exta_ref[s:s + ROW_BLK, :]
        mix_ref[rows, col(0)] = (z_ref[rows, col(0)] * conv_a).astype(_BF16)

        p_cur = extp_ref[HALO + r0:HALO + r0 + ROW_BLK, :]
        acc = p_cur
        for j in range(1, POOL_STATE + 1):
            s = HALO + r0 - j
            acc = acc + jnp.where(win > j, extp_ref[s:s + ROW_BLK, :], 0.0)
        pos = t * tt + r0 + row_iota
        cnt = jnp.minimum(pos + 1, win).astype(_F32)
        dpool_ref[rows, :] = (acc / cnt - p_cur).astype(_BF16)

        conv_c = None
        for r in range(8):
            base = HALO + r0 - (CONV_C - 1) + r
            n_a = (CONV_C - 1 - r) // 8 + 1
            xr = extc_ref[base:base + ROW_BLK + 8 * (n_a - 1), :]
            for a in range(n_a):
                k = 8 * a + r
                term = ccw_ref[k:k + 1, :] * xr[8 * a:8 * a + ROW_BLK, :]
                conv_c = term if conv_c is None else conv_c + term
        y_c = _layernorm(conv_c + ccb_ref[...], lncg_ref[...], lncb_ref[...])
        mix_ref[rows, col(2)] = _silu(y_c).astype(_BF16)

        vn_ref[rows, :] = _layernorm(z_ref[rows, col(7)], lndg_ref[...], lndb_ref[...])

    y_p = jnp.dot(dpool_ref[...], pw_ref[...], preferred_element_type=_F32) * ps_ref[...]
    mix_ref[:, col(1)] = y_p.astype(_BF16)

    lane = lax.broadcasted_iota(jnp.int32, (CHUNK, W_GROUP), 1)
    for c in range(tt // CHUNK):
        rows = slice(c * CHUNK, (c + 1) * CHUNK)
        vn_c = vn_ref[rows, :]
        mixed = sgb_ref[...]
        for hd in range(N_HEADS_D):
            vm = jnp.where(lane // HEAD_D == hd, vn_c, 0.0).astype(_BF16)
            mixed = mixed + jnp.dot(sgw_ref[hd], vm, preferred_element_type=_F32)
        mix_ref[rows, col(3)] = (z_ref[rows, col(6)] * mixed).astype(_BF16)

    x1_ref[0] = x + jnp.dot(mix_ref[...], wout_ref[...], preferred_element_type=_F32)

    @pl.when(t == n_t - 1)
    def _():
        end = HALO + tt
        sa_ref[0] = exta_ref[end - (CONV_A - 1):end, :]
        sp_ref[0] = extp_ref[end - POOL_STATE:end, :]
        sc_ref[0] = extc_ref[end - (CONV_C - 1):end, :]

    exta_ref[0:HALO, :] = exta_ref[tt:tt + HALO, :]
    extp_ref[0:HALO, :] = extp_ref[tt:tt + HALO, :]
    extc_ref[0:HALO, :] = extc_ref[tt:tt + HALO, :]


def _const_spec(shape):
    nd = len(shape)
    return pl.BlockSpec(shape, lambda *_: (0,) * nd)


def _prompt_mixer(x, lw):
    bsz, seq, _ = x.shape
    n_t = seq // T_TILE
    consts = [lw["g_mix"], lw["w_in"], lw["conv_a_w"], lw["pool_w_bd"], lw["pool_scale"], lw["conv_c_w"],
              lw["conv_c_b"], lw["ln_c_g"], lw["ln_c_b"], lw["ln_d_g"], lw["ln_d_b"], lw["sgu_w_tril"],
              lw["sgu_bias_rows"], lw["w_out"]]
    tile_spec = pl.BlockSpec((1, T_TILE, D_MODEL), lambda b, t: (b, t, 0))

    def state_spec(rows):
        return pl.BlockSpec((1, rows, W_GROUP), lambda b, t: (b, 0, 0))

    return pl.pallas_call(
        _prompt_mixer_kernel,
        grid=(bsz, n_t),
        in_specs=[tile_spec] + [_const_spec(c.shape) for c in consts],
        out_specs=[tile_spec, state_spec(CONV_A - 1), state_spec(POOL_STATE), state_spec(CONV_C - 1)],
        out_shape=[jax.ShapeDtypeStruct((bsz, seq, D_MODEL), _F32),
                   jax.ShapeDtypeStruct((bsz, CONV_A - 1, W_GROUP), _F32),
                   jax.ShapeDtypeStruct((bsz, POOL_STATE, W_GROUP), _F32),
                   jax.ShapeDtypeStruct((bsz, CONV_C - 1, W_GROUP), _F32)],
        scratch_shapes=[pltpu.VMEM((T_TILE, IN_COLS), _F32),
                        pltpu.VMEM((HALO + T_TILE, W_GROUP), _F32),
                        pltpu.VMEM((HALO + T_TILE, W_GROUP), _F32),
                        pltpu.VMEM((HALO + T_TILE, W_GROUP), _F32),
                        pltpu.VMEM((T_TILE, W_GROUP), _BF16),
                        pltpu.VMEM((T_TILE, W_GROUP), _F32),
                        pltpu.VMEM((T_TILE, D_MODEL), _BF16)],
        compiler_params=pltpu.CompilerParams(dimension_semantics=("arbitrary", "arbitrary"),
                                             vmem_limit_bytes=VMEM_LIMIT),
        name="prompt_mixer",
    )(x, *consts)


def _sample_mixer_kernel(batch_major_in, x_ref, sta_ref, stp_ref, stc_ref, gmix_ref, win_ref, caw_ref, pw_ref, ps_ref,
                         ccw_ref, ccb_ref, lncg_ref, lncb_ref, lndg_ref, lndb_ref, sgw_ref, sgb_ref, wout_ref,
                         x1_ref, nsa_ref, nsp_ref, nsc_ref, vrow_ref,
                         xt_ref, z_ref, exta_ref, extp_ref, extc_ref, dpool_ref, vn_ref, mix_ref):
    nb = sta_ref.shape[0]
    n_tok = x1_ref.shape[0]
    n_t = n_tok // nb

    def col(k):
        return slice(k * W_GROUP, (k + 1) * W_GROUP)

    def slab(j, n=1):
        return slice(j * nb, (j + n) * nb)

    if batch_major_in:
        for tstep in range(n_t):
            xt_ref[slab(tstep), :] = x_ref[:, tstep * D_MODEL:(tstep + 1) * D_MODEL]
    else:
        xt_ref[...] = x_ref[...]

    h = _rmsnorm(xt_ref[...], gmix_ref[...]).astype(_BF16)
    z_ref[...] = jnp.dot(h, win_ref[...], preferred_element_type=_F32)

    for j in range(CONV_A - 1):
        exta_ref[slab(j), :] = sta_ref[:, col(j)]
    for j in range(POOL_STATE):
        extp_ref[slab(j), :] = stp_ref[:, col(j)]
    for j in range(CONV_C - 1):
        extc_ref[slab(j), :] = stc_ref[:, col(j)]
    for tstep in range(n_t):
        rows = slab(tstep)
        exta_ref[slab(CONV_A - 1 + tstep), :] = z_ref[rows, col(1)] * z_ref[rows, col(2)]
        extp_ref[slab(POOL_STATE + tstep), :] = z_ref[rows, col(3)]
        extc_ref[slab(CONV_C - 1 + tstep), :] = z_ref[rows, col(4)] * jax.nn.sigmoid(z_ref[rows, col(5)])

    win = _pool_windows((nb, W_GROUP))
    start_pos = PAST_LEN
    for tstep in range(n_t):
        rows = slab(tstep)
        conv_a = None
        for k in range(CONV_A):
            term = caw_ref[k:k + 1, :] * exta_ref[slab(tstep + k), :]
            conv_a = term if conv_a is None else conv_a + term
        mix_ref[rows, col(0)] = (z_ref[rows, col(0)] * conv_a).astype(_BF16)

        p_cur = extp_ref[slab(POOL_STATE + tstep), :]
        acc = p_cur
        for j in range(1, POOL_STATE + 1):
            acc = acc + jnp.where(win > j, extp_ref[slab(POOL_STATE + tstep - j), :], 0.0)
        cnt = jnp.minimum(start_pos + tstep + 1, win).astype(_F32)
        dpool_ref[rows, :] = (acc / cnt - p_cur).astype(_BF16)

        conv_c = None
        for k in range(CONV_C):
            term = ccw_ref[k:k + 1, :] * extc_ref[slab(tstep + k), :]
            conv_c = term if conv_c is None else conv_c + term
        y_c = _layernorm(conv_c + ccb_ref[...], lncg_ref[...], lncb_ref[...])
        mix_ref[rows, col(2)] = _silu(y_c).astype(_BF16)

        v_n = _layernorm(z_ref[rows, col(7)], lndg_ref[...], lndb_ref[...])
        vn_ref[rows, :] = v_n
        vrow_ref[:, col(tstep)] = v_n

    y_p = jnp.dot(dpool_ref[...], pw_ref[...], preferred_element_type=_F32) * ps_ref[...]
    mix_ref[:, col(1)] = y_p.astype(_BF16)

    for i in range(n_t):
        mixed = sgb_ref[i:i + 1, :] + sgw_ref[i * n_t:i * n_t + 1, :] * vn_ref[slab(0), :]
        for j in range(1, i + 1):
            mixed = mixed + sgw_ref[i * n_t + j:i * n_t + j + 1, :] * vn_ref[slab(j), :]
        mix_ref[slab(i), col(3)] = (z_ref[slab(i), col(6)] * mixed).astype(_BF16)

    x1_ref[...] = xt_ref[...] + jnp.dot(mix_ref[...], wout_ref[...], preferred_element_type=_F32)

    for j in range(CONV_A - 1):
        nsa_ref[:, col(j)] = exta_ref[slab(n_t + j), :]
    for j in range(POOL_STATE):
        nsp_ref[:, col(j)] = extp_ref[slab(n_t + j), :]
    for j in range(CONV_C - 1):
        nsc_ref[:, col(j)] = extc_ref[slab(n_t + j), :]


def _sample_mixer(x, st_a, st_p, st_c, lw, n_t, batch_major_in):
    n_seq = st_a.shape[0]
    nb = SAMPLE_SEQ_BLK
    n_blk = n_seq // nb
    n_tok = nb * n_t
    consts = [lw["g_mix"], lw["w_in"], lw["conv_a_w"], lw["pool_w_bd"], lw["pool_scale"], lw["conv_c_w"],
              lw["conv_c_b"], lw["ln_c_g"], lw["ln_c_b"], lw["ln_d_g"], lw["ln_d_b"], lw["sgu_w_rows"],
              lw["sgu_b_rows"], lw["w_out"]]

    def seq_spec(width):
        return pl.BlockSpec((nb, width), lambda i: (i, 0))

    tok_spec = pl.BlockSpec((n_tok, D_MODEL), lambda i: (i, 0))
    state_widths = [(CONV_A - 1) * W_GROUP, POOL_STATE * W_GROUP, (CONV_C - 1) * W_GROUP]
    x_spec = seq_spec(n_t * D_MODEL) if batch_major_in else tok_spec
    return pl.pallas_call(
        functools.partial(_sample_mixer_kernel, batch_major_in),
        grid=(n_blk,),
        in_specs=[x_spec] + [seq_spec(w) for w in state_widths] + [_const_spec(c.shape) for c in consts],
        out_specs=[tok_spec] + [seq_spec(w) for w in state_widths] + [seq_spec(n_t * W_GROUP)],
        out_shape=[jax.ShapeDtypeStruct((n_blk * n_tok, D_MODEL), _F32)]
        + [jax.ShapeDtypeStruct((n_seq, w), _F32) for w in state_widths]
        + [jax.ShapeDtypeStruct((n_seq, n_t * W_GROUP), _F32)],
        scratch_shapes=[pltpu.VMEM((n_tok, D_MODEL), _F32),
                        pltpu.VMEM((n_tok, IN_COLS), _F32),
                        pltpu.VMEM(((CONV_A - 1 + n_t) * nb, W_GROUP), _F32),
                        pltpu.VMEM(((POOL_STATE + n_t) * nb, W_GROUP), _F32),
                        pltpu.VMEM(((CONV_C - 1 + n_t) * nb, W_GROUP), _F32),
                        pltpu.VMEM((n_tok, W_GROUP), _BF16),
                        pltpu.VMEM((n_tok, W_GROUP), _F32),
                        pltpu.VMEM((n_tok, D_MODEL), _BF16)],
        compiler_params=pltpu.CompilerParams(dimension_semantics=("arbitrary",), vmem_limit_bytes=VMEM_LIMIT),
        name="sample_mixer",
    )(x, st_a, st_p, st_c, *consts)


def _route(h2, rw_ref, rb_ref):
    logits = jnp.dot(h2, rw_ref[...], precision=lax.Precision.HIGHEST, preferred_element_type=_F32) + rb_ref[...]
    lg = logits[:, :LANES]
    le = logits[:, LANES:]
    lane = lax.broadcasted_iota(jnp.int32, lg.shape, 1)
    lane_f = lane.astype(_F32)
    neg = jnp.float32(-jnp.inf)
    big = jnp.float32(LANES)

    lg = jnp.where(lane < N_EXPERT_GROUPS, lg, neg)
    g_max = jnp.max(lg, axis=-1, keepdims=True)
    g_idx = jnp.min(jnp.where(lg == g_max, lane_f, big), axis=-1, keepdims=True)
    p_top = 1.0 / jnp.sum(jnp.exp(lg - g_max), axis=-1, keepdims=True)

    in_group = (lane // EXPERTS_PER_GROUP).astype(_F32) == g_idx
    le = jnp.where(jnp.logical_and(in_group, lane < N_EXPERTS), le, neg)
    m1 = jnp.max(le, axis=-1, keepdims=True)
    i1 = jnp.min(jnp.where(le == m1, lane_f, big), axis=-1, keepdims=True)
    le2 = jnp.where(lane_f == i1, neg, le)
    m2 = jnp.max(le2, axis=-1, keepdims=True)
    i2 = jnp.min(jnp.where(le2 == m2, lane_f, big), axis=-1, keepdims=True)
    e2 = jnp.exp(m2 - m1)
    w1 = p_top / (1.0 + e2)
    w2 = p_top * e2 / (1.0 + e2)
    return jnp.where(lane_f == i1, w1, 0.0) + jnp.where(lane_f == i2, w2, 0.0)


def _moe_dense_kernel(final_norm, x_ref, gffn_ref, rw_ref, rb_ref, wg_ref, wu_ref, wd_ref, ex_ref, gfin_ref,
                      out_ref, h2_ref, gates_ref, acc_ref):
    g = pl.program_id(1)
    n_g = pl.num_programs(1)

    @pl.when(g == 0)
    def _():
        x = x_ref[...]
        h2 = _rmsnorm(x, gffn_ref[...])
        h2_ref[...] = h2.astype(_BF16)
        gates_ref[...] = _route(h2, rw_ref, rb_ref)
        acc_ref[...] = x

    h2 = h2_ref[...]
    hg = jnp.dot(h2, wg_ref[...], preferred_element_type=_F32)
    hu = jnp.dot(h2, wu_ref[...], preferred_element_type=_F32)
    gexp = jnp.dot(gates_ref[...], ex_ref[...], precision=lax.Precision.HIGHEST, preferred_element_type=_F32)
    act = (_silu(hg) * hu * gexp).astype(_BF16)
    acc_ref[...] += jnp.dot(act, wd_ref[...], preferred_element_type=_F32)

    @pl.when(g == n_g - 1)
    def _():
        y = acc_ref[...]
        if final_norm:
            y = _rmsnorm(y, gfin_ref[...])
        out_ref[...] = y


def _moe_dense(x, lw, g_final, final_norm):
    n_tok = x.shape[0]
    n_tiles = n_tok // MOE_TILE
    n_g = N_EXPERTS // EXPERT_BLK
    blk = EXPERT_BLK * D_FF_EXPERT
    tile_spec = pl.BlockSpec((MOE_TILE, D_MODEL), lambda i, g: (i, 0))
    return pl.pallas_call(
        functools.partial(_moe_dense_kernel, final_norm),
        grid=(n_tiles, n_g),
        in_specs=[tile_spec,
                  _const_spec(lw["g_ffn"].shape), _const_spec(lw["router_w"].shape), _const_spec(lw["router_b"].shape),
                  pl.BlockSpec((D_MODEL, blk), lambda i, g: (0, g)),
                  pl.BlockSpec((D_MODEL, blk), lambda i, g: (0, g)),
                  pl.BlockSpec((blk, D_MODEL), lambda i, g: (g, 0)),
                  pl.BlockSpec((LANES, blk), lambda i, g: (0, g)),
                  _const_spec(g_final.shape)],
        out_specs=tile_spec,
        out_shape=jax.ShapeDtypeStruct((n_tok, D_MODEL), _F32),
        scratch_shapes=[pltpu.VMEM((MOE_TILE, D_MODEL), _BF16),
                        pltpu.VMEM((MOE_TILE, LANES), _F32),
                        pltpu.VMEM((MOE_TILE, D_MODEL), _F32)],
        compiler_params=pltpu.CompilerParams(dimension_semantics=("arbitrary", "arbitrary"),
                                             vmem_limit_bytes=VMEM_LIMIT),
        name="moe_dense",
    )(x, lw["g_ffn"], lw["router_w"], lw["router_b"], lw["w_gate"], lw["w_up"], lw["w_down"], lw["expand"], g_final)


def _layer_weights(l, g_mix, w_in, conv_a_w, pool_w, pool_scale, conv_c_w, conv_c_b, ln_c_g, ln_c_b, ln_d_g, ln_d_b,
                   sgu_w, sgu_b, w_out, g_ffn, router_group_w, router_group_b, router_expert_w, router_expert_b,
                   expert_w_gate, expert_w_up, expert_w_down, n_t_sample):
    row = lambda v: v[l].reshape(1, -1)
    pool_bd = jnp.zeros((W_GROUP, W_GROUP), _F32)
    for g in range(len(POOL_WINDOWS)):
        sl = slice(g * POOL_CH, (g + 1) * POOL_CH)
        pool_bd = pool_bd.at[sl, sl].set(pool_w[l, g])
    tril = jnp.tril(jnp.ones((CHUNK, CHUNK), dtype=bool))
    sgu_tril = jnp.where(tril, sgu_w[l], 0.0)
    w_small = sgu_tril[:, :n_t_sample, :n_t_sample]
    sgu_w_rows = jnp.repeat(jnp.transpose(w_small, (1, 2, 0)).reshape(n_t_sample * n_t_sample, N_HEADS_D), HEAD_D, axis=1)
    router_w = jnp.zeros((D_MODEL, 2 * LANES), _F32)
    router_w = router_w.at[:, :N_EXPERT_GROUPS].set(router_group_w[l])
    router_w = router_w.at[:, LANES:LANES + N_EXPERTS].set(router_expert_w[l])
    router_b = jnp.zeros((1, 2 * LANES), _F32)
    router_b = router_b.at[0, :N_EXPERT_GROUPS].set(router_group_b[l])
    router_b = router_b.at[0, LANES:LANES + N_EXPERTS].set(router_expert_b[l])
    ff = N_EXPERTS * D_FF_EXPERT
    expand = (jnp.arange(LANES)[:, None] == (jnp.arange(ff)[None, :] // D_FF_EXPERT)).astype(_F32)
    return {
        "g_mix": row(g_mix), "w_in": w_in[l].astype(_BF16), "conv_a_w": conv_a_w[l], "pool_w_bd": pool_bd.astype(_BF16),
        "pool_scale": row(pool_scale), "conv_c_w": conv_c_w[l], "conv_c_b": row(conv_c_b), "ln_c_g": row(ln_c_g),
        "ln_c_b": row(ln_c_b), "ln_d_g": row(ln_d_g), "ln_d_b": row(ln_d_b),
        "sgu_w_tril": sgu_tril.astype(_BF16),
        "sgu_bias_rows": jnp.repeat(sgu_b[l].T, HEAD_D, axis=1),
        "sgu_w_rows": sgu_w_rows,
        "sgu_b_rows": jnp.repeat(sgu_b[l][:, :n_t_sample].T, HEAD_D, axis=1),
        "w_out": w_out[l].astype(_BF16), "g_ffn": row(g_ffn), "router_w": router_w, "router_b": router_b,
        "w_gate": jnp.transpose(expert_w_gate[l], (1, 0, 2)).reshape(D_MODEL, ff).astype(_BF16),
        "w_up": jnp.transpose(expert_w_up[l], (1, 0, 2)).reshape(D_MODEL, ff).astype(_BF16),
        "w_down": expert_w_down[l].reshape(ff, D_MODEL).astype(_BF16),
        "expand": expand,
    }


def kernel(x_prompt, x_sample, state_conv_a, state_pool, state_conv_c, g_mix, w_in, conv_a_w, pool_w, pool_scale, conv_c_w, conv_c_b, ln_c_g, ln_c_b, ln_d_g, ln_d_b, sgu_w, sgu_b, w_out, g_ffn, router_group_w, router_group_b, router_expert_w, router_expert_b, expert_w_gate, expert_w_up, expert_w_down, g_final):
    depth = g_mix.shape[0]
    bsz, seq, _ = x_prompt.shape
    nb, n_t, _ = x_sample.shape
    g_fin = g_final.reshape(1, -1)

    xp = x_prompt
    xs = x_sample.reshape(nb, n_t * D_MODEL)
    outs = {k: [] for k in ("sa_p", "sp_p", "sc_p", "sa_s", "sp_s", "sc_s", "v")}
    for l in range(depth):
        lw = _layer_weights(l, g_mix, w_in, conv_a_w, pool_w, pool_scale, conv_c_w, conv_c_b, ln_c_g, ln_c_b, ln_d_g,
                            ln_d_b, sgu_w, sgu_b, w_out, g_ffn, router_group_w, router_group_b, router_expert_w,
                            router_expert_b, expert_w_gate, expert_w_up, expert_w_down, n_t)
        last = l == depth - 1
        x1p, sa, sp, sc = _prompt_mixer(xp, lw)
        outs["sa_p"].append(sa)
        outs["sp_p"].append(sp)
        outs["sc_p"].append(sc)
        xp = _moe_dense(x1p.reshape(bsz * seq, D_MODEL), lw, g_fin, last).reshape(bsz, seq, D_MODEL)

        x1s, nsa, nsp, nsc, vrow = _sample_mixer(
            xs, state_conv_a[l].reshape(nb, -1), state_pool[l].reshape(nb, -1), state_conv_c[l].reshape(nb, -1),
            lw, n_t, batch_major_in=(l == 0))
        outs["sa_s"].append(nsa.reshape(nb, CONV_A - 1, W_GROUP))
        outs["sp_s"].append(nsp.reshape(nb, POOL_STATE, W_GROUP))
        outs["sc_s"].append(nsc.reshape(nb, CONV_C - 1, W_GROUP))
        outs["v"].append(vrow.reshape(nb, n_t, W_GROUP))
        xs = _moe_dense(x1s, lw, g_fin, last)

    y_sample = jnp.transpose(xs.reshape(nb // SAMPLE_SEQ_BLK, n_t, SAMPLE_SEQ_BLK, D_MODEL),
                             (0, 2, 1, 3)).reshape(nb, n_t, D_MODEL)
    return (xp, y_sample, jnp.stack(outs["sa_p"]), jnp.stack(outs["sp_p"]), jnp.stack(outs["sc_p"]),
            jnp.stack(outs["sa_s"]), jnp.stack(outs["sp_s"]), jnp.stack(outs["sc_s"]), jnp.stack(outs["v"]))
```

```python
import functools

import jax
import jax.numpy as jnp
from jax import lax
from jax.experimental import pallas as pl
from jax.experimental.pallas import tpu as pltpu

D_MODEL = 1024
W_GROUP = 256
IN_COLS = 8 * W_GROUP
CONV_A = 3
POOL_WINDOWS = (2, 4, 8, 16)
POOL_CH = W_GROUP // len(POOL_WINDOWS)
POOL_STATE = max(POOL_WINDOWS) - 1
CONV_C = 31
CHUNK = 128
N_HEADS_D = 4
HEAD_D = W_GROUP // N_HEADS_D
N_EXPERT_GROUPS = 4
EXPERTS_PER_GROUP = 8
N_EXPERTS = N_EXPERT_GROUPS * EXPERTS_PER_GROUP
TOP_K = 2
D_FF_EXPERT = 128
EPS = 1e-6
PAST_LEN = 16384

LANES = 128
SUBLANES = 8
BF16_ROWS = 16
HALO = 32
ROW_BLK = 64
TOK_TILE = 512
SAMPLE_SEQ_BLK = 64
GMM_TILE = 512
SLOTS = -(-(TOP_K * TOK_TILE + N_EXPERTS * (BF16_ROWS - 1)) // 256) * 256
N_CHUNK_REAL = SLOTS // BF16_ROWS
N_CHUNK = 128
SLOT_BUF = N_CHUNK * BF16_ROWS
VMEM_LIMIT = 56 * 1024 * 1024

_F32 = jnp.float32
_BF16 = jnp.bfloat16
_I32 = jnp.int32
_HI = lax.Precision.HIGHEST


def _rmsnorm(x, g):
    return x * lax.rsqrt(jnp.mean(x * x, axis=-1, keepdims=True) + EPS) * g


def _layernorm(x, g, b):
    mu = jnp.mean(x, axis=-1, keepdims=True)
    xc = x - mu
    var = jnp.mean(xc * xc, axis=-1, keepdims=True)
    return xc * lax.rsqrt(var + EPS) * g + b


def _silu(x):
    return x * jax.nn.sigmoid(x)


def _pool_windows(shape):
    lane = lax.broadcasted_iota(_I32, shape, 1)
    return jnp.left_shift(2, lane // POOL_CH)


def _const_spec(shape):
    nd = len(shape)
    return pl.BlockSpec(shape, lambda *_: (0,) * nd)


def _route_tile(x1, gffn_ref, rw_ref, rb_ref, lstrict_ref, ustrict_ref):
    h2 = _rmsnorm(x1, gffn_ref[...])
    logits = jnp.dot(h2, rw_ref[...], precision=_HI, preferred_element_type=_F32) + rb_ref[...]
    lg = logits[:, :LANES]
    le = logits[:, LANES:]
    lane = lax.broadcasted_iota(_I32, lg.shape, 1)
    lane_f = lane.astype(_F32)
    neg = jnp.float32(-jnp.inf)
    big = jnp.float32(LANES)

    lg = jnp.where(lane < N_EXPERT_GROUPS, lg, neg)
    g_max = jnp.max(lg, axis=-1, keepdims=True)
    g_idx = jnp.min(jnp.where(lg == g_max, lane_f, big), axis=-1, keepdims=True)
    p_top = 1.0 / jnp.sum(jnp.exp(lg - g_max), axis=-1, keepdims=True)

    in_group = (lane // EXPERTS_PER_GROUP).astype(_F32) == g_idx
    le = jnp.where(jnp.logical_and(in_group, lane < N_EXPERTS), le, neg)
    m1 = jnp.max(le, axis=-1, keepdims=True)
    i1 = jnp.min(jnp.where(le == m1, lane_f, big), axis=-1, keepdims=True)
    le2 = jnp.where(lane_f == i1, neg, le)
    m2 = jnp.max(le2, axis=-1, keepdims=True)
    i2 = jnp.min(jnp.where(le2 == m2, lane_f, big), axis=-1, keepdims=True)
    e2 = jnp.exp(m2 - m1)
    w1 = p_top / (1.0 + e2)
    w2 = p_top * e2 / (1.0 + e2)

    o1 = jnp.where(lane_f == i1, 1.0, 0.0)
    o2 = jnp.where(lane_f == i2, 1.0, 0.0)
    before1 = jnp.dot(lstrict_ref[...], o1.astype(_BF16), preferred_element_type=_F32)
    before2 = jnp.dot(lstrict_ref[...], o2.astype(_BF16), preferred_element_type=_F32)
    n1 = jnp.sum(o1, axis=0, keepdims=True)
    n2 = jnp.sum(o2, axis=0, keepdims=True)
    npad = jnp.floor((n1 + n2 + (BF16_ROWS - 1)) * (1.0 / BF16_ROWS)) * BF16_ROWS
    seg_start = jnp.dot(jnp.broadcast_to(npad, (SUBLANES, LANES)), ustrict_ref[...], precision=_HI,
                        preferred_element_type=_F32)[0:1]
    s1 = jnp.sum(o1 * (seg_start + before1), axis=-1, keepdims=True)
    s2 = jnp.sum(o2 * (seg_start + n1 + before2), axis=-1, keepdims=True)
    route = jnp.where(lane == 0, s1, jnp.where(lane == 1, s2, jnp.where(lane == 2, w1, jnp.where(lane == 3, w2, 0.0))))
    return h2.astype(_BF16), route, npad


def _store_route(x1, route_refs, out_refs):
    gffn_ref, rw_ref, rb_ref, lstrict_ref, ustrict_ref = route_refs
    h2_ref, route_ref, routet_ref, npad_ref = out_refs
    h2, route, npad = _route_tile(x1, gffn_ref, rw_ref, rb_ref, lstrict_ref, ustrict_ref)
    h2_ref[...] = h2
    route_ref[...] = route
    routet_ref[0] = jnp.transpose(route)[0:SUBLANES, :]
    npad_ref[0] = npad


def _prompt_mixer_kernel(x_ref, gmix_ref, win_ref, caw_ref, pw_ref, ps_ref, ccw_ref, ccb_ref, lncg_ref, lncb_ref,
                         lndg_ref, lndb_ref, sgw_ref, sgb_ref, wout_ref, gffn_ref, rw_ref, rb_ref, lstrict_ref,
                         ustrict_ref,
                         x1_ref, sa_ref, sp_ref, sc_ref, h2_ref, route_ref, routet_ref, npad_ref,
                         z_ref, exta_ref, extp_ref, extc_ref, dpool_ref, vn_ref, mix_ref):
    t = pl.program_id(1)
    n_t = pl.num_programs(1)
    tt = x_ref.shape[0]

    @pl.when(t == 0)
    def _():
        zeros = jnp.zeros((HALO, W_GROUP), _F32)
        exta_ref[0:HALO, :] = zeros
        extp_ref[0:HALO, :] = zeros
        extc_ref[0:HALO, :] = zeros

    x = x_ref[...]
    h = _rmsnorm(x, gmix_ref[...]).astype(_BF16)
    z_ref[...] = jnp.dot(h, win_ref[...], preferred_element_type=_F32)

    def col(k):
        return slice(k * W_GROUP, (k + 1) * W_GROUP)

    for rb in range(tt // ROW_BLK):
        rows = slice(rb * ROW_BLK, (rb + 1) * ROW_BLK)
        ext_rows = slice(HALO + rb * ROW_BLK, HALO + (rb + 1) * ROW_BLK)
        exta_ref[ext_rows, :] = z_ref[rows, col(1)] * z_ref[rows, col(2)]
        extp_ref[ext_rows, :] = z_ref[rows, col(3)]
        extc_ref[ext_rows, :] = z_ref[rows, col(4)] * jax.nn.sigmoid(z_ref[rows, col(5)])

    win = _pool_windows((ROW_BLK, W_GROUP))
    row_iota = lax.broadcasted_iota(_I32, (ROW_BLK, W_GROUP), 0)

    for rb in range(tt // ROW_BLK):
        r0 = rb * ROW_BLK
        rows = slice(r0, r0 + ROW_BLK)

        conv_a = caw_ref[CONV_A - 1:CONV_A, :] * exta_ref[HALO + r0:HALO + r0 + ROW_BLK, :]
        for k in range(CONV_A - 1):
            s = HALO + r0 - (CONV_A - 1) + k
            conv_a = conv_a + caw_ref[k:k + 1, :] * exta_ref[s:s + ROW_BLK, :]
        mix_ref[rows, col(0)] = (z_ref[rows, col(0)] * conv_a).astype(_BF16)

        p_cur = extp_ref[HALO + r0:HALO + r0 + ROW_BLK, :]
        acc = p_cur
        for j in range(1, POOL_STATE + 1):
            s = HALO + r0 - j
            acc = acc + jnp.where(win > j, extp_ref[s:s + ROW_BLK, :], 0.0)
        pos = t * tt + r0 + row_iota
        cnt = jnp.minimum(pos + 1, win).astype(_F32)
        dpool_ref[rows, :] = (acc / cnt - p_cur).astype(_BF16)

        conv_c = None
        for r in range(SUBLANES):
            base = HALO + r0 - (CONV_C - 1) + r
            n_a = (CONV_C - 1 - r) // SUBLANES + 1
            xr = extc_ref[base:base + ROW_BLK + SUBLANES * (n_a - 1), :]
            for a in range(n_a):
                k = SUBLANES * a + r
                term = ccw_ref[k:k + 1, :] * xr[SUBLANES * a:SUBLANES * a + ROW_BLK, :]
                conv_c = term if conv_c is None else conv_c + term
        y_c = _layernorm(conv_c + ccb_ref[...], lncg_ref[...], lncb_ref[...])
        mix_ref[rows, col(2)] = _silu(y_c).astype(_BF16)

        vn_ref[rows, :] = _layernorm(z_ref[rows, col(7)], lndg_ref[...], lndb_ref[...])

    y_p = jnp.dot(dpool_ref[...], pw_ref[...], preferred_element_type=_F32) * ps_ref[...]
    mix_ref[:, col(1)] = y_p.astype(_BF16)

    lane = lax.broadcasted_iota(_I32, (CHUNK, W_GROUP), 1)
    for c in range(tt // CHUNK):
        rows = slice(c * CHUNK, (c + 1) * CHUNK)
        vn_c = vn_ref[rows, :]
        mixed = sgb_ref[...]
        for hd in range(N_HEADS_D):
            vm = jnp.where(lane // HEAD_D == hd, vn_c, 0.0).astype(_BF16)
            mixed = mixed + jnp.dot(sgw_ref[hd], vm, preferred_element_type=_F32)
        mix_ref[rows, col(3)] = (z_ref[rows, col(6)] * mixed).astype(_BF16)

    x1 = x + jnp.dot(mix_ref[...], wout_ref[...], preferred_element_type=_F32)
    x1_ref[...] = x1
    _store_route(x1, (gffn_ref, rw_ref, rb_ref, lstrict_ref, ustrict_ref), (h2_ref, route_ref, routet_ref, npad_ref))

    @pl.when(t == n_t - 1)
    def _():
        end = HALO + tt
        sa_ref[0] = exta_ref[end - (CONV_A - 1):end, :]
        sp_ref[0] = extp_ref[end - POOL_STATE:end, :]
        sc_ref[0] = extc_ref[end - (CONV_C - 1):end, :]

    exta_ref[0:HALO, :] = exta_ref[tt:tt + HALO, :]
    extp_ref[0:HALO, :] = extp_ref[tt:tt + HALO, :]
    extc_ref[0:HALO, :] = extc_ref[tt:tt + HALO, :]


def _route_out_shapes(n_tiles):
    n_tok = n_tiles * TOK_TILE
    return [jax.ShapeDtypeStruct((n_tok, D_MODEL), _BF16),
            jax.ShapeDtypeStruct((n_tok, LANES), _F32),
            jax.ShapeDtypeStruct((n_tiles, SUBLANES, TOK_TILE), _F32),
            jax.ShapeDtypeStruct((n_tiles, 1, LANES), _F32)]


def _route_out_specs(tile_of):
    return [pl.BlockSpec((TOK_TILE, D_MODEL), lambda *g: (tile_of(*g), 0)),
            pl.BlockSpec((TOK_TILE, LANES), lambda *g: (tile_of(*g), 0)),
            pl.BlockSpec((1, SUBLANES, TOK_TILE), lambda *g: (tile_of(*g), 0, 0)),
            pl.BlockSpec((1, 1, LANES), lambda *g: (tile_of(*g), 0, 0))]


def _mixer_consts(lw, sgu_w, sgu_b):
    return [lw["g_mix"], lw["w_in"], lw["conv_a_w"], lw["pool_w_bd"], lw["pool_scale"], lw["conv_c_w"],
            lw["conv_c_b"], lw["ln_c_g"], lw["ln_c_b"], lw["ln_d_g"], lw["ln_d_b"], lw[sgu_w], lw[sgu_b], lw["w_out"],
            lw["g_ffn"], lw["router_w"], lw["router_b"], lw["lstrict"], lw["ustrict"]]


def _prompt_mixer(x, bsz, lw):
    seq = x.shape[0] // bsz
    n_t = seq // TOK_TILE
    consts = _mixer_consts(lw, "sgu_w_tril", "sgu_bias_rows")
    tile_of = lambda b, t: b * n_t + t
    tile_spec = pl.BlockSpec((TOK_TILE, D_MODEL), lambda b, t: (tile_of(b, t), 0))

    def state_spec(rows):
        return pl.BlockSpec((1, rows, W_GROUP), lambda b, t: (b, 0, 0))

    return pl.pallas_call(
        _prompt_mixer_kernel,
        grid=(bsz, n_t),
        in_specs=[tile_spec] + [_const_spec(c.shape) for c in consts],
        out_specs=[tile_spec, state_spec(CONV_A - 1), state_spec(POOL_STATE), state_spec(CONV_C - 1)]
        + _route_out_specs(tile_of),
        out_shape=[jax.ShapeDtypeStruct((bsz * seq, D_MODEL), _F32),
                   jax.ShapeDtypeStruct((bsz, CONV_A - 1, W_GROUP), _F32),
                   jax.ShapeDtypeStruct((bsz, POOL_STATE, W_GROUP), _F32),
                   jax.ShapeDtypeStruct((bsz, CONV_C - 1, W_GROUP), _F32)] + _route_out_shapes(bsz * n_t),
        scratch_shapes=[pltpu.VMEM((TOK_TILE, IN_COLS), _F32),
                        pltpu.VMEM((HALO + TOK_TILE, W_GROUP), _F32),
                        pltpu.VMEM((HALO + TOK_TILE, W_GROUP), _F32),
                        pltpu.VMEM((HALO + TOK_TILE, W_GROUP), _F32),
                        pltpu.VMEM((TOK_TILE, W_GROUP), _BF16),
                        pltpu.VMEM((TOK_TILE, W_GROUP), _F32),
                        pltpu.VMEM((TOK_TILE, D_MODEL), _BF16)],
        compiler_params=pltpu.CompilerParams(dimension_semantics=("arbitrary", "arbitrary"),
                                             vmem_limit_bytes=VMEM_LIMIT),
        name="prompt_mixer",
    )(x, *consts)


def _sample_mixer_kernel(batch_major_in, x_ref, sta_ref, stp_ref, stc_ref, gmix_ref, win_ref, caw_ref, pw_ref, ps_ref,
                         ccw_ref, ccb_ref, lncg_ref, lncb_ref, lndg_ref, lndb_ref, sgw_ref, sgb_ref, wout_ref,
                         gffn_ref, rw_ref, rb_ref, lstrict_ref, ustrict_ref,
                         x1_ref, nsa_ref, nsp_ref, nsc_ref, vrow_ref, h2_ref, route_ref, routet_ref, npad_ref,
                         xt_ref, z_ref, exta_ref, extp_ref, extc_ref, dpool_ref, vn_ref, mix_ref):
    nb = sta_ref.shape[0]
    n_tok = x1_ref.shape[0]
    n_t = n_tok // nb

    def col(k):
        return slice(k * W_GROUP, (k + 1) * W_GROUP)

    def slab(j, n=1):
        return slice(j * nb, (j + n) * nb)

    if batch_major_in:
        for tstep in range(n_t):
            xt_ref[slab(tstep), :] = x_ref[:, tstep * D_MODEL:(tstep + 1) * D_MODEL]
    else:
        xt_ref[...] = x_ref[...]

    h = _rmsnorm(xt_ref[...], gmix_ref[...]).astype(_BF16)
    z_ref[...] = jnp.dot(h, win_ref[...], preferred_element_type=_F32)

    for j in range(CONV_A - 1):
        exta_ref[slab(j), :] = sta_ref[:, col(j)]
    for j in range(POOL_STATE):
        extp_ref[slab(j), :] = stp_ref[:, col(j)]
    for j in range(CONV_C - 1):
        extc_ref[slab(j), :] = stc_ref[:, col(j)]
    for tstep in range(n_t):
        rows = slab(tstep)
        exta_ref[slab(CONV_A - 1 + tstep), :] = z_ref[rows, col(1)] * z_ref[rows, col(2)]
        extp_ref[slab(POOL_STATE + tstep), :] = z_ref[rows, col(3)]
        extc_ref[slab(CONV_C - 1 + tstep), :] = z_ref[rows, col(4)] * jax.nn.sigmoid(z_ref[rows, col(5)])

    win = _pool_windows((nb, W_GROUP))
    for tstep in range(n_t):
        rows = slab(tstep)
        conv_a = None
        for k in range(CONV_A):
            term = caw_ref[k:k + 1, :] * exta_ref[slab(tstep + k), :]
            conv_a = term if conv_a is None else conv_a + term
        mix_ref[rows, col(0)] = (z_ref[rows, col(0)] * conv_a).astype(_BF16)

        p_cur = extp_ref[slab(POOL_STATE + tstep), :]
        acc = p_cur
        for j in range(1, POOL_STATE + 1):
            acc = acc + jnp.where(win > j, extp_ref[slab(POOL_STATE + tstep - j), :], 0.0)
        cnt = jnp.minimum(PAST_LEN + tstep + 1, win).astype(_F32)
        dpool_ref[rows, :] = (acc / cnt - p_cur).astype(_BF16)

        conv_c = None
        for k in range(CONV_C):
            term = ccw_ref[k:k + 1, :] * extc_ref[slab(tstep + k), :]
            conv_c = term if conv_c is None else conv_c + term
        y_c = _layernorm(conv_c + ccb_ref[...], lncg_ref[...], lncb_ref[...])
        mix_ref[rows, col(2)] = _silu(y_c).astype(_BF16)

        v_n = _layernorm(z_ref[rows, col(7)], lndg_ref[...], lndb_ref[...])
        vn_ref[rows, :] = v_n
        vrow_ref[:, col(tstep)] = v_n

    y_p = jnp.dot(dpool_ref[...], pw_ref[...], preferred_element_type=_F32) * ps_ref[...]
    mix_ref[:, col(1)] = y_p.astype(_BF16)

    for i in range(n_t):
        mixed = sgb_ref[i:i + 1, :] + sgw_ref[i * n_t:i * n_t + 1, :] * vn_ref[slab(0), :]
        for j in range(1, i + 1):
            mixed = mixed + sgw_ref[i * n_t + j:i * n_t + j + 1, :] * vn_ref[slab(j), :]
        mix_ref[slab(i), col(3)] = (z_ref[slab(i), col(6)] * mixed).astype(_BF16)

    x1 = xt_ref[...] + jnp.dot(mix_ref[...], wout_ref[...], preferred_element_type=_F32)
    x1_ref[...] = x1
    _store_route(x1, (gffn_ref, rw_ref, rb_ref, lstrict_ref, ustrict_ref), (h2_ref, route_ref, routet_ref, npad_ref))

    for j in range(CONV_A - 1):
        nsa_ref[:, col(j)] = exta_ref[slab(n_t + j), :]
    for j in range(POOL_STATE):
        nsp_ref[:, col(j)] = extp_ref[slab(n_t + j), :]
    for j in range(CONV_C - 1):
        nsc_ref[:, col(j)] = extc_ref[slab(n_t + j), :]


def _sample_mixer(x, st_a, st_p, st_c, lw, n_t, batch_major_in):
    n_seq = st_a.shape[0]
    nb = SAMPLE_SEQ_BLK
    n_blk = n_seq // nb
    n_tok = nb * n_t
    assert n_tok == TOK_TILE
    consts = _mixer_consts(lw, "sgu_w_rows", "sgu_b_rows")

    def seq_spec(width):
        return pl.BlockSpec((nb, width), lambda i: (i, 0))

    tok_spec = pl.BlockSpec((n_tok, D_MODEL), lambda i: (i, 0))
    state_widths = [(CONV_A - 1) * W_GROUP, POOL_STATE * W_GROUP, (CONV_C - 1) * W_GROUP]
    x_spec = seq_spec(n_t * D_MODEL) if batch_major_in else tok_spec
    return pl.pallas_call(
        functools.partial(_sample_mixer_kernel, batch_major_in),
        grid=(n_blk,),
        in_specs=[x_spec] + [seq_spec(w) for w in state_widths] + [_const_spec(c.shape) for c in consts],
        out_specs=[tok_spec] + [seq_spec(w) for w in state_widths] + [seq_spec(n_t * W_GROUP)]
        + _route_out_specs(lambda i: i),
        out_shape=[jax.ShapeDtypeStruct((n_blk * n_tok, D_MODEL), _F32)]
        + [jax.ShapeDtypeStruct((n_seq, w), _F32) for w in state_widths]
        + [jax.ShapeDtypeStruct((n_seq, n_t * W_GROUP), _F32)] + _route_out_shapes(n_blk),
        scratch_shapes=[pltpu.VMEM((n_tok, D_MODEL), _F32),
                        pltpu.VMEM((n_tok, IN_COLS), _F32),
                        pltpu.VMEM(((CONV_A - 1 + n_t) * nb, W_GROUP), _F32),
                        pltpu.VMEM(((POOL_STATE + n_t) * nb, W_GROUP), _F32),
                        pltpu.VMEM(((CONV_C - 1 + n_t) * nb, W_GROUP), _F32),
                        pltpu.VMEM((n_tok, W_GROUP), _BF16),
                        pltpu.VMEM((n_tok, W_GROUP), _F32),
                        pltpu.VMEM((n_tok, D_MODEL), _BF16)],
        compiler_params=pltpu.CompilerParams(dimension_semantics=("arbitrary",), vmem_limit_bytes=VMEM_LIMIT),
        name="sample_mixer",
    )(x, st_a, st_p, st_c, *consts)


def _plan_kernel(np_ref, ustrict_ref, dest_ref, tab_ref, npx_ref, toff_ref, zc_ref):
    n_tiles = np_ref.shape[0]
    nt_pad = npx_ref.shape[0]
    zeros = jnp.zeros((nt_pad, LANES), _F32)
    npx_ref[...] = zeros
    toff_ref[...] = zeros
    zc_ref[...] = zeros
    npx_ref[0:n_tiles, :] = np_ref[...]
    np_all = npx_ref[...]

    tile_row = lax.broadcasted_iota(_I32, (nt_pad, 1), 0)
    n_real = jnp.sum(np_all, axis=-1, keepdims=True) * (1.0 / BF16_ROWS)
    n_zero = jnp.where(tile_row < n_tiles, N_CHUNK - n_real, 0.0)

    run = jnp.zeros((1, LANES), _F32)
    zrun = jnp.zeros((1, LANES), _F32)
    for i in range(n_tiles):
        toff_ref[i:i + 1, :] = run
        zc_ref[i:i + 1, :] = zrun
        run = run + npx_ref[i:i + 1, :]
        zrun = zrun + n_zero[i:i + 1, :]
    rows_e = run
    rows_pad = jnp.ceil(rows_e * (1.0 / GMM_TILE)) * GMM_TILE
    gap = (rows_pad - rows_e) * (1.0 / BF16_ROWS)

    def excl_lanes(v):
        return jnp.dot(jnp.broadcast_to(v, (SUBLANES, LANES)), ustrict_ref[...], precision=_HI,
                       preferred_element_type=_F32)[0:1]

    gstart = excl_lanes(rows_pad)
    gap_start = excl_lanes(gap)
    gap_total = jnp.sum(gap, axis=-1, keepdims=True)
    rows_total = jnp.sum(rows_pad, axis=-1, keepdims=True)
    seg_start = jnp.dot(np_all, ustrict_ref[...], precision=_HI, preferred_element_type=_F32)
    delta = gstart + toff_ref[...] - seg_start

    chunk = lax.broadcasted_iota(_I32, (nt_pad, LANES), 1).astype(_F32)
    pos = chunk * BF16_ROWS
    q = zc_ref[...] + (chunk - n_real)
    real = pos
    gap_addr = q * BF16_ROWS
    for e in range(N_EXPERTS):
        ss = seg_start[:, e:e + 1]
        se = ss + np_all[:, e:e + 1]
        real = real + jnp.where(jnp.logical_and(ss <= pos, pos < se), delta[:, e:e + 1], 0.0)
        gs = gap_start[:, e:e + 1]
        ge = gs + gap[:, e:e + 1]
        base = gstart[:, e:e + 1] + rows_e[:, e:e + 1] - gs * BF16_ROWS
        gap_addr = gap_addr + jnp.where(jnp.logical_and(gs <= q, q < ge), base, 0.0)
    tail_addr = rows_total + (q - gap_total) * BF16_ROWS
    zero_addr = jnp.where(q < gap_total, gap_addr, tail_addr)
    dest = jnp.where(chunk < n_real, real, zero_addr)
    dest_ref[...] = dest[0:n_tiles, :].astype(_I32)

    n_cols = tab_ref.shape[1]
    row_pos = lax.broadcasted_iota(_I32, (SUBLANES, n_cols), 1).astype(_F32) * GMM_TILE
    t_exp = jnp.zeros((SUBLANES, n_cols), _F32)
    t_val = jnp.zeros((SUBLANES, n_cols), _F32)
    for e in range(N_EXPERTS):
        gs = gstart[:, e:e + 1]
        t_exp = t_exp + jnp.where(gs + rows_pad[:, e:e + 1] <= row_pos, 1.0, 0.0)
        t_val = t_val + jnp.where(jnp.logical_and(gs <= row_pos, row_pos < gs + rows_e[:, e:e + 1]), 1.0, 0.0)
    t_exp = jnp.minimum(t_exp, N_EXPERTS - 1.0)
    sub = lax.broadcasted_iota(_I32, (SUBLANES, n_cols), 0)
    tab_ref[...] = jnp.where(sub == 0, t_exp, jnp.where(sub == 1, t_val, 0.0)).astype(_I32)


def _plan(npad_all, lw, n_gmm_tiles):
    n_tiles = npad_all.shape[0]
    nt_pad = -(-n_tiles // SUBLANES) * SUBLANES
    n_cols = -(-n_gmm_tiles // LANES) * LANES
    out_shape = [jax.ShapeDtypeStruct((n_tiles, LANES), _I32), jax.ShapeDtypeStruct((SUBLANES, n_cols), _I32)]
    return pl.pallas_call(
        _plan_kernel,
        grid=(1,),
        in_specs=[_const_spec(npad_all.shape), _const_spec(lw["ustrict"].shape)],
        out_specs=[_const_spec(s.shape) for s in out_shape],
        out_shape=out_shape,
        scratch_shapes=[pltpu.VMEM((nt_pad, LANES), _F32)] * 3,
        name="moe_plan",
    )(npad_all, lw["ustrict"])


def _sort_kernel(n_prompt_tiles, dest_ref, h2p_ref, h2s_ref, rtp_ref, rts_ref, xs_ref, buf_ref, sem_ref):
    i = pl.program_id(0)
    n = pl.num_programs(0)
    cur = lax.rem(i, 2)
    is_p = i < n_prompt_tiles
    h2 = jnp.where(is_p, h2p_ref[...], h2s_ref[...])
    rt = jnp.where(is_p, rtp_ref[0], rts_ref[0])
    s1 = rt[0:1, :]
    s2 = rt[1:2, :]

    def chunk_copy(tile, c, slot):
        dst = pl.multiple_of(dest_ref[tile * N_CHUNK + c], BF16_ROWS)
        return pltpu.make_async_copy(buf_ref.at[slot, pl.ds(c * BF16_ROWS, BF16_ROWS), :],
                                     xs_ref.at[pl.ds(dst, BF16_ROWS), :], sem_ref.at[slot])

    @pl.when(i < 2)
    def _():
        buf_ref[cur, SLOTS:SLOT_BUF, :] = jnp.zeros((SLOT_BUF - SLOTS, D_MODEL), _BF16)

    grp = 256
    for g in range(SLOTS // grp):
        slot_id = (g * grp + lax.broadcasted_iota(_I32, (grp, TOK_TILE), 0)).astype(_F32)
        perm = jnp.where(jnp.logical_or(slot_id == s1, slot_id == s2), 1.0, 0.0).astype(_BF16)
        buf_ref[cur, g * grp:(g + 1) * grp, :] = jnp.dot(perm, h2, preferred_element_type=_F32).astype(_BF16)

    @pl.when(i > 0)
    def _():
        for c in range(N_CHUNK):
            chunk_copy(i - 1, c, 1 - cur).wait()

    for c in range(N_CHUNK):
        chunk_copy(i, c, cur).start()

    @pl.when(i == n - 1)
    def _():
        for c in range(N_CHUNK):
            chunk_copy(i, c, cur).wait()


def _sort(dest_flat, h2p, h2s, rtp, rts):
    n_p = h2p.shape[0] // TOK_TILE
    n_s = h2s.shape[0] // TOK_TILE
    n_tiles = n_p + n_s
    p_idx = lambda i, d: jnp.minimum(i, n_p - 1)
    s_idx = lambda i, d: jnp.maximum(i - n_p, 0)
    return pl.pallas_call(
        functools.partial(_sort_kernel, n_p),
        grid_spec=pltpu.PrefetchScalarGridSpec(
            num_scalar_prefetch=1,
            grid=(n_tiles,),
            in_specs=[pl.BlockSpec((TOK_TILE, D_MODEL), lambda i, d: (p_idx(i, d), 0)),
                      pl.BlockSpec((TOK_TILE, D_MODEL), lambda i, d: (s_idx(i, d), 0)),
                      pl.BlockSpec((1, SUBLANES, TOK_TILE), lambda i, d: (p_idx(i, d), 0, 0)),
                      pl.BlockSpec((1, SUBLANES, TOK_TILE), lambda i, d: (s_idx(i, d), 0, 0))],
            out_specs=pl.BlockSpec(memory_space=pl.ANY),
            scratch_shapes=[pltpu.VMEM((2, SLOT_BUF, D_MODEL), _BF16), pltpu.SemaphoreType.DMA((2,))],
        ),
        out_shape=jax.ShapeDtypeStruct((n_tiles * SLOT_BUF, D_MODEL), _BF16),
        compiler_params=pltpu.CompilerParams(dimension_semantics=("arbitrary",), vmem_limit_bytes=VMEM_LIMIT),
        name="moe_sort",
    )(dest_flat, h2p, h2s, rtp, rts)


def _gmm_kernel(texp_ref, tval_ref, xs_ref, wgu_ref, wd_ref, ys_ref):
    r = pl.program_id(0)
    valid = tval_ref[r] > 0

    @pl.when(valid)
    def _():
        gu = jnp.dot(xs_ref[...], wgu_ref[0], preferred_element_type=_F32)
        act = (_silu(gu[:, :D_FF_EXPERT]) * gu[:, D_FF_EXPERT:]).astype(_BF16)
        ys_ref[...] = jnp.dot(act, wd_ref[0], preferred_element_type=_F32).astype(_BF16)

    @pl.when(jnp.logical_not(valid))
    def _():
        ys_ref[...] = jnp.zeros(ys_ref.shape, _BF16)


def _gmm(t_exp, t_val, xs, lw):
    n_gmm = xs.shape[0] // GMM_TILE
    return pl.pallas_call(
        _gmm_kernel,
        grid_spec=pltpu.PrefetchScalarGridSpec(
            num_scalar_prefetch=2,
            grid=(n_gmm,),
            in_specs=[pl.BlockSpec((GMM_TILE, D_MODEL), lambda r, te, tv: (r * tv[r], 0)),
                      pl.BlockSpec((1, D_MODEL, 2 * D_FF_EXPERT), lambda r, te, tv: (te[r], 0, 0)),
                      pl.BlockSpec((1, D_FF_EXPERT, D_MODEL), lambda r, te, tv: (te[r], 0, 0))],
            out_specs=pl.BlockSpec((GMM_TILE, D_MODEL), lambda r, te, tv: (r, 0)),
        ),
        out_shape=jax.ShapeDtypeStruct(xs.shape, _BF16),
        compiler_params=pltpu.CompilerParams(dimension_semantics=("arbitrary",), vmem_limit_bytes=VMEM_LIMIT),
        name="moe_experts",
    )(t_exp, t_val, xs, lw["w_gate_up"], lw["w_down"])


def _combine_kernel(n_prompt_tiles, final_norm, dest_ref, x1p_ref, x1s_ref, rp_ref, rs_ref, gfin_ref, ys_ref,
                    outp_ref, outs_ref, ybuf_ref, sem_ref):
    i = pl.program_id(0)
    n = pl.num_programs(0)
    cur = lax.rem(i, 2)
    is_p = i < n_prompt_tiles

    def chunk_copy(tile, c, slot):
        src = pl.multiple_of(dest_ref[tile * N_CHUNK + c], BF16_ROWS)
        return pltpu.make_async_copy(ys_ref.at[pl.ds(src, BF16_ROWS), :],
                                     ybuf_ref.at[slot, pl.ds(c * BF16_ROWS, BF16_ROWS), :], sem_ref.at[slot])

    @pl.when(i == 0)
    def _():
        for c in range(N_CHUNK_REAL):
            chunk_copy(0, c, 0).start()

    @pl.when(i + 1 < n)
    def _():
        for c in range(N_CHUNK_REAL):
            chunk_copy(i + 1, c, 1 - cur).start()

    for c in range(N_CHUNK_REAL):
        chunk_copy(i, c, cur).wait()

    route = jnp.where(is_p, rp_ref[...], rs_ref[...])
    acc = jnp.where(is_p, x1p_ref[...], x1s_ref[...])
    s1 = route[:, 0:1]
    s2 = route[:, 1:2]
    w1 = route[:, 2:3]
    w2 = route[:, 3:4]
    grp = 256
    for g in range(SLOTS // grp):
        slot_id = (g * grp + lax.broadcasted_iota(_I32, (TOK_TILE, grp), 1)).astype(_F32)
        unperm = (jnp.where(slot_id == s1, w1, 0.0) + jnp.where(slot_id == s2, w2, 0.0)).astype(_BF16)
        acc = acc + jnp.dot(unperm, ybuf_ref[cur, g * grp:(g + 1) * grp, :], preferred_element_type=_F32)
    if final_norm:
        acc = _rmsnorm(acc, gfin_ref[...])

    @pl.when(is_p)
    def _():
        outp_ref[...] = acc

    @pl.when(jnp.logical_not(is_p))
    def _():
        outs_ref[...] = acc


def _combine(dest_flat, x1p, x1s, rp, rs, g_fin, ys, final_norm):
    n_p = x1p.shape[0] // TOK_TILE
    n_s = x1s.shape[0] // TOK_TILE
    p_idx = lambda i, d: (jnp.minimum(i, n_p - 1), 0)
    s_idx = lambda i, d: (jnp.maximum(i - n_p, 0), 0)
    return pl.pallas_call(
        functools.partial(_combine_kernel, n_p, final_norm),
        grid_spec=pltpu.PrefetchScalarGridSpec(
            num_scalar_prefetch=1,
            grid=(n_p + n_s,),
            in_specs=[pl.BlockSpec((TOK_TILE, D_MODEL), p_idx), pl.BlockSpec((TOK_TILE, D_MODEL), s_idx),
                      pl.BlockSpec((TOK_TILE, LANES), p_idx), pl.BlockSpec((TOK_TILE, LANES), s_idx),
                      pl.BlockSpec(g_fin.shape, lambda i, d: (0, 0)),
                      pl.BlockSpec(memory_space=pl.ANY)],
            out_specs=[pl.BlockSpec((TOK_TILE, D_MODEL), p_idx), pl.BlockSpec((TOK_TILE, D_MODEL), s_idx)],
            scratch_shapes=[pltpu.VMEM((2, SLOTS, D_MODEL), _BF16), pltpu.SemaphoreType.DMA((2,))],
        ),
        out_shape=[jax.ShapeDtypeStruct(x1p.shape, _F32), jax.ShapeDtypeStruct(x1s.shape, _F32)],
        compiler_params=pltpu.CompilerParams(dimension_semantics=("arbitrary",), vmem_limit_bytes=VMEM_LIMIT),
        name="moe_combine",
    )(dest_flat, x1p, x1s, rp, rs, g_fin, ys)


def _moe(x1p, x1s, routing_p, routing_s, lw, g_fin, final_norm):
    h2p, rp, rtp, npp = routing_p
    h2s, rs, rts, nps = routing_s
    n_tiles = npp.shape[0] + nps.shape[0]
    n_gmm = n_tiles * SLOT_BUF // GMM_TILE
    assert n_tiles * (SLOT_BUF - SLOTS) >= N_EXPERTS * (GMM_TILE - BF16_ROWS)
    npad_all = jnp.concatenate([npp, nps], axis=0).reshape(n_tiles, LANES)
    dest, tab = _plan(npad_all, lw, n_gmm)
    dest_flat = dest.reshape(-1)
    xs = _sort(dest_flat, h2p, h2s, rtp, rts)
    ys = _gmm(tab[0], tab[1], xs, lw)
    return _combine(dest_flat, x1p, x1s, rp, rs, g_fin, ys, final_norm)


def _layer_weights(l, g_mix, w_in, conv_a_w, pool_w, pool_scale, conv_c_w, conv_c_b, ln_c_g, ln_c_b, ln_d_g, ln_d_b,
                   sgu_w, sgu_b, w_out, g_ffn, router_group_w, router_group_b, router_expert_w, router_expert_b,
                   expert_w_gate, expert_w_up, expert_w_down, n_t_sample):
    row = lambda v: v[l].reshape(1, -1)
    pool_bd = jnp.zeros((W_GROUP, W_GROUP), _F32)
    for g in range(len(POOL_WINDOWS)):
        sl = slice(g * POOL_CH, (g + 1) * POOL_CH)
        pool_bd = pool_bd.at[sl, sl].set(pool_w[l, g])
    tril = jnp.tril(jnp.ones((CHUNK, CHUNK), dtype=bool))
    sgu_tril = jnp.where(tril, sgu_w[l], 0.0)
    w_small = sgu_tril[:, :n_t_sample, :n_t_sample]
    sgu_w_rows = jnp.repeat(jnp.transpose(w_small, (1, 2, 0)).reshape(n_t_sample * n_t_sample, N_HEADS_D), HEAD_D, axis=1)
    router_w = jnp.zeros((D_MODEL, 2 * LANES), _F32)
    router_w = router_w.at[:, :N_EXPERT_GROUPS].set(router_group_w[l])
    router_w = router_w.at[:, LANES:LANES + N_EXPERTS].set(router_expert_w[l])
    router_b = jnp.zeros((1, 2 * LANES), _F32)
    router_b = router_b.at[0, :N_EXPERT_GROUPS].set(router_group_b[l])
    router_b = router_b.at[0, LANES:LANES + N_EXPERTS].set(router_expert_b[l])
    return {
        "g_mix": row(g_mix), "w_in": w_in[l].astype(_BF16), "conv_a_w": conv_a_w[l], "pool_w_bd": pool_bd.astype(_BF16),
        "pool_scale": row(pool_scale), "conv_c_w": conv_c_w[l], "conv_c_b": row(conv_c_b), "ln_c_g": row(ln_c_g),
        "ln_c_b": row(ln_c_b), "ln_d_g": row(ln_d_g), "ln_d_b": row(ln_d_b),
        "sgu_w_tril": sgu_tril.astype(_BF16),
        "sgu_bias_rows": jnp.repeat(sgu_b[l].T, HEAD_D, axis=1),
        "sgu_w_rows": sgu_w_rows,
        "sgu_b_rows": jnp.repeat(sgu_b[l][:, :n_t_sample].T, HEAD_D, axis=1),
        "w_out": w_out[l].astype(_BF16), "g_ffn": row(g_ffn), "router_w": router_w, "router_b": router_b,
        "w_gate_up": jnp.concatenate([expert_w_gate[l], expert_w_up[l]], axis=-1).astype(_BF16),
        "w_down": expert_w_down[l].astype(_BF16),
        "lstrict": jnp.tril(jnp.ones((TOK_TILE, TOK_TILE), _F32), -1).astype(_BF16),
        "ustrict": jnp.triu(jnp.ones((LANES, LANES), _F32), 1),
    }


def kernel(x_prompt, x_sample, state_conv_a, state_pool, state_conv_c, g_mix, w_in, conv_a_w, pool_w, pool_scale, conv_c_w, conv_c_b, ln_c_g, ln_c_b, ln_d_g, ln_d_b, sgu_w, sgu_b, w_out, g_ffn, router_group_w, router_group_b, router_expert_w, router_expert_b, expert_w_gate, expert_w_up, expert_w_down, g_final):
    depth = g_mix.shape[0]
    bsz, seq, _ = x_prompt.shape
    nb, n_t, _ = x_sample.shape
    g_fin = g_final.reshape(1, -1)

    xp = x_prompt.reshape(bsz * seq, D_MODEL)
    xs = x_sample.reshape(nb, n_t * D_MODEL)
    outs = {k: [] for k in ("sa_p", "sp_p", "sc_p", "sa_s", "sp_s", "sc_s", "v")}
    for l in range(depth):
        lw = _layer_weights(l, g_mix, w_in, conv_a_w, pool_w, pool_scale, conv_c_w, conv_c_b, ln_c_g, ln_c_b, ln_d_g,
                            ln_d_b, sgu_w, sgu_b, w_out, g_ffn, router_group_w, router_group_b, router_expert_w,
                            router_expert_b, expert_w_gate, expert_w_up, expert_w_down, n_t)
        x1p, sa, sp, sc, *routing_p = _prompt_mixer(xp, bsz, lw)
        outs["sa_p"].append(sa)
        outs["sp_p"].append(sp)
        outs["sc_p"].append(sc)
        x1s, nsa, nsp, nsc, vrow, *routing_s = _sample_mixer(
            xs, state_conv_a[l].reshape(nb, -1), state_pool[l].reshape(nb, -1), state_conv_c[l].reshape(nb, -1),
            lw, n_t, batch_major_in=(l == 0))
        outs["sa_s"].append(nsa.reshape(nb, CONV_A - 1, W_GROUP))
        outs["sp_s"].append(nsp.reshape(nb, POOL_STATE, W_GROUP))
        outs["sc_s"].append(nsc.reshape(nb, CONV_C - 1, W_GROUP))
        outs["v"].append(vrow.reshape(nb, n_t, W_GROUP))
        xp, xs = _moe(x1p, x1s, routing_p, routing_s, lw, g_fin, final_norm=(l == depth - 1))

    y_prompt = xp.reshape(bsz, seq, D_MODEL)
    y_sample = jnp.transpose(xs.reshape(nb // SAMPLE_SEQ_BLK, n_t, SAMPLE_SEQ_BLK, D_MODEL),
                             (0, 2, 1, 3)).reshape(nb, n_t, D_MODEL)
    return (y_prompt, y_sample, jnp.stack(outs["sa_p"]), jnp.stack(outs["sp_p"]), jnp.stack(outs["sc_p"]),
            jnp.stack(outs["sa_s"]), jnp.stack(outs["sp_s"]), jnp.stack(outs["sc_s"]), jnp.stack(outs["v"]))
```

```python
import functools

import jax
import jax.numpy as jnp
from jax import lax
from jax.experimental import pallas as pl
from jax.experimental.pallas import tpu as pltpu

D_MODEL = 1024
W_GROUP = 256
IN_COLS = 8 * W_GROUP
CONV_A = 3
POOL_WINDOWS = (2, 4, 8, 16)
POOL_CH = W_GROUP // len(POOL_WINDOWS)
POOL_STATE = max(POOL_WINDOWS) - 1
CONV_C = 31
CHUNK = 128
N_HEADS_D = 4
HEAD_D = W_GROUP // N_HEADS_D
N_EXPERT_GROUPS = 4
EXPERTS_PER_GROUP = 8
N_EXPERTS = N_EXPERT_GROUPS * EXPERTS_PER_GROUP
TOP_K = 2
D_FF_EXPERT = 128
EPS = 1e-6
PAST_LEN = 16384

LANES = 128
SUBLANES = 8
BF16_ROWS = 16
HALO = 32
ROW_BLK = 64
TOK_TILE = 512
SAMPLE_SEQ_BLK = 64
GMM_TILE = 512
SLOTS = -(-(TOP_K * TOK_TILE + N_EXPERTS * (BF16_ROWS - 1)) // 256) * 256
N_CHUNK_REAL = SLOTS // BF16_ROWS
N_CHUNK = 128
SLOT_BUF = N_CHUNK * BF16_ROWS
VMEM_LIMIT = 56 * 1024 * 1024

_F32 = jnp.float32
_BF16 = jnp.bfloat16
_I32 = jnp.int32
_HI = lax.Precision.HIGHEST


def _rmsnorm(x, g):
    return x * lax.rsqrt(jnp.mean(x * x, axis=-1, keepdims=True) + EPS) * g


def _layernorm(x, g, b):
    mu = jnp.mean(x, axis=-1, keepdims=True)
    xc = x - mu
    var = jnp.mean(xc * xc, axis=-1, keepdims=True)
    return xc * lax.rsqrt(var + EPS) * g + b


def _silu(x):
    return x * jax.nn.sigmoid(x)


def _split_bf16(a):
    bits = lax.bitcast_convert_type(a, jnp.uint32)
    rounded = bits + jnp.uint32(0x7FFF) + ((bits >> 16) & jnp.uint32(1))
    hi = lax.bitcast_convert_type(rounded & jnp.uint32(0xFFFF0000), _F32)
    return hi.astype(_BF16), (a - hi).astype(_BF16)


def _dot_split(a, wh_ref, wl_ref):
    a_hi, a_lo = _split_bf16(a)
    return (jnp.dot(a_hi, wh_ref[...], preferred_element_type=_F32)
            + jnp.dot(a_lo, wh_ref[...], preferred_element_type=_F32)
            + jnp.dot(a_hi, wl_ref[...], preferred_element_type=_F32))


def _weight_split_kernel(w_ref, hi_ref, lo_ref):
    w = w_ref[...]
    hi = w.astype(_BF16)
    hi_ref[...] = hi
    lo_ref[...] = (w - hi.astype(_F32)).astype(_BF16)


def _weight_split(w):
    rows, cols = w.shape
    blk = min(rows, 256)
    spec = pl.BlockSpec((blk, cols), lambda i: (i, 0))
    return pl.pallas_call(
        _weight_split_kernel,
        grid=(rows // blk,),
        in_specs=[spec],
        out_specs=[spec, spec],
        out_shape=[jax.ShapeDtypeStruct(w.shape, _BF16)] * 2,
        name="weight_split",
    )(w)


def _rows_back(x, r):
    return pltpu.roll(x, r, axis=0)


def _rows_ahead(x, r):
    return x if r == 0 else pltpu.roll(x, x.shape[0] - r, axis=0)


def _pool_windows(shape):
    lane = lax.broadcasted_iota(_I32, shape, 1)
    return jnp.left_shift(2, lane // POOL_CH)


def _const_spec(shape):
    nd = len(shape)
    return pl.BlockSpec(shape, lambda *_: (0,) * nd)


def _route_tile(x1, gffn_ref, rwh_ref, rwl_ref, rb_ref, lstrict_ref, ustrict_ref):
    h2 = _rmsnorm(x1, gffn_ref[...])
    h_hi = h2.astype(_BF16)
    logits = _dot_split(h2, rwh_ref, rwl_ref) + rb_ref[...]
    lane = lax.broadcasted_iota(_I32, logits.shape, 1)
    lane_f = lane.astype(_F32)
    neg = jnp.float32(-jnp.inf)
    big = jnp.float32(LANES)

    is_group = jnp.logical_and(lane >= N_EXPERTS, lane < N_EXPERTS + N_EXPERT_GROUPS)
    lg = jnp.where(is_group, logits, neg)
    g_max = jnp.max(lg, axis=-1, keepdims=True)
    g_idx = jnp.min(jnp.where(lg == g_max, lane_f, big), axis=-1, keepdims=True) - N_EXPERTS
    p_top = 1.0 / jnp.sum(jnp.exp(lg - g_max), axis=-1, keepdims=True)

    in_group = (lane // EXPERTS_PER_GROUP).astype(_F32) == g_idx
    le = jnp.where(jnp.logical_and(in_group, lane < N_EXPERTS), logits, neg)
    m1 = jnp.max(le, axis=-1, keepdims=True)
    i1 = jnp.min(jnp.where(le == m1, lane_f, big), axis=-1, keepdims=True)
    le2 = jnp.where(lane_f == i1, neg, le)
    m2 = jnp.max(le2, axis=-1, keepdims=True)
    i2 = jnp.min(jnp.where(le2 == m2, lane_f, big), axis=-1, keepdims=True)
    e2 = jnp.exp(m2 - m1)
    w1 = p_top / (1.0 + e2)
    w2 = p_top * e2 / (1.0 + e2)

    o1 = jnp.where(lane_f == i1, 1.0, 0.0)
    o2 = jnp.where(lane_f == i2, 1.0, 0.0)
    before1 = jnp.dot(lstrict_ref[...], o1.astype(_BF16), preferred_element_type=_F32)
    before2 = jnp.dot(lstrict_ref[...], o2.astype(_BF16), preferred_element_type=_F32)
    n1 = jnp.sum(o1, axis=0, keepdims=True)
    n2 = jnp.sum(o2, axis=0, keepdims=True)
    n_tiles16 = jnp.floor((n1 + n2 + (BF16_ROWS - 1)) * (1.0 / BF16_ROWS))
    npad = n_tiles16 * BF16_ROWS
    seg_start = jnp.dot(jnp.broadcast_to(n_tiles16, (SUBLANES, LANES)).astype(_BF16), ustrict_ref[...],
                        preferred_element_type=_F32)[0:1] * BF16_ROWS
    s1 = jnp.sum(o1 * (seg_start + before1), axis=-1, keepdims=True)
    s2 = jnp.sum(o2 * (seg_start + n1 + before2), axis=-1, keepdims=True)
    route = jnp.where(lane == 0, s1, jnp.where(lane == 1, s2, jnp.where(lane == 2, w1, jnp.where(lane == 3, w2, 0.0))))
    return h_hi, route, npad


def _store_route(x1, route_refs, out_refs):
    h2_ref, route_ref, routet_ref, npad_ref = out_refs
    h2, route, npad = _route_tile(x1, *route_refs)
    h2_ref[...] = h2
    route_ref[...] = route
    routet_ref[0] = jnp.transpose(route)[0:SUBLANES, :]
    npad_ref[0] = npad


def _prompt_mixer_kernel(precise_tail, x_ref, gmix_ref, win_ref, caw_ref, pw_ref, ps_ref, ccw_ref, ccb_ref, lncg_ref,
                         lncb_ref, lndg_ref, lndb_ref, sgw_ref, sgb_ref, wout_ref, gffn_ref, rwh_ref, rwl_ref, rb_ref,
                         lstrict_ref, ustrict_ref, winl_ref, pwl_ref, sgwl_ref, woutl_ref,
                         x1_ref, sa_ref, sp_ref, sc_ref, h2_ref, route_ref, routet_ref, npad_ref,
                         z_ref, exta_ref, extp_ref, extc_ref, dpool_ref, vn_ref, mix_ref, mixf_ref, dpoolf_ref):
    t = pl.program_id(1)
    n_t = pl.num_programs(1)
    tt = x_ref.shape[0]
    tail = slice(tt - CHUNK, tt)
    is_last = t == n_t - 1

    @pl.when(t == 0)
    def _():
        zeros = jnp.zeros((HALO, W_GROUP), _F32)
        exta_ref[0:HALO, :] = zeros
        extp_ref[0:HALO, :] = zeros
        extc_ref[0:HALO, :] = zeros

    x = x_ref[...]
    h = _rmsnorm(x, gmix_ref[...])
    z_ref[...] = jnp.dot(h.astype(_BF16), win_ref[...], preferred_element_type=_F32)
    if precise_tail:
        @pl.when(is_last)
        def _():
            z_ref[tail, :] = _dot_split(h[tt - CHUNK:, :], win_ref, winl_ref)

    def col(k):
        return slice(k * W_GROUP, (k + 1) * W_GROUP)

    for rb in range(tt // ROW_BLK):
        rows = slice(rb * ROW_BLK, (rb + 1) * ROW_BLK)
        ext_rows = slice(HALO + rb * ROW_BLK, HALO + (rb + 1) * ROW_BLK)
        exta_ref[ext_rows, :] = z_ref[rows, col(1)] * z_ref[rows, col(2)]
        extp_ref[ext_rows, :] = z_ref[rows, col(3)]
        extc_ref[ext_rows, :] = z_ref[rows, col(4)] * jax.nn.sigmoid(z_ref[rows, col(5)])

    win = _pool_windows((ROW_BLK, W_GROUP))
    row_iota = lax.broadcasted_iota(_I32, (ROW_BLK, W_GROUP), 0)
    low_group = lax.broadcasted_iota(_I32, (ROW_BLK + 2 * SUBLANES, LANES), 1) < POOL_CH

    for rb in range(tt // ROW_BLK):
        r0 = rb * ROW_BLK
        rows = slice(r0, r0 + ROW_BLK)

        ua = exta_ref[HALO + r0 - SUBLANES:HALO + r0 + ROW_BLK, :]
        conv_a = caw_ref[CONV_A - 1:CONV_A, :] * ua
        for k in range(CONV_A - 1):
            conv_a = conv_a + caw_ref[k:k + 1, :] * _rows_back(ua, CONV_A - 1 - k)
        y_a = z_ref[rows, col(0)] * conv_a[SUBLANES:, :]
        mix_ref[rows, col(0)] = y_a.astype(_BF16)

        pe = extp_ref[HALO + r0 - 2 * SUBLANES:HALO + r0 + ROW_BLK, :]
        s2 = pe + _rows_back(pe, 1)
        s4 = s2 + _rows_back(s2, 2)
        s4_hi = s4[:, LANES:]
        s8 = s4_hi + _rows_back(s4_hi, 4)
        s16 = s8 + _rows_back(s8, 8)
        sums = jnp.concatenate([jnp.where(low_group, s2[:, :LANES], s4[:, :LANES]), jnp.where(low_group, s8, s16)],
                               axis=1)[2 * SUBLANES:, :]
        pos = t * tt + r0 + row_iota
        cnt = jnp.minimum(pos + 1, win).astype(_F32)
        d_pool = sums / cnt - pe[2 * SUBLANES:, :]
        dpool_ref[rows, :] = d_pool.astype(_BF16)

        halves = []
        for hc in range(W_GROUP // LANES):
            lanes = slice(hc * LANES, (hc + 1) * LANES)
            xe = extc_ref[HALO + r0 - HALO:HALO + r0 + ROW_BLK, lanes]
            conv_c = None
            for r in range(SUBLANES):
                xr = _rows_ahead(xe, r)
                for a in range(HALO // SUBLANES + 1):
                    k = SUBLANES * a + r - (HALO - (CONV_C - 1))
                    if 0 <= k < CONV_C:
                        term = ccw_ref[k:k + 1, lanes] * xr[SUBLANES * a:SUBLANES * a + ROW_BLK, :]
                        conv_c = term if conv_c is None else conv_c + term
            halves.append(conv_c)
        y_c = _layernorm(jnp.concatenate(halves, axis=1) + ccb_ref[...], lncg_ref[...], lncb_ref[...])
        y_c = _silu(y_c)
        mix_ref[rows, col(2)] = y_c.astype(_BF16)

        vn_ref[rows, :] = _layernorm(z_ref[rows, col(7)], lndg_ref[...], lndb_ref[...])

        if precise_tail and r0 >= tt - CHUNK:
            tail_rows = slice(r0 - (tt - CHUNK), r0 - (tt - CHUNK) + ROW_BLK)
            mixf_ref[tail_rows, col(0)] = y_a
            mixf_ref[tail_rows, col(2)] = y_c
            dpoolf_ref[tail_rows, :] = d_pool

    y_p = jnp.dot(dpool_ref[...], pw_ref[...], preferred_element_type=_F32) * ps_ref[...]
    mix_ref[:, col(1)] = y_p.astype(_BF16)

    lane = lax.broadcasted_iota(_I32, (CHUNK, W_GROUP), 1)
    for c in range(tt // CHUNK):
        rows = slice(c * CHUNK, (c + 1) * CHUNK)
        vn_c = vn_ref[rows, :]
        mixed = sgb_ref[...]
        for hd in range(N_HEADS_D):
            vm = jnp.where(lane // HEAD_D == hd, vn_c, 0.0).astype(_BF16)
            mixed = mixed + jnp.dot(sgw_ref[hd], vm, preferred_element_type=_F32)
        mix_ref[rows, col(3)] = (z_ref[rows, col(6)] * mixed).astype(_BF16)

    x1_ref[...] = x + jnp.dot(mix_ref[...], wout_ref[...], preferred_element_type=_F32)

    if precise_tail:
        @pl.when(is_last)
        def _():
            mixf_ref[:, col(1)] = _dot_split(dpoolf_ref[...], pw_ref, pwl_ref) * ps_ref[...]
            vn_c = vn_ref[tail, :]
            mixed = sgb_ref[...]
            for hd in range(N_HEADS_D):
                vm_hi, vm_lo = _split_bf16(jnp.where(lane // HEAD_D == hd, vn_c, 0.0))
                mixed = (mixed + jnp.dot(sgw_ref[hd], vm_hi, preferred_element_type=_F32)
                         + jnp.dot(sgw_ref[hd], vm_lo, preferred_element_type=_F32)
                         + jnp.dot(sgwl_ref[hd], vm_hi, preferred_element_type=_F32))
            mixf_ref[:, col(3)] = z_ref[tail, col(6)] * mixed
            x1_ref[tail, :] = x_ref[tail, :] + _dot_split(mixf_ref[...], wout_ref, woutl_ref)

    _store_route(x1_ref[...], (gffn_ref, rwh_ref, rwl_ref, rb_ref, lstrict_ref, ustrict_ref),
                 (h2_ref, route_ref, routet_ref, npad_ref))

    @pl.when(is_last)
    def _():
        end = HALO + tt
        sa_ref[0] = exta_ref[end - (CONV_A - 1):end, :]
        sp_ref[0] = extp_ref[end - POOL_STATE:end, :]
        sc_ref[0] = extc_ref[end - (CONV_C - 1):end, :]

    exta_ref[0:HALO, :] = exta_ref[tt:tt + HALO, :]
    extp_ref[0:HALO, :] = extp_ref[tt:tt + HALO, :]
    extc_ref[0:HALO, :] = extc_ref[tt:tt + HALO, :]


def _route_out_shapes(n_tiles):
    n_tok = n_tiles * TOK_TILE
    return [jax.ShapeDtypeStruct((n_tok, D_MODEL), _BF16),
            jax.ShapeDtypeStruct((n_tok, LANES), _F32),
            jax.ShapeDtypeStruct((n_tiles, SUBLANES, TOK_TILE), _F32),
            jax.ShapeDtypeStruct((n_tiles, 1, LANES), _F32)]


def _route_out_specs(tile_of):
    return [pl.BlockSpec((TOK_TILE, D_MODEL), lambda *g: (tile_of(*g), 0)),
            pl.BlockSpec((TOK_TILE, LANES), lambda *g: (tile_of(*g), 0)),
            pl.BlockSpec((1, SUBLANES, TOK_TILE), lambda *g: (tile_of(*g), 0, 0)),
            pl.BlockSpec((1, 1, LANES), lambda *g: (tile_of(*g), 0, 0))]


def _mixer_consts(lw, sgu_w, sgu_b):
    return [lw["g_mix"], lw["w_in"], lw["conv_a_w"], lw["pool_w_bd"], lw["pool_scale"], lw["conv_c_w"],
            lw["conv_c_b"], lw["ln_c_g"], lw["ln_c_b"], lw["ln_d_g"], lw["ln_d_b"], lw[sgu_w], lw[sgu_b], lw["w_out"],
            lw["g_ffn"], lw["router_w_hi"], lw["router_w_lo"], lw["router_b"], lw["lstrict"], lw["ustrict"]]


def _prompt_mixer(x, bsz, lw, precise_tail):
    seq = x.shape[0] // bsz
    n_t = seq // TOK_TILE
    consts = _mixer_consts(lw, "sgu_w_tril", "sgu_bias_rows") + [lw["w_in_lo"], lw["pool_w_bd_lo"],
                                                                 lw["sgu_w_tril_lo"], lw["w_out_lo"]]
    tile_of = lambda b, t: b * n_t + t
    tile_spec = pl.BlockSpec((TOK_TILE, D_MODEL), lambda b, t: (tile_of(b, t), 0))

    def state_spec(rows):
        return pl.BlockSpec((1, rows, W_GROUP), lambda b, t: (b, 0, 0))

    return pl.pallas_call(
        functools.partial(_prompt_mixer_kernel, precise_tail),
        grid=(bsz, n_t),
        in_specs=[tile_spec] + [_const_spec(c.shape) for c in consts],
        out_specs=[tile_spec, state_spec(CONV_A - 1), state_spec(POOL_STATE), state_spec(CONV_C - 1)]
        + _route_out_specs(tile_of),
        out_shape=[jax.ShapeDtypeStruct((bsz * seq, D_MODEL), _F32),
                   jax.ShapeDtypeStruct((bsz, CONV_A - 1, W_GROUP), _F32),
                   jax.ShapeDtypeStruct((bsz, POOL_STATE, W_GROUP), _F32),
                   jax.ShapeDtypeStruct((bsz, CONV_C - 1, W_GROUP), _F32)] + _route_out_shapes(bsz * n_t),
        scratch_shapes=[pltpu.VMEM((TOK_TILE, IN_COLS), _F32),
                        pltpu.VMEM((HALO + TOK_TILE, W_GROUP), _F32),
                        pltpu.VMEM((HALO + TOK_TILE, W_GROUP), _F32),
                        pltpu.VMEM((HALO + TOK_TILE, W_GROUP), _F32),
                        pltpu.VMEM((TOK_TILE, W_GROUP), _BF16),
                        pltpu.VMEM((TOK_TILE, W_GROUP), _F32),
                        pltpu.VMEM((TOK_TILE, D_MODEL), _BF16),
                        pltpu.VMEM((CHUNK, D_MODEL), _F32),
                        pltpu.VMEM((CHUNK, W_GROUP), _F32)],
        compiler_params=pltpu.CompilerParams(dimension_semantics=("arbitrary", "arbitrary"),
                                             vmem_limit_bytes=VMEM_LIMIT),
        name="prompt_mixer",
    )(x, *consts)


def _sample_mixer_kernel(batch_major_in, x_ref, sta_ref, stp_ref, stc_ref, gmix_ref, win_ref, caw_ref, pw_ref, ps_ref,
                         ccw_ref, ccb_ref, lncg_ref, lncb_ref, lndg_ref, lndb_ref, sgw_ref, sgb_ref, wout_ref,
                         gffn_ref, rwh_ref, rwl_ref, rb_ref, lstrict_ref, ustrict_ref,
                         x1_ref, nsa_ref, nsp_ref, nsc_ref, vrow_ref, h2_ref, route_ref, routet_ref, npad_ref,
                         xt_ref, z_ref, exta_ref, extp_ref, extc_ref, dpool_ref, vn_ref, mix_ref):
    nb = sta_ref.shape[0]
    n_tok = x1_ref.shape[0]
    n_t = n_tok // nb

    def col(k):
        return slice(k * W_GROUP, (k + 1) * W_GROUP)

    def slab(j, n=1):
        return slice(j * nb, (j + n) * nb)

    if batch_major_in:
        for tstep in range(n_t):
            xt_ref[slab(tstep), :] = x_ref[:, tstep * D_MODEL:(tstep + 1) * D_MODEL]
    else:
        xt_ref[...] = x_ref[...]

    h = _rmsnorm(xt_ref[...], gmix_ref[...]).astype(_BF16)
    z_ref[...] = jnp.dot(h, win_ref[...], preferred_element_type=_F32)

    for j in range(CONV_A - 1):
        exta_ref[slab(j), :] = sta_ref[:, col(j)]
    for j in range(POOL_STATE):
        extp_ref[slab(j), :] = stp_ref[:, col(j)]
    for j in range(CONV_C - 1):
        extc_ref[slab(j), :] = stc_ref[:, col(j)]
    for tstep in range(n_t):
        rows = slab(tstep)
        exta_ref[slab(CONV_A - 1 + tstep), :] = z_ref[rows, col(1)] * z_ref[rows, col(2)]
        extp_ref[slab(POOL_STATE + tstep), :] = z_ref[rows, col(3)]
        extc_ref[slab(CONV_C - 1 + tstep), :] = z_ref[rows, col(4)] * jax.nn.sigmoid(z_ref[rows, col(5)])

    win = _pool_windows((nb, W_GROUP))
    for tstep in range(n_t):
        rows = slab(tstep)
        conv_a = None
        for k in range(CONV_A):
            term = caw_ref[k:k + 1, :] * exta_ref[slab(tstep + k), :]
            conv_a = term if conv_a is None else conv_a + term
        mix_ref[rows, col(0)] = (z_ref[rows, col(0)] * conv_a).astype(_BF16)

        p_cur = extp_ref[slab(POOL_STATE + tstep), :]
        acc = p_cur
        for j in range(1, POOL_STATE + 1):
            acc = acc + jnp.where(win > j, extp_ref[slab(POOL_STATE + tstep - j), :], 0.0)
        cnt = jnp.minimum(PAST_LEN + tstep + 1, win).astype(_F32)
        dpool_ref[rows, :] = (acc / cnt - p_cur).astype(_BF16)

        conv_c = None
        for k in range(CONV_C):
            term = ccw_ref[k:k + 1, :] * extc_ref[slab(tstep + k), :]
            conv_c = term if conv_c is None else conv_c + term
        y_c = _layernorm(conv_c + ccb_ref[...], lncg_ref[...], lncb_ref[...])
        mix_ref[rows, col(2)] = _silu(y_c).astype(_BF16)

        v_n = _layernorm(z_ref[rows, col(7)], lndg_ref[...], lndb_ref[...])
        vn_ref[rows, :] = v_n
        vrow_ref[:, col(tstep)] = v_n

    y_p = jnp.dot(dpool_ref[...], pw_ref[...], preferred_element_type=_F32) * ps_ref[...]
    mix_ref[:, col(1)] = y_p.astype(_BF16)

    for i in range(n_t):
        mixed = sgb_ref[i:i + 1, :] + sgw_ref[i * n_t:i * n_t + 1, :] * vn_ref[slab(0), :]
        for j in range(1, i + 1):
            mixed = mixed + sgw_ref[i * n_t + j:i * n_t + j + 1, :] * vn_ref[slab(j), :]
        mix_ref[slab(i), col(3)] = (z_ref[slab(i), col(6)] * mixed).astype(_BF16)

    x1 = xt_ref[...] + jnp.dot(mix_ref[...], wout_ref[...], preferred_element_type=_F32)
    x1_ref[...] = x1
    _store_route(x1, (gffn_ref, rwh_ref, rwl_ref, rb_ref, lstrict_ref, ustrict_ref),
                 (h2_ref, route_ref, routet_ref, npad_ref))

    for j in range(CONV_A - 1):
        nsa_ref[:, col(j)] = exta_ref[slab(n_t + j), :]
    for j in range(POOL_STATE):
        nsp_ref[:, col(j)] = extp_ref[slab(n_t + j), :]
    for j in range(CONV_C - 1):
        nsc_ref[:, col(j)] = extc_ref[slab(n_t + j), :]


def _sample_mixer(x, st_a, st_p, st_c, lw, n_t, batch_major_in):
    n_seq = st_a.shape[0]
    nb = SAMPLE_SEQ_BLK
    n_blk = n_seq // nb
    n_tok = nb * n_t
    assert n_tok == TOK_TILE
    consts = _mixer_consts(lw, "sgu_w_rows", "sgu_b_rows")

    def seq_spec(width):
        return pl.BlockSpec((nb, width), lambda i: (i, 0))

    tok_spec = pl.BlockSpec((n_tok, D_MODEL), lambda i: (i, 0))
    state_widths = [(CONV_A - 1) * W_GROUP, POOL_STATE * W_GROUP, (CONV_C - 1) * W_GROUP]
    x_spec = seq_spec(n_t * D_MODEL) if batch_major_in else tok_spec
    return pl.pallas_call(
        functools.partial(_sample_mixer_kernel, batch_major_in),
        grid=(n_blk,),
        in_specs=[x_spec] + [seq_spec(w) for w in state_widths] + [_const_spec(c.shape) for c in consts],
        out_specs=[tok_spec] + [seq_spec(w) for w in state_widths] + [seq_spec(n_t * W_GROUP)]
        + _route_out_specs(lambda i: i),
        out_shape=[jax.ShapeDtypeStruct((n_blk * n_tok, D_MODEL), _F32)]
        + [jax.ShapeDtypeStruct((n_seq, w), _F32) for w in state_widths]
        + [jax.ShapeDtypeStruct((n_seq, n_t * W_GROUP), _F32)] + _route_out_shapes(n_blk),
        scratch_shapes=[pltpu.VMEM((n_tok, D_MODEL), _F32),
                        pltpu.VMEM((n_tok, IN_COLS), _F32),
                        pltpu.VMEM(((CONV_A - 1 + n_t) * nb, W_GROUP), _F32),
                        pltpu.VMEM(((POOL_STATE + n_t) * nb, W_GROUP), _F32),
                        pltpu.VMEM(((CONV_C - 1 + n_t) * nb, W_GROUP), _F32),
                        pltpu.VMEM((n_tok, W_GROUP), _BF16),
                        pltpu.VMEM((n_tok, W_GROUP), _F32),
                        pltpu.VMEM((n_tok, D_MODEL), _BF16)],
        compiler_params=pltpu.CompilerParams(dimension_semantics=("arbitrary",), vmem_limit_bytes=VMEM_LIMIT),
        name="sample_mixer",
    )(x, st_a, st_p, st_c, *consts)


def _plan_kernel(np_ref, ustrict_ref, dest_ref, tab_ref, npx_ref, toff_ref, zc_ref):
    n_tiles = np_ref.shape[0]
    nt_pad = npx_ref.shape[0]
    zeros = jnp.zeros((nt_pad, LANES), _F32)
    npx_ref[...] = zeros
    toff_ref[...] = zeros
    zc_ref[...] = zeros
    npx_ref[0:n_tiles, :] = np_ref[...]
    np_all = npx_ref[...]

    tile_row = lax.broadcasted_iota(_I32, (nt_pad, 1), 0)
    n_real = jnp.sum(np_all, axis=-1, keepdims=True) * (1.0 / BF16_ROWS)
    n_zero = jnp.where(tile_row < n_tiles, N_CHUNK - n_real, 0.0)

    run = jnp.zeros((1, LANES), _F32)
    zrun = jnp.zeros((1, LANES), _F32)
    for i in range(n_tiles):
        toff_ref[i:i + 1, :] = run
        zc_ref[i:i + 1, :] = zrun
        run = run + npx_ref[i:i + 1, :]
        zrun = zrun + n_zero[i:i + 1, :]
    rows_e = run
    rows_pad = jnp.ceil(rows_e * (1.0 / GMM_TILE)) * GMM_TILE
    gap = (rows_pad - rows_e) * (1.0 / BF16_ROWS)

    def excl_lanes(v):
        return jnp.dot(v.astype(_BF16), ustrict_ref[...], preferred_element_type=_F32)

    gstart = excl_lanes(jnp.broadcast_to(rows_pad * (1.0 / GMM_TILE), (SUBLANES, LANES)))[0:1] * GMM_TILE
    gap_start = excl_lanes(jnp.broadcast_to(gap, (SUBLANES, LANES)))[0:1]
    gap_total = jnp.sum(gap, axis=-1, keepdims=True)
    rows_total = jnp.sum(rows_pad, axis=-1, keepdims=True)
    seg_start = excl_lanes(np_all * (1.0 / BF16_ROWS)) * BF16_ROWS
    delta = gstart + toff_ref[...] - seg_start

    chunk = lax.broadcasted_iota(_I32, (nt_pad, LANES), 1).astype(_F32)
    pos = chunk * BF16_ROWS
    q = zc_ref[...] + (chunk - n_real)
    real = pos
    gap_addr = q * BF16_ROWS
    for e in range(N_EXPERTS):
        ss = seg_start[:, e:e + 1]
        se = ss + np_all[:, e:e + 1]
        real = real + jnp.where(jnp.logical_and(ss <= pos, pos < se), delta[:, e:e + 1], 0.0)
        gs = gap_start[:, e:e + 1]
        ge = gs + gap[:, e:e + 1]
        base = gstart[:, e:e + 1] + rows_e[:, e:e + 1] - gs * BF16_ROWS
        gap_addr = gap_addr + jnp.where(jnp.logical_and(gs <= q, q < ge), base, 0.0)
    tail_addr = rows_total + (q - gap_total) * BF16_ROWS
    zero_addr = jnp.where(q < gap_total, gap_addr, tail_addr)
    dest = jnp.where(chunk < n_real, real, zero_addr)
    dest_ref[...] = dest[0:n_tiles, :].astype(_I32)

    n_cols = tab_ref.shape[1]
    row_pos = lax.broadcasted_iota(_I32, (SUBLANES, n_cols), 1).astype(_F32) * GMM_TILE
    t_exp = jnp.zeros((SUBLANES, n_cols), _F32)
    t_val = jnp.zeros((SUBLANES, n_cols), _F32)
    for e in range(N_EXPERTS):
        gs = gstart[:, e:e + 1]
        t_exp = t_exp + jnp.where(gs + rows_pad[:, e:e + 1] <= row_pos, 1.0, 0.0)
        t_val = t_val + jnp.where(jnp.logical_and(gs <= row_pos, row_pos < gs + rows_e[:, e:e + 1]), 1.0, 0.0)
    t_exp = jnp.minimum(t_exp, N_EXPERTS - 1.0)
    sub = lax.broadcasted_iota(_I32, (SUBLANES, n_cols), 0)
    tab_ref[...] = jnp.where(sub == 0, t_exp, jnp.where(sub == 1, t_val, 0.0)).astype(_I32)


def _plan(npad_all, lw, n_gmm_tiles):
    n_tiles = npad_all.shape[0]
    nt_pad = -(-n_tiles // SUBLANES) * SUBLANES
    n_cols = -(-n_gmm_tiles // LANES) * LANES
    out_shape = [jax.ShapeDtypeStruct((n_tiles, LANES), _I32), jax.ShapeDtypeStruct((SUBLANES, n_cols), _I32)]
    return pl.pallas_call(
        _plan_kernel,
        grid=(1,),
        in_specs=[_const_spec(npad_all.shape), _const_spec(lw["ustrict"].shape)],
        out_specs=[_const_spec(s.shape) for s in out_shape],
        out_shape=out_shape,
        scratch_shapes=[pltpu.VMEM((nt_pad, LANES), _F32)] * 3,
        name="moe_plan",
    )(npad_all, lw["ustrict"])


def _sort_kernel(n_prompt_tiles, dest_ref, h2p_ref, h2s_ref, rtp_ref, rts_ref, xs_ref, buf_ref, sem_ref):
    i = pl.program_id(0)
    n = pl.num_programs(0)
    cur = lax.rem(i, 2)
    is_p = i < n_prompt_tiles
    h2 = jnp.where(is_p, h2p_ref[...], h2s_ref[...])
    rt = jnp.where(is_p, rtp_ref[0], rts_ref[0])
    s1 = rt[0:1, :]
    s2 = rt[1:2, :]

    def chunk_copy(tile, c, slot):
        dst = pl.multiple_of(dest_ref[tile * N_CHUNK + c], BF16_ROWS)
        return pltpu.make_async_copy(buf_ref.at[slot, pl.ds(c * BF16_ROWS, BF16_ROWS), :],
                                     xs_ref.at[pl.ds(dst, BF16_ROWS), :], sem_ref.at[slot])

    @pl.when(i < 2)
    def _():
        buf_ref[cur, SLOTS:SLOT_BUF, :] = jnp.zeros((SLOT_BUF - SLOTS, D_MODEL), _BF16)

    grp = 256
    for g in range(SLOTS // grp):
        slot_id = (g * grp + lax.broadcasted_iota(_I32, (grp, TOK_TILE), 0)).astype(_F32)
        perm = jnp.where(jnp.logical_or(slot_id == s1, slot_id == s2), 1.0, 0.0).astype(_BF16)
        buf_ref[cur, g * grp:(g + 1) * grp, :] = jnp.dot(perm, h2, preferred_element_type=_F32).astype(_BF16)

    @pl.when(i > 0)
    def _():
        for c in range(N_CHUNK):
            chunk_copy(i - 1, c, 1 - cur).wait()

    for c in range(N_CHUNK):
        chunk_copy(i, c, cur).start()

    @pl.when(i == n - 1)
    def _():
        for c in range(N_CHUNK):
            chunk_copy(i, c, cur).wait()


def _sort(dest_flat, h2p, h2s, rtp, rts):
    n_p = h2p.shape[0] // TOK_TILE
    n_s = h2s.shape[0] // TOK_TILE
    n_tiles = n_p + n_s
    p_idx = lambda i, d: jnp.minimum(i, n_p - 1)
    s_idx = lambda i, d: jnp.maximum(i - n_p, 0)
    return pl.pallas_call(
        functools.partial(_sort_kernel, n_p),
        grid_spec=pltpu.PrefetchScalarGridSpec(
            num_scalar_prefetch=1,
            grid=(n_tiles,),
            in_specs=[pl.BlockSpec((TOK_TILE, D_MODEL), lambda i, d: (p_idx(i, d), 0)),
                      pl.BlockSpec((TOK_TILE, D_MODEL), lambda i, d: (s_idx(i, d), 0)),
                      pl.BlockSpec((1, SUBLANES, TOK_TILE), lambda i, d: (p_idx(i, d), 0, 0)),
                      pl.BlockSpec((1, SUBLANES, TOK_TILE), lambda i, d: (s_idx(i, d), 0, 0))],
            out_specs=pl.BlockSpec(memory_space=pl.ANY),
            scratch_shapes=[pltpu.VMEM((2, SLOT_BUF, D_MODEL), _BF16), pltpu.SemaphoreType.DMA((2,))],
        ),
        out_shape=jax.ShapeDtypeStruct((n_tiles * SLOT_BUF, D_MODEL), _BF16),
        compiler_params=pltpu.CompilerParams(dimension_semantics=("arbitrary",), vmem_limit_bytes=VMEM_LIMIT),
        name="moe_sort",
    )(dest_flat, h2p, h2s, rtp, rts)


def _gmm_kernel(texp_ref, tval_ref, xs_ref, wgu_ref, wd_ref, ys_ref):
    r = pl.program_id(0)
    valid = tval_ref[r] > 0

    @pl.when(valid)
    def _():
        gu = jnp.dot(xs_ref[...], wgu_ref[0], preferred_element_type=_F32)
        act = (_silu(gu[:, :D_FF_EXPERT]) * gu[:, D_FF_EXPERT:]).astype(_BF16)
        ys_ref[...] = jnp.dot(act, wd_ref[0], preferred_element_type=_F32).astype(_BF16)

    @pl.when(jnp.logical_not(valid))
    def _():
        ys_ref[...] = jnp.zeros(ys_ref.shape, _BF16)


def _gmm(t_exp, t_val, xs, lw):
    n_gmm = xs.shape[0] // GMM_TILE
    return pl.pallas_call(
        _gmm_kernel,
        grid_spec=pltpu.PrefetchScalarGridSpec(
            num_scalar_prefetch=2,
            grid=(n_gmm,),
            in_specs=[pl.BlockSpec((GMM_TILE, D_MODEL), lambda r, te, tv: (r * tv[r], 0)),
                      pl.BlockSpec((1, D_MODEL, 2 * D_FF_EXPERT), lambda r, te, tv: (te[r], 0, 0)),
                      pl.BlockSpec((1, D_FF_EXPERT, D_MODEL), lambda r, te, tv: (te[r], 0, 0))],
            out_specs=pl.BlockSpec((GMM_TILE, D_MODEL), lambda r, te, tv: (r, 0)),
        ),
        out_shape=jax.ShapeDtypeStruct(xs.shape, _BF16),
        compiler_params=pltpu.CompilerParams(dimension_semantics=("arbitrary",), vmem_limit_bytes=VMEM_LIMIT),
        name="moe_experts",
    )(t_exp, t_val, xs, lw["w_gate_up"], lw["w_down"])


def _combine_kernel(n_prompt_tiles, final_norm, dest_ref, x1p_ref, x1s_ref, rp_ref, rs_ref, gfin_ref, ys_ref,
                    outp_ref, outs_ref, ybuf_ref, sem_ref):
    i = pl.program_id(0)
    n = pl.num_programs(0)
    cur = lax.rem(i, 2)
    is_p = i < n_prompt_tiles

    def chunk_copy(tile, c, slot):
        src = pl.multiple_of(dest_ref[tile * N_CHUNK + c], BF16_ROWS)
        return pltpu.make_async_copy(ys_ref.at[pl.ds(src, BF16_ROWS), :],
                                     ybuf_ref.at[slot, pl.ds(c * BF16_ROWS, BF16_ROWS), :], sem_ref.at[slot])

    @pl.when(i == 0)
    def _():
        for c in range(N_CHUNK_REAL):
            chunk_copy(0, c, 0).start()

    @pl.when(i + 1 < n)
    def _():
        for c in range(N_CHUNK_REAL):
            chunk_copy(i + 1, c, 1 - cur).start()

    for c in range(N_CHUNK_REAL):
        chunk_copy(i, c, cur).wait()

    route = jnp.where(is_p, rp_ref[...], rs_ref[...])
    acc = jnp.where(is_p, x1p_ref[...], x1s_ref[...])
    s1 = route[:, 0:1]
    s2 = route[:, 1:2]
    w1 = route[:, 2:3]
    w2 = route[:, 3:4]
    grp = 256
    for g in range(SLOTS // grp):
        slot_id = (g * grp + lax.broadcasted_iota(_I32, (TOK_TILE, grp), 1)).astype(_F32)
        unperm = (jnp.where(slot_id == s1, w1, 0.0) + jnp.where(slot_id == s2, w2, 0.0)).astype(_BF16)
        acc = acc + jnp.dot(unperm, ybuf_ref[cur, g * grp:(g + 1) * grp, :], preferred_element_type=_F32)
    if final_norm:
        acc = _rmsnorm(acc, gfin_ref[...])

    @pl.when(is_p)
    def _():
        outp_ref[...] = acc

    @pl.when(jnp.logical_not(is_p))
    def _():
        outs_ref[...] = acc


def _combine(dest_flat, x1p, x1s, rp, rs, g_fin, ys, final_norm):
    n_p = x1p.shape[0] // TOK_TILE
    n_s = x1s.shape[0] // TOK_TILE
    p_idx = lambda i, d: (jnp.minimum(i, n_p - 1), 0)
    s_idx = lambda i, d: (jnp.maximum(i - n_p, 0), 0)
    return pl.pallas_call(
        functools.partial(_combine_kernel, n_p, final_norm),
        grid_spec=pltpu.PrefetchScalarGridSpec(
            num_scalar_prefetch=1,
            grid=(n_p + n_s,),
            in_specs=[pl.BlockSpec((TOK_TILE, D_MODEL), p_idx), pl.BlockSpec((TOK_TILE, D_MODEL), s_idx),
                      pl.BlockSpec((TOK_TILE, LANES), p_idx), pl.BlockSpec((TOK_TILE, LANES), s_idx),
                      pl.BlockSpec(g_fin.shape, lambda i, d: (0, 0)),
                      pl.BlockSpec(memory_space=pl.ANY)],
            out_specs=[pl.BlockSpec((TOK_TILE, D_MODEL), p_idx), pl.BlockSpec((TOK_TILE, D_MODEL), s_idx)],
            scratch_shapes=[pltpu.VMEM((2, SLOTS, D_MODEL), _BF16), pltpu.SemaphoreType.DMA((2,))],
        ),
        out_shape=[jax.ShapeDtypeStruct(x1p.shape, _F32), jax.ShapeDtypeStruct(x1s.shape, _F32)],
        compiler_params=pltpu.CompilerParams(dimension_semantics=("arbitrary",), vmem_limit_bytes=VMEM_LIMIT),
        name="moe_combine",
    )(dest_flat, x1p, x1s, rp, rs, g_fin, ys)


def _moe(x1p, x1s, routing_p, routing_s, lw, g_fin, final_norm):
    h2p, rp, rtp, npp = routing_p
    h2s, rs, rts, nps = routing_s
    n_tiles = npp.shape[0] + nps.shape[0]
    n_gmm = n_tiles * SLOT_BUF // GMM_TILE
    assert n_tiles * (SLOT_BUF - SLOTS) >= N_EXPERTS * (GMM_TILE - BF16_ROWS)
    npad_all = jnp.concatenate([npp, nps], axis=0).reshape(n_tiles, LANES)
    dest, tab = _plan(npad_all, lw, n_gmm)
    dest_flat = dest.reshape(-1)
    xs = _sort(dest_flat, h2p, h2s, rtp, rts)
    ys = _gmm(tab[0], tab[1], xs, lw)
    return _combine(dest_flat, x1p, x1s, rp, rs, g_fin, ys, final_norm)


def _layer_weights(l, g_mix, w_in, conv_a_w, pool_w, pool_scale, conv_c_w, conv_c_b, ln_c_g, ln_c_b, ln_d_g, ln_d_b,
                   sgu_w, sgu_b, w_out, g_ffn, router_group_w, router_group_b, router_expert_w, router_expert_b,
                   expert_w_gate, expert_w_up, expert_w_down, n_t_sample, precise_tail):
    row = lambda v: v[l].reshape(1, -1)
    pool_bd = jnp.zeros((W_GROUP, W_GROUP), _F32)
    for g in range(len(POOL_WINDOWS)):
        sl = slice(g * POOL_CH, (g + 1) * POOL_CH)
        pool_bd = pool_bd.at[sl, sl].set(pool_w[l, g])
    tril = jnp.tril(jnp.ones((CHUNK, CHUNK), dtype=bool))
    sgu_tril = jnp.where(tril, sgu_w[l], 0.0)
    w_small = sgu_tril[:, :n_t_sample, :n_t_sample]
    sgu_w_rows = jnp.repeat(jnp.transpose(w_small, (1, 2, 0)).reshape(n_t_sample * n_t_sample, N_HEADS_D), HEAD_D, axis=1)
    n_route = N_EXPERTS + N_EXPERT_GROUPS
    router_w = jnp.pad(jnp.concatenate([router_expert_w[l], router_group_w[l]], axis=1), ((0, 0), (0, LANES - n_route)))
    router_b = jnp.pad(jnp.concatenate([router_expert_b[l], router_group_b[l]]), (0, LANES - n_route)).reshape(1, LANES)
    router_w_hi, router_w_lo = _weight_split(router_w)
    if precise_tail:
        w_in_hi, w_in_lo = _weight_split(w_in[l])
        w_out_hi, w_out_lo = _weight_split(w_out[l])
        pool_hi, pool_lo = _weight_split(pool_bd)
        sgu_hi, sgu_lo = (s.reshape(sgu_tril.shape) for s in _weight_split(sgu_tril.reshape(-1, CHUNK)))
    else:
        w_in_hi = w_in_lo = w_in[l].astype(_BF16)
        w_out_hi = w_out_lo = w_out[l].astype(_BF16)
        pool_hi = pool_lo = pool_bd.astype(_BF16)
        sgu_hi = sgu_lo = sgu_tril.astype(_BF16)
    return {
        "w_in_lo": w_in_lo, "pool_w_bd_lo": pool_lo, "sgu_w_tril_lo": sgu_lo, "w_out_lo": w_out_lo,
        "g_mix": row(g_mix), "w_in": w_in_hi, "conv_a_w": conv_a_w[l], "pool_w_bd": pool_hi,
        "pool_scale": row(pool_scale), "conv_c_w": conv_c_w[l], "conv_c_b": row(conv_c_b), "ln_c_g": row(ln_c_g),
        "ln_c_b": row(ln_c_b), "ln_d_g": row(ln_d_g), "ln_d_b": row(ln_d_b),
        "sgu_w_tril": sgu_hi,
        "sgu_bias_rows": jnp.repeat(sgu_b[l].T, HEAD_D, axis=1),
        "sgu_w_rows": sgu_w_rows,
        "sgu_b_rows": jnp.repeat(sgu_b[l][:, :n_t_sample].T, HEAD_D, axis=1),
        "w_out": w_out_hi, "g_ffn": row(g_ffn), "router_w_hi": router_w_hi, "router_w_lo": router_w_lo,
        "router_b": router_b,
        "w_gate_up": jnp.concatenate([expert_w_gate[l], expert_w_up[l]], axis=-1).astype(_BF16),
        "w_down": expert_w_down[l].astype(_BF16),
        "lstrict": jnp.tril(jnp.ones((TOK_TILE, TOK_TILE), _F32), -1).astype(_BF16),
        "ustrict": jnp.triu(jnp.ones((LANES, LANES), _F32), 1).astype(_BF16),
    }


def kernel(x_prompt, x_sample, state_conv_a, state_pool, state_conv_c, g_mix, w_in, conv_a_w, pool_w, pool_scale, conv_c_w, conv_c_b, ln_c_g, ln_c_b, ln_d_g, ln_d_b, sgu_w, sgu_b, w_out, g_ffn, router_group_w, router_group_b, router_expert_w, router_expert_b, expert_w_gate, expert_w_up, expert_w_down, g_final):
    depth = g_mix.shape[0]
    bsz, seq, _ = x_prompt.shape
    nb, n_t, _ = x_sample.shape
    g_fin = g_final.reshape(1, -1)

    xp = x_prompt.reshape(bsz * seq, D_MODEL)
    xs = x_sample.reshape(nb, n_t * D_MODEL)
    outs = {k: [] for k in ("sa_p", "sp_p", "sc_p", "sa_s", "sp_s", "sc_s", "v")}
    for l in range(depth):
        precise_tail = l + 1 < depth
        lw = _layer_weights(l, g_mix, w_in, conv_a_w, pool_w, pool_scale, conv_c_w, conv_c_b, ln_c_g, ln_c_b, ln_d_g,
                            ln_d_b, sgu_w, sgu_b, w_out, g_ffn, router_group_w, router_group_b, router_expert_w,
                            router_expert_b, expert_w_gate, expert_w_up, expert_w_down, n_t, precise_tail)
        x1p, sa, sp, sc, *routing_p = _prompt_mixer(xp, bsz, lw, precise_tail)
        outs["sa_p"].append(sa)
        outs["sp_p"].append(sp)
        outs["sc_p"].append(sc)
        x1s, nsa, nsp, nsc, vrow, *routing_s = _sample_mixer(
            xs, state_conv_a[l].reshape(nb, -1), state_pool[l].reshape(nb, -1), state_conv_c[l].reshape(nb, -1),
            lw, n_t, batch_major_in=(l == 0))
        outs["sa_s"].append(nsa.reshape(nb, CONV_A - 1, W_GROUP))
        outs["sp_s"].append(nsp.reshape(nb, POOL_STATE, W_GROUP))
        outs["sc_s"].append(nsc.reshape(nb, CONV_C - 1, W_GROUP))
        outs["v"].append(vrow.reshape(nb, n_t, W_GROUP))
        xp, xs = _moe(x1p, x1s, routing_p, routing_s, lw, g_fin, final_norm=(l == depth - 1))

    y_prompt = xp.reshape(bsz, seq, D_MODEL)
    y_sample = jnp.transpose(xs.reshape(nb // SAMPLE_SEQ_BLK, n_t, SAMPLE_SEQ_BLK, D_MODEL),
                             (0, 2, 1, 3)).reshape(nb, n_t, D_MODEL)
    return (y_prompt, y_sample, jnp.stack(outs["sa_p"]), jnp.stack(outs["sp_p"]), jnp.stack(outs["sc_p"]),
            jnp.stack(outs["sa_s"]), jnp.stack(outs["sp_s"]), jnp.stack(outs["sc_s"]), jnp.stack(outs["v"]))
```

```python
import functools

import jax
import jax.numpy as jnp
from jax import lax
from jax.experimental import pallas as pl
from jax.experimental.pallas import tpu as pltpu

D_MODEL = 1024
W_GROUP = 256
IN_COLS = 8 * W_GROUP
CONV_A = 3
POOL_WINDOWS = (2, 4, 8, 16)
POOL_CH = W_GROUP // len(POOL_WINDOWS)
POOL_STATE = max(POOL_WINDOWS) - 1
CONV_C = 31
CHUNK = 128
N_HEADS_D = 4
HEAD_D = W_GROUP // N_HEADS_D
N_EXPERT_GROUPS = 4
EXPERTS_PER_GROUP = 8
N_EXPERTS = N_EXPERT_GROUPS * EXPERTS_PER_GROUP
TOP_K = 2
D_FF_EXPERT = 128
EPS = 1e-6
PAST_LEN = 16384

LANES = 128
SUBLANES = 8
BF16_ROWS = 16
HALO = 32
ROW_BLK = 64
TOK_TILE = 512
SAMPLE_SEQ_BLK = 64
GMM_TILE = 512
GMM_GROUP = 2
SLOTS = -(-(TOP_K * TOK_TILE + N_EXPERTS * (BF16_ROWS - 1)) // 256) * 256
N_CHUNK_REAL = SLOTS // BF16_ROWS
N_CHUNK = 128
SLOT_BUF = N_CHUNK * BF16_ROWS
VMEM_LIMIT = 56 * 1024 * 1024

_F32 = jnp.float32
_BF16 = jnp.bfloat16
_I32 = jnp.int32
_HI = lax.Precision.HIGHEST


def _rmsnorm(x, g):
    return x * lax.rsqrt(jnp.mean(x * x, axis=-1, keepdims=True) + EPS) * g


def _layernorm(x, g, b):
    mu = jnp.mean(x, axis=-1, keepdims=True)
    xc = x - mu
    var = jnp.mean(xc * xc, axis=-1, keepdims=True)
    return xc * lax.rsqrt(var + EPS) * g + b


def _silu(x):
    return x * jax.nn.sigmoid(x)


def _split_bf16(a):
    bits = lax.bitcast_convert_type(a, jnp.uint32)
    rounded = bits + jnp.uint32(0x7FFF) + ((bits >> 16) & jnp.uint32(1))
    hi = lax.bitcast_convert_type(rounded & jnp.uint32(0xFFFF0000), _F32)
    return hi.astype(_BF16), (a - hi).astype(_BF16)


def _dot_split(a, wh_ref, wl_ref):
    a_hi, a_lo = _split_bf16(a)
    return (jnp.dot(a_hi, wh_ref[...], preferred_element_type=_F32)
            + jnp.dot(a_lo, wh_ref[...], preferred_element_type=_F32)
            + jnp.dot(a_hi, wl_ref[...], preferred_element_type=_F32))


def _weight_split_kernel(w_ref, hi_ref, lo_ref):
    w = w_ref[...]
    hi = w.astype(_BF16)
    hi_ref[...] = hi
    lo_ref[...] = (w - hi.astype(_F32)).astype(_BF16)


def _weight_split(w):
    rows, cols = w.shape
    blk = min(rows, 256)
    spec = pl.BlockSpec((blk, cols), lambda i: (i, 0))
    return pl.pallas_call(
        _weight_split_kernel,
        grid=(rows // blk,),
        in_specs=[spec],
        out_specs=[spec, spec],
        out_shape=[jax.ShapeDtypeStruct(w.shape, _BF16)] * 2,
        name="weight_split",
    )(w)


def _rows_back(x, r):
    return pltpu.roll(x, r, axis=0)


def _rows_ahead(x, r):
    return x if r == 0 else pltpu.roll(x, x.shape[0] - r, axis=0)


def _pool_windows(shape):
    lane = lax.broadcasted_iota(_I32, shape, 1)
    return jnp.left_shift(2, lane // POOL_CH)


def _const_spec(shape):
    nd = len(shape)
    return pl.BlockSpec(shape, lambda *_: (0,) * nd)


def _route_tile(x1, gffn_ref, rwh_ref, rwl_ref, rb_ref, lstrict_ref, ustrict_ref):
    h2 = _rmsnorm(x1, gffn_ref[...])
    h_hi = h2.astype(_BF16)
    logits = _dot_split(h2, rwh_ref, rwl_ref) + rb_ref[...]
    lane = lax.broadcasted_iota(_I32, logits.shape, 1)
    lane_f = lane.astype(_F32)
    neg = jnp.float32(-jnp.inf)
    big = jnp.float32(LANES)

    is_group = jnp.logical_and(lane >= N_EXPERTS, lane < N_EXPERTS + N_EXPERT_GROUPS)
    lg = jnp.where(is_group, logits, neg)
    g_max = jnp.max(lg, axis=-1, keepdims=True)
    g_idx = jnp.min(jnp.where(lg == g_max, lane_f, big), axis=-1, keepdims=True) - N_EXPERTS
    p_top = 1.0 / jnp.sum(jnp.exp(lg - g_max), axis=-1, keepdims=True)

    in_group = (lane // EXPERTS_PER_GROUP).astype(_F32) == g_idx
    le = jnp.where(jnp.logical_and(in_group, lane < N_EXPERTS), logits, neg)
    m1 = jnp.max(le, axis=-1, keepdims=True)
    i1 = jnp.min(jnp.where(le == m1, lane_f, big), axis=-1, keepdims=True)
    le2 = jnp.where(lane_f == i1, neg, le)
    m2 = jnp.max(le2, axis=-1, keepdims=True)
    i2 = jnp.min(jnp.where(le2 == m2, lane_f, big), axis=-1, keepdims=True)
    e2 = jnp.exp(m2 - m1)
    w1 = p_top / (1.0 + e2)
    w2 = p_top * e2 / (1.0 + e2)

    o1 = jnp.where(lane_f == i1, 1.0, 0.0)
    o2 = jnp.where(lane_f == i2, 1.0, 0.0)
    before1 = jnp.dot(lstrict_ref[...], o1.astype(_BF16), preferred_element_type=_F32)
    before2 = jnp.dot(lstrict_ref[...], o2.astype(_BF16), preferred_element_type=_F32)
    n1 = jnp.sum(o1, axis=0, keepdims=True)
    n2 = jnp.sum(o2, axis=0, keepdims=True)
    n_tiles16 = jnp.floor((n1 + n2 + (BF16_ROWS - 1)) * (1.0 / BF16_ROWS))
    npad = n_tiles16 * BF16_ROWS
    seg_start = jnp.dot(jnp.broadcast_to(n_tiles16, (SUBLANES, LANES)).astype(_BF16), ustrict_ref[...],
                        preferred_element_type=_F32)[0:1] * BF16_ROWS
    s1 = jnp.sum(o1 * (seg_start + before1), axis=-1, keepdims=True)
    s2 = jnp.sum(o2 * (seg_start + n1 + before2), axis=-1, keepdims=True)
    route = jnp.where(lane == 0, s1, jnp.where(lane == 1, s2, jnp.where(lane == 2, w1, jnp.where(lane == 3, w2, 0.0))))
    return h_hi, route, npad


def _store_route(x1, route_refs, out_refs):
    h2_ref, route_ref, routet_ref, npad_ref = out_refs
    h2, route, npad = _route_tile(x1, *route_refs)
    h2_ref[...] = h2
    route_ref[...] = route
    routet_ref[0] = jnp.transpose(route)[0:SUBLANES, :]
    npad_ref[0] = npad


def _prompt_mixer_kernel(precise_tail, x_ref, gmix_ref, win_ref, caw_ref, pw_ref, ps_ref, ccw_ref, ccb_ref, lncg_ref,
                         lncb_ref, lndg_ref, lndb_ref, sgw_ref, sgb_ref, wout_ref, gffn_ref, rwh_ref, rwl_ref, rb_ref,
                         lstrict_ref, ustrict_ref, winl_ref, pwl_ref, sgwl_ref, woutl_ref,
                         x1_ref, sa_ref, sp_ref, sc_ref, h2_ref, route_ref, routet_ref, npad_ref,
                         z_ref, exta_ref, extp_ref, extc_ref, dpool_ref, vn_ref, mix_ref, mixf_ref, dpoolf_ref):
    t = pl.program_id(1)
    n_t = pl.num_programs(1)
    tt = x_ref.shape[0]
    tail = slice(tt - CHUNK, tt)
    is_last = t == n_t - 1

    @pl.when(t == 0)
    def _():
        zeros = jnp.zeros((HALO, W_GROUP), _F32)
        exta_ref[0:HALO, :] = zeros
        extp_ref[0:HALO, :] = zeros
        extc_ref[0:HALO, :] = zeros

    x = x_ref[...]
    h = _rmsnorm(x, gmix_ref[...])
    z_ref[...] = jnp.dot(h.astype(_BF16), win_ref[...], preferred_element_type=_F32)
    if precise_tail:
        @pl.when(is_last)
        def _():
            z_ref[tail, :] = _dot_split(h[tt - CHUNK:, :], win_ref, winl_ref)

    def col(k):
        return slice(k * W_GROUP, (k + 1) * W_GROUP)

    for rb in range(tt // ROW_BLK):
        rows = slice(rb * ROW_BLK, (rb + 1) * ROW_BLK)
        ext_rows = slice(HALO + rb * ROW_BLK, HALO + (rb + 1) * ROW_BLK)
        exta_ref[ext_rows, :] = z_ref[rows, col(1)] * z_ref[rows, col(2)]
        extp_ref[ext_rows, :] = z_ref[rows, col(3)]
        extc_ref[ext_rows, :] = z_ref[rows, col(4)] * jax.nn.sigmoid(z_ref[rows, col(5)])

    win = _pool_windows((ROW_BLK, W_GROUP))
    row_iota = lax.broadcasted_iota(_I32, (ROW_BLK, W_GROUP), 0)
    low_group = lax.broadcasted_iota(_I32, (ROW_BLK + 2 * SUBLANES, LANES), 1) < POOL_CH

    for rb in range(tt // ROW_BLK):
        r0 = rb * ROW_BLK
        rows = slice(r0, r0 + ROW_BLK)

        ua = exta_ref[HALO + r0 - SUBLANES:HALO + r0 + ROW_BLK, :]
        conv_a = caw_ref[CONV_A - 1:CONV_A, :] * ua
        for k in range(CONV_A - 1):
            conv_a = conv_a + caw_ref[k:k + 1, :] * _rows_back(ua, CONV_A - 1 - k)
        y_a = z_ref[rows, col(0)] * conv_a[SUBLANES:, :]
        mix_ref[rows, col(0)] = y_a.astype(_BF16)

        pe = extp_ref[HALO + r0 - 2 * SUBLANES:HALO + r0 + ROW_BLK, :]
        s2 = pe + _rows_back(pe, 1)
        s4 = s2 + _rows_back(s2, 2)
        s4_hi = s4[:, LANES:]
        s8 = s4_hi + _rows_back(s4_hi, 4)
        s16 = s8 + _rows_back(s8, 8)
        sums = jnp.concatenate([jnp.where(low_group, s2[:, :LANES], s4[:, :LANES]), jnp.where(low_group, s8, s16)],
                               axis=1)[2 * SUBLANES:, :]
        pos = t * tt + r0 + row_iota
        cnt = jnp.minimum(pos + 1, win).astype(_F32)
        d_pool = sums / cnt - pe[2 * SUBLANES:, :]
        dpool_ref[rows, :] = d_pool.astype(_BF16)

        halves = []
        for hc in range(W_GROUP // LANES):
            lanes = slice(hc * LANES, (hc + 1) * LANES)
            xe = extc_ref[HALO + r0 - HALO:HALO + r0 + ROW_BLK, lanes]
            conv_c = None
            for r in range(SUBLANES):
                xr = _rows_ahead(xe, r)
                for a in range(HALO // SUBLANES + 1):
                    k = SUBLANES * a + r - (HALO - (CONV_C - 1))
                    if 0 <= k < CONV_C:
                        term = ccw_ref[k:k + 1, lanes] * xr[SUBLANES * a:SUBLANES * a + ROW_BLK, :]
                        conv_c = term if conv_c is None else conv_c + term
            halves.append(conv_c)
        y_c = _layernorm(jnp.concatenate(halves, axis=1) + ccb_ref[...], lncg_ref[...], lncb_ref[...])
        y_c = _silu(y_c)
        mix_ref[rows, col(2)] = y_c.astype(_BF16)

        vn_ref[rows, :] = _layernorm(z_ref[rows, col(7)], lndg_ref[...], lndb_ref[...])

        if precise_tail and r0 >= tt - CHUNK:
            tail_rows = slice(r0 - (tt - CHUNK), r0 - (tt - CHUNK) + ROW_BLK)
            mixf_ref[tail_rows, col(0)] = y_a
            mixf_ref[tail_rows, col(2)] = y_c
            dpoolf_ref[tail_rows, :] = d_pool

    y_p = jnp.dot(dpool_ref[...], pw_ref[...], preferred_element_type=_F32) * ps_ref[...]
    mix_ref[:, col(1)] = y_p.astype(_BF16)

    lane = lax.broadcasted_iota(_I32, (CHUNK, W_GROUP), 1)
    for c in range(tt // CHUNK):
        rows = slice(c * CHUNK, (c + 1) * CHUNK)
        vn_c = vn_ref[rows, :]
        mixed = sgb_ref[...]
        for hd in range(N_HEADS_D):
            vm = jnp.where(lane // HEAD_D == hd, vn_c, 0.0).astype(_BF16)
            mixed = mixed + jnp.dot(sgw_ref[hd], vm, preferred_element_type=_F32)
        mix_ref[rows, col(3)] = (z_ref[rows, col(6)] * mixed).astype(_BF16)

    x1_ref[...] = x + jnp.dot(mix_ref[...], wout_ref[...], preferred_element_type=_F32)

    if precise_tail:
        @pl.when(is_last)
        def _():
            mixf_ref[:, col(1)] = _dot_split(dpoolf_ref[...], pw_ref, pwl_ref) * ps_ref[...]
            vn_c = vn_ref[tail, :]
            mixed = sgb_ref[...]
            for hd in range(N_HEADS_D):
                vm_hi, vm_lo = _split_bf16(jnp.where(lane // HEAD_D == hd, vn_c, 0.0))
                mixed = (mixed + jnp.dot(sgw_ref[hd], vm_hi, preferred_element_type=_F32)
                         + jnp.dot(sgw_ref[hd], vm_lo, preferred_element_type=_F32)
                         + jnp.dot(sgwl_ref[hd], vm_hi, preferred_element_type=_F32))
            mixf_ref[:, col(3)] = z_ref[tail, col(6)] * mixed
            x1_ref[tail, :] = x_ref[tail, :] + _dot_split(mixf_ref[...], wout_ref, woutl_ref)

    _store_route(x1_ref[...], (gffn_ref, rwh_ref, rwl_ref, rb_ref, lstrict_ref, ustrict_ref),
                 (h2_ref, route_ref, routet_ref, npad_ref))

    @pl.when(is_last)
    def _():
        end = HALO + tt
        sa_ref[0] = exta_ref[end - (CONV_A - 1):end, :]
        sp_ref[0] = extp_ref[end - POOL_STATE:end, :]
        sc_ref[0] = extc_ref[end - (CONV_C - 1):end, :]

    exta_ref[0:HALO, :] = exta_ref[tt:tt + HALO, :]
    extp_ref[0:HALO, :] = extp_ref[tt:tt + HALO, :]
    extc_ref[0:HALO, :] = extc_ref[tt:tt + HALO, :]


def _route_out_shapes(n_tiles):
    n_tok = n_tiles * TOK_TILE
    return [jax.ShapeDtypeStruct((n_tok, D_MODEL), _BF16),
            jax.ShapeDtypeStruct((n_tok, LANES), _F32),
            jax.ShapeDtypeStruct((n_tiles, SUBLANES, TOK_TILE), _F32),
            jax.ShapeDtypeStruct((n_tiles, 1, LANES), _F32)]


def _route_out_specs(tile_of):
    return [pl.BlockSpec((TOK_TILE, D_MODEL), lambda *g: (tile_of(*g), 0)),
            pl.BlockSpec((TOK_TILE, LANES), lambda *g: (tile_of(*g), 0)),
            pl.BlockSpec((1, SUBLANES, TOK_TILE), lambda *g: (tile_of(*g), 0, 0)),
            pl.BlockSpec((1, 1, LANES), lambda *g: (tile_of(*g), 0, 0))]


def _mixer_consts(lw, sgu_w, sgu_b):
    return [lw["g_mix"], lw["w_in"], lw["conv_a_w"], lw["pool_w_bd"], lw["pool_scale"], lw["conv_c_w"],
            lw["conv_c_b"], lw["ln_c_g"], lw["ln_c_b"], lw["ln_d_g"], lw["ln_d_b"], lw[sgu_w], lw[sgu_b], lw["w_out"],
            lw["g_ffn"], lw["router_w_hi"], lw["router_w_lo"], lw["router_b"], lw["lstrict"], lw["ustrict"]]


def _prompt_mixer(x, bsz, lw, precise_tail):
    seq = x.shape[0] // bsz
    n_t = seq // TOK_TILE
    consts = _mixer_consts(lw, "sgu_w_tril", "sgu_bias_rows") + [lw["w_in_lo"], lw["pool_w_bd_lo"],
                                                                 lw["sgu_w_tril_lo"], lw["w_out_lo"]]
    tile_of = lambda b, t: b * n_t + t
    tile_spec = pl.BlockSpec((TOK_TILE, D_MODEL), lambda b, t: (tile_of(b, t), 0))

    def state_spec(rows):
        return pl.BlockSpec((1, rows, W_GROUP), lambda b, t: (b, 0, 0))

    return pl.pallas_call(
        functools.partial(_prompt_mixer_kernel, precise_tail),
        grid=(bsz, n_t),
        in_specs=[tile_spec] + [_const_spec(c.shape) for c in consts],
        out_specs=[tile_spec, state_spec(CONV_A - 1), state_spec(POOL_STATE), state_spec(CONV_C - 1)]
        + _route_out_specs(tile_of),
        out_shape=[jax.ShapeDtypeStruct((bsz * seq, D_MODEL), _F32),
                   jax.ShapeDtypeStruct((bsz, CONV_A - 1, W_GROUP), _F32),
                   jax.ShapeDtypeStruct((bsz, POOL_STATE, W_GROUP), _F32),
                   jax.ShapeDtypeStruct((bsz, CONV_C - 1, W_GROUP), _F32)] + _route_out_shapes(bsz * n_t),
        scratch_shapes=[pltpu.VMEM((TOK_TILE, IN_COLS), _F32),
                        pltpu.VMEM((HALO + TOK_TILE, W_GROUP), _F32),
                        pltpu.VMEM((HALO + TOK_TILE, W_GROUP), _F32),
                        pltpu.VMEM((HALO + TOK_TILE, W_GROUP), _F32),
                        pltpu.VMEM((TOK_TILE, W_GROUP), _BF16),
                        pltpu.VMEM((TOK_TILE, W_GROUP), _F32),
                        pltpu.VMEM((TOK_TILE, D_MODEL), _BF16),
                        pltpu.VMEM((CHUNK, D_MODEL), _F32),
                        pltpu.VMEM((CHUNK, W_GROUP), _F32)],
        compiler_params=pltpu.CompilerParams(dimension_semantics=("arbitrary", "arbitrary"),
                                             vmem_limit_bytes=VMEM_LIMIT),
        name="prompt_mixer",
    )(x, *consts)


def _sample_mixer_kernel(batch_major_in, x_ref, sta_ref, stp_ref, stc_ref, gmix_ref, win_ref, caw_ref, pw_ref, ps_ref,
                         ccw_ref, ccb_ref, lncg_ref, lncb_ref, lndg_ref, lndb_ref, sgw_ref, sgb_ref, wout_ref,
                         gffn_ref, rwh_ref, rwl_ref, rb_ref, lstrict_ref, ustrict_ref,
                         x1_ref, nsa_ref, nsp_ref, nsc_ref, vrow_ref, h2_ref, route_ref, routet_ref, npad_ref,
                         xt_ref, z_ref, exta_ref, extp_ref, extc_ref, dpool_ref, vn_ref, mix_ref):
    nb = sta_ref.shape[0]
    n_tok = x1_ref.shape[0]
    n_t = n_tok // nb

    def col(k):
        return slice(k * W_GROUP, (k + 1) * W_GROUP)

    def slab(j, n=1):
        return slice(j * nb, (j + n) * nb)

    if batch_major_in:
        for tstep in range(n_t):
            xt_ref[slab(tstep), :] = x_ref[:, tstep * D_MODEL:(tstep + 1) * D_MODEL]
    else:
        xt_ref[...] = x_ref[...]

    h = _rmsnorm(xt_ref[...], gmix_ref[...]).astype(_BF16)
    z_ref[...] = jnp.dot(h, win_ref[...], preferred_element_type=_F32)

    for j in range(CONV_A - 1):
        exta_ref[slab(j), :] = sta_ref[:, col(j)]
    for j in range(POOL_STATE):
        extp_ref[slab(j), :] = stp_ref[:, col(j)]
    for j in range(CONV_C - 1):
        extc_ref[slab(j), :] = stc_ref[:, col(j)]
    for tstep in range(n_t):
        rows = slab(tstep)
        exta_ref[slab(CONV_A - 1 + tstep), :] = z_ref[rows, col(1)] * z_ref[rows, col(2)]
        extp_ref[slab(POOL_STATE + tstep), :] = z_ref[rows, col(3)]
        extc_ref[slab(CONV_C - 1 + tstep), :] = z_ref[rows, col(4)] * jax.nn.sigmoid(z_ref[rows, col(5)])

    win = _pool_windows((nb, W_GROUP))
    for tstep in range(n_t):
        rows = slab(tstep)
        conv_a = None
        for k in range(CONV_A):
            term = caw_ref[k:k + 1, :] * exta_ref[slab(tstep + k), :]
            conv_a = term if conv_a is None else conv_a + term
        mix_ref[rows, col(0)] = (z_ref[rows, col(0)] * conv_a).astype(_BF16)

        p_cur = extp_ref[slab(POOL_STATE + tstep), :]
        acc = p_cur
        for j in range(1, POOL_STATE + 1):
            acc = acc + jnp.where(win > j, extp_ref[slab(POOL_STATE + tstep - j), :], 0.0)
        cnt = jnp.minimum(PAST_LEN + tstep + 1, win).astype(_F32)
        dpool_ref[rows, :] = (acc / cnt - p_cur).astype(_BF16)

        conv_c = None
        for k in range(CONV_C):
            term = ccw_ref[k:k + 1, :] * extc_ref[slab(tstep + k), :]
            conv_c = term if conv_c is None else conv_c + term
        y_c = _layernorm(conv_c + ccb_ref[...], lncg_ref[...], lncb_ref[...])
        mix_ref[rows, col(2)] = _silu(y_c).astype(_BF16)

        v_n = _layernorm(z_ref[rows, col(7)], lndg_ref[...], lndb_ref[...])
        vn_ref[rows, :] = v_n
        vrow_ref[:, col(tstep)] = v_n

    y_p = jnp.dot(dpool_ref[...], pw_ref[...], preferred_element_type=_F32) * ps_ref[...]
    mix_ref[:, col(1)] = y_p.astype(_BF16)

    for i in range(n_t):
        mixed = sgb_ref[i:i + 1, :] + sgw_ref[i * n_t:i * n_t + 1, :] * vn_ref[slab(0), :]
        for j in range(1, i + 1):
            mixed = mixed + sgw_ref[i * n_t + j:i * n_t + j + 1, :] * vn_ref[slab(j), :]
        mix_ref[slab(i), col(3)] = (z_ref[slab(i), col(6)] * mixed).astype(_BF16)

    x1 = xt_ref[...] + jnp.dot(mix_ref[...], wout_ref[...], preferred_element_type=_F32)
    x1_ref[...] = x1
    _store_route(x1, (gffn_ref, rwh_ref, rwl_ref, rb_ref, lstrict_ref, ustrict_ref),
                 (h2_ref, route_ref, routet_ref, npad_ref))

    for j in range(CONV_A - 1):
        nsa_ref[:, col(j)] = exta_ref[slab(n_t + j), :]
    for j in range(POOL_STATE):
        nsp_ref[:, col(j)] = extp_ref[slab(n_t + j), :]
    for j in range(CONV_C - 1):
        nsc_ref[:, col(j)] = extc_ref[slab(n_t + j), :]


def _sample_mixer(x, st_a, st_p, st_c, lw, n_t, batch_major_in):
    n_seq = st_a.shape[0]
    nb = SAMPLE_SEQ_BLK
    n_blk = n_seq // nb
    n_tok = nb * n_t
    assert n_tok == TOK_TILE
    consts = _mixer_consts(lw, "sgu_w_rows", "sgu_b_rows")

    def seq_spec(width):
        return pl.BlockSpec((nb, width), lambda i: (i, 0))

    tok_spec = pl.BlockSpec((n_tok, D_MODEL), lambda i: (i, 0))
    state_widths = [(CONV_A - 1) * W_GROUP, POOL_STATE * W_GROUP, (CONV_C - 1) * W_GROUP]
    x_spec = seq_spec(n_t * D_MODEL) if batch_major_in else tok_spec
    return pl.pallas_call(
        functools.partial(_sample_mixer_kernel, batch_major_in),
        grid=(n_blk,),
        in_specs=[x_spec] + [seq_spec(w) for w in state_widths] + [_const_spec(c.shape) for c in consts],
        out_specs=[tok_spec] + [seq_spec(w) for w in state_widths] + [seq_spec(n_t * W_GROUP)]
        + _route_out_specs(lambda i: i),
        out_shape=[jax.ShapeDtypeStruct((n_blk * n_tok, D_MODEL), _F32)]
        + [jax.ShapeDtypeStruct((n_seq, w), _F32) for w in state_widths]
        + [jax.ShapeDtypeStruct((n_seq, n_t * W_GROUP), _F32)] + _route_out_shapes(n_blk),
        scratch_shapes=[pltpu.VMEM((n_tok, D_MODEL), _F32),
                        pltpu.VMEM((n_tok, IN_COLS), _F32),
                        pltpu.VMEM(((CONV_A - 1 + n_t) * nb, W_GROUP), _F32),
                        pltpu.VMEM(((POOL_STATE + n_t) * nb, W_GROUP), _F32),
                        pltpu.VMEM(((CONV_C - 1 + n_t) * nb, W_GROUP), _F32),
                        pltpu.VMEM((n_tok, W_GROUP), _BF16),
                        pltpu.VMEM((n_tok, W_GROUP), _F32),
                        pltpu.VMEM((n_tok, D_MODEL), _BF16)],
        compiler_params=pltpu.CompilerParams(dimension_semantics=("arbitrary",), vmem_limit_bytes=VMEM_LIMIT),
        name="sample_mixer",
    )(x, st_a, st_p, st_c, *consts)


def _plan_kernel(np_ref, ustrict_ref, dest_ref, tab_ref, npx_ref, toff_ref, zc_ref):
    n_tiles = np_ref.shape[0]
    nt_pad = npx_ref.shape[0]
    zeros = jnp.zeros((nt_pad, LANES), _F32)
    npx_ref[...] = zeros
    toff_ref[...] = zeros
    zc_ref[...] = zeros
    npx_ref[0:n_tiles, :] = np_ref[...]
    np_all = npx_ref[...]

    tile_row = lax.broadcasted_iota(_I32, (nt_pad, 1), 0)
    n_real = jnp.sum(np_all, axis=-1, keepdims=True) * (1.0 / BF16_ROWS)
    n_zero = jnp.where(tile_row < n_tiles, N_CHUNK - n_real, 0.0)

    run = jnp.zeros((1, LANES), _F32)
    zrun = jnp.zeros((1, LANES), _F32)
    for i in range(n_tiles):
        toff_ref[i:i + 1, :] = run
        zc_ref[i:i + 1, :] = zrun
        run = run + npx_ref[i:i + 1, :]
        zrun = zrun + n_zero[i:i + 1, :]
    rows_e = run
    rows_pad = jnp.ceil(rows_e * (1.0 / GMM_TILE)) * GMM_TILE
    gap = (rows_pad - rows_e) * (1.0 / BF16_ROWS)

    def excl_lanes(v):
        return jnp.dot(v.astype(_BF16), ustrict_ref[...], preferred_element_type=_F32)

    gstart = excl_lanes(jnp.broadcast_to(rows_pad * (1.0 / GMM_TILE), (SUBLANES, LANES)))[0:1] * GMM_TILE
    gap_start = excl_lanes(jnp.broadcast_to(gap, (SUBLANES, LANES)))[0:1]
    gap_total = jnp.sum(gap, axis=-1, keepdims=True)
    rows_total = jnp.sum(rows_pad, axis=-1, keepdims=True)
    seg_start = excl_lanes(np_all * (1.0 / BF16_ROWS)) * BF16_ROWS
    delta = gstart + toff_ref[...] - seg_start

    chunk = lax.broadcasted_iota(_I32, (nt_pad, LANES), 1).astype(_F32)
    pos = chunk * BF16_ROWS
    q = zc_ref[...] + (chunk - n_real)
    real = pos
    gap_addr = q * BF16_ROWS
    for e in range(N_EXPERTS):
        ss = seg_start[:, e:e + 1]
        se = ss + np_all[:, e:e + 1]
        real = real + jnp.where(jnp.logical_and(ss <= pos, pos < se), delta[:, e:e + 1], 0.0)
        gs = gap_start[:, e:e + 1]
        ge = gs + gap[:, e:e + 1]
        base = gstart[:, e:e + 1] + rows_e[:, e:e + 1] - gs * BF16_ROWS
        gap_addr = gap_addr + jnp.where(jnp.logical_and(gs <= q, q < ge), base, 0.0)
    tail_addr = rows_total + (q - gap_total) * BF16_ROWS
    zero_addr = jnp.where(q < gap_total, gap_addr, tail_addr)
    dest = jnp.where(chunk < n_real, real, zero_addr)
    dest_ref[...] = dest[0:n_tiles, :].astype(_I32)

    n_cols = tab_ref.shape[1]
    row_pos = lax.broadcasted_iota(_I32, (SUBLANES, n_cols), 1).astype(_F32) * GMM_TILE
    t_exp = jnp.zeros((SUBLANES, n_cols), _F32)
    t_val = jnp.zeros((SUBLANES, n_cols), _F32)
    for e in range(N_EXPERTS):
        gs = gstart[:, e:e + 1]
        t_exp = t_exp + jnp.where(gs + rows_pad[:, e:e + 1] <= row_pos, 1.0, 0.0)
        t_val = t_val + jnp.where(jnp.logical_and(gs <= row_pos, row_pos < gs + rows_e[:, e:e + 1]), 1.0, 0.0)
    t_exp = jnp.minimum(t_exp, N_EXPERTS - 1.0)
    n_valid = jnp.sum(t_val, axis=-1, keepdims=True)
    n_groups = jnp.ceil(n_valid * (1.0 / GMM_GROUP))
    s_idx = jnp.minimum(row_pos * (1.0 / GMM_TILE), n_groups - 1.0)
    sub = lax.broadcasted_iota(_I32, (SUBLANES, n_cols), 0)
    tab_ref[...] = jnp.where(sub == 0, t_exp, jnp.where(sub == 1, s_idx, 0.0)).astype(_I32)


def _plan(npad_all, lw, n_gmm_tiles):
    n_tiles = npad_all.shape[0]
    nt_pad = -(-n_tiles // SUBLANES) * SUBLANES
    n_cols = -(-n_gmm_tiles // LANES) * LANES
    out_shape = [jax.ShapeDtypeStruct((n_tiles, LANES), _I32), jax.ShapeDtypeStruct((SUBLANES, n_cols), _I32)]
    return pl.pallas_call(
        _plan_kernel,
        grid=(1,),
        in_specs=[_const_spec(npad_all.shape), _const_spec(lw["ustrict"].shape)],
        out_specs=[_const_spec(s.shape) for s in out_shape],
        out_shape=out_shape,
        scratch_shapes=[pltpu.VMEM((nt_pad, LANES), _F32)] * 3,
        name="moe_plan",
    )(npad_all, lw["ustrict"])


def _sort_kernel(n_prompt_tiles, dest_ref, h2p_ref, h2s_ref, rtp_ref, rts_ref, xs_ref, buf_ref, sem_ref):
    i = pl.program_id(0)
    n = pl.num_programs(0)
    cur = lax.rem(i, 2)
    is_p = i < n_prompt_tiles
    h2 = jnp.where(is_p, h2p_ref[...], h2s_ref[...])
    rt = jnp.where(is_p, rtp_ref[0], rts_ref[0])
    s1 = rt[0:1, :]
    s2 = rt[1:2, :]

    def chunk_copy(tile, c, slot):
        dst = pl.multiple_of(dest_ref[tile * N_CHUNK + c], BF16_ROWS)
        return pltpu.make_async_copy(buf_ref.at[slot, pl.ds(c * BF16_ROWS, BF16_ROWS), :],
                                     xs_ref.at[pl.ds(dst, BF16_ROWS), :], sem_ref.at[slot])

    @pl.when(i < 2)
    def _():
        buf_ref[cur, SLOTS:SLOT_BUF, :] = jnp.zeros((SLOT_BUF - SLOTS, D_MODEL), _BF16)

    grp = 256
    for g in range(SLOTS // grp):
        slot_id = (g * grp + lax.broadcasted_iota(_I32, (grp, TOK_TILE), 0)).astype(_F32)
        perm = jnp.where(jnp.logical_or(slot_id == s1, slot_id == s2), 1.0, 0.0).astype(_BF16)
        buf_ref[cur, g * grp:(g + 1) * grp, :] = jnp.dot(perm, h2, preferred_element_type=_F32).astype(_BF16)

    @pl.when(i > 0)
    def _():
        for c in range(N_CHUNK):
            chunk_copy(i - 1, c, 1 - cur).wait()

    for c in range(N_CHUNK):
        chunk_copy(i, c, cur).start()

    @pl.when(i == n - 1)
    def _():
        for c in range(N_CHUNK):
            chunk_copy(i, c, cur).wait()


def _sort(dest_flat, h2p, h2s, rtp, rts):
    n_p = h2p.shape[0] // TOK_TILE
    n_s = h2s.shape[0] // TOK_TILE
    n_tiles = n_p + n_s
    p_idx = lambda i, d: jnp.minimum(i, n_p - 1)
    s_idx = lambda i, d: jnp.maximum(i - n_p, 0)
    return pl.pallas_call(
        functools.partial(_sort_kernel, n_p),
        grid_spec=pltpu.PrefetchScalarGridSpec(
            num_scalar_prefetch=1,
            grid=(n_tiles,),
            in_specs=[pl.BlockSpec((TOK_TILE, D_MODEL), lambda i, d: (p_idx(i, d), 0)),
                      pl.BlockSpec((TOK_TILE, D_MODEL), lambda i, d: (s_idx(i, d), 0)),
                      pl.BlockSpec((1, SUBLANES, TOK_TILE), lambda i, d: (p_idx(i, d), 0, 0)),
                      pl.BlockSpec((1, SUBLANES, TOK_TILE), lambda i, d: (s_idx(i, d), 0, 0))],
            out_specs=pl.BlockSpec(memory_space=pl.ANY),
            scratch_shapes=[pltpu.VMEM((2, SLOT_BUF, D_MODEL), _BF16), pltpu.SemaphoreType.DMA((2,))],
        ),
        out_shape=jax.ShapeDtypeStruct((n_tiles * SLOT_BUF, D_MODEL), _BF16),
        compiler_params=pltpu.CompilerParams(dimension_semantics=("arbitrary",), vmem_limit_bytes=VMEM_LIMIT),
        name="moe_sort",
    )(dest_flat, h2p, h2s, rtp, rts)


def _gmm_kernel(texp_ref, sidx_ref, xs_ref, *refs):
    w_refs, ys_ref = refs[:-1], refs[-1]
    s = pl.program_id(0)

    @pl.when(sidx_ref[s] == s)
    def _():
        tiles = [slice(j * GMM_TILE, (j + 1) * GMM_TILE) for j in range(GMM_GROUP)]
        gate_up = []
        for j, rows in enumerate(tiles):
            wg_ref, wu_ref, _ = w_refs[3 * j:3 * j + 3]
            w_gu = jnp.concatenate([wg_ref[0].astype(_BF16), wu_ref[0].astype(_BF16)], axis=1)
            gate_up.append(jnp.dot(xs_ref[rows, :], w_gu, preferred_element_type=_F32))
        for j, rows in enumerate(tiles):
            gu = gate_up[j]
            act = (_silu(gu[:, :D_FF_EXPERT]) * gu[:, D_FF_EXPERT:]).astype(_BF16)
            ys_ref[rows, :] = jnp.dot(act, w_refs[3 * j + 2][0].astype(_BF16), preferred_element_type=_F32).astype(_BF16)


def _gmm(t_exp, s_idx, xs, w_gate, w_up, w_down):
    n_steps = xs.shape[0] // (GMM_GROUP * GMM_TILE)
    group_spec = pl.BlockSpec((GMM_GROUP * GMM_TILE, D_MODEL), lambda s, te, si: (si[s], 0))
    w_specs, w_args = [], []
    for j in range(GMM_GROUP):
        expert = lambda s, te, si, j=j: (te[GMM_GROUP * s + j], 0, 0)
        w_specs += [pl.BlockSpec((1, D_MODEL, D_FF_EXPERT), expert), pl.BlockSpec((1, D_MODEL, D_FF_EXPERT), expert),
                    pl.BlockSpec((1, D_FF_EXPERT, D_MODEL), expert)]
        w_args += [w_gate, w_up, w_down]
    return pl.pallas_call(
        _gmm_kernel,
        grid_spec=pltpu.PrefetchScalarGridSpec(
            num_scalar_prefetch=2,
            grid=(n_steps,),
            in_specs=[group_spec] + w_specs,
            out_specs=group_spec,
        ),
        out_shape=jax.ShapeDtypeStruct(xs.shape, _BF16),
        input_output_aliases={2: 0},
        compiler_params=pltpu.CompilerParams(dimension_semantics=("arbitrary",), vmem_limit_bytes=VMEM_LIMIT),
        name="moe_experts",
    )(t_exp, s_idx, xs, *w_args)


def _combine_kernel(n_prompt_tiles, final_norm, dest_ref, x1p_ref, x1s_ref, rp_ref, rs_ref, gfin_ref, ys_ref,
                    outp_ref, outs_ref, ybuf_ref, sem_ref):
    i = pl.program_id(0)
    n = pl.num_programs(0)
    cur = lax.rem(i, 2)
    is_p = i < n_prompt_tiles

    def chunk_copy(tile, c, slot):
        src = pl.multiple_of(dest_ref[tile * N_CHUNK + c], BF16_ROWS)
        return pltpu.make_async_copy(ys_ref.at[pl.ds(src, BF16_ROWS), :],
                                     ybuf_ref.at[slot, pl.ds(c * BF16_ROWS, BF16_ROWS), :], sem_ref.at[slot])

    @pl.when(i == 0)
    def _():
        for c in range(N_CHUNK_REAL):
            chunk_copy(0, c, 0).start()

    @pl.when(i + 1 < n)
    def _():
        for c in range(N_CHUNK_REAL):
            chunk_copy(i + 1, c, 1 - cur).start()

    for c in range(N_CHUNK_REAL):
        chunk_copy(i, c, cur).wait()

    route = jnp.where(is_p, rp_ref[...], rs_ref[...])
    acc = jnp.where(is_p, x1p_ref[...], x1s_ref[...])
    s1 = route[:, 0:1]
    s2 = route[:, 1:2]
    w1 = route[:, 2:3]
    w2 = route[:, 3:4]
    grp = 256
    for g in range(SLOTS // grp):
        slot_id = (g * grp + lax.broadcasted_iota(_I32, (TOK_TILE, grp), 1)).astype(_F32)
        unperm = (jnp.where(slot_id == s1, w1, 0.0) + jnp.where(slot_id == s2, w2, 0.0)).astype(_BF16)
        acc = acc + jnp.dot(unperm, ybuf_ref[cur, g * grp:(g + 1) * grp, :], preferred_element_type=_F32)
    if final_norm:
        acc = _rmsnorm(acc, gfin_ref[...])

    @pl.when(is_p)
    def _():
        outp_ref[...] = acc

    @pl.when(jnp.logical_not(is_p))
    def _():
        outs_ref[...] = acc


def _combine(dest_flat, x1p, x1s, rp, rs, g_fin, ys, final_norm):
    n_p = x1p.shape[0] // TOK_TILE
    n_s = x1s.shape[0] // TOK_TILE
    p_idx = lambda i, d: (jnp.minimum(i, n_p - 1), 0)
    s_idx = lambda i, d: (jnp.maximum(i - n_p, 0), 0)
    return pl.pallas_call(
        functools.partial(_combine_kernel, n_p, final_norm),
        grid_spec=pltpu.PrefetchScalarGridSpec(
            num_scalar_prefetch=1,
            grid=(n_p + n_s,),
            in_specs=[pl.BlockSpec((TOK_TILE, D_MODEL), p_idx), pl.BlockSpec((TOK_TILE, D_MODEL), s_idx),
                      pl.BlockSpec((TOK_TILE, LANES), p_idx), pl.BlockSpec((TOK_TILE, LANES), s_idx),
                      pl.BlockSpec(g_fin.shape, lambda i, d: (0, 0)),
                      pl.BlockSpec(memory_space=pl.ANY)],
            out_specs=[pl.BlockSpec((TOK_TILE, D_MODEL), p_idx), pl.BlockSpec((TOK_TILE, D_MODEL), s_idx)],
            scratch_shapes=[pltpu.VMEM((2, SLOTS, D_MODEL), _BF16), pltpu.SemaphoreType.DMA((2,))],
        ),
        out_shape=[jax.ShapeDtypeStruct(x1p.shape, _F32), jax.ShapeDtypeStruct(x1s.shape, _F32)],
        compiler_params=pltpu.CompilerParams(dimension_semantics=("arbitrary",), vmem_limit_bytes=VMEM_LIMIT),
        name="moe_combine",
    )(dest_flat, x1p, x1s, rp, rs, g_fin, ys)


def _moe(x1p, x1s, routing_p, routing_s, lw, g_fin, final_norm):
    h2p, rp, rtp, npp = routing_p
    h2s, rs, rts, nps = routing_s
    n_tiles = npp.shape[0] + nps.shape[0]
    n_gmm = n_tiles * SLOT_BUF // GMM_TILE
    assert n_tiles * (SLOT_BUF - SLOTS) >= N_EXPERTS * (GMM_TILE - BF16_ROWS)
    npad_all = jnp.concatenate([npp, nps], axis=0).reshape(n_tiles, LANES)
    dest, tab = _plan(npad_all, lw, n_gmm)
    dest_flat = dest.reshape(-1)
    xs = _sort(dest_flat, h2p, h2s, rtp, rts)
    ys = _gmm(tab[0], tab[1], xs, lw["w_gate"], lw["w_up"], lw["w_down"])
    return _combine(dest_flat, x1p, x1s, rp, rs, g_fin, ys, final_norm)


def _layer_weights(l, g_mix, w_in, conv_a_w, pool_w, pool_scale, conv_c_w, conv_c_b, ln_c_g, ln_c_b, ln_d_g, ln_d_b,
                   sgu_w, sgu_b, w_out, g_ffn, router_group_w, router_group_b, router_expert_w, router_expert_b,
                   expert_w_gate, expert_w_up, expert_w_down, n_t_sample, precise_tail):
    row = lambda v: v[l].reshape(1, -1)
    pool_bd = jnp.zeros((W_GROUP, W_GROUP), _F32)
    for g in range(len(POOL_WINDOWS)):
        sl = slice(g * POOL_CH, (g + 1) * POOL_CH)
        pool_bd = pool_bd.at[sl, sl].set(pool_w[l, g])
    tril = jnp.tril(jnp.ones((CHUNK, CHUNK), dtype=bool))
    sgu_tril = jnp.where(tril, sgu_w[l], 0.0)
    w_small = sgu_tril[:, :n_t_sample, :n_t_sample]
    sgu_w_rows = jnp.repeat(jnp.transpose(w_small, (1, 2, 0)).reshape(n_t_sample * n_t_sample, N_HEADS_D), HEAD_D, axis=1)
    n_route = N_EXPERTS + N_EXPERT_GROUPS
    router_w = jnp.pad(jnp.concatenate([router_expert_w[l], router_group_w[l]], axis=1), ((0, 0), (0, LANES - n_route)))
    router_b = jnp.pad(jnp.concatenate([router_expert_b[l], router_group_b[l]]), (0, LANES - n_route)).reshape(1, LANES)
    router_w_hi, router_w_lo = _weight_split(router_w)
    if precise_tail:
        w_in_hi, w_in_lo = _weight_split(w_in[l])
        w_out_hi, w_out_lo = _weight_split(w_out[l])
        pool_hi, pool_lo = _weight_split(pool_bd)
        sgu_hi, sgu_lo = (s.reshape(sgu_tril.shape) for s in _weight_split(sgu_tril.reshape(-1, CHUNK)))
    else:
        w_in_hi = w_in_lo = w_in[l].astype(_BF16)
        w_out_hi = w_out_lo = w_out[l].astype(_BF16)
        pool_hi = pool_lo = pool_bd.astype(_BF16)
        sgu_hi = sgu_lo = sgu_tril.astype(_BF16)
    return {
        "w_in_lo": w_in_lo, "pool_w_bd_lo": pool_lo, "sgu_w_tril_lo": sgu_lo, "w_out_lo": w_out_lo,
        "g_mix": row(g_mix), "w_in": w_in_hi, "conv_a_w": conv_a_w[l], "pool_w_bd": pool_hi,
        "pool_scale": row(pool_scale), "conv_c_w": conv_c_w[l], "conv_c_b": row(conv_c_b), "ln_c_g": row(ln_c_g),
        "ln_c_b": row(ln_c_b), "ln_d_g": row(ln_d_g), "ln_d_b": row(ln_d_b),
        "sgu_w_tril": sgu_hi,
        "sgu_bias_rows": jnp.repeat(sgu_b[l].T, HEAD_D, axis=1),
        "sgu_w_rows": sgu_w_rows,
        "sgu_b_rows": jnp.repeat(sgu_b[l][:, :n_t_sample].T, HEAD_D, axis=1),
        "w_out": w_out_hi, "g_ffn": row(g_ffn), "router_w_hi": router_w_hi, "router_w_lo": router_w_lo,
        "router_b": router_b,
        "w_gate": expert_w_gate[l], "w_up": expert_w_up[l], "w_down": expert_w_down[l],
        "lstrict": jnp.tril(jnp.ones((TOK_TILE, TOK_TILE), _F32), -1).astype(_BF16),
        "ustrict": jnp.triu(jnp.ones((LANES, LANES), _F32), 1).astype(_BF16),
    }


def kernel(x_prompt, x_sample, state_conv_a, state_pool, state_conv_c, g_mix, w_in, conv_a_w, pool_w, pool_scale, conv_c_w, conv_c_b, ln_c_g, ln_c_b, ln_d_g, ln_d_b, sgu_w, sgu_b, w_out, g_ffn, router_group_w, router_group_b, router_expert_w, router_expert_b, expert_w_gate, expert_w_up, expert_w_down, g_final):
    depth = g_mix.shape[0]
    bsz, seq, _ = x_prompt.shape
    nb, n_t, _ = x_sample.shape
    g_fin = g_final.reshape(1, -1)

    xp = x_prompt.reshape(bsz * seq, D_MODEL)
    xs = x_sample.reshape(nb, n_t * D_MODEL)
    outs = {k: [] for k in ("sa_p", "sp_p", "sc_p", "sa_s", "sp_s", "sc_s", "v")}
    for l in range(depth):
        precise_tail = l + 1 < depth
        lw = _layer_weights(l, g_mix, w_in, conv_a_w, pool_w, pool_scale, conv_c_w, conv_c_b, ln_c_g, ln_c_b, ln_d_g,
                            ln_d_b, sgu_w, sgu_b, w_out, g_ffn, router_group_w, router_group_b, router_expert_w,
                            router_expert_b, expert_w_gate, expert_w_up, expert_w_down, n_t, precise_tail)
        x1p, sa, sp, sc, *routing_p = _prompt_mixer(xp, bsz, lw, precise_tail)
        outs["sa_p"].append(sa)
        outs["sp_p"].append(sp)
        outs["sc_p"].append(sc)
        x1s, nsa, nsp, nsc, vrow, *routing_s = _sample_mixer(
            xs, state_conv_a[l].reshape(nb, -1), state_pool[l].reshape(nb, -1), state_conv_c[l].reshape(nb, -1),
            lw, n_t, batch_major_in=(l == 0))
        outs["sa_s"].append(nsa.reshape(nb, CONV_A - 1, W_GROUP))
        outs["sp_s"].append(nsp.reshape(nb, POOL_STATE, W_GROUP))
        outs["sc_s"].append(nsc.reshape(nb, CONV_C - 1, W_GROUP))
        outs["v"].append(vrow.reshape(nb, n_t, W_GROUP))
        xp, xs = _moe(x1p, x1s, routing_p, routing_s, lw, g_fin, final_norm=(l == depth - 1))

    y_prompt = xp.reshape(bsz, seq, D_MODEL)
    y_sample = jnp.transpose(xs.reshape(nb // SAMPLE_SEQ_BLK, n_t, SAMPLE_SEQ_BLK, D_MODEL),
                             (0, 2, 1, 3)).reshape(nb, n_t, D_MODEL)
    return (y_prompt, y_sample, jnp.stack(outs["sa_p"]), jnp.stack(outs["sp_p"]), jnp.stack(outs["sc_p"]),
            jnp.stack(outs["sa_s"]), jnp.stack(outs["sp_s"]), jnp.stack(outs["sc_s"]), jnp.stack(outs["v"]))
```

```python
import functools

import jax
import jax.numpy as jnp
from jax import lax
from jax.experimental import pallas as pl
from jax.experimental.pallas import tpu as pltpu

D_MODEL = 1024
W_GROUP = 256
IN_COLS = 8 * W_GROUP
CONV_A = 3
POOL_WINDOWS = (2, 4, 8, 16)
POOL_CH = W_GROUP // len(POOL_WINDOWS)
POOL_STATE = max(POOL_WINDOWS) - 1
CONV_C = 31
CHUNK = 128
N_HEADS_D = 4
HEAD_D = W_GROUP // N_HEADS_D
N_EXPERT_GROUPS = 4
EXPERTS_PER_GROUP = 8
N_EXPERTS = N_EXPERT_GROUPS * EXPERTS_PER_GROUP
TOP_K = 2
D_FF_EXPERT = 128
EPS = 1e-6
PAST_LEN = 16384

LANES = 128
SUBLANES = 8
BF16_ROWS = 16
HALO = 32
ROW_BLK = 64
TOK_TILE = 512
SAMPLE_SEQ_BLK = 64
GMM_TILE = 512
GMM_GROUP = 1
GMM_ROW_SPLIT = 4
SLOTS = -(-(TOP_K * TOK_TILE + N_EXPERTS * (BF16_ROWS - 1)) // 256) * 256
N_CHUNK_REAL = SLOTS // BF16_ROWS
N_CHUNK = 128
SLOT_BUF = N_CHUNK * BF16_ROWS
SLOT_GRP = 256
VMEM_LIMIT = 56 * 1024 * 1024

_F32 = jnp.float32
_BF16 = jnp.bfloat16
_I32 = jnp.int32
_HI = lax.Precision.HIGHEST


def _rmsnorm(x, g):
    return x * lax.rsqrt(jnp.mean(x * x, axis=-1, keepdims=True) + EPS) * g


def _layernorm(x, g, b):
    mu = jnp.mean(x, axis=-1, keepdims=True)
    xc = x - mu
    var = jnp.mean(xc * xc, axis=-1, keepdims=True)
    return xc * lax.rsqrt(var + EPS) * g + b


def _silu(x):
    return x * jax.nn.sigmoid(x)


def _split_bf16(a):
    bits = lax.bitcast_convert_type(a, jnp.uint32)
    rounded = bits + jnp.uint32(0x7FFF) + ((bits >> 16) & jnp.uint32(1))
    hi = lax.bitcast_convert_type(rounded & jnp.uint32(0xFFFF0000), _F32)
    return hi.astype(_BF16), (a - hi).astype(_BF16)


def _dot_split(a, wh_ref, wl_ref):
    a_hi, a_lo = _split_bf16(a)
    return (jnp.dot(a_hi, wh_ref[...], preferred_element_type=_F32)
            + jnp.dot(a_lo, wh_ref[...], preferred_element_type=_F32)
            + jnp.dot(a_hi, wl_ref[...], preferred_element_type=_F32))


def _weight_split_kernel(w_ref, hi_ref, lo_ref):
    w = w_ref[...]
    hi = w.astype(_BF16)
    hi_ref[...] = hi
    lo_ref[...] = (w - hi.astype(_F32)).astype(_BF16)


def _weight_split(w):
    rows, cols = w.shape
    blk = min(rows, 256)
    spec = pl.BlockSpec((blk, cols), lambda i: (i, 0))
    return pl.pallas_call(
        _weight_split_kernel,
        grid=(rows // blk,),
        in_specs=[spec],
        out_specs=[spec, spec],
        out_shape=[jax.ShapeDtypeStruct(w.shape, _BF16)] * 2,
        name="weight_split",
    )(w)


def _rows_back(x, r):
    return pltpu.roll(x, r, axis=0)


def _rows_ahead(x, r):
    return x if r == 0 else pltpu.roll(x, x.shape[0] - r, axis=0)


def _pool_windows(shape):
    lane = lax.broadcasted_iota(_I32, shape, 1)
    return jnp.left_shift(2, lane // POOL_CH)


def _const_spec(shape):
    nd = len(shape)
    return pl.BlockSpec(shape, lambda *_: (0,) * nd)


def _route_tile(x1, gffn_ref, rwh_ref, rwl_ref, rb_ref, lstrict_ref, ustrict_ref):
    h2 = _rmsnorm(x1, gffn_ref[...])
    h_hi = h2.astype(_BF16)
    logits = _dot_split(h2, rwh_ref, rwl_ref) + rb_ref[...]
    lane = lax.broadcasted_iota(_I32, logits.shape, 1)
    lane_f = lane.astype(_F32)
    neg = jnp.float32(-jnp.inf)
    big = jnp.float32(LANES)

    is_group = jnp.logical_and(lane >= N_EXPERTS, lane < N_EXPERTS + N_EXPERT_GROUPS)
    lg = jnp.where(is_group, logits, neg)
    g_max = jnp.max(lg, axis=-1, keepdims=True)
    g_idx = jnp.min(jnp.where(lg == g_max, lane_f, big), axis=-1, keepdims=True) - N_EXPERTS
    p_top = 1.0 / jnp.sum(jnp.exp(lg - g_max), axis=-1, keepdims=True)

    in_group = (lane // EXPERTS_PER_GROUP).astype(_F32) == g_idx
    le = jnp.where(jnp.logical_and(in_group, lane < N_EXPERTS), logits, neg)
    m1 = jnp.max(le, axis=-1, keepdims=True)
    i1 = jnp.min(jnp.where(le == m1, lane_f, big), axis=-1, keepdims=True)
    le2 = jnp.where(lane_f == i1, neg, le)
    m2 = jnp.max(le2, axis=-1, keepdims=True)
    i2 = jnp.min(jnp.where(le2 == m2, lane_f, big), axis=-1, keepdims=True)
    e2 = jnp.exp(m2 - m1)
    w1 = p_top / (1.0 + e2)
    w2 = p_top * e2 / (1.0 + e2)

    o1 = jnp.where(lane_f == i1, 1.0, 0.0)
    o2 = jnp.where(lane_f == i2, 1.0, 0.0)
    before1 = jnp.dot(lstrict_ref[...], o1.astype(_BF16), preferred_element_type=_F32)
    before2 = jnp.dot(lstrict_ref[...], o2.astype(_BF16), preferred_element_type=_F32)
    n1 = jnp.sum(o1, axis=0, keepdims=True)
    n2 = jnp.sum(o2, axis=0, keepdims=True)
    n_tiles16 = jnp.floor((n1 + n2 + (BF16_ROWS - 1)) * (1.0 / BF16_ROWS))
    npad = n_tiles16 * BF16_ROWS
    seg_start = jnp.dot(jnp.broadcast_to(n_tiles16, (SUBLANES, LANES)).astype(_BF16), ustrict_ref[...],
                        preferred_element_type=_F32)[0:1] * BF16_ROWS
    s1 = jnp.sum(o1 * (seg_start + before1), axis=-1, keepdims=True)
    s2 = jnp.sum(o2 * (seg_start + n1 + before2), axis=-1, keepdims=True)
    route = jnp.where(lane == 0, s1, jnp.where(lane == 1, s2, jnp.where(lane == 2, w1, jnp.where(lane == 3, w2, 0.0))))
    return h_hi, route, npad


def _store_route(x1, route_refs, out_refs):
    h2_ref, route_ref, routet_ref, npad_ref = out_refs
    h2, route, npad = _route_tile(x1, *route_refs)
    h2_ref[...] = h2
    route_ref[...] = route
    routet_ref[0] = jnp.transpose(route)[0:SUBLANES, :]
    npad_ref[0] = npad


def _prompt_mixer_kernel(precise_tail, x_ref, gmix_ref, win_ref, caw_ref, pw_ref, ps_ref, ccw_ref, ccb_ref, lncg_ref,
                         lncb_ref, lndg_ref, lndb_ref, sgw_ref, sgb_ref, wout_ref, gffn_ref, rwh_ref, rwl_ref, rb_ref,
                         lstrict_ref, ustrict_ref, winl_ref, pwl_ref, sgwl_ref, woutl_ref,
                         x1_ref, sa_ref, sp_ref, sc_ref, h2_ref, route_ref, routet_ref, npad_ref,
                         z_ref, exta_ref, extp_ref, extc_ref, dpool_ref, vn_ref, mix_ref, mixf_ref, dpoolf_ref):
    t = pl.program_id(1)
    n_t = pl.num_programs(1)
    tt = x_ref.shape[0]
    tail = slice(tt - CHUNK, tt)
    is_last = t == n_t - 1

    @pl.when(t == 0)
    def _():
        zeros = jnp.zeros((HALO, W_GROUP), _F32)
        exta_ref[0:HALO, :] = zeros
        extp_ref[0:HALO, :] = zeros
        extc_ref[0:HALO, :] = zeros

    x = x_ref[...]
    h = _rmsnorm(x, gmix_ref[...])
    z_ref[...] = jnp.dot(h.astype(_BF16), win_ref[...], preferred_element_type=_F32)
    if precise_tail:
        @pl.when(is_last)
        def _():
            z_ref[tail, :] = _dot_split(h[tt - CHUNK:, :], win_ref, winl_ref)

    def col(k):
        return slice(k * W_GROUP, (k + 1) * W_GROUP)

    for rb in range(tt // ROW_BLK):
        rows = slice(rb * ROW_BLK, (rb + 1) * ROW_BLK)
        ext_rows = slice(HALO + rb * ROW_BLK, HALO + (rb + 1) * ROW_BLK)
        exta_ref[ext_rows, :] = z_ref[rows, col(1)] * z_ref[rows, col(2)]
        extp_ref[ext_rows, :] = z_ref[rows, col(3)]
        extc_ref[ext_rows, :] = z_ref[rows, col(4)] * jax.nn.sigmoid(z_ref[rows, col(5)])

    win = _pool_windows((ROW_BLK, W_GROUP))
    row_iota = lax.broadcasted_iota(_I32, (ROW_BLK, W_GROUP), 0)
    low_group = lax.broadcasted_iota(_I32, (ROW_BLK + 2 * SUBLANES, LANES), 1) < POOL_CH

    for rb in range(tt // ROW_BLK):
        r0 = rb * ROW_BLK
        rows = slice(r0, r0 + ROW_BLK)

        ua = exta_ref[HALO + r0 - SUBLANES:HALO + r0 + ROW_BLK, :]
        conv_a = caw_ref[CONV_A - 1:CONV_A, :] * ua
        for k in range(CONV_A - 1):
            conv_a = conv_a + caw_ref[k:k + 1, :] * _rows_back(ua, CONV_A - 1 - k)
        y_a = z_ref[rows, col(0)] * conv_a[SUBLANES:, :]
        mix_ref[rows, col(0)] = y_a.astype(_BF16)

        pe = extp_ref[HALO + r0 - 2 * SUBLANES:HALO + r0 + ROW_BLK, :]
        s2 = pe + _rows_back(pe, 1)
        s4 = s2 + _rows_back(s2, 2)
        s4_hi = s4[:, LANES:]
        s8 = s4_hi + _rows_back(s4_hi, 4)
        s16 = s8 + _rows_back(s8, 8)
        sums = jnp.concatenate([jnp.where(low_group, s2[:, :LANES], s4[:, :LANES]), jnp.where(low_group, s8, s16)],
                               axis=1)[2 * SUBLANES:, :]
        pos = t * tt + r0 + row_iota
        cnt = jnp.minimum(pos + 1, win).astype(_F32)
        d_pool = sums / cnt - pe[2 * SUBLANES:, :]
        dpool_ref[rows, :] = d_pool.astype(_BF16)

        halves = []
        for hc in range(W_GROUP // LANES):
            lanes = slice(hc * LANES, (hc + 1) * LANES)
            xe = extc_ref[HALO + r0 - HALO:HALO + r0 + ROW_BLK, lanes]
            conv_c = None
            for r in range(SUBLANES):
                xr = _rows_ahead(xe, r)
                for a in range(HALO // SUBLANES + 1):
                    k = SUBLANES * a + r - (HALO - (CONV_C - 1))
                    if 0 <= k < CONV_C:
                        term = ccw_ref[k:k + 1, lanes] * xr[SUBLANES * a:SUBLANES * a + ROW_BLK, :]
                        conv_c = term if conv_c is None else conv_c + term
            halves.append(conv_c)
        y_c = _layernorm(jnp.concatenate(halves, axis=1) + ccb_ref[...], lncg_ref[...], lncb_ref[...])
        y_c = _silu(y_c)
        mix_ref[rows, col(2)] = y_c.astype(_BF16)

        vn_ref[rows, :] = _layernorm(z_ref[rows, col(7)], lndg_ref[...], lndb_ref[...])

        if precise_tail and r0 >= tt - CHUNK:
            tail_rows = slice(r0 - (tt - CHUNK), r0 - (tt - CHUNK) + ROW_BLK)
            mixf_ref[tail_rows, col(0)] = y_a
            mixf_ref[tail_rows, col(2)] = y_c
            dpoolf_ref[tail_rows, :] = d_pool

    y_p = jnp.dot(dpool_ref[...], pw_ref[...], preferred_element_type=_F32) * ps_ref[...]
    mix_ref[:, col(1)] = y_p.astype(_BF16)

    lane = lax.broadcasted_iota(_I32, (CHUNK, W_GROUP), 1)
    for c in range(tt // CHUNK):
        rows = slice(c * CHUNK, (c + 1) * CHUNK)
        vn_c = vn_ref[rows, :]
        mixed = sgb_ref[...]
        for hd in range(N_HEADS_D):
            vm = jnp.where(lane // HEAD_D == hd, vn_c, 0.0).astype(_BF16)
            mixed = mixed + jnp.dot(sgw_ref[hd], vm, preferred_element_type=_F32)
        mix_ref[rows, col(3)] = (z_ref[rows, col(6)] * mixed).astype(_BF16)

    x1_ref[...] = x + jnp.dot(mix_ref[...], wout_ref[...], preferred_element_type=_F32)

    if precise_tail:
        @pl.when(is_last)
        def _():
            mixf_ref[:, col(1)] = _dot_split(dpoolf_ref[...], pw_ref, pwl_ref) * ps_ref[...]
            vn_c = vn_ref[tail, :]
            mixed = sgb_ref[...]
            for hd in range(N_HEADS_D):
                vm_hi, vm_lo = _split_bf16(jnp.where(lane // HEAD_D == hd, vn_c, 0.0))
                mixed = (mixed + jnp.dot(sgw_ref[hd], vm_hi, preferred_element_type=_F32)
                         + jnp.dot(sgw_ref[hd], vm_lo, preferred_element_type=_F32)
                         + jnp.dot(sgwl_ref[hd], vm_hi, preferred_element_type=_F32))
            mixf_ref[:, col(3)] = z_ref[tail, col(6)] * mixed
            x1_ref[tail, :] = x_ref[tail, :] + _dot_split(mixf_ref[...], wout_ref, woutl_ref)

    _store_route(x1_ref[...], (gffn_ref, rwh_ref, rwl_ref, rb_ref, lstrict_ref, ustrict_ref),
                 (h2_ref, route_ref, routet_ref, npad_ref))

    @pl.when(is_last)
    def _():
        end = HALO + tt
        sa_ref[0] = exta_ref[end - (CONV_A - 1):end, :]
        sp_ref[0] = extp_ref[end - POOL_STATE:end, :]
        sc_ref[0] = extc_ref[end - (CONV_C - 1):end, :]

    exta_ref[0:HALO, :] = exta_ref[tt:tt + HALO, :]
    extp_ref[0:HALO, :] = extp_ref[tt:tt + HALO, :]
    extc_ref[0:HALO, :] = extc_ref[tt:tt + HALO, :]


def _route_out_shapes(n_tiles):
    n_tok = n_tiles * TOK_TILE
    return [jax.ShapeDtypeStruct((n_tok, D_MODEL), _BF16),
            jax.ShapeDtypeStruct((n_tok, LANES), _F32),
            jax.ShapeDtypeStruct((n_tiles, SUBLANES, TOK_TILE), _F32),
            jax.ShapeDtypeStruct((n_tiles, 1, LANES), _F32)]


def _route_out_specs(tile_of):
    return [pl.BlockSpec((TOK_TILE, D_MODEL), lambda *g: (tile_of(*g), 0)),
            pl.BlockSpec((TOK_TILE, LANES), lambda *g: (tile_of(*g), 0)),
            pl.BlockSpec((1, SUBLANES, TOK_TILE), lambda *g: (tile_of(*g), 0, 0)),
            pl.BlockSpec((1, 1, LANES), lambda *g: (tile_of(*g), 0, 0))]


def _mixer_consts(lw, sgu_w, sgu_b):
    return [lw["g_mix"], lw["w_in"], lw["conv_a_w"], lw["pool_w_bd"], lw["pool_scale"], lw["conv_c_w"],
            lw["conv_c_b"], lw["ln_c_g"], lw["ln_c_b"], lw["ln_d_g"], lw["ln_d_b"], lw[sgu_w], lw[sgu_b], lw["w_out"],
            lw["g_ffn"], lw["router_w_hi"], lw["router_w_lo"], lw["router_b"], lw["lstrict"], lw["ustrict"]]


def _prompt_mixer(x, bsz, lw, precise_tail):
    seq = x.shape[0] // bsz
    n_t = seq // TOK_TILE
    consts = _mixer_consts(lw, "sgu_w_tril", "sgu_bias_rows") + [lw["w_in_lo"], lw["pool_w_bd_lo"],
                                                                 lw["sgu_w_tril_lo"], lw["w_out_lo"]]
    tile_of = lambda b, t: b * n_t + t
    tile_spec = pl.BlockSpec((TOK_TILE, D_MODEL), lambda b, t: (tile_of(b, t), 0))

    def state_spec(rows):
        return pl.BlockSpec((1, rows, W_GROUP), lambda b, t: (b, 0, 0))

    return pl.pallas_call(
        functools.partial(_prompt_mixer_kernel, precise_tail),
        grid=(bsz, n_t),
        in_specs=[tile_spec] + [_const_spec(c.shape) for c in consts],
        out_specs=[tile_spec, state_spec(CONV_A - 1), state_spec(POOL_STATE), state_spec(CONV_C - 1)]
        + _route_out_specs(tile_of),
        out_shape=[jax.ShapeDtypeStruct((bsz * seq, D_MODEL), _F32),
                   jax.ShapeDtypeStruct((bsz, CONV_A - 1, W_GROUP), _F32),
                   jax.ShapeDtypeStruct((bsz, POOL_STATE, W_GROUP), _F32),
                   jax.ShapeDtypeStruct((bsz, CONV_C - 1, W_GROUP), _F32)] + _route_out_shapes(bsz * n_t),
        scratch_shapes=[pltpu.VMEM((TOK_TILE, IN_COLS), _F32),
                        pltpu.VMEM((HALO + TOK_TILE, W_GROUP), _F32),
                        pltpu.VMEM((HALO + TOK_TILE, W_GROUP), _F32),
                        pltpu.VMEM((HALO + TOK_TILE, W_GROUP), _F32),
                        pltpu.VMEM((TOK_TILE, W_GROUP), _BF16),
                        pltpu.VMEM((TOK_TILE, W_GROUP), _F32),
                        pltpu.VMEM((TOK_TILE, D_MODEL), _BF16),
                        pltpu.VMEM((CHUNK, D_MODEL), _F32),
                        pltpu.VMEM((CHUNK, W_GROUP), _F32)],
        compiler_params=pltpu.CompilerParams(dimension_semantics=("arbitrary", "arbitrary"),
                                             vmem_limit_bytes=VMEM_LIMIT),
        name="prompt_mixer",
    )(x, *consts)


def _sample_mixer_kernel(batch_major_in, x_ref, sta_ref, stp_ref, stc_ref, gmix_ref, win_ref, caw_ref, pw_ref, ps_ref,
                         ccw_ref, ccb_ref, lncg_ref, lncb_ref, lndg_ref, lndb_ref, sgw_ref, sgb_ref, wout_ref,
                         gffn_ref, rwh_ref, rwl_ref, rb_ref, lstrict_ref, ustrict_ref,
                         x1_ref, nsa_ref, nsp_ref, nsc_ref, vrow_ref, h2_ref, route_ref, routet_ref, npad_ref,
                         xt_ref, z_ref, exta_ref, extp_ref, extc_ref, dpool_ref, vn_ref, mix_ref):
    nb = sta_ref.shape[0]
    n_tok = x1_ref.shape[0]
    n_t = n_tok // nb

    def col(k):
        return slice(k * W_GROUP, (k + 1) * W_GROUP)

    def slab(j, n=1):
        return slice(j * nb, (j + n) * nb)

    if batch_major_in:
        for tstep in range(n_t):
            xt_ref[slab(tstep), :] = x_ref[:, tstep * D_MODEL:(tstep + 1) * D_MODEL]
    else:
        xt_ref[...] = x_ref[...]

    h = _rmsnorm(xt_ref[...], gmix_ref[...]).astype(_BF16)
    z_ref[...] = jnp.dot(h, win_ref[...], preferred_element_type=_F32)

    for j in range(CONV_A - 1):
        exta_ref[slab(j), :] = sta_ref[:, col(j)]
    for j in range(POOL_STATE):
        extp_ref[slab(j), :] = stp_ref[:, col(j)]
    for j in range(CONV_C - 1):
        extc_ref[slab(j), :] = stc_ref[:, col(j)]
    for tstep in range(n_t):
        rows = slab(tstep)
        exta_ref[slab(CONV_A - 1 + tstep), :] = z_ref[rows, col(1)] * z_ref[rows, col(2)]
        extp_ref[slab(POOL_STATE + tstep), :] = z_ref[rows, col(3)]
        extc_ref[slab(CONV_C - 1 + tstep), :] = z_ref[rows, col(4)] * jax.nn.sigmoid(z_ref[rows, col(5)])

    win = _pool_windows((nb, W_GROUP))
    for tstep in range(n_t):
        rows = slab(tstep)
        conv_a = None
        for k in range(CONV_A):
            term = caw_ref[k:k + 1, :] * exta_ref[slab(tstep + k), :]
            conv_a = term if conv_a is None else conv_a + term
        mix_ref[rows, col(0)] = (z_ref[rows, col(0)] * conv_a).astype(_BF16)

        p_cur = extp_ref[slab(POOL_STATE + tstep), :]
        acc = p_cur
        for j in range(1, POOL_STATE + 1):
            acc = acc + jnp.where(win > j, extp_ref[slab(POOL_STATE + tstep - j), :], 0.0)
        cnt = jnp.minimum(PAST_LEN + tstep + 1, win).astype(_F32)
        dpool_ref[rows, :] = (acc / cnt - p_cur).astype(_BF16)

        conv_c = None
        for k in range(CONV_C):
            term = ccw_ref[k:k + 1, :] * extc_ref[slab(tstep + k), :]
            conv_c = term if conv_c is None else conv_c + term
        y_c = _layernorm(conv_c + ccb_ref[...], lncg_ref[...], lncb_ref[...])
        mix_ref[rows, col(2)] = _silu(y_c).astype(_BF16)

        v_n = _layernorm(z_ref[rows, col(7)], lndg_ref[...], lndb_ref[...])
        vn_ref[rows, :] = v_n
        vrow_ref[:, col(tstep)] = v_n

    y_p = jnp.dot(dpool_ref[...], pw_ref[...], preferred_element_type=_F32) * ps_ref[...]
    mix_ref[:, col(1)] = y_p.astype(_BF16)

    for i in range(n_t):
        mixed = sgb_ref[i:i + 1, :] + sgw_ref[i * n_t:i * n_t + 1, :] * vn_ref[slab(0), :]
        for j in range(1, i + 1):
            mixed = mixed + sgw_ref[i * n_t + j:i * n_t + j + 1, :] * vn_ref[slab(j), :]
        mix_ref[slab(i), col(3)] = (z_ref[slab(i), col(6)] * mixed).astype(_BF16)

    x1 = xt_ref[...] + jnp.dot(mix_ref[...], wout_ref[...], preferred_element_type=_F32)
    x1_ref[...] = x1
    _store_route(x1, (gffn_ref, rwh_ref, rwl_ref, rb_ref, lstrict_ref, ustrict_ref),
                 (h2_ref, route_ref, routet_ref, npad_ref))

    for j in range(CONV_A - 1):
        nsa_ref[:, col(j)] = exta_ref[slab(n_t + j), :]
    for j in range(POOL_STATE):
        nsp_ref[:, col(j)] = extp_ref[slab(n_t + j), :]
    for j in range(CONV_C - 1):
        nsc_ref[:, col(j)] = extc_ref[slab(n_t + j), :]


def _sample_mixer(x, st_a, st_p, st_c, lw, n_t, batch_major_in):
    n_seq = st_a.shape[0]
    nb = SAMPLE_SEQ_BLK
    n_blk = n_seq // nb
    n_tok = nb * n_t
    assert n_tok == TOK_TILE
    consts = _mixer_consts(lw, "sgu_w_rows", "sgu_b_rows")

    def seq_spec(width):
        return pl.BlockSpec((nb, width), lambda i: (i, 0))

    tok_spec = pl.BlockSpec((n_tok, D_MODEL), lambda i: (i, 0))
    state_widths = [(CONV_A - 1) * W_GROUP, POOL_STATE * W_GROUP, (CONV_C - 1) * W_GROUP]
    x_spec = seq_spec(n_t * D_MODEL) if batch_major_in else tok_spec
    return pl.pallas_call(
        functools.partial(_sample_mixer_kernel, batch_major_in),
        grid=(n_blk,),
        in_specs=[x_spec] + [seq_spec(w) for w in state_widths] + [_const_spec(c.shape) for c in consts],
        out_specs=[tok_spec] + [seq_spec(w) for w in state_widths] + [seq_spec(n_t * W_GROUP)]
        + _route_out_specs(lambda i: i),
        out_shape=[jax.ShapeDtypeStruct((n_blk * n_tok, D_MODEL), _F32)]
        + [jax.ShapeDtypeStruct((n_seq, w), _F32) for w in state_widths]
        + [jax.ShapeDtypeStruct((n_seq, n_t * W_GROUP), _F32)] + _route_out_shapes(n_blk),
        scratch_shapes=[pltpu.VMEM((n_tok, D_MODEL), _F32),
                        pltpu.VMEM((n_tok, IN_COLS), _F32),
                        pltpu.VMEM(((CONV_A - 1 + n_t) * nb, W_GROUP), _F32),
                        pltpu.VMEM(((POOL_STATE + n_t) * nb, W_GROUP), _F32),
                        pltpu.VMEM(((CONV_C - 1 + n_t) * nb, W_GROUP), _F32),
                        pltpu.VMEM((n_tok, W_GROUP), _BF16),
                        pltpu.VMEM((n_tok, W_GROUP), _F32),
                        pltpu.VMEM((n_tok, D_MODEL), _BF16)],
        compiler_params=pltpu.CompilerParams(dimension_semantics=("arbitrary",), vmem_limit_bytes=VMEM_LIMIT),
        name="sample_mixer",
    )(x, st_a, st_p, st_c, *consts)


def _plan_kernel(np_ref, ustrict_ref, dest_ref, tab_ref, used_ref, npx_ref, toff_ref, zc_ref):
    n_tiles = np_ref.shape[0]
    nt_pad = npx_ref.shape[0]
    zeros = jnp.zeros((nt_pad, LANES), _F32)
    npx_ref[...] = zeros
    toff_ref[...] = zeros
    zc_ref[...] = zeros
    npx_ref[0:n_tiles, :] = np_ref[...]
    np_all = npx_ref[...]

    tile_row = lax.broadcasted_iota(_I32, (nt_pad, 1), 0)
    n_real = jnp.sum(np_all, axis=-1, keepdims=True) * (1.0 / BF16_ROWS)
    n_zero = jnp.where(tile_row < n_tiles, N_CHUNK - n_real, 0.0)

    run = jnp.zeros((1, LANES), _F32)
    zrun = jnp.zeros((1, LANES), _F32)
    for i in range(n_tiles):
        toff_ref[i:i + 1, :] = run
        zc_ref[i:i + 1, :] = zrun
        run = run + npx_ref[i:i + 1, :]
        zrun = zrun + n_zero[i:i + 1, :]
    rows_e = run
    rows_pad = jnp.ceil(rows_e * (1.0 / GMM_TILE)) * GMM_TILE
    gap = (rows_pad - rows_e) * (1.0 / BF16_ROWS)

    def excl_lanes(v):
        return jnp.dot(v.astype(_BF16), ustrict_ref[...], preferred_element_type=_F32)

    gstart = excl_lanes(jnp.broadcast_to(rows_pad * (1.0 / GMM_TILE), (SUBLANES, LANES)))[0:1] * GMM_TILE
    gap_start = excl_lanes(jnp.broadcast_to(gap, (SUBLANES, LANES)))[0:1]
    gap_total = jnp.sum(gap, axis=-1, keepdims=True)
    rows_total = jnp.sum(rows_pad, axis=-1, keepdims=True)
    seg_start = excl_lanes(np_all * (1.0 / BF16_ROWS)) * BF16_ROWS
    delta = gstart + toff_ref[...] - seg_start

    chunk = lax.broadcasted_iota(_I32, (nt_pad, LANES), 1).astype(_F32)
    pos = chunk * BF16_ROWS
    q = zc_ref[...] + (chunk - n_real)
    real = pos
    gap_addr = q * BF16_ROWS
    for e in range(N_EXPERTS):
        ss = seg_start[:, e:e + 1]
        se = ss + np_all[:, e:e + 1]
        real = real + jnp.where(jnp.logical_and(ss <= pos, pos < se), delta[:, e:e + 1], 0.0)
        gs = gap_start[:, e:e + 1]
        ge = gs + gap[:, e:e + 1]
        base = gstart[:, e:e + 1] + rows_e[:, e:e + 1] - gs * BF16_ROWS
        gap_addr = gap_addr + jnp.where(jnp.logical_and(gs <= q, q < ge), base, 0.0)
    tail_addr = rows_total + (q - gap_total) * BF16_ROWS
    zero_addr = jnp.where(q < gap_total, gap_addr, tail_addr)
    dest = jnp.where(chunk < n_real, real, zero_addr)
    dest_ref[...] = dest[0:n_tiles, :].astype(_I32)
    used_ref[...] = jnp.broadcast_to(n_real * BF16_ROWS, (nt_pad, LANES))[0:n_tiles, :].astype(_I32)

    n_cols = tab_ref.shape[1]
    row_pos = lax.broadcasted_iota(_I32, (SUBLANES, n_cols), 1).astype(_F32) * GMM_TILE
    t_exp = jnp.zeros((SUBLANES, n_cols), _F32)
    t_val = jnp.zeros((SUBLANES, n_cols), _F32)
    for e in range(N_EXPERTS):
        gs = gstart[:, e:e + 1]
        t_exp = t_exp + jnp.where(gs + rows_pad[:, e:e + 1] <= row_pos, 1.0, 0.0)
        t_val = t_val + jnp.where(jnp.logical_and(gs <= row_pos, row_pos < gs + rows_e[:, e:e + 1]), 1.0, 0.0)
    t_exp = jnp.minimum(t_exp, N_EXPERTS - 1.0)
    n_valid = jnp.sum(t_val, axis=-1, keepdims=True)
    n_groups = jnp.ceil(n_valid * (1.0 / GMM_GROUP))
    s_idx = jnp.minimum(row_pos * (1.0 / GMM_TILE), n_groups - 1.0)
    sub = lax.broadcasted_iota(_I32, (SUBLANES, n_cols), 0)
    tab_ref[...] = jnp.where(sub == 0, t_exp, jnp.where(sub == 1, s_idx, 0.0)).astype(_I32)


def _plan(npad_all, lw, n_gmm_tiles):
    n_tiles = npad_all.shape[0]
    nt_pad = -(-n_tiles // SUBLANES) * SUBLANES
    n_cols = -(-n_gmm_tiles // LANES) * LANES
    out_shape = [jax.ShapeDtypeStruct((n_tiles, LANES), _I32), jax.ShapeDtypeStruct((SUBLANES, n_cols), _I32),
                 jax.ShapeDtypeStruct((n_tiles, LANES), _I32)]
    return pl.pallas_call(
        _plan_kernel,
        grid=(1,),
        in_specs=[_const_spec(npad_all.shape), _const_spec(lw["ustrict"].shape)],
        out_specs=[_const_spec(s.shape) for s in out_shape],
        out_shape=out_shape,
        scratch_shapes=[pltpu.VMEM((nt_pad, LANES), _F32)] * 3,
        name="moe_plan",
    )(npad_all, lw["ustrict"])


def _sort_kernel(n_prompt_tiles, dest_ref, used_ref, h2p_ref, h2s_ref, rtp_ref, rts_ref, xs_ref, buf_ref, sem_ref):
    i = pl.program_id(0)
    n = pl.num_programs(0)
    cur = lax.rem(i, 2)
    is_p = i < n_prompt_tiles
    h2 = jnp.where(is_p, h2p_ref[...], h2s_ref[...])
    rt = jnp.where(is_p, rtp_ref[0], rts_ref[0])
    s1 = rt[0:1, :]
    s2 = rt[1:2, :]

    def chunk_copy(tile, c, slot):
        dst = pl.multiple_of(dest_ref[tile * N_CHUNK + c], BF16_ROWS)
        return pltpu.make_async_copy(buf_ref.at[slot, pl.ds(c * BF16_ROWS, BF16_ROWS), :],
                                     xs_ref.at[pl.ds(dst, BF16_ROWS), :], sem_ref.at[slot])

    @pl.when(i < 2)
    def _():
        buf_ref[cur, SLOTS:SLOT_BUF, :] = jnp.zeros((SLOT_BUF - SLOTS, D_MODEL), _BF16)

    def sort_group(g):
        slot_id = (g * SLOT_GRP + lax.broadcasted_iota(_I32, (SLOT_GRP, TOK_TILE), 0)).astype(_F32)
        perm = jnp.where(jnp.logical_or(slot_id == s1, slot_id == s2), 1.0, 0.0).astype(_BF16)
        buf_ref[cur, g * SLOT_GRP:(g + 1) * SLOT_GRP, :] = jnp.dot(perm, h2, preferred_element_type=_F32).astype(_BF16)

    n_grp = SLOTS // SLOT_GRP
    for g in range(n_grp - 1):
        sort_group(g)
    last_used = used_ref[i * LANES] > (n_grp - 1) * SLOT_GRP

    @pl.when(last_used)
    def _():
        sort_group(n_grp - 1)

    @pl.when(jnp.logical_not(last_used))
    def _():
        buf_ref[cur, (n_grp - 1) * SLOT_GRP:SLOTS, :] = jnp.zeros((SLOT_GRP, D_MODEL), _BF16)

    @pl.when(i > 0)
    def _():
        for c in range(N_CHUNK):
            chunk_copy(i - 1, c, 1 - cur).wait()

    for c in range(N_CHUNK):
        chunk_copy(i, c, cur).start()

    @pl.when(i == n - 1)
    def _():
        for c in range(N_CHUNK):
            chunk_copy(i, c, cur).wait()


def _sort(dest_flat, used_flat, h2p, h2s, rtp, rts):
    n_p = h2p.shape[0] // TOK_TILE
    n_s = h2s.shape[0] // TOK_TILE
    n_tiles = n_p + n_s
    p_idx = lambda i: jnp.minimum(i, n_p - 1)
    s_idx = lambda i: jnp.maximum(i - n_p, 0)
    return pl.pallas_call(
        functools.partial(_sort_kernel, n_p),
        grid_spec=pltpu.PrefetchScalarGridSpec(
            num_scalar_prefetch=2,
            grid=(n_tiles,),
            in_specs=[pl.BlockSpec((TOK_TILE, D_MODEL), lambda i, d, u: (p_idx(i), 0)),
                      pl.BlockSpec((TOK_TILE, D_MODEL), lambda i, d, u: (s_idx(i), 0)),
                      pl.BlockSpec((1, SUBLANES, TOK_TILE), lambda i, d, u: (p_idx(i), 0, 0)),
                      pl.BlockSpec((1, SUBLANES, TOK_TILE), lambda i, d, u: (s_idx(i), 0, 0))],
            out_specs=pl.BlockSpec(memory_space=pl.ANY),
            scratch_shapes=[pltpu.VMEM((2, SLOT_BUF, D_MODEL), _BF16), pltpu.SemaphoreType.DMA((2,))],
        ),
        out_shape=jax.ShapeDtypeStruct((n_tiles * SLOT_BUF, D_MODEL), _BF16),
        compiler_params=pltpu.CompilerParams(dimension_semantics=("arbitrary",), vmem_limit_bytes=VMEM_LIMIT),
        name="moe_sort",
    )(dest_flat, used_flat, h2p, h2s, rtp, rts)


def _gmm_kernel(texp_ref, sidx_ref, xs_ref, *refs):
    w_refs, ys_ref = refs[:-1], refs[-1]
    s = pl.program_id(0)

    @pl.when(sidx_ref[s] == s)
    def _():
        blk = GMM_TILE // GMM_ROW_SPLIT
        chains = [(j, slice(j * GMM_TILE + k * blk, j * GMM_TILE + (k + 1) * blk))
                  for j in range(GMM_GROUP) for k in range(GMM_ROW_SPLIT)]
        w_gu = [jnp.concatenate([w_refs[3 * j][0, 0].astype(_BF16), w_refs[3 * j + 1][0, 0].astype(_BF16)], axis=1)
                for j in range(GMM_GROUP)]
        w_d = [w_refs[3 * j + 2][0, 0].astype(_BF16) for j in range(GMM_GROUP)]
        gate_up = [jnp.dot(xs_ref[rows, :], w_gu[j], preferred_element_type=_F32) for j, rows in chains]
        for (j, rows), gu in zip(chains, gate_up):
            act = (_silu(gu[:, :D_FF_EXPERT]) * gu[:, D_FF_EXPERT:]).astype(_BF16)
            ys_ref[rows, :] = jnp.dot(act, w_d[j], preferred_element_type=_F32).astype(_BF16)


def _gmm(t_exp, s_idx, xs, layer, w_gate, w_up, w_down):
    n_steps = xs.shape[0] // (GMM_GROUP * GMM_TILE)
    group_spec = pl.BlockSpec((GMM_GROUP * GMM_TILE, D_MODEL), lambda s, te, si: (si[s], 0))
    w_specs, w_args = [], []
    for j in range(GMM_GROUP):
        expert = lambda s, te, si, j=j: (layer, te[GMM_GROUP * s + j], 0, 0)
        w_specs += [pl.BlockSpec((1, 1, D_MODEL, D_FF_EXPERT), expert),
                    pl.BlockSpec((1, 1, D_MODEL, D_FF_EXPERT), expert),
                    pl.BlockSpec((1, 1, D_FF_EXPERT, D_MODEL), expert)]
        w_args += [w_gate, w_up, w_down]
    return pl.pallas_call(
        _gmm_kernel,
        grid_spec=pltpu.PrefetchScalarGridSpec(
            num_scalar_prefetch=2,
            grid=(n_steps,),
            in_specs=[group_spec] + w_specs,
            out_specs=group_spec,
        ),
        out_shape=jax.ShapeDtypeStruct(xs.shape, _BF16),
        input_output_aliases={2: 0},
        compiler_params=pltpu.CompilerParams(dimension_semantics=("arbitrary",), vmem_limit_bytes=VMEM_LIMIT),
        name="moe_experts",
    )(t_exp, s_idx, xs, *w_args)


def _combine_kernel(n_prompt_tiles, final_norm, dest_ref, used_ref, x1p_ref, x1s_ref, rp_ref, rs_ref, gfin_ref, ys_ref,
                    outp_ref, outs_ref, ybuf_ref, sem_ref):
    i = pl.program_id(0)
    n = pl.num_programs(0)
    cur = lax.rem(i, 2)
    is_p = i < n_prompt_tiles

    def chunk_copy(tile, c, slot):
        src = pl.multiple_of(dest_ref[tile * N_CHUNK + c], BF16_ROWS)
        return pltpu.make_async_copy(ys_ref.at[pl.ds(src, BF16_ROWS), :],
                                     ybuf_ref.at[slot, pl.ds(c * BF16_ROWS, BF16_ROWS), :], sem_ref.at[slot])

    @pl.when(i == 0)
    def _():
        for c in range(N_CHUNK_REAL):
            chunk_copy(0, c, 0).start()

    @pl.when(i + 1 < n)
    def _():
        for c in range(N_CHUNK_REAL):
            chunk_copy(i + 1, c, 1 - cur).start()

    for c in range(N_CHUNK_REAL):
        chunk_copy(i, c, cur).wait()

    route = jnp.where(is_p, rp_ref[...], rs_ref[...])
    acc = jnp.where(is_p, x1p_ref[...], x1s_ref[...])
    s1 = route[:, 0:1]
    s2 = route[:, 1:2]
    w1 = route[:, 2:3]
    w2 = route[:, 3:4]
    def unperm_dot(g):
        slot_id = (g * SLOT_GRP + lax.broadcasted_iota(_I32, (TOK_TILE, SLOT_GRP), 1)).astype(_F32)
        unperm = (jnp.where(slot_id == s1, w1, 0.0) + jnp.where(slot_id == s2, w2, 0.0)).astype(_BF16)
        return jnp.dot(unperm, ybuf_ref[cur, g * SLOT_GRP:(g + 1) * SLOT_GRP, :], preferred_element_type=_F32)

    def finish(y):
        if final_norm:
            y = _rmsnorm(y, gfin_ref[...])

        @pl.when(is_p)
        def _():
            outp_ref[...] = y

        @pl.when(jnp.logical_not(is_p))
        def _():
            outs_ref[...] = y

    n_grp = SLOTS // SLOT_GRP
    for g in range(n_grp - 1):
        acc = acc + unperm_dot(g)
    last_used = used_ref[i * LANES] > (n_grp - 1) * SLOT_GRP

    @pl.when(last_used)
    def _():
        finish(acc + unperm_dot(n_grp - 1))

    @pl.when(jnp.logical_not(last_used))
    def _():
        finish(acc)


def _combine(dest_flat, used_flat, x1p, x1s, rp, rs, g_fin, ys, final_norm):
    n_p = x1p.shape[0] // TOK_TILE
    n_s = x1s.shape[0] // TOK_TILE
    p_idx = lambda i, d, u: (jnp.minimum(i, n_p - 1), 0)
    s_idx = lambda i, d, u: (jnp.maximum(i - n_p, 0), 0)
    return pl.pallas_call(
        functools.partial(_combine_kernel, n_p, final_norm),
        grid_spec=pltpu.PrefetchScalarGridSpec(
            num_scalar_prefetch=2,
            grid=(n_p + n_s,),
            in_specs=[pl.BlockSpec((TOK_TILE, D_MODEL), p_idx), pl.BlockSpec((TOK_TILE, D_MODEL), s_idx),
                      pl.BlockSpec((TOK_TILE, LANES), p_idx), pl.BlockSpec((TOK_TILE, LANES), s_idx),
                      pl.BlockSpec(g_fin.shape, lambda i, d, u: (0, 0)),
                      pl.BlockSpec(memory_space=pl.ANY)],
            out_specs=[pl.BlockSpec((TOK_TILE, D_MODEL), p_idx), pl.BlockSpec((TOK_TILE, D_MODEL), s_idx)],
            scratch_shapes=[pltpu.VMEM((2, SLOTS, D_MODEL), _BF16), pltpu.SemaphoreType.DMA((2,))],
        ),
        out_shape=[jax.ShapeDtypeStruct(x1p.shape, _F32), jax.ShapeDtypeStruct(x1s.shape, _F32)],
        compiler_params=pltpu.CompilerParams(dimension_semantics=("arbitrary",), vmem_limit_bytes=VMEM_LIMIT),
        name="moe_combine",
    )(dest_flat, used_flat, x1p, x1s, rp, rs, g_fin, ys)


def _moe(x1p, x1s, routing_p, routing_s, lw, g_fin, final_norm):
    h2p, rp, rtp, npp = routing_p
    h2s, rs, rts, nps = routing_s
    n_tiles = npp.shape[0] + nps.shape[0]
    n_gmm = n_tiles * SLOT_BUF // GMM_TILE
    assert n_tiles * (SLOT_BUF - SLOTS) >= N_EXPERTS * (GMM_TILE - BF16_ROWS)
    npad_all = jnp.concatenate([npp, nps], axis=0).reshape(n_tiles, LANES)
    dest, tab, used = _plan(npad_all, lw, n_gmm)
    dest_flat = dest.reshape(-1)
    used_flat = used.reshape(-1)
    xs = _sort(dest_flat, used_flat, h2p, h2s, rtp, rts)
    ys = _gmm(tab[0], tab[1], xs, lw["layer"], *lw["expert_w"])
    return _combine(dest_flat, used_flat, x1p, x1s, rp, rs, g_fin, ys, final_norm)


def _layer_weights(l, g_mix, w_in, conv_a_w, pool_w, pool_scale, conv_c_w, conv_c_b, ln_c_g, ln_c_b, ln_d_g, ln_d_b,
                   sgu_w, sgu_b, w_out, g_ffn, router_group_w, router_group_b, router_expert_w, router_expert_b,
                   expert_w_gate, expert_w_up, expert_w_down, n_t_sample, precise_tail):
    row = lambda v: v[l].reshape(1, -1)
    pool_bd = jnp.zeros((W_GROUP, W_GROUP), _F32)
    for g in range(len(POOL_WINDOWS)):
        sl = slice(g * POOL_CH, (g + 1) * POOL_CH)
        pool_bd = pool_bd.at[sl, sl].set(pool_w[l, g])
    tril = jnp.tril(jnp.ones((CHUNK, CHUNK), dtype=bool))
    sgu_tril = jnp.where(tril, sgu_w[l], 0.0)
    w_small = sgu_tril[:, :n_t_sample, :n_t_sample]
    sgu_w_rows = jnp.repeat(jnp.transpose(w_small, (1, 2, 0)).reshape(n_t_sample * n_t_sample, N_HEADS_D), HEAD_D, axis=1)
    n_route = N_EXPERTS + N_EXPERT_GROUPS
    router_w = jnp.pad(jnp.concatenate([router_expert_w[l], router_group_w[l]], axis=1), ((0, 0), (0, LANES - n_route)))
    router_b = jnp.pad(jnp.concatenate([router_expert_b[l], router_group_b[l]]), (0, LANES - n_route)).reshape(1, LANES)
    router_w_hi, router_w_lo = _weight_split(router_w)
    if precise_tail:
        w_in_hi, w_in_lo = _weight_split(w_in[l])
        w_out_hi, w_out_lo = _weight_split(w_out[l])
        pool_hi, pool_lo = _weight_split(pool_bd)
        sgu_hi, sgu_lo = (s.reshape(sgu_tril.shape) for s in _weight_split(sgu_tril.reshape(-1, CHUNK)))
    else:
        w_in_hi = w_in_lo = w_in[l].astype(_BF16)
        w_out_hi = w_out_lo = w_out[l].astype(_BF16)
        pool_hi = pool_lo = pool_bd.astype(_BF16)
        sgu_hi = sgu_lo = sgu_tril.astype(_BF16)
    return {
        "w_in_lo": w_in_lo, "pool_w_bd_lo": pool_lo, "sgu_w_tril_lo": sgu_lo, "w_out_lo": w_out_lo,
        "g_mix": row(g_mix), "w_in": w_in_hi, "conv_a_w": conv_a_w[l], "pool_w_bd": pool_hi,
        "pool_scale": row(pool_scale), "conv_c_w": conv_c_w[l], "conv_c_b": row(conv_c_b), "ln_c_g": row(ln_c_g),
        "ln_c_b": row(ln_c_b), "ln_d_g": row(ln_d_g), "ln_d_b": row(ln_d_b),
        "sgu_w_tril": sgu_hi,
        "sgu_bias_rows": jnp.repeat(sgu_b[l].T, HEAD_D, axis=1),
        "sgu_w_rows": sgu_w_rows,
        "sgu_b_rows": jnp.repeat(sgu_b[l][:, :n_t_sample].T, HEAD_D, axis=1),
        "w_out": w_out_hi, "g_ffn": row(g_ffn), "router_w_hi": router_w_hi, "router_w_lo": router_w_lo,
        "router_b": router_b,
        "layer": l, "expert_w": (expert_w_gate, expert_w_up, expert_w_down),
        "lstrict": jnp.tril(jnp.ones((TOK_TILE, TOK_TILE), _F32), -1).astype(_BF16),
        "ustrict": jnp.triu(jnp.ones((LANES, LANES), _F32), 1).astype(_BF16),
    }


def kernel(x_prompt, x_sample, state_conv_a, state_pool, state_conv_c, g_mix, w_in, conv_a_w, pool_w, pool_scale, conv_c_w, conv_c_b, ln_c_g, ln_c_b, ln_d_g, ln_d_b, sgu_w, sgu_b, w_out, g_ffn, router_group_w, router_group_b, router_expert_w, router_expert_b, expert_w_gate, expert_w_up, expert_w_down, g_final):
    depth = g_mix.shape[0]
    bsz, seq, _ = x_prompt.shape
    nb, n_t, _ = x_sample.shape
    g_fin = g_final.reshape(1, -1)

    xp = x_prompt.reshape(bsz * seq, D_MODEL)
    xs = x_sample.reshape(nb, n_t * D_MODEL)
    outs = {k: [] for k in ("sa_p", "sp_p", "sc_p", "sa_s", "sp_s", "sc_s", "v")}
    for l in range(depth):
        precise_tail = l + 1 < depth
        lw = _layer_weights(l, g_mix, w_in, conv_a_w, pool_w, pool_scale, conv_c_w, conv_c_b, ln_c_g, ln_c_b, ln_d_g,
                            ln_d_b, sgu_w, sgu_b, w_out, g_ffn, router_group_w, router_group_b, router_expert_w,
                            router_expert_b, expert_w_gate, expert_w_up, expert_w_down, n_t, precise_tail)
        x1p, sa, sp, sc, *routing_p = _prompt_mixer(xp, bsz, lw, precise_tail)
        outs["sa_p"].append(sa)
        outs["sp_p"].append(sp)
        outs["sc_p"].append(sc)
        x1s, nsa, nsp, nsc, vrow, *routing_s = _sample_mixer(
            xs, state_conv_a[l].reshape(nb, -1), state_pool[l].reshape(nb, -1), state_conv_c[l].reshape(nb, -1),
            lw, n_t, batch_major_in=(l == 0))
        outs["sa_s"].append(nsa.reshape(nb, CONV_A - 1, W_GROUP))
        outs["sp_s"].append(nsp.reshape(nb, POOL_STATE, W_GROUP))
        outs["sc_s"].append(nsc.reshape(nb, CONV_C - 1, W_GROUP))
        outs["v"].append(vrow.reshape(nb, n_t, W_GROUP))
        xp, xs = _moe(x1p, x1s, routing_p, routing_s, lw, g_fin, final_norm=(l == depth - 1))

    y_prompt = xp.reshape(bsz, seq, D_MODEL)
    y_sample = jnp.transpose(xs.reshape(nb // SAMPLE_SEQ_BLK, n_t, SAMPLE_SEQ_BLK, D_MODEL),
                             (0, 2, 1, 3)).reshape(nb, n_t, D_MODEL)
    return (y_prompt, y_sample, jnp.stack(outs["sa_p"]), jnp.stack(outs["sp_p"]), jnp.stack(outs["sc_p"]),
            jnp.stack(outs["sa_s"]), jnp.stack(outs["sp_s"]), jnp.stack(outs["sc_s"]), jnp.stack(outs["v"]))
```

```python
import functools

import jax
import jax.numpy as jnp
from jax import lax
from jax.experimental import pallas as pl
from jax.experimental.pallas import tpu as pltpu

D_MODEL = 1024
W_GROUP = 256
IN_COLS = 8 * W_GROUP
CONV_A = 3
POOL_WINDOWS = (2, 4, 8, 16)
POOL_CH = W_GROUP // len(POOL_WINDOWS)
POOL_STATE = max(POOL_WINDOWS) - 1
CONV_C = 31
CHUNK = 128
N_HEADS_D = 4
HEAD_D = W_GROUP // N_HEADS_D
N_EXPERT_GROUPS = 4
EXPERTS_PER_GROUP = 8
N_EXPERTS = N_EXPERT_GROUPS * EXPERTS_PER_GROUP
TOP_K = 2
D_FF_EXPERT = 128
EPS = 1e-6
PAST_LEN = 16384

LANES = 128
SUBLANES = 8
BF16_ROWS = 16
HALO = 32
ROW_BLK = 64
MIX_STRIPE = 256
TOK_TILE = 512
SAMPLE_SEQ_BLK = 64
GMM_TILE = 512
GMM_GROUP = 1
GMM_ROW_SPLIT = 4
SLOTS = -(-(TOP_K * TOK_TILE + N_EXPERTS * (BF16_ROWS - 1)) // 256) * 256
N_CHUNK_REAL = SLOTS // BF16_ROWS
N_CHUNK = 128
SLOT_BUF = N_CHUNK * BF16_ROWS
SLOT_GRP = 256
VMEM_LIMIT = 56 * 1024 * 1024

_F32 = jnp.float32
_BF16 = jnp.bfloat16
_I32 = jnp.int32
_HI = lax.Precision.HIGHEST


def _rmsnorm(x, g):
    return x * lax.rsqrt(jnp.mean(x * x, axis=-1, keepdims=True) + EPS) * g


def _layernorm(x, g, b):
    mu = jnp.mean(x, axis=-1, keepdims=True)
    xc = x - mu
    var = jnp.mean(xc * xc, axis=-1, keepdims=True)
    return xc * lax.rsqrt(var + EPS) * g + b


def _silu(x):
    return x * jax.nn.sigmoid(x)


def _split_bf16(a):
    bits = lax.bitcast_convert_type(a, jnp.uint32)
    hi = lax.bitcast_convert_type(bits & jnp.uint32(0xFFFF0000), _F32)
    return hi.astype(_BF16), (a - hi).astype(_BF16)


def _dot_split(a, wh_ref, wl_ref):
    a_hi, a_lo = _split_bf16(a)
    return (jnp.dot(a_hi, wh_ref[...], preferred_element_type=_F32)
            + jnp.dot(a_lo, wh_ref[...], preferred_element_type=_F32)
            + jnp.dot(a_hi, wl_ref[...], preferred_element_type=_F32))


def _weight_split_kernel(w_ref, hi_ref, lo_ref):
    w = w_ref[...]
    hi = w.astype(_BF16)
    hi_ref[...] = hi
    lo_ref[...] = (w - hi.astype(_F32)).astype(_BF16)


def _weight_split(w):
    rows, cols = w.shape
    blk = min(rows, 256)
    spec = pl.BlockSpec((blk, cols), lambda i: (i, 0))
    return pl.pallas_call(
        _weight_split_kernel,
        grid=(rows // blk,),
        in_specs=[spec],
        out_specs=[spec, spec],
        out_shape=[jax.ShapeDtypeStruct(w.shape, _BF16)] * 2,
        name="weight_split",
    )(w)


def _rows_back(x, r):
    return pltpu.roll(x, r, axis=0)


def _rows_ahead(x, r):
    return x if r == 0 else pltpu.roll(x, x.shape[0] - r, axis=0)


def _pool_windows(shape):
    lane = lax.broadcasted_iota(_I32, shape, 1)
    return jnp.left_shift(2, lane // POOL_CH)


def _const_spec(shape):
    nd = len(shape)
    return pl.BlockSpec(shape, lambda *_: (0,) * nd)


def _route_tile(x1, gffn_ref, rwh_ref, rw2_ref, rb_ref, lstrict_ref, ustrict_ref):
    h2 = _rmsnorm(x1, gffn_ref[...])
    h_hi, h_lo = _split_bf16(h2)
    hi_both = jnp.dot(h_hi, rw2_ref[...], preferred_element_type=_F32)
    logits = (hi_both[:, :LANES] + hi_both[:, LANES:]
              + jnp.dot(h_lo, rwh_ref[...], preferred_element_type=_F32)) + rb_ref[...]
    lane = lax.broadcasted_iota(_I32, logits.shape, 1)
    lane_f = lane.astype(_F32)
    neg = jnp.float32(-jnp.inf)
    big = jnp.float32(LANES)

    is_group = jnp.logical_and(lane >= N_EXPERTS, lane < N_EXPERTS + N_EXPERT_GROUPS)
    lg = jnp.where(is_group, logits, neg)
    g_max = jnp.max(lg, axis=-1, keepdims=True)
    g_idx = jnp.min(jnp.where(lg == g_max, lane_f, big), axis=-1, keepdims=True) - N_EXPERTS
    p_top = 1.0 / jnp.sum(jnp.exp(lg - g_max), axis=-1, keepdims=True)

    in_group = (lane // EXPERTS_PER_GROUP).astype(_F32) == g_idx
    le = jnp.where(jnp.logical_and(in_group, lane < N_EXPERTS), logits, neg)
    m1 = jnp.max(le, axis=-1, keepdims=True)
    i1 = jnp.min(jnp.where(le == m1, lane_f, big), axis=-1, keepdims=True)
    le2 = jnp.where(lane_f == i1, neg, le)
    m2 = jnp.max(le2, axis=-1, keepdims=True)
    i2 = jnp.min(jnp.where(le2 == m2, lane_f, big), axis=-1, keepdims=True)
    e2 = jnp.exp(m2 - m1)
    w1 = p_top / (1.0 + e2)
    w2 = p_top * e2 / (1.0 + e2)

    o1 = jnp.where(lane_f == i1, 1.0, 0.0)
    o2 = jnp.where(lane_f == i2, 1.0, 0.0)
    lane2 = lax.broadcasted_iota(_I32, (x1.shape[0], 2 * LANES), 1).astype(_F32)
    o12 = jnp.where(jnp.logical_or(lane2 == i1, lane2 == i2 + LANES), 1.0, 0.0).astype(_BF16)
    before = jnp.dot(lstrict_ref[...], o12, preferred_element_type=_F32)
    before1 = before[:, :LANES]
    before2 = before[:, LANES:]
    n1 = jnp.sum(o1, axis=0, keepdims=True)
    n2 = jnp.sum(o2, axis=0, keepdims=True)
    n_tiles16 = jnp.floor((n1 + n2 + (BF16_ROWS - 1)) * (1.0 / BF16_ROWS))
    npad = n_tiles16 * BF16_ROWS
    seg_start = jnp.dot(jnp.broadcast_to(n_tiles16, (SUBLANES, LANES)).astype(_BF16), ustrict_ref[...],
                        preferred_element_type=_F32)[0:1] * BF16_ROWS
    s1 = jnp.sum(o1 * (seg_start + before1), axis=-1, keepdims=True)
    s2 = jnp.sum(o2 * (seg_start + n1 + before2), axis=-1, keepdims=True)
    route = jnp.where(lane == 0, s1, jnp.where(lane == 1, s2, jnp.where(lane == 2, w1, jnp.where(lane == 3, w2, 0.0))))
    return h2.astype(_BF16), route, npad


def _store_route(x1, route_refs, out_refs):
    h2_ref, route_ref, routet_ref, npad_ref = out_refs
    h2, route, npad = _route_tile(x1, *route_refs)
    h2_ref[...] = h2
    route_ref[...] = route
    routet_ref[0] = jnp.transpose(route)[0:SUBLANES, :]
    npad_ref[0] = npad


def _prompt_mixer_kernel(precise_tail, x_ref, gmix_ref, win_ref, caw_ref, pw_ref, ps_ref, ccw_ref, ccb_ref, lncg_ref,
                         lncb_ref, lndg_ref, lndb_ref, sgw_ref, sgb_ref, wout_ref, gffn_ref, rwh_ref, rw2_ref, rb_ref,
                         lstrict_ref, ustrict_ref, winl_ref, pwl_ref, sgwl_ref, woutl_ref,
                         x1_ref, sa_ref, sp_ref, sc_ref, h2_ref, route_ref, routet_ref, npad_ref,
                         z_ref, exta_ref, extp_ref, extc_ref, dpool_ref, vn_ref, mix_ref, mixf_ref, dpoolf_ref):
    t = pl.program_id(1)
    n_t = pl.num_programs(1)
    tt = x_ref.shape[0]
    tail = slice(tt - CHUNK, tt)
    is_last = t == n_t - 1

    @pl.when(t == 0)
    def _():
        zeros = jnp.zeros((HALO, W_GROUP), _F32)
        exta_ref[0:HALO, :] = zeros
        extp_ref[0:HALO, :] = zeros
        extc_ref[0:HALO, :] = zeros

    def col(k):
        return slice(k * W_GROUP, (k + 1) * W_GROUP)

    stripes = [slice(s0, s0 + MIX_STRIPE) for s0 in range(0, tt, MIX_STRIPE)]
    for rows in stripes:
        h = _rmsnorm(x_ref[rows, :], gmix_ref[...])
        z_ref[rows, :] = jnp.dot(h.astype(_BF16), win_ref[...], preferred_element_type=_F32)
    if precise_tail:
        @pl.when(is_last)
        def _():
            z_ref[tail, :] = _dot_split(_rmsnorm(x_ref[tail, :], gmix_ref[...]), win_ref, winl_ref)

    win = _pool_windows((ROW_BLK, W_GROUP))
    row_iota = lax.broadcasted_iota(_I32, (ROW_BLK, W_GROUP), 0)
    low_group = lax.broadcasted_iota(_I32, (ROW_BLK + 2 * SUBLANES, LANES), 1) < POOL_CH
    lane = lax.broadcasted_iota(_I32, (CHUNK, W_GROUP), 1)

    def mix_row_block(r0):
        rows = slice(r0, r0 + ROW_BLK)
        ext_rows = slice(HALO + r0, HALO + r0 + ROW_BLK)
        exta_ref[ext_rows, :] = z_ref[rows, col(1)] * z_ref[rows, col(2)]
        extp_ref[ext_rows, :] = z_ref[rows, col(3)]
        extc_ref[ext_rows, :] = z_ref[rows, col(4)] * jax.nn.sigmoid(z_ref[rows, col(5)])

        ua = exta_ref[HALO + r0 - SUBLANES:HALO + r0 + ROW_BLK, :]
        conv_a = caw_ref[CONV_A - 1:CONV_A, :] * ua
        for k in range(CONV_A - 1):
            conv_a = conv_a + caw_ref[k:k + 1, :] * _rows_back(ua, CONV_A - 1 - k)
        y_a = z_ref[rows, col(0)] * conv_a[SUBLANES:, :]
        mix_ref[rows, col(0)] = y_a.astype(_BF16)

        pe = extp_ref[HALO + r0 - 2 * SUBLANES:HALO + r0 + ROW_BLK, :]
        s2 = pe + _rows_back(pe, 1)
        s4 = s2 + _rows_back(s2, 2)
        s4_hi = s4[:, LANES:]
        s8 = s4_hi + _rows_back(s4_hi, 4)
        s16 = s8 + _rows_back(s8, 8)
        sums = jnp.concatenate([jnp.where(low_group, s2[:, :LANES], s4[:, :LANES]), jnp.where(low_group, s8, s16)],
                               axis=1)[2 * SUBLANES:, :]
        pos = t * tt + r0 + row_iota
        cnt = jnp.minimum(pos + 1, win).astype(_F32)
        d_pool = sums / cnt - pe[2 * SUBLANES:, :]
        dpool_ref[rows, :] = d_pool.astype(_BF16)

        halves = []
        for hc in range(W_GROUP // LANES):
            lanes = slice(hc * LANES, (hc + 1) * LANES)
            xe = extc_ref[HALO + r0 - HALO:HALO + r0 + ROW_BLK, lanes]
            conv_c = None
            for r in range(SUBLANES):
                xr = _rows_ahead(xe, r)
                for a in range(HALO // SUBLANES + 1):
                    k = SUBLANES * a + r - (HALO - (CONV_C - 1))
                    if 0 <= k < CONV_C:
                        term = ccw_ref[k:k + 1, lanes] * xr[SUBLANES * a:SUBLANES * a + ROW_BLK, :]
                        conv_c = term if conv_c is None else conv_c + term
            halves.append(conv_c)
        y_c = _layernorm(jnp.concatenate(halves, axis=1) + ccb_ref[...], lncg_ref[...], lncb_ref[...])
        y_c = _silu(y_c)
        mix_ref[rows, col(2)] = y_c.astype(_BF16)

        vn_ref[rows, :] = _layernorm(z_ref[rows, col(7)], lndg_ref[...], lndb_ref[...])

        if precise_tail and r0 >= tt - CHUNK:
            tail_rows = slice(r0 - (tt - CHUNK), r0 - (tt - CHUNK) + ROW_BLK)
            mixf_ref[tail_rows, col(0)] = y_a
            mixf_ref[tail_rows, col(2)] = y_c
            dpoolf_ref[tail_rows, :] = d_pool

    for rows in stripes:
        for r0 in range(rows.start, rows.stop, ROW_BLK):
            mix_row_block(r0)

        y_p = jnp.dot(dpool_ref[rows, :], pw_ref[...], preferred_element_type=_F32) * ps_ref[...]
        mix_ref[rows, col(1)] = y_p.astype(_BF16)

        for c0 in range(rows.start, rows.stop, CHUNK):
            chunk = slice(c0, c0 + CHUNK)
            vn_c = vn_ref[chunk, :]
            mixed = sgb_ref[...]
            for hd in range(N_HEADS_D):
                vm = jnp.where(lane // HEAD_D == hd, vn_c, 0.0).astype(_BF16)
                mixed = mixed + jnp.dot(sgw_ref[hd], vm, preferred_element_type=_F32)
            mix_ref[chunk, col(3)] = (z_ref[chunk, col(6)] * mixed).astype(_BF16)

        x1_ref[rows, :] = x_ref[rows, :] + jnp.dot(mix_ref[rows, :], wout_ref[...], preferred_element_type=_F32)

    if precise_tail:
        @pl.when(is_last)
        def _():
            mixf_ref[:, col(1)] = _dot_split(dpoolf_ref[...], pw_ref, pwl_ref) * ps_ref[...]
            vn_c = vn_ref[tail, :]
            mixed = sgb_ref[...]
            for hd in range(N_HEADS_D):
                vm_hi, vm_lo = _split_bf16(jnp.where(lane // HEAD_D == hd, vn_c, 0.0))
                mixed = (mixed + jnp.dot(sgw_ref[hd], vm_hi, preferred_element_type=_F32)
                         + jnp.dot(sgw_ref[hd], vm_lo, preferred_element_type=_F32)
                         + jnp.dot(sgwl_ref[hd], vm_hi, preferred_element_type=_F32))
            mixf_ref[:, col(3)] = z_ref[tail, col(6)] * mixed
            x1_ref[tail, :] = x_ref[tail, :] + _dot_split(mixf_ref[...], wout_ref, woutl_ref)

    _store_route(x1_ref[...], (gffn_ref, rwh_ref, rw2_ref, rb_ref, lstrict_ref, ustrict_ref),
                 (h2_ref, route_ref, routet_ref, npad_ref))

    @pl.when(is_last)
    def _():
        end = HALO + tt
        sa_ref[0] = exta_ref[end - (CONV_A - 1):end, :]
        sp_ref[0] = extp_ref[end - POOL_STATE:end, :]
        sc_ref[0] = extc_ref[end - (CONV_C - 1):end, :]

    exta_ref[0:HALO, :] = exta_ref[tt:tt + HALO, :]
    extp_ref[0:HALO, :] = extp_ref[tt:tt + HALO, :]
    extc_ref[0:HALO, :] = extc_ref[tt:tt + HALO, :]


def _route_out_shapes(n_tiles):
    n_tok = n_tiles * TOK_TILE
    return [jax.ShapeDtypeStruct((n_tok, D_MODEL), _BF16),
            jax.ShapeDtypeStruct((n_tok, LANES), _F32),
            jax.ShapeDtypeStruct((n_tiles, SUBLANES, TOK_TILE), _F32),
            jax.ShapeDtypeStruct((n_tiles, 1, LANES), _F32)]


def _route_out_specs(tile_of):
    return [pl.BlockSpec((TOK_TILE, D_MODEL), lambda *g: (tile_of(*g), 0)),
            pl.BlockSpec((TOK_TILE, LANES), lambda *g: (tile_of(*g), 0)),
            pl.BlockSpec((1, SUBLANES, TOK_TILE), lambda *g: (tile_of(*g), 0, 0)),
            pl.BlockSpec((1, 1, LANES), lambda *g: (tile_of(*g), 0, 0))]


def _mixer_consts(lw, sgu_w, sgu_b):
    return [lw["g_mix"], lw["w_in"], lw["conv_a_w"], lw["pool_w_bd"], lw["pool_scale"], lw["conv_c_w"],
            lw["conv_c_b"], lw["ln_c_g"], lw["ln_c_b"], lw["ln_d_g"], lw["ln_d_b"], lw[sgu_w], lw[sgu_b], lw["w_out"],
            lw["g_ffn"], lw["router_w_hi"], lw["router_w_both"], lw["router_b"], lw["lstrict"], lw["ustrict"]]


def _prompt_mixer(x, bsz, lw, precise_tail):
    seq = x.shape[0] // bsz
    n_t = seq // TOK_TILE
    consts = _mixer_consts(lw, "sgu_w_tril", "sgu_bias_rows") + [lw["w_in_lo"], lw["pool_w_bd_lo"],
                                                                 lw["sgu_w_tril_lo"], lw["w_out_lo"]]
    tile_of = lambda b, t: b * n_t + t
    tile_spec = pl.BlockSpec((TOK_TILE, D_MODEL), lambda b, t: (tile_of(b, t), 0))

    def state_spec(rows):
        return pl.BlockSpec((1, rows, W_GROUP), lambda b, t: (b, 0, 0))

    return pl.pallas_call(
        functools.partial(_prompt_mixer_kernel, precise_tail),
        grid=(bsz, n_t),
        in_specs=[tile_spec] + [_const_spec(c.shape) for c in consts],
        out_specs=[tile_spec, state_spec(CONV_A - 1), state_spec(POOL_STATE), state_spec(CONV_C - 1)]
        + _route_out_specs(tile_of),
        out_shape=[jax.ShapeDtypeStruct((bsz * seq, D_MODEL), _F32),
                   jax.ShapeDtypeStruct((bsz, CONV_A - 1, W_GROUP), _F32),
                   jax.ShapeDtypeStruct((bsz, POOL_STATE, W_GROUP), _F32),
                   jax.ShapeDtypeStruct((bsz, CONV_C - 1, W_GROUP), _F32)] + _route_out_shapes(bsz * n_t),
        scratch_shapes=[pltpu.VMEM((TOK_TILE, IN_COLS), _F32),
                        pltpu.VMEM((HALO + TOK_TILE, W_GROUP), _F32),
                        pltpu.VMEM((HALO + TOK_TILE, W_GROUP), _F32),
                        pltpu.VMEM((HALO + TOK_TILE, W_GROUP), _F32),
                        pltpu.VMEM((TOK_TILE, W_GROUP), _BF16),
                        pltpu.VMEM((TOK_TILE, W_GROUP), _F32),
                        pltpu.VMEM((TOK_TILE, D_MODEL), _BF16),
                        pltpu.VMEM((CHUNK, D_MODEL), _F32),
                        pltpu.VMEM((CHUNK, W_GROUP), _F32)],
        compiler_params=pltpu.CompilerParams(dimension_semantics=("arbitrary", "arbitrary"),
                                             vmem_limit_bytes=VMEM_LIMIT),
        name="prompt_mixer",
    )(x, *consts)


def _sample_mixer_kernel(batch_major_in, x_ref, sta_ref, stp_ref, stc_ref, gmix_ref, win_ref, caw_ref, pw_ref, ps_ref,
                         ccw_ref, ccb_ref, lncg_ref, lncb_ref, lndg_ref, lndb_ref, sgw_ref, sgb_ref, wout_ref,
                         gffn_ref, rwh_ref, rw2_ref, rb_ref, lstrict_ref, ustrict_ref,
                         x1_ref, nsa_ref, nsp_ref, nsc_ref, vrow_ref, h2_ref, route_ref, routet_ref, npad_ref,
                         xt_ref, z_ref, exta_ref, extp_ref, extc_ref, dpool_ref, vn_ref, mix_ref):
    nb = sta_ref.shape[0]
    n_tok = x1_ref.shape[0]
    n_t = n_tok // nb

    def col(k):
        return slice(k * W_GROUP, (k + 1) * W_GROUP)

    def slab(j, n=1):
        return slice(j * nb, (j + n) * nb)

    if batch_major_in:
        for tstep in range(n_t):
            xt_ref[slab(tstep), :] = x_ref[:, tstep * D_MODEL:(tstep + 1) * D_MODEL]
    else:
        xt_ref[...] = x_ref[...]

    h = _rmsnorm(xt_ref[...], gmix_ref[...]).astype(_BF16)
    z_ref[...] = jnp.dot(h, win_ref[...], preferred_element_type=_F32)

    for j in range(CONV_A - 1):
        exta_ref[slab(j), :] = sta_ref[:, col(j)]
    for j in range(POOL_STATE):
        extp_ref[slab(j), :] = stp_ref[:, col(j)]
    for j in range(CONV_C - 1):
        extc_ref[slab(j), :] = stc_ref[:, col(j)]
    for tstep in range(n_t):
        rows = slab(tstep)
        exta_ref[slab(CONV_A - 1 + tstep), :] = z_ref[rows, col(1)] * z_ref[rows, col(2)]
        extp_ref[slab(POOL_STATE + tstep), :] = z_ref[rows, col(3)]
        extc_ref[slab(CONV_C - 1 + tstep), :] = z_ref[rows, col(4)] * jax.nn.sigmoid(z_ref[rows, col(5)])

    win = _pool_windows((nb, W_GROUP))
    for tstep in range(n_t):
        rows = slab(tstep)
        conv_a = None
        for k in range(CONV_A):
            term = caw_ref[k:k + 1, :] * exta_ref[slab(tstep + k), :]
            conv_a = term if conv_a is None else conv_a + term
        mix_ref[rows, col(0)] = (z_ref[rows, col(0)] * conv_a).astype(_BF16)

        p_cur = extp_ref[slab(POOL_STATE + tstep), :]
        acc = p_cur
        for j in range(1, POOL_STATE + 1):
            acc = acc + jnp.where(win > j, extp_ref[slab(POOL_STATE + tstep - j), :], 0.0)
        cnt = jnp.minimum(PAST_LEN + tstep + 1, win).astype(_F32)
        dpool_ref[rows, :] = (acc / cnt - p_cur).astype(_BF16)

        conv_c = None
        for k in range(CONV_C):
            term = ccw_ref[k:k + 1, :] * extc_ref[slab(tstep + k), :]
            conv_c = term if conv_c is None else conv_c + term
        y_c = _layernorm(conv_c + ccb_ref[...], lncg_ref[...], lncb_ref[...])
        mix_ref[rows, col(2)] = _silu(y_c).astype(_BF16)

        v_n = _layernorm(z_ref[rows, col(7)], lndg_ref[...], lndb_ref[...])
        vn_ref[rows, :] = v_n
        vrow_ref[:, col(tstep)] = v_n

    y_p = jnp.dot(dpool_ref[...], pw_ref[...], preferred_element_type=_F32) * ps_ref[...]
    mix_ref[:, col(1)] = y_p.astype(_BF16)

    for i in range(n_t):
        mixed = sgb_ref[i:i + 1, :] + sgw_ref[i * n_t:i * n_t + 1, :] * vn_ref[slab(0), :]
        for j in range(1, i + 1):
            mixed = mixed + sgw_ref[i * n_t + j:i * n_t + j + 1, :] * vn_ref[slab(j), :]
        mix_ref[slab(i), col(3)] = (z_ref[slab(i), col(6)] * mixed).astype(_BF16)

    x1 = xt_ref[...] + jnp.dot(mix_ref[...], wout_ref[...], preferred_element_type=_F32)
    x1_ref[...] = x1
    _store_route(x1, (gffn_ref, rwh_ref, rw2_ref, rb_ref, lstrict_ref, ustrict_ref),
                 (h2_ref, route_ref, routet_ref, npad_ref))

    for j in range(CONV_A - 1):
        nsa_ref[:, col(j)] = exta_ref[slab(n_t + j), :]
    for j in range(POOL_STATE):
        nsp_ref[:, col(j)] = extp_ref[slab(n_t + j), :]
    for j in range(CONV_C - 1):
        nsc_ref[:, col(j)] = extc_ref[slab(n_t + j), :]


def _sample_mixer(x, st_a, st_p, st_c, lw, n_t, batch_major_in):
    n_seq = st_a.shape[0]
    nb = SAMPLE_SEQ_BLK
    n_blk = n_seq // nb
    n_tok = nb * n_t
    assert n_tok == TOK_TILE
    consts = _mixer_consts(lw, "sgu_w_rows", "sgu_b_rows")

    def seq_spec(width):
        return pl.BlockSpec((nb, width), lambda i: (i, 0))

    tok_spec = pl.BlockSpec((n_tok, D_MODEL), lambda i: (i, 0))
    state_widths = [(CONV_A - 1) * W_GROUP, POOL_STATE * W_GROUP, (CONV_C - 1) * W_GROUP]
    x_spec = seq_spec(n_t * D_MODEL) if batch_major_in else tok_spec
    return pl.pallas_call(
        functools.partial(_sample_mixer_kernel, batch_major_in),
        grid=(n_blk,),
        in_specs=[x_spec] + [seq_spec(w) for w in state_widths] + [_const_spec(c.shape) for c in consts],
        out_specs=[tok_spec] + [seq_spec(w) for w in state_widths] + [seq_spec(n_t * W_GROUP)]
        + _route_out_specs(lambda i: i),
        out_shape=[jax.ShapeDtypeStruct((n_blk * n_tok, D_MODEL), _F32)]
        + [jax.ShapeDtypeStruct((n_seq, w), _F32) for w in state_widths]
        + [jax.ShapeDtypeStruct((n_seq, n_t * W_GROUP), _F32)] + _route_out_shapes(n_blk),
        scratch_shapes=[pltpu.VMEM((n_tok, D_MODEL), _F32),
                        pltpu.VMEM((n_tok, IN_COLS), _F32),
                        pltpu.VMEM(((CONV_A - 1 + n_t) * nb, W_GROUP), _F32),
                        pltpu.VMEM(((POOL_STATE + n_t) * nb, W_GROUP), _F32),
                        pltpu.VMEM(((CONV_C - 1 + n_t) * nb, W_GROUP), _F32),
                        pltpu.VMEM((n_tok, W_GROUP), _BF16),
                        pltpu.VMEM((n_tok, W_GROUP), _F32),
                        pltpu.VMEM((n_tok, D_MODEL), _BF16)],
        compiler_params=pltpu.CompilerParams(dimension_semantics=("arbitrary",), vmem_limit_bytes=VMEM_LIMIT),
        name="sample_mixer",
    )(x, st_a, st_p, st_c, *consts)


def _plan_kernel(np_ref, ustrict_ref, dest_ref, tab_ref, used_ref, npx_ref, toff_ref, zc_ref):
    n_tiles = np_ref.shape[0]
    nt_pad = npx_ref.shape[0]
    zeros = jnp.zeros((nt_pad, LANES), _F32)
    npx_ref[...] = zeros
    toff_ref[...] = zeros
    zc_ref[...] = zeros
    npx_ref[0:n_tiles, :] = np_ref[...]
    np_all = npx_ref[...]

    tile_row = lax.broadcasted_iota(_I32, (nt_pad, 1), 0)
    n_real = jnp.sum(np_all, axis=-1, keepdims=True) * (1.0 / BF16_ROWS)
    n_zero = jnp.where(tile_row < n_tiles, N_CHUNK - n_real, 0.0)

    run = jnp.zeros((1, LANES), _F32)
    zrun = jnp.zeros((1, LANES), _F32)
    for i in range(n_tiles):
        toff_ref[i:i + 1, :] = run
        zc_ref[i:i + 1, :] = zrun
        run = run + npx_ref[i:i + 1, :]
        zrun = zrun + n_zero[i:i + 1, :]
    rows_e = run
    rows_pad = jnp.ceil(rows_e * (1.0 / GMM_TILE)) * GMM_TILE
    gap = (rows_pad - rows_e) * (1.0 / BF16_ROWS)

    def excl_lanes(v):
        return jnp.dot(v.astype(_BF16), ustrict_ref[...], preferred_element_type=_F32)

    gstart = excl_lanes(jnp.broadcast_to(rows_pad * (1.0 / GMM_TILE), (SUBLANES, LANES)))[0:1] * GMM_TILE
    gap_start = excl_lanes(jnp.broadcast_to(gap, (SUBLANES, LANES)))[0:1]
    gap_total = jnp.sum(gap, axis=-1, keepdims=True)
    rows_total = jnp.sum(rows_pad, axis=-1, keepdims=True)
    seg_start = excl_lanes(np_all * (1.0 / BF16_ROWS)) * BF16_ROWS
    delta = gstart + toff_ref[...] - seg_start

    chunk = lax.broadcasted_iota(_I32, (nt_pad, LANES), 1).astype(_F32)
    pos = chunk * BF16_ROWS
    q = zc_ref[...] + (chunk - n_real)
    real = pos
    gap_addr = q * BF16_ROWS
    for e in range(N_EXPERTS):
        ss = seg_start[:, e:e + 1]
        se = ss + np_all[:, e:e + 1]
        real = real + jnp.where(jnp.logical_and(ss <= pos, pos < se), delta[:, e:e + 1], 0.0)
        gs = gap_start[:, e:e + 1]
        ge = gs + gap[:, e:e + 1]
        base = gstart[:, e:e + 1] + rows_e[:, e:e + 1] - gs * BF16_ROWS
        gap_addr = gap_addr + jnp.where(jnp.logical_and(gs <= q, q < ge), base, 0.0)
    tail_addr = rows_total + (q - gap_total) * BF16_ROWS
    zero_addr = jnp.where(q < gap_total, gap_addr, tail_addr)
    dest = jnp.where(chunk < n_real, real, zero_addr)
    dest_ref[...] = dest[0:n_tiles, :].astype(_I32)
    used_ref[...] = jnp.broadcast_to(n_real * BF16_ROWS, (nt_pad, LANES))[0:n_tiles, :].astype(_I32)

    n_cols = tab_ref.shape[1]
    row_pos = lax.broadcasted_iota(_I32, (SUBLANES, n_cols), 1).astype(_F32) * GMM_TILE
    t_exp = jnp.zeros((SUBLANES, n_cols), _F32)
    t_val = jnp.zeros((SUBLANES, n_cols), _F32)
    for e in range(N_EXPERTS):
        gs = gstart[:, e:e + 1]
        t_exp = t_exp + jnp.where(gs + rows_pad[:, e:e + 1] <= row_pos, 1.0, 0.0)
        t_val = t_val + jnp.where(jnp.logical_and(gs <= row_pos, row_pos < gs + rows_e[:, e:e + 1]), 1.0, 0.0)
    t_exp = jnp.minimum(t_exp, N_EXPERTS - 1.0)
    n_valid = jnp.sum(t_val, axis=-1, keepdims=True)
    n_groups = jnp.ceil(n_valid * (1.0 / GMM_GROUP))
    s_idx = jnp.minimum(row_pos * (1.0 / GMM_TILE), n_groups - 1.0)
    sub = lax.broadcasted_iota(_I32, (SUBLANES, n_cols), 0)
    tab_ref[...] = jnp.where(sub == 0, t_exp, jnp.where(sub == 1, s_idx, 0.0)).astype(_I32)


def _plan(npad_all, lw, n_gmm_tiles):
    n_tiles = npad_all.shape[0]
    nt_pad = -(-n_tiles // SUBLANES) * SUBLANES
    n_cols = -(-n_gmm_tiles // LANES) * LANES
    out_shape = [jax.ShapeDtypeStruct((n_tiles, LANES), _I32), jax.ShapeDtypeStruct((SUBLANES, n_cols), _I32),
                 jax.ShapeDtypeStruct((n_tiles, LANES), _I32)]
    return pl.pallas_call(
        _plan_kernel,
        grid=(1,),
        in_specs=[_const_spec(npad_all.shape), _const_spec(lw["ustrict"].shape)],
        out_specs=[_const_spec(s.shape) for s in out_shape],
        out_shape=out_shape,
        scratch_shapes=[pltpu.VMEM((nt_pad, LANES), _F32)] * 3,
        name="moe_plan",
    )(npad_all, lw["ustrict"])


def _sort_kernel(n_prompt_tiles, dest_ref, used_ref, h2p_ref, h2s_ref, rtp_ref, rts_ref, xs_ref, buf_ref, sem_ref):
    i = pl.program_id(0)
    n = pl.num_programs(0)
    cur = lax.rem(i, 2)
    is_p = i < n_prompt_tiles
    h2 = jnp.where(is_p, h2p_ref[...], h2s_ref[...])
    rt = jnp.where(is_p, rtp_ref[0], rts_ref[0])
    s1 = rt[0:1, :]
    s2 = rt[1:2, :]

    def chunk_copy(tile, c, slot):
        dst = pl.multiple_of(dest_ref[tile * N_CHUNK + c], BF16_ROWS)
        return pltpu.make_async_copy(buf_ref.at[slot, pl.ds(c * BF16_ROWS, BF16_ROWS), :],
                                     xs_ref.at[pl.ds(dst, BF16_ROWS), :], sem_ref.at[slot])

    @pl.when(i < 2)
    def _():
        buf_ref[cur, SLOTS:SLOT_BUF, :] = jnp.zeros((SLOT_BUF - SLOTS, D_MODEL), _BF16)

    def sort_group(g):
        slot_id = (g * SLOT_GRP + lax.broadcasted_iota(_I32, (SLOT_GRP, TOK_TILE), 0)).astype(_F32)
        perm = jnp.where(jnp.logical_or(slot_id == s1, slot_id == s2), 1.0, 0.0).astype(_BF16)
        buf_ref[cur, g * SLOT_GRP:(g + 1) * SLOT_GRP, :] = jnp.dot(perm, h2, preferred_element_type=_F32).astype(_BF16)

    n_grp = SLOTS // SLOT_GRP
    for g in range(n_grp - 1):
        sort_group(g)
    last_used = used_ref[i * LANES] > (n_grp - 1) * SLOT_GRP

    @pl.when(last_used)
    def _():
        sort_group(n_grp - 1)

    @pl.when(jnp.logical_not(last_used))
    def _():
        buf_ref[cur, (n_grp - 1) * SLOT_GRP:SLOTS, :] = jnp.zeros((SLOT_GRP, D_MODEL), _BF16)

    @pl.when(i > 0)
    def _():
        for c in range(N_CHUNK):
            chunk_copy(i - 1, c, 1 - cur).wait()

    for c in range(N_CHUNK):
        chunk_copy(i, c, cur).start()

    @pl.when(i == n - 1)
    def _():
        for c in range(N_CHUNK):
            chunk_copy(i, c, cur).wait()


def _sort(dest_flat, used_flat, h2p, h2s, rtp, rts):
    n_p = h2p.shape[0] // TOK_TILE
    n_s = h2s.shape[0] // TOK_TILE
    n_tiles = n_p + n_s
    p_idx = lambda i: jnp.minimum(i, n_p - 1)
    s_idx = lambda i: jnp.maximum(i - n_p, 0)
    return pl.pallas_call(
        functools.partial(_sort_kernel, n_p),
        grid_spec=pltpu.PrefetchScalarGridSpec(
            num_scalar_prefetch=2,
            grid=(n_tiles,),
            in_specs=[pl.BlockSpec((TOK_TILE, D_MODEL), lambda i, d, u: (p_idx(i), 0)),
                      pl.BlockSpec((TOK_TILE, D_MODEL), lambda i, d, u: (s_idx(i), 0)),
                      pl.BlockSpec((1, SUBLANES, TOK_TILE), lambda i, d, u: (p_idx(i), 0, 0)),
                      pl.BlockSpec((1, SUBLANES, TOK_TILE), lambda i, d, u: (s_idx(i), 0, 0))],
            out_specs=pl.BlockSpec(memory_space=pl.ANY),
            scratch_shapes=[pltpu.VMEM((2, SLOT_BUF, D_MODEL), _BF16), pltpu.SemaphoreType.DMA((2,))],
        ),
        out_shape=jax.ShapeDtypeStruct((n_tiles * SLOT_BUF, D_MODEL), _BF16),
        compiler_params=pltpu.CompilerParams(dimension_semantics=("arbitrary",), vmem_limit_bytes=VMEM_LIMIT),
        name="moe_sort",
    )(dest_flat, used_flat, h2p, h2s, rtp, rts)


def _gmm_kernel(texp_ref, sidx_ref, xs_ref, *refs):
    w_refs, ys_ref = refs[:-1], refs[-1]
    s = pl.program_id(0)

    @pl.when(sidx_ref[s] == s)
    def _():
        blk = GMM_TILE // GMM_ROW_SPLIT
        chains = [(j, slice(j * GMM_TILE + k * blk, j * GMM_TILE + (k + 1) * blk))
                  for j in range(GMM_GROUP) for k in range(GMM_ROW_SPLIT)]
        w_gu = [jnp.concatenate([w_refs[3 * j][0, 0].astype(_BF16), w_refs[3 * j + 1][0, 0].astype(_BF16)], axis=1)
                for j in range(GMM_GROUP)]
        w_d = [w_refs[3 * j + 2][0, 0].astype(_BF16) for j in range(GMM_GROUP)]
        gate_up = [jnp.dot(xs_ref[rows, :], w_gu[j], preferred_element_type=_F32) for j, rows in chains]
        for (j, rows), gu in zip(chains, gate_up):
            act = (_silu(gu[:, :D_FF_EXPERT]) * gu[:, D_FF_EXPERT:]).astype(_BF16)
            ys_ref[rows, :] = jnp.dot(act, w_d[j], preferred_element_type=_F32).astype(_BF16)


def _gmm(t_exp, s_idx, xs, layer, w_gate, w_up, w_down):
    n_steps = xs.shape[0] // (GMM_GROUP * GMM_TILE)
    group_spec = pl.BlockSpec((GMM_GROUP * GMM_TILE, D_MODEL), lambda s, te, si: (si[s], 0))
    w_specs, w_args = [], []
    for j in range(GMM_GROUP):
        expert = lambda s, te, si, j=j: (layer, te[GMM_GROUP * s + j], 0, 0)
        w_specs += [pl.BlockSpec((1, 1, D_MODEL, D_FF_EXPERT), expert),
                    pl.BlockSpec((1, 1, D_MODEL, D_FF_EXPERT), expert),
                    pl.BlockSpec((1, 1, D_FF_EXPERT, D_MODEL), expert)]
        w_args += [w_gate, w_up, w_down]
    return pl.pallas_call(
        _gmm_kernel,
        grid_spec=pltpu.PrefetchScalarGridSpec(
            num_scalar_prefetch=2,
            grid=(n_steps,),
            in_specs=[group_spec] + w_specs,
            out_specs=group_spec,
        ),
        out_shape=jax.ShapeDtypeStruct(xs.shape, _BF16),
        input_output_aliases={2: 0},
        compiler_params=pltpu.CompilerParams(dimension_semantics=("arbitrary",), vmem_limit_bytes=VMEM_LIMIT),
        name="moe_experts",
    )(t_exp, s_idx, xs, *w_args)


def _combine_kernel(n_prompt_tiles, final_norm, dest_ref, used_ref, x1p_ref, x1s_ref, rp_ref, rs_ref, gfin_ref, ys_ref,
                    outp_ref, outs_ref, ybuf_ref, sem_ref):
    i = pl.program_id(0)
    n = pl.num_programs(0)
    cur = lax.rem(i, 2)
    is_p = i < n_prompt_tiles

    def chunk_copy(tile, c, slot):
        src = pl.multiple_of(dest_ref[tile * N_CHUNK + c], BF16_ROWS)
        return pltpu.make_async_copy(ys_ref.at[pl.ds(src, BF16_ROWS), :],
                                     ybuf_ref.at[slot, pl.ds(c * BF16_ROWS, BF16_ROWS), :], sem_ref.at[slot])

    @pl.when(i == 0)
    def _():
        for c in range(N_CHUNK_REAL):
            chunk_copy(0, c, 0).start()

    @pl.when(i + 1 < n)
    def _():
        for c in range(N_CHUNK_REAL):
            chunk_copy(i + 1, c, 1 - cur).start()

    for c in range(N_CHUNK_REAL):
        chunk_copy(i, c, cur).wait()

    route = jnp.where(is_p, rp_ref[...], rs_ref[...])
    acc = jnp.where(is_p, x1p_ref[...], x1s_ref[...])
    s1 = route[:, 0:1]
    s2 = route[:, 1:2]
    w1 = route[:, 2:3]
    w2 = route[:, 3:4]
    def unperm_dot(g):
        slot_id = (g * SLOT_GRP + lax.broadcasted_iota(_I32, (TOK_TILE, SLOT_GRP), 1)).astype(_F32)
        unperm = (jnp.where(slot_id == s1, w1, 0.0) + jnp.where(slot_id == s2, w2, 0.0)).astype(_BF16)
        return jnp.dot(unperm, ybuf_ref[cur, g * SLOT_GRP:(g + 1) * SLOT_GRP, :], preferred_element_type=_F32)

    def finish(y):
        if final_norm:
            y = _rmsnorm(y, gfin_ref[...])

        @pl.when(is_p)
        def _():
            outp_ref[...] = y

        @pl.when(jnp.logical_not(is_p))
        def _():
            outs_ref[...] = y

    n_grp = SLOTS // SLOT_GRP
    for g in range(n_grp - 1):
        acc = acc + unperm_dot(g)
    last_used = used_ref[i * LANES] > (n_grp - 1) * SLOT_GRP

    @pl.when(last_used)
    def _():
        finish(acc + unperm_dot(n_grp - 1))

    @pl.when(jnp.logical_not(last_used))
    def _():
        finish(acc)


def _combine(dest_flat, used_flat, x1p, x1s, rp, rs, g_fin, ys, final_norm):
    n_p = x1p.shape[0] // TOK_TILE
    n_s = x1s.shape[0] // TOK_TILE
    p_idx = lambda i, d, u: (jnp.minimum(i, n_p - 1), 0)
    s_idx = lambda i, d, u: (jnp.maximum(i - n_p, 0), 0)
    return pl.pallas_call(
        functools.partial(_combine_kernel, n_p, final_norm),
        grid_spec=pltpu.PrefetchScalarGridSpec(
            num_scalar_prefetch=2,
            grid=(n_p + n_s,),
            in_specs=[pl.BlockSpec((TOK_TILE, D_MODEL), p_idx), pl.BlockSpec((TOK_TILE, D_MODEL), s_idx),
                      pl.BlockSpec((TOK_TILE, LANES), p_idx), pl.BlockSpec((TOK_TILE, LANES), s_idx),
                      pl.BlockSpec(g_fin.shape, lambda i, d, u: (0, 0)),
                      pl.BlockSpec(memory_space=pl.ANY)],
            out_specs=[pl.BlockSpec((TOK_TILE, D_MODEL), p_idx), pl.BlockSpec((TOK_TILE, D_MODEL), s_idx)],
            scratch_shapes=[pltpu.VMEM((2, SLOTS, D_MODEL), _BF16), pltpu.SemaphoreType.DMA((2,))],
        ),
        out_shape=[jax.ShapeDtypeStruct(x1p.shape, _F32), jax.ShapeDtypeStruct(x1s.shape, _F32)],
        compiler_params=pltpu.CompilerParams(dimension_semantics=("arbitrary",), vmem_limit_bytes=VMEM_LIMIT),
        name="moe_combine",
    )(dest_flat, used_flat, x1p, x1s, rp, rs, g_fin, ys)


def _moe(x1p, x1s, routing_p, routing_s, lw, g_fin, final_norm):
    h2p, rp, rtp, npp = routing_p
    h2s, rs, rts, nps = routing_s
    n_tiles = npp.shape[0] + nps.shape[0]
    n_gmm = n_tiles * SLOT_BUF // GMM_TILE
    assert n_tiles * (SLOT_BUF - SLOTS) >= N_EXPERTS * (GMM_TILE - BF16_ROWS)
    npad_all = jnp.concatenate([npp, nps], axis=0).reshape(n_tiles, LANES)
    dest, tab, used = _plan(npad_all, lw, n_gmm)
    dest_flat = dest.reshape(-1)
    used_flat = used.reshape(-1)
    xs = _sort(dest_flat, used_flat, h2p, h2s, rtp, rts)
    ys = _gmm(tab[0], tab[1], xs, lw["layer"], *lw["expert_w"])
    return _combine(dest_flat, used_flat, x1p, x1s, rp, rs, g_fin, ys, final_norm)


def _layer_weights(l, g_mix, w_in, conv_a_w, pool_w, pool_scale, conv_c_w, conv_c_b, ln_c_g, ln_c_b, ln_d_g, ln_d_b,
                   sgu_w, sgu_b, w_out, g_ffn, router_group_w, router_group_b, router_expert_w, router_expert_b,
                   expert_w_gate, expert_w_up, expert_w_down, n_t_sample, precise_tail):
    row = lambda v: v[l].reshape(1, -1)
    pool_bd = jnp.zeros((W_GROUP, W_GROUP), _F32)
    for g in range(len(POOL_WINDOWS)):
        sl = slice(g * POOL_CH, (g + 1) * POOL_CH)
        pool_bd = pool_bd.at[sl, sl].set(pool_w[l, g])
    tril = jnp.tril(jnp.ones((CHUNK, CHUNK), dtype=bool))
    sgu_tril = jnp.where(tril, sgu_w[l], 0.0)
    w_small = sgu_tril[:, :n_t_sample, :n_t_sample]
    sgu_w_rows = jnp.repeat(jnp.transpose(w_small, (1, 2, 0)).reshape(n_t_sample * n_t_sample, N_HEADS_D), HEAD_D, axis=1)
    n_route = N_EXPERTS + N_EXPERT_GROUPS
    router_w = jnp.pad(jnp.concatenate([router_expert_w[l], router_group_w[l]], axis=1), ((0, 0), (0, LANES - n_route)))
    router_b = jnp.pad(jnp.concatenate([router_expert_b[l], router_group_b[l]]), (0, LANES - n_route)).reshape(1, LANES)
    router_w_hi, router_w_lo = _weight_split(router_w)
    if precise_tail:
        w_in_hi, w_in_lo = _weight_split(w_in[l])
        w_out_hi, w_out_lo = _weight_split(w_out[l])
        pool_hi, pool_lo = _weight_split(pool_bd)
        sgu_hi, sgu_lo = (s.reshape(sgu_tril.shape) for s in _weight_split(sgu_tril.reshape(-1, CHUNK)))
    else:
        w_in_hi = w_in_lo = w_in[l].astype(_BF16)
        w_out_hi = w_out_lo = w_out[l].astype(_BF16)
        pool_hi = pool_lo = pool_bd.astype(_BF16)
        sgu_hi = sgu_lo = sgu_tril.astype(_BF16)
    return {
        "w_in_lo": w_in_lo, "pool_w_bd_lo": pool_lo, "sgu_w_tril_lo": sgu_lo, "w_out_lo": w_out_lo,
        "g_mix": row(g_mix), "w_in": w_in_hi, "conv_a_w": conv_a_w[l], "pool_w_bd": pool_hi,
        "pool_scale": row(pool_scale), "conv_c_w": conv_c_w[l], "conv_c_b": row(conv_c_b), "ln_c_g": row(ln_c_g),
        "ln_c_b": row(ln_c_b), "ln_d_g": row(ln_d_g), "ln_d_b": row(ln_d_b),
        "sgu_w_tril": sgu_hi,
        "sgu_bias_rows": jnp.repeat(sgu_b[l].T, HEAD_D, axis=1),
        "sgu_w_rows": sgu_w_rows,
        "sgu_b_rows": jnp.repeat(sgu_b[l][:, :n_t_sample].T, HEAD_D, axis=1),
        "w_out": w_out_hi, "g_ffn": row(g_ffn), "router_w_hi": router_w_hi,
        "router_w_both": jnp.concatenate([router_w_hi, router_w_lo], axis=1),
        "router_b": router_b,
        "layer": l, "expert_w": (expert_w_gate, expert_w_up, expert_w_down),
        "lstrict": jnp.tril(jnp.ones((TOK_TILE, TOK_TILE), _F32), -1).astype(_BF16),
        "ustrict": jnp.triu(jnp.ones((LANES, LANES), _F32), 1).astype(_BF16),
    }


def kernel(x_prompt, x_sample, state_conv_a, state_pool, state_conv_c, g_mix, w_in, conv_a_w, pool_w, pool_scale, conv_c_w, conv_c_b, ln_c_g, ln_c_b, ln_d_g, ln_d_b, sgu_w, sgu_b, w_out, g_ffn, router_group_w, router_group_b, router_expert_w, router_expert_b, expert_w_gate, expert_w_up, expert_w_down, g_final):
    depth = g_mix.shape[0]
    bsz, seq, _ = x_prompt.shape
    nb, n_t, _ = x_sample.shape
    g_fin = g_final.reshape(1, -1)

    xp = x_prompt.reshape(bsz * seq, D_MODEL)
    xs = x_sample.reshape(nb, n_t * D_MODEL)
    outs = {k: [] for k in ("sa_p", "sp_p", "sc_p", "sa_s", "sp_s", "sc_s", "v")}
    for l in range(depth):
        precise_tail = l + 1 < depth
        lw = _layer_weights(l, g_mix, w_in, conv_a_w, pool_w, pool_scale, conv_c_w, conv_c_b, ln_c_g, ln_c_b, ln_d_g,
                            ln_d_b, sgu_w, sgu_b, w_out, g_ffn, router_group_w, router_group_b, router_expert_w,
                            router_expert_b, expert_w_gate, expert_w_up, expert_w_down, n_t, precise_tail)
        x1p, sa, sp, sc, *routing_p = _prompt_mixer(xp, bsz, lw, precise_tail)
        outs["sa_p"].append(sa)
        outs["sp_p"].append(sp)
        outs["sc_p"].append(sc)
        x1s, nsa, nsp, nsc, vrow, *routing_s = _sample_mixer(
            xs, state_conv_a[l].reshape(nb, -1), state_pool[l].reshape(nb, -1), state_conv_c[l].reshape(nb, -1),
            lw, n_t, batch_major_in=(l == 0))
        outs["sa_s"].append(nsa.reshape(nb, CONV_A - 1, W_GROUP))
        outs["sp_s"].append(nsp.reshape(nb, POOL_STATE, W_GROUP))
        outs["sc_s"].append(nsc.reshape(nb, CONV_C - 1, W_GROUP))
        outs["v"].append(vrow.reshape(nb, n_t, W_GROUP))
        xp, xs = _moe(x1p, x1s, routing_p, routing_s, lw, g_fin, final_norm=(l == depth - 1))

    y_prompt = xp.reshape(bsz, seq, D_MODEL)
    y_sample = jnp.transpose(xs.reshape(nb // SAMPLE_SEQ_BLK, n_t, SAMPLE_SEQ_BLK, D_MODEL),
                             (0, 2, 1, 3)).reshape(nb, n_t, D_MODEL)
    return (y_prompt, y_sample, jnp.stack(outs["sa_p"]), jnp.stack(outs["sp_p"]), jnp.stack(outs["sc_p"]),
            jnp.stack(outs["sa_s"]), jnp.stack(outs["sp_s"]), jnp.stack(outs["sc_s"]), jnp.stack(outs["v"]))
```

```python
import functools

import jax
import jax.numpy as jnp
from jax import lax
from jax.experimental import pallas as pl
from jax.experimental.pallas import tpu as pltpu

D_MODEL = 1024
W_GROUP = 256
IN_COLS = 8 * W_GROUP
CONV_A = 3
POOL_WINDOWS = (2, 4, 8, 16)
POOL_CH = W_GROUP // len(POOL_WINDOWS)
POOL_STATE = max(POOL_WINDOWS) - 1
CONV_C = 31
CHUNK = 128
N_HEADS_D = 4
HEAD_D = W_GROUP // N_HEADS_D
N_EXPERT_GROUPS = 4
EXPERTS_PER_GROUP = 8
N_EXPERTS = N_EXPERT_GROUPS * EXPERTS_PER_GROUP
TOP_K = 2
D_FF_EXPERT = 128
EPS = 1e-6
PAST_LEN = 16384

LANES = 128
SUBLANES = 8
BF16_ROWS = 16
HALO = 32
ROW_BLK = 64
MIX_STRIPE = 256
TOK_TILE = 512
SAMPLE_SEQ_BLK = 64
GMM_TILE = 512
GMM_GROUP = 1
GMM_ROW_SPLIT = 4
SLOTS = -(-(TOP_K * TOK_TILE + N_EXPERTS * (BF16_ROWS - 1)) // 256) * 256
N_CHUNK_REAL = SLOTS // BF16_ROWS
N_CHUNK = 128
SLOT_BUF = N_CHUNK * BF16_ROWS
SLOT_GRP = 256
VMEM_LIMIT = 56 * 1024 * 1024

_F32 = jnp.float32
_BF16 = jnp.bfloat16
_I32 = jnp.int32
_HI = lax.Precision.HIGHEST


def _rmsnorm(x, g):
    return x * lax.rsqrt(jnp.mean(x * x, axis=-1, keepdims=True) + EPS) * g


def _layernorm(x, g, b):
    mu = jnp.mean(x, axis=-1, keepdims=True)
    xc = x - mu
    var = jnp.mean(xc * xc, axis=-1, keepdims=True)
    return xc * lax.rsqrt(var + EPS) * g + b


def _silu(x):
    return x * jax.nn.sigmoid(x)


def _split_bf16(a):
    bits = lax.bitcast_convert_type(a, jnp.uint32)
    hi = lax.bitcast_convert_type(bits & jnp.uint32(0xFFFF0000), _F32)
    return hi.astype(_BF16), (a - hi).astype(_BF16)


def _dot_split(a, wh_ref, wl_ref):
    a_hi, a_lo = _split_bf16(a)
    return (jnp.dot(a_hi, wh_ref[...], preferred_element_type=_F32)
            + jnp.dot(a_lo, wh_ref[...], preferred_element_type=_F32)
            + jnp.dot(a_hi, wl_ref[...], preferred_element_type=_F32))


def _weight_split_kernel(w_ref, hi_ref, lo_ref):
    w = w_ref[...]
    hi = w.astype(_BF16)
    hi_ref[...] = hi
    lo_ref[...] = (w - hi.astype(_F32)).astype(_BF16)


def _weight_split(w):
    rows, cols = w.shape
    blk = min(rows, 256)
    spec = pl.BlockSpec((blk, cols), lambda i: (i, 0))
    return pl.pallas_call(
        _weight_split_kernel,
        grid=(rows // blk,),
        in_specs=[spec],
        out_specs=[spec, spec],
        out_shape=[jax.ShapeDtypeStruct(w.shape, _BF16)] * 2,
        name="weight_split",
    )(w)


def _rows_back(x, r):
    return pltpu.roll(x, r, axis=0)


def _rows_ahead(x, r):
    return x if r == 0 else pltpu.roll(x, x.shape[0] - r, axis=0)


def _pool_windows(shape):
    lane = lax.broadcasted_iota(_I32, shape, 1)
    return jnp.left_shift(2, lane // POOL_CH)


def _const_spec(shape):
    nd = len(shape)
    return pl.BlockSpec(shape, lambda *_: (0,) * nd)


def _route_tile(x1, gffn_ref, rwh_ref, rw2_ref, rb_ref, lstrict_ref, ustrict_ref):
    h2 = _rmsnorm(x1, gffn_ref[...])
    h_hi, h_lo = _split_bf16(h2)
    hi_both = jnp.dot(h_hi, rw2_ref[...], preferred_element_type=_F32)
    logits = (hi_both[:, :LANES] + hi_both[:, LANES:]
              + jnp.dot(h_lo, rwh_ref[...], preferred_element_type=_F32)) + rb_ref[...]
    lane = lax.broadcasted_iota(_I32, logits.shape, 1)
    lane_f = lane.astype(_F32)
    neg = jnp.float32(-jnp.inf)
    big = jnp.float32(LANES)

    is_group = jnp.logical_and(lane >= N_EXPERTS, lane < N_EXPERTS + N_EXPERT_GROUPS)
    lg = jnp.where(is_group, logits, neg)
    g_max = jnp.max(lg, axis=-1, keepdims=True)
    g_idx = jnp.min(jnp.where(lg == g_max, lane_f, big), axis=-1, keepdims=True) - N_EXPERTS
    p_top = 1.0 / jnp.sum(jnp.exp(lg - g_max), axis=-1, keepdims=True)

    in_group = (lane // EXPERTS_PER_GROUP).astype(_F32) == g_idx
    le = jnp.where(jnp.logical_and(in_group, lane < N_EXPERTS), logits, neg)
    m1 = jnp.max(le, axis=-1, keepdims=True)
    i1 = jnp.min(jnp.where(le == m1, lane_f, big), axis=-1, keepdims=True)
    le2 = jnp.where(lane_f == i1, neg, le)
    m2 = jnp.max(le2, axis=-1, keepdims=True)
    i2 = jnp.min(jnp.where(le2 == m2, lane_f, big), axis=-1, keepdims=True)
    e2 = jnp.exp(m2 - m1)
    w1 = p_top / (1.0 + e2)
    w2 = p_top * e2 / (1.0 + e2)

    o1 = jnp.where(lane_f == i1, 1.0, 0.0)
    o2 = jnp.where(lane_f == i2, 1.0, 0.0)
    lane2 = lax.broadcasted_iota(_I32, (x1.shape[0], 2 * LANES), 1).astype(_F32)
    o12 = jnp.where(jnp.logical_or(lane2 == i1, lane2 == i2 + LANES), 1.0, 0.0).astype(_BF16)
    before = jnp.dot(lstrict_ref[...], o12, preferred_element_type=_F32)
    before1 = before[:, :LANES]
    before2 = before[:, LANES:]
    n1 = jnp.sum(o1, axis=0, keepdims=True)
    n2 = jnp.sum(o2, axis=0, keepdims=True)
    n_tiles16 = jnp.floor((n1 + n2 + (BF16_ROWS - 1)) * (1.0 / BF16_ROWS))
    npad = n_tiles16 * BF16_ROWS
    seg_start = jnp.dot(jnp.broadcast_to(n_tiles16, (SUBLANES, LANES)).astype(_BF16), ustrict_ref[...],
                        preferred_element_type=_F32)[0:1] * BF16_ROWS
    s1 = jnp.sum(o1 * (seg_start + before1), axis=-1, keepdims=True)
    s2 = jnp.sum(o2 * (seg_start + n1 + before2), axis=-1, keepdims=True)
    route = jnp.where(lane == 0, s1, jnp.where(lane == 1, s2, jnp.where(lane == 2, w1, jnp.where(lane == 3, w2, 0.0))))
    return h2.astype(_BF16), route, npad


def _store_route(x1, route_refs, out_refs):
    h2_ref, route_ref, routet_ref, npad_ref = out_refs
    h2, route, npad = _route_tile(x1, *route_refs)
    h2_ref[...] = h2
    route_ref[...] = route
    routet_ref[0] = jnp.transpose(route)[0:SUBLANES, :]
    npad_ref[0] = npad


def _prompt_mixer_kernel(precise_tail, x_ref, gmix_ref, win_ref, caw_ref, pw_ref, ps_ref, ccw_ref, ccb_ref, lncg_ref,
                         lncb_ref, lndg_ref, lndb_ref, sgw_ref, sgb_ref, wout_ref, gffn_ref, rwh_ref, rw2_ref, rb_ref,
                         lstrict_ref, ustrict_ref, winl_ref, pwl_ref, sgwl_ref, woutl_ref,
                         x1_ref, sa_ref, sp_ref, sc_ref, h2_ref, route_ref, routet_ref, npad_ref,
                         z_ref, exta_ref, extp_ref, extc_ref, dpool_ref, vn_ref, mix_ref, mixf_ref, dpoolf_ref):
    t = pl.program_id(1)
    n_t = pl.num_programs(1)
    tt = x_ref.shape[0]
    tail = slice(tt - CHUNK, tt)
    is_last = t == n_t - 1

    @pl.when(t == 0)
    def _():
        zeros = jnp.zeros((HALO, W_GROUP), _F32)
        exta_ref[0:HALO, :] = zeros
        extp_ref[0:HALO, :] = zeros
        extc_ref[0:HALO, :] = zeros

    def col(k):
        return slice(k * W_GROUP, (k + 1) * W_GROUP)

    stripes = [slice(s0, s0 + MIX_STRIPE) for s0 in range(0, tt, MIX_STRIPE)]
    for rows in stripes:
        h = _rmsnorm(x_ref[rows, :], gmix_ref[...])
        z_ref[rows, :] = jnp.dot(h.astype(_BF16), win_ref[...], preferred_element_type=_F32)
    if precise_tail:
        @pl.when(is_last)
        def _():
            z_ref[tail, :] = _dot_split(_rmsnorm(x_ref[tail, :], gmix_ref[...]), win_ref, winl_ref)

    win = _pool_windows((ROW_BLK, W_GROUP))
    row_iota = lax.broadcasted_iota(_I32, (ROW_BLK, W_GROUP), 0)
    low_group = lax.broadcasted_iota(_I32, (ROW_BLK + 2 * SUBLANES, LANES), 1) < POOL_CH
    lane = lax.broadcasted_iota(_I32, (CHUNK, W_GROUP), 1)

    def mix_row_block(r0):
        rows = slice(r0, r0 + ROW_BLK)
        ext_rows = slice(HALO + r0, HALO + r0 + ROW_BLK)
        exta_ref[ext_rows, :] = z_ref[rows, col(1)] * z_ref[rows, col(2)]
        extp_ref[ext_rows, :] = z_ref[rows, col(3)]
        extc_ref[ext_rows, :] = z_ref[rows, col(4)] * jax.nn.sigmoid(z_ref[rows, col(5)])

        ua = exta_ref[HALO + r0 - SUBLANES:HALO + r0 + ROW_BLK, :]
        conv_a = caw_ref[CONV_A - 1:CONV_A, :] * ua
        for k in range(CONV_A - 1):
            conv_a = conv_a + caw_ref[k:k + 1, :] * _rows_back(ua, CONV_A - 1 - k)
        y_a = z_ref[rows, col(0)] * conv_a[SUBLANES:, :]
        mix_ref[rows, col(0)] = y_a.astype(_BF16)

        pe = extp_ref[HALO + r0 - 2 * SUBLANES:HALO + r0 + ROW_BLK, :]
        s2 = pe + _rows_back(pe, 1)
        s4 = s2 + _rows_back(s2, 2)
        s4_hi = s4[:, LANES:]
        s8 = s4_hi + _rows_back(s4_hi, 4)
        s16 = s8 + _rows_back(s8, 8)
        sums = jnp.concatenate([jnp.where(low_group, s2[:, :LANES], s4[:, :LANES]), jnp.where(low_group, s8, s16)],
                               axis=1)[2 * SUBLANES:, :]
        pos = t * tt + r0 + row_iota
        cnt = jnp.minimum(pos + 1, win).astype(_F32)
        d_pool = sums / cnt - pe[2 * SUBLANES:, :]
        dpool_ref[rows, :] = d_pool.astype(_BF16)

        halves = []
        for hc in range(W_GROUP // LANES):
            lanes = slice(hc * LANES, (hc + 1) * LANES)
            xe = extc_ref[HALO + r0 - HALO:HALO + r0 + ROW_BLK, lanes]
            conv_c = None
            for r in range(SUBLANES):
                xr = _rows_ahead(xe, r)
                for a in range(HALO // SUBLANES + 1):
                    k = SUBLANES * a + r - (HALO - (CONV_C - 1))
                    if 0 <= k < CONV_C:
                        term = ccw_ref[k:k + 1, lanes] * xr[SUBLANES * a:SUBLANES * a + ROW_BLK, :]
                        conv_c = term if conv_c is None else conv_c + term
            halves.append(conv_c)
        y_c = _layernorm(jnp.concatenate(halves, axis=1) + ccb_ref[...], lncg_ref[...], lncb_ref[...])
        y_c = _silu(y_c)
        mix_ref[rows, col(2)] = y_c.astype(_BF16)

        vn_ref[rows, :] = _layernorm(z_ref[rows, col(7)], lndg_ref[...], lndb_ref[...])

        if precise_tail and r0 >= tt - CHUNK:
            tail_rows = slice(r0 - (tt - CHUNK), r0 - (tt - CHUNK) + ROW_BLK)
            mixf_ref[tail_rows, col(0)] = y_a
            mixf_ref[tail_rows, col(2)] = y_c
            dpoolf_ref[tail_rows, :] = d_pool

    for rows in stripes:
        for r0 in range(rows.start, rows.stop, ROW_BLK):
            mix_row_block(r0)

        y_p = jnp.dot(dpool_ref[rows, :], pw_ref[...], preferred_element_type=_F32) * ps_ref[...]
        mix_ref[rows, col(1)] = y_p.astype(_BF16)

        for c0 in range(rows.start, rows.stop, CHUNK):
            chunk = slice(c0, c0 + CHUNK)
            vn_c = vn_ref[chunk, :]
            mixed = sgb_ref[...]
            for hd in range(N_HEADS_D):
                vm = jnp.where(lane // HEAD_D == hd, vn_c, 0.0).astype(_BF16)
                mixed = mixed + jnp.dot(sgw_ref[hd], vm, preferred_element_type=_F32)
            mix_ref[chunk, col(3)] = (z_ref[chunk, col(6)] * mixed).astype(_BF16)

        x1_ref[rows, :] = x_ref[rows, :] + jnp.dot(mix_ref[rows, :], wout_ref[...], preferred_element_type=_F32)

    if precise_tail:
        @pl.when(is_last)
        def _():
            mixf_ref[:, col(1)] = _dot_split(dpoolf_ref[...], pw_ref, pwl_ref) * ps_ref[...]
            vn_c = vn_ref[tail, :]
            mixed = sgb_ref[...]
            for hd in range(N_HEADS_D):
                vm_hi, vm_lo = _split_bf16(jnp.where(lane // HEAD_D == hd, vn_c, 0.0))
                mixed = (mixed + jnp.dot(sgw_ref[hd], vm_hi, preferred_element_type=_F32)
                         + jnp.dot(sgw_ref[hd], vm_lo, preferred_element_type=_F32)
                         + jnp.dot(sgwl_ref[hd], vm_hi, preferred_element_type=_F32))
            mixf_ref[:, col(3)] = z_ref[tail, col(6)] * mixed
            x1_ref[tail, :] = x_ref[tail, :] + _dot_split(mixf_ref[...], wout_ref, woutl_ref)

    _store_route(x1_ref[...], (gffn_ref, rwh_ref, rw2_ref, rb_ref, lstrict_ref, ustrict_ref),
                 (h2_ref, route_ref, routet_ref, npad_ref))

    @pl.when(is_last)
    def _():
        end = HALO + tt
        sa_ref[0] = exta_ref[end - (CONV_A - 1):end, :]
        sp_ref[0] = extp_ref[end - POOL_STATE:end, :]
        sc_ref[0] = extc_ref[end - (CONV_C - 1):end, :]

    exta_ref[0:HALO, :] = exta_ref[tt:tt + HALO, :]
    extp_ref[0:HALO, :] = extp_ref[tt:tt + HALO, :]
    extc_ref[0:HALO, :] = extc_ref[tt:tt + HALO, :]


def _route_out_shapes(n_tiles):
    n_tok = n_tiles * TOK_TILE
    return [jax.ShapeDtypeStruct((n_tok, D_MODEL), _BF16),
            jax.ShapeDtypeStruct((n_tok, LANES), _F32),
            jax.ShapeDtypeStruct((n_tiles, SUBLANES, TOK_TILE), _F32),
            jax.ShapeDtypeStruct((n_tiles, 1, LANES), _F32)]


def _route_out_specs(tile_of):
    return [pl.BlockSpec((TOK_TILE, D_MODEL), lambda *g: (tile_of(*g), 0)),
            pl.BlockSpec((TOK_TILE, LANES), lambda *g: (tile_of(*g), 0)),
            pl.BlockSpec((1, SUBLANES, TOK_TILE), lambda *g: (tile_of(*g), 0, 0)),
            pl.BlockSpec((1, 1, LANES), lambda *g: (tile_of(*g), 0, 0))]


def _mixer_consts(lw, sgu_w, sgu_b):
    return [lw["g_mix"], lw["w_in"], lw["conv_a_w"], lw["pool_w_bd"], lw["pool_scale"], lw["conv_c_w"],
            lw["conv_c_b"], lw["ln_c_g"], lw["ln_c_b"], lw["ln_d_g"], lw["ln_d_b"], lw[sgu_w], lw[sgu_b], lw["w_out"],
            lw["g_ffn"], lw["router_w_hi"], lw["router_w_both"], lw["router_b"], lw["lstrict"], lw["ustrict"]]


def _prompt_mixer(x, bsz, lw, precise_tail):
    seq = x.shape[0] // bsz
    n_t = seq // TOK_TILE
    consts = _mixer_consts(lw, "sgu_w_tril", "sgu_bias_rows") + [lw["w_in_lo"], lw["pool_w_bd_lo"],
                                                                 lw["sgu_w_tril_lo"], lw["w_out_lo"]]
    tile_of = lambda b, t: b * n_t + t
    tile_spec = pl.BlockSpec((TOK_TILE, D_MODEL), lambda b, t: (tile_of(b, t), 0))

    def state_spec(rows):
        return pl.BlockSpec((1, rows, W_GROUP), lambda b, t: (b, 0, 0))

    return pl.pallas_call(
        functools.partial(_prompt_mixer_kernel, precise_tail),
        grid=(bsz, n_t),
        in_specs=[tile_spec] + [_const_spec(c.shape) for c in consts],
        out_specs=[tile_spec, state_spec(CONV_A - 1), state_spec(POOL_STATE), state_spec(CONV_C - 1)]
        + _route_out_specs(tile_of),
        out_shape=[jax.ShapeDtypeStruct((bsz * seq, D_MODEL), _F32),
                   jax.ShapeDtypeStruct((bsz, CONV_A - 1, W_GROUP), _F32),
                   jax.ShapeDtypeStruct((bsz, POOL_STATE, W_GROUP), _F32),
                   jax.ShapeDtypeStruct((bsz, CONV_C - 1, W_GROUP), _F32)] + _route_out_shapes(bsz * n_t),
        scratch_shapes=[pltpu.VMEM((TOK_TILE, IN_COLS), _F32),
                        pltpu.VMEM((HALO + TOK_TILE, W_GROUP), _F32),
                        pltpu.VMEM((HALO + TOK_TILE, W_GROUP), _F32),
                        pltpu.VMEM((HALO + TOK_TILE, W_GROUP), _F32),
                        pltpu.VMEM((TOK_TILE, W_GROUP), _BF16),
                        pltpu.VMEM((TOK_TILE, W_GROUP), _F32),
                        pltpu.VMEM((TOK_TILE, D_MODEL), _BF16),
                        pltpu.VMEM((CHUNK, D_MODEL), _F32),
                        pltpu.VMEM((CHUNK, W_GROUP), _F32)],
        compiler_params=pltpu.CompilerParams(dimension_semantics=("arbitrary", "arbitrary"),
                                             vmem_limit_bytes=VMEM_LIMIT),
        name="prompt_mixer",
    )(x, *consts)


def _sample_mixer_kernel(batch_major_in, x_ref, sta_ref, stp_ref, stc_ref, gmix_ref, win_ref, caw_ref, pw_ref, ps_ref,
                         ccw_ref, ccb_ref, lncg_ref, lncb_ref, lndg_ref, lndb_ref, sgw_ref, sgb_ref, wout_ref,
                         gffn_ref, rwh_ref, rw2_ref, rb_ref, lstrict_ref, ustrict_ref,
                         x1_ref, nsa_ref, nsp_ref, nsc_ref, vrow_ref, h2_ref, route_ref, routet_ref, npad_ref,
                         xt_ref, z_ref, exta_ref, extp_ref, extc_ref, dpool_ref, vn_ref, mix_ref):
    nb = sta_ref.shape[0]
    n_tok = x1_ref.shape[0]
    n_t = n_tok // nb

    def col(k):
        return slice(k * W_GROUP, (k + 1) * W_GROUP)

    def slab(j, n=1):
        return slice(j * nb, (j + n) * nb)

    if batch_major_in:
        for tstep in range(n_t):
            xt_ref[slab(tstep), :] = x_ref[:, tstep * D_MODEL:(tstep + 1) * D_MODEL]
    else:
        xt_ref[...] = x_ref[...]

    h = _rmsnorm(xt_ref[...], gmix_ref[...]).astype(_BF16)
    z_ref[...] = jnp.dot(h, win_ref[...], preferred_element_type=_F32)

    for j in range(CONV_A - 1):
        exta_ref[slab(j), :] = sta_ref[:, col(j)]
    for j in range(POOL_STATE):
        extp_ref[slab(j), :] = stp_ref[:, col(j)]
    for j in range(CONV_C - 1):
        extc_ref[slab(j), :] = stc_ref[:, col(j)]
    for tstep in range(n_t):
        rows = slab(tstep)
        exta_ref[slab(CONV_A - 1 + tstep), :] = z_ref[rows, col(1)] * z_ref[rows, col(2)]
        extp_ref[slab(POOL_STATE + tstep), :] = z_ref[rows, col(3)]
        extc_ref[slab(CONV_C - 1 + tstep), :] = z_ref[rows, col(4)] * jax.nn.sigmoid(z_ref[rows, col(5)])

    win = _pool_windows((nb, W_GROUP))
    for tstep in range(n_t):
        rows = slab(tstep)
        conv_a = None
        for k in range(CONV_A):
            term = caw_ref[k:k + 1, :] * exta_ref[slab(tstep + k), :]
            conv_a = term if conv_a is None else conv_a + term
        mix_ref[rows, col(0)] = (z_ref[rows, col(0)] * conv_a).astype(_BF16)

        p_cur = extp_ref[slab(POOL_STATE + tstep), :]
        acc = p_cur
        for j in range(1, POOL_STATE + 1):
            acc = acc + jnp.where(win > j, extp_ref[slab(POOL_STATE + tstep - j), :], 0.0)
        cnt = jnp.minimum(PAST_LEN + tstep + 1, win).astype(_F32)
        dpool_ref[rows, :] = (acc / cnt - p_cur).astype(_BF16)

        conv_c = None
        for k in range(CONV_C):
            term = ccw_ref[k:k + 1, :] * extc_ref[slab(tstep + k), :]
            conv_c = term if conv_c is None else conv_c + term
        y_c = _layernorm(conv_c + ccb_ref[...], lncg_ref[...], lncb_ref[...])
        mix_ref[rows, col(2)] = _silu(y_c).astype(_BF16)

        v_n = _layernorm(z_ref[rows, col(7)], lndg_ref[...], lndb_ref[...])
        vn_ref[rows, :] = v_n
        vrow_ref[:, col(tstep)] = v_n

    y_p = jnp.dot(dpool_ref[...], pw_ref[...], preferred_element_type=_F32) * ps_ref[...]
    mix_ref[:, col(1)] = y_p.astype(_BF16)

    for i in range(n_t):
        mixed = sgb_ref[i:i + 1, :] + sgw_ref[i * n_t:i * n_t + 1, :] * vn_ref[slab(0), :]
        for j in range(1, i + 1):
            mixed = mixed + sgw_ref[i * n_t + j:i * n_t + j + 1, :] * vn_ref[slab(j), :]
        mix_ref[slab(i), col(3)] = (z_ref[slab(i), col(6)] * mixed).astype(_BF16)

    x1 = xt_ref[...] + jnp.dot(mix_ref[...], wout_ref[...], preferred_element_type=_F32)
    x1_ref[...] = x1
    _store_route(x1, (gffn_ref, rwh_ref, rw2_ref, rb_ref, lstrict_ref, ustrict_ref),
                 (h2_ref, route_ref, routet_ref, npad_ref))

    for j in range(CONV_A - 1):
        nsa_ref[:, col(j)] = exta_ref[slab(n_t + j), :]
    for j in range(POOL_STATE):
        nsp_ref[:, col(j)] = extp_ref[slab(n_t + j), :]
    for j in range(CONV_C - 1):
        nsc_ref[:, col(j)] = extc_ref[slab(n_t + j), :]


def _sample_mixer(x, st_a, st_p, st_c, lw, n_t, batch_major_in):
    n_seq = st_a.shape[0]
    nb = SAMPLE_SEQ_BLK
    n_blk = n_seq // nb
    n_tok = nb * n_t
    assert n_tok == TOK_TILE
    consts = _mixer_consts(lw, "sgu_w_rows", "sgu_b_rows")

    def seq_spec(width):
        return pl.BlockSpec((nb, width), lambda i: (i, 0))

    tok_spec = pl.BlockSpec((n_tok, D_MODEL), lambda i: (i, 0))
    state_widths = [(CONV_A - 1) * W_GROUP, POOL_STATE * W_GROUP, (CONV_C - 1) * W_GROUP]
    x_spec = seq_spec(n_t * D_MODEL) if batch_major_in else tok_spec
    return pl.pallas_call(
        functools.partial(_sample_mixer_kernel, batch_major_in),
        grid=(n_blk,),
        in_specs=[x_spec] + [seq_spec(w) for w in state_widths] + [_const_spec(c.shape) for c in consts],
        out_specs=[tok_spec] + [seq_spec(w) for w in state_widths] + [seq_spec(n_t * W_GROUP)]
        + _route_out_specs(lambda i: i),
        out_shape=[jax.ShapeDtypeStruct((n_blk * n_tok, D_MODEL), _F32)]
        + [jax.ShapeDtypeStruct((n_seq, w), _F32) for w in state_widths]
        + [jax.ShapeDtypeStruct((n_seq, n_t * W_GROUP), _F32)] + _route_out_shapes(n_blk),
        scratch_shapes=[pltpu.VMEM((n_tok, D_MODEL), _F32),
                        pltpu.VMEM((n_tok, IN_COLS), _F32),
                        pltpu.VMEM(((CONV_A - 1 + n_t) * nb, W_GROUP), _F32),
                        pltpu.VMEM(((POOL_STATE + n_t) * nb, W_GROUP), _F32),
                        pltpu.VMEM(((CONV_C - 1 + n_t) * nb, W_GROUP), _F32),
                        pltpu.VMEM((n_tok, W_GROUP), _BF16),
                        pltpu.VMEM((n_tok, W_GROUP), _F32),
                        pltpu.VMEM((n_tok, D_MODEL), _BF16)],
        compiler_params=pltpu.CompilerParams(dimension_semantics=("arbitrary",), vmem_limit_bytes=VMEM_LIMIT),
        name="sample_mixer",
    )(x, st_a, st_p, st_c, *consts)


def _plan_kernel(np_ref, ustrict_ref, dest_ref, tab_ref, used_ref, npx_ref, toff_ref, zc_ref):
    n_tiles = np_ref.shape[0]
    nt_pad = npx_ref.shape[0]
    zeros = jnp.zeros((nt_pad, LANES), _F32)
    npx_ref[...] = zeros
    toff_ref[...] = zeros
    zc_ref[...] = zeros
    npx_ref[0:n_tiles, :] = np_ref[...]
    np_all = npx_ref[...]

    tile_row = lax.broadcasted_iota(_I32, (nt_pad, 1), 0)
    n_real = jnp.sum(np_all, axis=-1, keepdims=True) * (1.0 / BF16_ROWS)
    n_zero = jnp.where(tile_row < n_tiles, N_CHUNK - n_real, 0.0)

    run = jnp.zeros((1, LANES), _F32)
    zrun = jnp.zeros((1, LANES), _F32)
    for i in range(n_tiles):
        toff_ref[i:i + 1, :] = run
        zc_ref[i:i + 1, :] = zrun
        run = run + npx_ref[i:i + 1, :]
        zrun = zrun + n_zero[i:i + 1, :]
    rows_e = run
    rows_pad = jnp.ceil(rows_e * (1.0 / GMM_TILE)) * GMM_TILE
    gap = (rows_pad - rows_e) * (1.0 / BF16_ROWS)

    def excl_lanes(v):
        return jnp.dot(v.astype(_BF16), ustrict_ref[...], preferred_element_type=_F32)

    gstart = excl_lanes(jnp.broadcast_to(rows_pad * (1.0 / GMM_TILE), (SUBLANES, LANES)))[0:1] * GMM_TILE
    gap_start = excl_lanes(jnp.broadcast_to(gap, (SUBLANES, LANES)))[0:1]
    gap_total = jnp.sum(gap, axis=-1, keepdims=True)
    rows_total = jnp.sum(rows_pad, axis=-1, keepdims=True)
    seg_start = excl_lanes(np_all * (1.0 / BF16_ROWS)) * BF16_ROWS
    delta = gstart + toff_ref[...] - seg_start

    chunk = lax.broadcasted_iota(_I32, (nt_pad, LANES), 1).astype(_F32)
    pos = chunk * BF16_ROWS
    q = zc_ref[...] + (chunk - n_real)
    real = pos
    gap_addr = q * BF16_ROWS
    for e in range(N_EXPERTS):
        ss = seg_start[:, e:e + 1]
        se = ss + np_all[:, e:e + 1]
        real = real + jnp.where(jnp.logical_and(ss <= pos, pos < se), delta[:, e:e + 1], 0.0)
        gs = gap_start[:, e:e + 1]
        ge = gs + gap[:, e:e + 1]
        base = gstart[:, e:e + 1] + rows_e[:, e:e + 1] - gs * BF16_ROWS
        gap_addr = gap_addr + jnp.where(jnp.logical_and(gs <= q, q < ge), base, 0.0)
    tail_addr = rows_total + (q - gap_total) * BF16_ROWS
    zero_addr = jnp.where(q < gap_total, gap_addr, tail_addr)
    dest = jnp.where(chunk < n_real, real, zero_addr)
    dest_ref[...] = dest[0:n_tiles, :].astype(_I32)
    used_ref[...] = jnp.broadcast_to(n_real * BF16_ROWS, (nt_pad, LANES))[0:n_tiles, :].astype(_I32)

    n_cols = tab_ref.shape[1]
    row_pos = lax.broadcasted_iota(_I32, (SUBLANES, n_cols), 1).astype(_F32) * GMM_TILE
    t_exp = jnp.zeros((SUBLANES, n_cols), _F32)
    t_val = jnp.zeros((SUBLANES, n_cols), _F32)
    for e in range(N_EXPERTS):
        gs = gstart[:, e:e + 1]
        t_exp = t_exp + jnp.where(gs + rows_pad[:, e:e + 1] <= row_pos, 1.0, 0.0)
        t_val = t_val + jnp.where(jnp.logical_and(gs <= row_pos, row_pos < gs + rows_e[:, e:e + 1]), 1.0, 0.0)
    t_exp = jnp.minimum(t_exp, N_EXPERTS - 1.0)
    n_valid = jnp.sum(t_val, axis=-1, keepdims=True)
    n_groups = jnp.ceil(n_valid * (1.0 / GMM_GROUP))
    s_idx = jnp.minimum(row_pos * (1.0 / GMM_TILE), n_groups - 1.0)
    sub = lax.broadcasted_iota(_I32, (SUBLANES, n_cols), 0)
    tab_ref[...] = jnp.where(sub == 0, t_exp, jnp.where(sub == 1, s_idx, 0.0)).astype(_I32)


def _plan(npad_all, lw, n_gmm_tiles):
    n_tiles = npad_all.shape[0]
    nt_pad = -(-n_tiles // SUBLANES) * SUBLANES
    n_cols = -(-n_gmm_tiles // LANES) * LANES
    out_shape = [jax.ShapeDtypeStruct((n_tiles, LANES), _I32), jax.ShapeDtypeStruct((SUBLANES, n_cols), _I32),
                 jax.ShapeDtypeStruct((n_tiles, LANES), _I32)]
    return pl.pallas_call(
        _plan_kernel,
        grid=(1,),
        in_specs=[_const_spec(npad_all.shape), _const_spec(lw["ustrict"].shape)],
        out_specs=[_const_spec(s.shape) for s in out_shape],
        out_shape=out_shape,
        scratch_shapes=[pltpu.VMEM((nt_pad, LANES), _F32)] * 3,
        name="moe_plan",
    )(npad_all, lw["ustrict"])


def _sort_kernel(n_prompt_tiles, dest_ref, used_ref, h2p_ref, h2s_ref, rtp_ref, rts_ref, xs_ref, buf_ref, sem_ref):
    i = pl.program_id(0)
    n = pl.num_programs(0)
    cur = lax.rem(i, 2)
    is_p = i < n_prompt_tiles
    h2 = jnp.where(is_p, h2p_ref[...], h2s_ref[...])
    rt = jnp.where(is_p, rtp_ref[0], rts_ref[0])
    s1 = rt[0:1, :]
    s2 = rt[1:2, :]

    def chunk_copy(tile, c, slot):
        dst = pl.multiple_of(dest_ref[tile * N_CHUNK + c], BF16_ROWS)
        return pltpu.make_async_copy(buf_ref.at[slot, pl.ds(c * BF16_ROWS, BF16_ROWS), :],
                                     xs_ref.at[pl.ds(dst, BF16_ROWS), :], sem_ref.at[slot])

    @pl.when(i < 2)
    def _():
        buf_ref[cur, SLOTS:SLOT_BUF, :] = jnp.zeros((SLOT_BUF - SLOTS, D_MODEL), _BF16)

    def sort_slots(lo, hi):
        slot_id = (lo + lax.broadcasted_iota(_I32, (hi - lo, TOK_TILE), 0)).astype(_F32)
        perm = jnp.where(jnp.logical_or(slot_id == s1, slot_id == s2), 1.0, 0.0).astype(_BF16)
        buf_ref[cur, lo:hi, :] = jnp.dot(perm, h2, preferred_element_type=_F32).astype(_BF16)

    def start_chunks(lo, hi):
        for c in range(lo // BF16_ROWS, hi // BF16_ROWS):
            chunk_copy(i, c, cur).start()

    start_chunks(SLOTS, SLOT_BUF)
    n_grp = SLOTS // SLOT_GRP
    for g in range(n_grp - 1):
        sort_slots(g * SLOT_GRP, (g + 1) * SLOT_GRP)
        start_chunks(g * SLOT_GRP, (g + 1) * SLOT_GRP)
    last_used = used_ref[i * LANES] > (n_grp - 1) * SLOT_GRP

    @pl.when(last_used)
    def _():
        sort_slots((n_grp - 1) * SLOT_GRP, SLOTS)

    @pl.when(jnp.logical_not(last_used))
    def _():
        buf_ref[cur, (n_grp - 1) * SLOT_GRP:SLOTS, :] = jnp.zeros((SLOT_GRP, D_MODEL), _BF16)

    start_chunks((n_grp - 1) * SLOT_GRP, SLOTS)

    @pl.when(i > 0)
    def _():
        for c in range(N_CHUNK):
            chunk_copy(i - 1, c, 1 - cur).wait()

    @pl.when(i == n - 1)
    def _():
        for c in range(N_CHUNK):
            chunk_copy(i, c, cur).wait()


def _sort(dest_flat, used_flat, h2p, h2s, rtp, rts):
    n_p = h2p.shape[0] // TOK_TILE
    n_s = h2s.shape[0] // TOK_TILE
    n_tiles = n_p + n_s
    p_idx = lambda i: jnp.minimum(i, n_p - 1)
    s_idx = lambda i: jnp.maximum(i - n_p, 0)
    return pl.pallas_call(
        functools.partial(_sort_kernel, n_p),
        grid_spec=pltpu.PrefetchScalarGridSpec(
            num_scalar_prefetch=2,
            grid=(n_tiles,),
            in_specs=[pl.BlockSpec((TOK_TILE, D_MODEL), lambda i, d, u: (p_idx(i), 0)),
                      pl.BlockSpec((TOK_TILE, D_MODEL), lambda i, d, u: (s_idx(i), 0)),
                      pl.BlockSpec((1, SUBLANES, TOK_TILE), lambda i, d, u: (p_idx(i), 0, 0)),
                      pl.BlockSpec((1, SUBLANES, TOK_TILE), lambda i, d, u: (s_idx(i), 0, 0))],
            out_specs=pl.BlockSpec(memory_space=pl.ANY),
            scratch_shapes=[pltpu.VMEM((2, SLOT_BUF, D_MODEL), _BF16), pltpu.SemaphoreType.DMA((2,))],
        ),
        out_shape=jax.ShapeDtypeStruct((n_tiles * SLOT_BUF, D_MODEL), _BF16),
        compiler_params=pltpu.CompilerParams(dimension_semantics=("arbitrary",), vmem_limit_bytes=VMEM_LIMIT),
        name="moe_sort",
    )(dest_flat, used_flat, h2p, h2s, rtp, rts)


def _gmm_kernel(texp_ref, sidx_ref, xs_ref, *refs):
    w_refs, ys_ref = refs[:-1], refs[-1]
    s = pl.program_id(0)

    @pl.when(sidx_ref[s] == s)
    def _():
        blk = GMM_TILE // GMM_ROW_SPLIT
        chains = [(j, slice(j * GMM_TILE + k * blk, j * GMM_TILE + (k + 1) * blk))
                  for j in range(GMM_GROUP) for k in range(GMM_ROW_SPLIT)]
        w_gu = [jnp.concatenate([w_refs[3 * j][0, 0].astype(_BF16), w_refs[3 * j + 1][0, 0].astype(_BF16)], axis=1)
                for j in range(GMM_GROUP)]
        w_d = [w_refs[3 * j + 2][0, 0].astype(_BF16) for j in range(GMM_GROUP)]
        gate_up = [jnp.dot(xs_ref[rows, :], w_gu[j], preferred_element_type=_F32) for j, rows in chains]
        for (j, rows), gu in zip(chains, gate_up):
            act = (_silu(gu[:, :D_FF_EXPERT]) * gu[:, D_FF_EXPERT:]).astype(_BF16)
            ys_ref[rows, :] = jnp.dot(act, w_d[j], preferred_element_type=_F32).astype(_BF16)


def _gmm(t_exp, s_idx, xs, layer, w_gate, w_up, w_down):
    n_steps = xs.shape[0] // (GMM_GROUP * GMM_TILE)
    group_spec = pl.BlockSpec((GMM_GROUP * GMM_TILE, D_MODEL), lambda s, te, si: (si[s], 0))
    w_specs, w_args = [], []
    for j in range(GMM_GROUP):
        expert = lambda s, te, si, j=j: (layer, te[GMM_GROUP * s + j], 0, 0)
        w_specs += [pl.BlockSpec((1, 1, D_MODEL, D_FF_EXPERT), expert),
                    pl.BlockSpec((1, 1, D_MODEL, D_FF_EXPERT), expert),
                    pl.BlockSpec((1, 1, D_FF_EXPERT, D_MODEL), expert)]
        w_args += [w_gate, w_up, w_down]
    return pl.pallas_call(
        _gmm_kernel,
        grid_spec=pltpu.PrefetchScalarGridSpec(
            num_scalar_prefetch=2,
            grid=(n_steps,),
            in_specs=[group_spec] + w_specs,
            out_specs=group_spec,
        ),
        out_shape=jax.ShapeDtypeStruct(xs.shape, _BF16),
        input_output_aliases={2: 0},
        compiler_params=pltpu.CompilerParams(dimension_semantics=("arbitrary",), vmem_limit_bytes=VMEM_LIMIT),
        name="moe_experts",
    )(t_exp, s_idx, xs, *w_args)


def _combine_kernel(n_prompt_tiles, final_norm, dest_ref, used_ref, x1p_ref, x1s_ref, rp_ref, rs_ref, gfin_ref, ys_ref,
                    outp_ref, outs_ref, ybuf_ref, sem_ref):
    i = pl.program_id(0)
    n = pl.num_programs(0)
    cur = lax.rem(i, 2)
    is_p = i < n_prompt_tiles

    def chunk_copy(tile, c, slot):
        src = pl.multiple_of(dest_ref[tile * N_CHUNK + c], BF16_ROWS)
        return pltpu.make_async_copy(ys_ref.at[pl.ds(src, BF16_ROWS), :],
                                     ybuf_ref.at[slot, pl.ds(c * BF16_ROWS, BF16_ROWS), :], sem_ref.at[slot])

    @pl.when(i == 0)
    def _():
        for c in range(N_CHUNK_REAL):
            chunk_copy(0, c, 0).start()

    for c in range(N_CHUNK_REAL):
        chunk_copy(i, c, cur).wait()

    nxt = jnp.minimum(i + 1, n - 1)

    def prefetch_chunks(lo, hi):
        for c in range(lo // BF16_ROWS, hi // BF16_ROWS):
            chunk_copy(nxt, c, 1 - cur).start()

    route = jnp.where(is_p, rp_ref[...], rs_ref[...])
    acc = jnp.where(is_p, x1p_ref[...], x1s_ref[...])
    s1 = route[:, 0:1]
    s2 = route[:, 1:2]
    w1 = route[:, 2:3]
    w2 = route[:, 3:4]
    def unperm_dot(lo, hi):
        slot_id = (lo + lax.broadcasted_iota(_I32, (TOK_TILE, hi - lo), 1)).astype(_F32)
        unperm = (jnp.where(slot_id == s1, w1, 0.0) + jnp.where(slot_id == s2, w2, 0.0)).astype(_BF16)
        return jnp.dot(unperm, ybuf_ref[cur, lo:hi, :], preferred_element_type=_F32)

    def finish(y):
        if final_norm:
            y = _rmsnorm(y, gfin_ref[...])

        @pl.when(is_p)
        def _():
            outp_ref[...] = y

        @pl.when(jnp.logical_not(is_p))
        def _():
            outs_ref[...] = y

    n_grp = SLOTS // SLOT_GRP
    for g in range(n_grp - 1):
        acc = acc + unperm_dot(g * SLOT_GRP, (g + 1) * SLOT_GRP)
        prefetch_chunks(g * SLOT_GRP, (g + 1) * SLOT_GRP)
    prefetch_chunks((n_grp - 1) * SLOT_GRP, SLOTS)
    last_used = used_ref[i * LANES] > (n_grp - 1) * SLOT_GRP

    @pl.when(last_used)
    def _():
        finish(acc + unperm_dot((n_grp - 1) * SLOT_GRP, SLOTS))

    @pl.when(jnp.logical_not(last_used))
    def _():
        finish(acc)

    @pl.when(i == n - 1)
    def _():
        for c in range(N_CHUNK_REAL):
            chunk_copy(nxt, c, 1 - cur).wait()


def _combine(dest_flat, used_flat, x1p, x1s, rp, rs, g_fin, ys, final_norm):
    n_p = x1p.shape[0] // TOK_TILE
    n_s = x1s.shape[0] // TOK_TILE
    p_idx = lambda i, d, u: (jnp.minimum(i, n_p - 1), 0)
    s_idx = lambda i, d, u: (jnp.maximum(i - n_p, 0), 0)
    return pl.pallas_call(
        functools.partial(_combine_kernel, n_p, final_norm),
        grid_spec=pltpu.PrefetchScalarGridSpec(
            num_scalar_prefetch=2,
            grid=(n_p + n_s,),
            in_specs=[pl.BlockSpec((TOK_TILE, D_MODEL), p_idx), pl.BlockSpec((TOK_TILE, D_MODEL), s_idx),
                      pl.BlockSpec((TOK_TILE, LANES), p_idx), pl.BlockSpec((TOK_TILE, LANES), s_idx),
                      pl.BlockSpec(g_fin.shape, lambda i, d, u: (0, 0)),
                      pl.BlockSpec(memory_space=pl.ANY)],
            out_specs=[pl.BlockSpec((TOK_TILE, D_MODEL), p_idx), pl.BlockSpec((TOK_TILE, D_MODEL), s_idx)],
            scratch_shapes=[pltpu.VMEM((2, SLOTS, D_MODEL), _BF16), pltpu.SemaphoreType.DMA((2,))],
        ),
        out_shape=[jax.ShapeDtypeStruct(x1p.shape, _F32), jax.ShapeDtypeStruct(x1s.shape, _F32)],
        compiler_params=pltpu.CompilerParams(dimension_semantics=("arbitrary",), vmem_limit_bytes=VMEM_LIMIT),
        name="moe_combine",
    )(dest_flat, used_flat, x1p, x1s, rp, rs, g_fin, ys)


def _moe(x1p, x1s, routing_p, routing_s, lw, g_fin, final_norm):
    h2p, rp, rtp, npp = routing_p
    h2s, rs, rts, nps = routing_s
    n_tiles = npp.shape[0] + nps.shape[0]
    n_gmm = n_tiles * SLOT_BUF // GMM_TILE
    assert n_tiles * (SLOT_BUF - SLOTS) >= N_EXPERTS * (GMM_TILE - BF16_ROWS)
    npad_all = jnp.concatenate([npp, nps], axis=0).reshape(n_tiles, LANES)
    dest, tab, used = _plan(npad_all, lw, n_gmm)
    dest_flat = dest.reshape(-1)
    used_flat = used.reshape(-1)
    xs = _sort(dest_flat, used_flat, h2p, h2s, rtp, rts)
    ys = _gmm(tab[0], tab[1], xs, lw["layer"], *lw["expert_w"])
    return _combine(dest_flat, used_flat, x1p, x1s, rp, rs, g_fin, ys, final_norm)


def _layer_weights(l, g_mix, w_in, conv_a_w, pool_w, pool_scale, conv_c_w, conv_c_b, ln_c_g, ln_c_b, ln_d_g, ln_d_b,
                   sgu_w, sgu_b, w_out, g_ffn, router_group_w, router_group_b, router_expert_w, router_expert_b,
                   expert_w_gate, expert_w_up, expert_w_down, n_t_sample, precise_tail):
    row = lambda v: v[l].reshape(1, -1)
    pool_bd = jnp.zeros((W_GROUP, W_GROUP), _F32)
    for g in range(len(POOL_WINDOWS)):
        sl = slice(g * POOL_CH, (g + 1) * POOL_CH)
        pool_bd = pool_bd.at[sl, sl].set(pool_w[l, g])
    tril = jnp.tril(jnp.ones((CHUNK, CHUNK), dtype=bool))
    sgu_tril = jnp.where(tril, sgu_w[l], 0.0)
    w_small = sgu_tril[:, :n_t_sample, :n_t_sample]
    sgu_w_rows = jnp.repeat(jnp.transpose(w_small, (1, 2, 0)).reshape(n_t_sample * n_t_sample, N_HEADS_D), HEAD_D, axis=1)
    n_route = N_EXPERTS + N_EXPERT_GROUPS
    router_w = jnp.pad(jnp.concatenate([router_expert_w[l], router_group_w[l]], axis=1), ((0, 0), (0, LANES - n_route)))
    router_b = jnp.pad(jnp.concatenate([router_expert_b[l], router_group_b[l]]), (0, LANES - n_route)).reshape(1, LANES)
    router_w_hi, router_w_lo = _weight_split(router_w)
    if precise_tail:
        w_in_hi, w_in_lo = _weight_split(w_in[l])
        w_out_hi, w_out_lo = _weight_split(w_out[l])
        pool_hi, pool_lo = _weight_split(pool_bd)
        sgu_hi, sgu_lo = (s.reshape(sgu_tril.shape) for s in _weight_split(sgu_tril.reshape(-1, CHUNK)))
    else:
        w_in_hi = w_in_lo = w_in[l].astype(_BF16)
        w_out_hi = w_out_lo = w_out[l].astype(_BF16)
        pool_hi = pool_lo = pool_bd.astype(_BF16)
        sgu_hi = sgu_lo = sgu_tril.astype(_BF16)
    return {
        "w_in_lo": w_in_lo, "pool_w_bd_lo": pool_lo, "sgu_w_tril_lo": sgu_lo, "w_out_lo": w_out_lo,
        "g_mix": row(g_mix), "w_in": w_in_hi, "conv_a_w": conv_a_w[l], "pool_w_bd": pool_hi,
        "pool_scale": row(pool_scale), "conv_c_w": conv_c_w[l], "conv_c_b": row(conv_c_b), "ln_c_g": row(ln_c_g),
        "ln_c_b": row(ln_c_b), "ln_d_g": row(ln_d_g), "ln_d_b": row(ln_d_b),
        "sgu_w_tril": sgu_hi,
        "sgu_bias_rows": jnp.repeat(sgu_b[l].T, HEAD_D, axis=1),
        "sgu_w_rows": sgu_w_rows,
        "sgu_b_rows": jnp.repeat(sgu_b[l][:, :n_t_sample].T, HEAD_D, axis=1),
        "w_out": w_out_hi, "g_ffn": row(g_ffn), "router_w_hi": router_w_hi,
        "router_w_both": jnp.concatenate([router_w_hi, router_w_lo], axis=1),
        "router_b": router_b,
        "layer": l, "expert_w": (expert_w_gate, expert_w_up, expert_w_down),
        "lstrict": jnp.tril(jnp.ones((TOK_TILE, TOK_TILE), _F32), -1).astype(_BF16),
        "ustrict": jnp.triu(jnp.ones((LANES, LANES), _F32), 1).astype(_BF16),
    }


def kernel(x_prompt, x_sample, state_conv_a, state_pool, state_conv_c, g_mix, w_in, conv_a_w, pool_w, pool_scale, conv_c_w, conv_c_b, ln_c_g, ln_c_b, ln_d_g, ln_d_b, sgu_w, sgu_b, w_out, g_ffn, router_group_w, router_group_b, router_expert_w, router_expert_b, expert_w_gate, expert_w_up, expert_w_down, g_final):
    depth = g_mix.shape[0]
    bsz, seq, _ = x_prompt.shape
    nb, n_t, _ = x_sample.shape
    g_fin = g_final.reshape(1, -1)

    xp = x_prompt.reshape(bsz * seq, D_MODEL)
    xs = x_sample.reshape(nb, n_t * D_MODEL)
    outs = {k: [] for k in ("sa_p", "sp_p", "sc_p", "sa_s", "sp_s", "sc_s", "v")}
    for l in range(depth):
        precise_tail = l + 1 < depth
        lw = _layer_weights(l, g_mix, w_in, conv_a_w, pool_w, pool_scale, conv_c_w, conv_c_b, ln_c_g, ln_c_b, ln_d_g,
                            ln_d_b, sgu_w, sgu_b, w_out, g_ffn, router_group_w, router_group_b, router_expert_w,
                            router_expert_b, expert_w_gate, expert_w_up, expert_w_down, n_t, precise_tail)
        x1p, sa, sp, sc, *routing_p = _prompt_mixer(xp, bsz, lw, precise_tail)
        outs["sa_p"].append(sa)
        outs["sp_p"].append(sp)
        outs["sc_p"].append(sc)
        x1s, nsa, nsp, nsc, vrow, *routing_s = _sample_mixer(
            xs, state_conv_a[l].reshape(nb, -1), state_pool[l].reshape(nb, -1), state_conv_c[l].reshape(nb, -1),
            lw, n_t, batch_major_in=(l == 0))
        outs["sa_s"].append(nsa.reshape(nb, CONV_A - 1, W_GROUP))
        outs["sp_s"].append(nsp.reshape(nb, POOL_STATE, W_GROUP))
        outs["sc_s"].append(nsc.reshape(nb, CONV_C - 1, W_GROUP))
        outs["v"].append(vrow.reshape(nb, n_t, W_GROUP))
        xp, xs = _moe(x1p, x1s, routing_p, routing_s, lw, g_fin, final_norm=(l == depth - 1))

    y_prompt = xp.reshape(bsz, seq, D_MODEL)
    y_sample = jnp.transpose(xs.reshape(nb // SAMPLE_SEQ_BLK, n_t, SAMPLE_SEQ_BLK, D_MODEL),
                             (0, 2, 1, 3)).reshape(nb, n_t, D_MODEL)
    return (y_prompt, y_sample, jnp.stack(outs["sa_p"]), jnp.stack(outs["sp_p"]), jnp.stack(outs["sc_p"]),
            jnp.stack(outs["sa_s"]), jnp.stack(outs["sp_s"]), jnp.stack(outs["sc_s"]), jnp.stack(outs["v"]))
```

```python
import functools

import jax
import jax.numpy as jnp
from jax import lax
from jax.experimental import pallas as pl
from jax.experimental.pallas import tpu as pltpu

D_MODEL = 1024
W_GROUP = 256
IN_COLS = 8 * W_GROUP
CONV_A = 3
POOL_WINDOWS = (2, 4, 8, 16)
POOL_CH = W_GROUP // len(POOL_WINDOWS)
POOL_STATE = max(POOL_WINDOWS) - 1
CONV_C = 31
CHUNK = 128
N_HEADS_D = 4
HEAD_D = W_GROUP // N_HEADS_D
N_EXPERT_GROUPS = 4
EXPERTS_PER_GROUP = 8
N_EXPERTS = N_EXPERT_GROUPS * EXPERTS_PER_GROUP
TOP_K = 2
D_FF_EXPERT = 128
EPS = 1e-6
PAST_LEN = 16384

LANES = 128
SUBLANES = 8
BF16_ROWS = 16
HALO = 32
ROW_BLK = 64
MIX_STRIPE = 256
TOK_TILE = 512
SAMPLE_SEQ_BLK = 64
GMM_TILE = 512
GMM_GROUP = 1
GMM_ROW_SPLIT = 4
SLOTS = -(-(TOP_K * TOK_TILE + N_EXPERTS * (BF16_ROWS - 1)) // 256) * 256
N_CHUNK_REAL = SLOTS // BF16_ROWS
N_CHUNK = 128
SLOT_BUF = N_CHUNK * BF16_ROWS
SLOT_GRP = 256
VMEM_LIMIT = 56 * 1024 * 1024

_F32 = jnp.float32
_BF16 = jnp.bfloat16
_I32 = jnp.int32
_HI = lax.Precision.HIGHEST


def _rmsnorm(x, g):
    return x * lax.rsqrt(jnp.mean(x * x, axis=-1, keepdims=True) + EPS) * g


def _layernorm(x, g, b):
    mu = jnp.mean(x, axis=-1, keepdims=True)
    xc = x - mu
    var = jnp.mean(xc * xc, axis=-1, keepdims=True)
    return xc * lax.rsqrt(var + EPS) * g + b


def _silu(x):
    return x * jax.nn.sigmoid(x)


def _split_bf16(a):
    bits = lax.bitcast_convert_type(a, jnp.uint32)
    hi = lax.bitcast_convert_type(bits & jnp.uint32(0xFFFF0000), _F32)
    return hi.astype(_BF16), (a - hi).astype(_BF16)


def _dot_split(a, wh_ref, wl_ref):
    a_hi, a_lo = _split_bf16(a)
    return (jnp.dot(a_hi, wh_ref[...], preferred_element_type=_F32)
            + jnp.dot(a_lo, wh_ref[...], preferred_element_type=_F32)
            + jnp.dot(a_hi, wl_ref[...], preferred_element_type=_F32))


def _weight_split_kernel(w_ref, hi_ref, lo_ref):
    w = w_ref[0]
    hi = w.astype(_BF16)
    hi_ref[...] = hi
    lo_ref[...] = (w - hi.astype(_F32)).astype(_BF16)


def _weight_split(w, layer):
    _, rows, cols = w.shape
    blk = min(rows, 256)
    spec = pl.BlockSpec((blk, cols), lambda i: (i, 0))
    return pl.pallas_call(
        _weight_split_kernel,
        grid=(rows // blk,),
        in_specs=[pl.BlockSpec((1, blk, cols), lambda i: (layer, i, 0))],
        out_specs=[spec, spec],
        out_shape=[jax.ShapeDtypeStruct((rows, cols), _BF16)] * 2,
        name="weight_split",
    )(w)


def _rows_back(x, r):
    return pltpu.roll(x, r, axis=0)


def _rows_ahead(x, r):
    return x if r == 0 else pltpu.roll(x, x.shape[0] - r, axis=0)


def _pool_windows(shape):
    lane = lax.broadcasted_iota(_I32, shape, 1)
    return jnp.left_shift(2, lane // POOL_CH)


def _const_spec(shape):
    nd = len(shape)
    return pl.BlockSpec(shape, lambda *_: (0,) * nd)


def _route_tile(x1, gffn_ref, rwh_ref, rw2_ref, rb_ref, lstrict_ref, ustrict_ref):
    h2 = _rmsnorm(x1, gffn_ref[...])
    h_hi, h_lo = _split_bf16(h2)
    hi_both = jnp.dot(h_hi, rw2_ref[...], preferred_element_type=_F32)
    logits = (hi_both[:, :LANES] + hi_both[:, LANES:]
              + jnp.dot(h_lo, rwh_ref[...], preferred_element_type=_F32)) + rb_ref[...]
    lane = lax.broadcasted_iota(_I32, logits.shape, 1)
    lane_f = lane.astype(_F32)
    neg = jnp.float32(-jnp.inf)
    big = jnp.float32(LANES)

    is_group = jnp.logical_and(lane >= N_EXPERTS, lane < N_EXPERTS + N_EXPERT_GROUPS)
    lg = jnp.where(is_group, logits, neg)
    g_max = jnp.max(lg, axis=-1, keepdims=True)
    g_idx = jnp.min(jnp.where(lg == g_max, lane_f, big), axis=-1, keepdims=True) - N_EXPERTS
    p_top = 1.0 / jnp.sum(jnp.exp(lg - g_max), axis=-1, keepdims=True)

    in_group = (lane // EXPERTS_PER_GROUP).astype(_F32) == g_idx
    le = jnp.where(jnp.logical_and(in_group, lane < N_EXPERTS), logits, neg)
    m1 = jnp.max(le, axis=-1, keepdims=True)
    i1 = jnp.min(jnp.where(le == m1, lane_f, big), axis=-1, keepdims=True)
    le2 = jnp.where(lane_f == i1, neg, le)
    m2 = jnp.max(le2, axis=-1, keepdims=True)
    i2 = jnp.min(jnp.where(le2 == m2, lane_f, big), axis=-1, keepdims=True)
    e2 = jnp.exp(m2 - m1)
    w1 = p_top / (1.0 + e2)
    w2 = p_top * e2 / (1.0 + e2)

    o1 = jnp.where(lane_f == i1, 1.0, 0.0)
    o2 = jnp.where(lane_f == i2, 1.0, 0.0)
    lane2 = lax.broadcasted_iota(_I32, (x1.shape[0], 2 * LANES), 1).astype(_F32)
    o12 = jnp.where(jnp.logical_or(lane2 == i1, lane2 == i2 + LANES), 1.0, 0.0).astype(_BF16)
    before = jnp.dot(lstrict_ref[...], o12, preferred_element_type=_F32)
    before1 = before[:, :LANES]
    before2 = before[:, LANES:]
    n1 = jnp.sum(o1, axis=0, keepdims=True)
    n2 = jnp.sum(o2, axis=0, keepdims=True)
    n_tiles16 = jnp.floor((n1 + n2 + (BF16_ROWS - 1)) * (1.0 / BF16_ROWS))
    npad = n_tiles16 * BF16_ROWS
    seg_start = jnp.dot(jnp.broadcast_to(n_tiles16, (SUBLANES, LANES)).astype(_BF16), ustrict_ref[...],
                        preferred_element_type=_F32)[0:1] * BF16_ROWS
    s1 = jnp.sum(o1 * (seg_start + before1), axis=-1, keepdims=True)
    s2 = jnp.sum(o2 * (seg_start + n1 + before2), axis=-1, keepdims=True)
    route = jnp.where(lane == 0, s1, jnp.where(lane == 1, s2, jnp.where(lane == 2, w1, jnp.where(lane == 3, w2, 0.0))))
    return h2.astype(_BF16), route, npad


def _store_route(x1, route_refs, out_refs):
    h2_ref, route_ref, routet_ref, npad_ref = out_refs
    h2, route, npad = _route_tile(x1, *route_refs)
    h2_ref[...] = h2
    route_ref[...] = route
    routet_ref[0] = jnp.transpose(route)[0:SUBLANES, :]
    npad_ref[0] = npad


def _layer_views(layer, row_refs, mat_refs):
    return [r.at[pl.ds(layer, 1)] for r in row_refs] + [r.at[layer] for r in mat_refs]


def _prompt_mixer_kernel(precise_tail, layer, x_ref, gmix_ref, win_ref, caw_ref, pw_ref, ps_ref, ccw_ref, ccb_ref,
                         lncg_ref, lncb_ref, lndg_ref, lndb_ref, sgw_ref, sgb_ref, wout_ref, gffn_ref, rwh_ref, rw2_ref,
                         rb_ref, lstrict_ref, ustrict_ref, winl_ref, pwl_ref, sgwl_ref, woutl_ref,
                         x1_ref, sa_ref, sp_ref, sc_ref, h2_ref, route_ref, routet_ref, npad_ref,
                         z_ref, exta_ref, extp_ref, extc_ref, dpool_ref, vn_ref, mix_ref, mixf_ref, dpoolf_ref):
    (gmix_ref, ps_ref, ccb_ref, lncg_ref, lncb_ref, lndg_ref, lndb_ref, gffn_ref, rb_ref, caw_ref, ccw_ref,
     sgb_ref) = _layer_views(layer, (gmix_ref, ps_ref, ccb_ref, lncg_ref, lncb_ref, lndg_ref, lndb_ref, gffn_ref, rb_ref),
                             (caw_ref, ccw_ref, sgb_ref))
    t = pl.program_id(1)
    n_t = pl.num_programs(1)
    tt = x_ref.shape[0]
    tail = slice(tt - CHUNK, tt)
    is_last = t == n_t - 1

    @pl.when(t == 0)
    def _():
        zeros = jnp.zeros((HALO, W_GROUP), _F32)
        exta_ref[0:HALO, :] = zeros
        extp_ref[0:HALO, :] = zeros
        extc_ref[0:HALO, :] = zeros

    def col(k):
        return slice(k * W_GROUP, (k + 1) * W_GROUP)

    stripes = [slice(s0, s0 + MIX_STRIPE) for s0 in range(0, tt, MIX_STRIPE)]
    for rows in stripes:
        h = _rmsnorm(x_ref[rows, :], gmix_ref[...])
        z_ref[rows, :] = jnp.dot(h.astype(_BF16), win_ref[...], preferred_element_type=_F32)
    if precise_tail:
        @pl.when(is_last)
        def _():
            z_ref[tail, :] = _dot_split(_rmsnorm(x_ref[tail, :], gmix_ref[...]), win_ref, winl_ref)

    win = _pool_windows((ROW_BLK, W_GROUP))
    row_iota = lax.broadcasted_iota(_I32, (ROW_BLK, W_GROUP), 0)
    low_group = lax.broadcasted_iota(_I32, (ROW_BLK + 2 * SUBLANES, LANES), 1) < POOL_CH
    lane = lax.broadcasted_iota(_I32, (CHUNK, W_GROUP), 1)

    def mix_row_block(r0):
        rows = slice(r0, r0 + ROW_BLK)
        ext_rows = slice(HALO + r0, HALO + r0 + ROW_BLK)
        exta_ref[ext_rows, :] = z_ref[rows, col(1)] * z_ref[rows, col(2)]
        extp_ref[ext_rows, :] = z_ref[rows, col(3)]
        extc_ref[ext_rows, :] = z_ref[rows, col(4)] * jax.nn.sigmoid(z_ref[rows, col(5)])

        ua = exta_ref[HALO + r0 - SUBLANES:HALO + r0 + ROW_BLK, :]
        conv_a = caw_ref[CONV_A - 1:CONV_A, :] * ua
        for k in range(CONV_A - 1):
            conv_a = conv_a + caw_ref[k:k + 1, :] * _rows_back(ua, CONV_A - 1 - k)
        y_a = z_ref[rows, col(0)] * conv_a[SUBLANES:, :]
        mix_ref[rows, col(0)] = y_a.astype(_BF16)

        pe = extp_ref[HALO + r0 - 2 * SUBLANES:HALO + r0 + ROW_BLK, :]
        s2 = pe + _rows_back(pe, 1)
        s4 = s2 + _rows_back(s2, 2)
        s4_hi = s4[:, LANES:]
        s8 = s4_hi + _rows_back(s4_hi, 4)
        s16 = s8 + _rows_back(s8, 8)
        sums = jnp.concatenate([jnp.where(low_group, s2[:, :LANES], s4[:, :LANES]), jnp.where(low_group, s8, s16)],
                               axis=1)[2 * SUBLANES:, :]
        pos = t * tt + r0 + row_iota
        cnt = jnp.minimum(pos + 1, win).astype(_F32)
        d_pool = sums / cnt - pe[2 * SUBLANES:, :]
        dpool_ref[rows, :] = d_pool.astype(_BF16)

        halves = []
        for hc in range(W_GROUP // LANES):
            lanes = slice(hc * LANES, (hc + 1) * LANES)
            xe = extc_ref[HALO + r0 - HALO:HALO + r0 + ROW_BLK, lanes]
            conv_c = None
            for r in range(SUBLANES):
                xr = _rows_ahead(xe, r)
                for a in range(HALO // SUBLANES + 1):
                    k = SUBLANES * a + r - (HALO - (CONV_C - 1))
                    if 0 <= k < CONV_C:
                        term = ccw_ref[k:k + 1, lanes] * xr[SUBLANES * a:SUBLANES * a + ROW_BLK, :]
                        conv_c = term if conv_c is None else conv_c + term
            halves.append(conv_c)
        y_c = _layernorm(jnp.concatenate(halves, axis=1) + ccb_ref[...], lncg_ref[...], lncb_ref[...])
        y_c = _silu(y_c)
        mix_ref[rows, col(2)] = y_c.astype(_BF16)

        vn_ref[rows, :] = _layernorm(z_ref[rows, col(7)], lndg_ref[...], lndb_ref[...])

        if precise_tail and r0 >= tt - CHUNK:
            tail_rows = slice(r0 - (tt - CHUNK), r0 - (tt - CHUNK) + ROW_BLK)
            mixf_ref[tail_rows, col(0)] = y_a
            mixf_ref[tail_rows, col(2)] = y_c
            dpoolf_ref[tail_rows, :] = d_pool

    for rows in stripes:
        for r0 in range(rows.start, rows.stop, ROW_BLK):
            mix_row_block(r0)

        y_p = jnp.dot(dpool_ref[rows, :], pw_ref[...], preferred_element_type=_F32) * ps_ref[...]
        mix_ref[rows, col(1)] = y_p.astype(_BF16)

        for c0 in range(rows.start, rows.stop, CHUNK):
            chunk = slice(c0, c0 + CHUNK)
            vn_c = vn_ref[chunk, :]
            mixed = sgb_ref[...]
            for hd in range(N_HEADS_D):
                vm = jnp.where(lane // HEAD_D == hd, vn_c, 0.0).astype(_BF16)
                mixed = mixed + jnp.dot(sgw_ref[hd], vm, preferred_element_type=_F32)
            mix_ref[chunk, col(3)] = (z_ref[chunk, col(6)] * mixed).astype(_BF16)

        x1_ref[rows, :] = x_ref[rows, :] + jnp.dot(mix_ref[rows, :], wout_ref[...], preferred_element_type=_F32)

    if precise_tail:
        @pl.when(is_last)
        def _():
            mixf_ref[:, col(1)] = _dot_split(dpoolf_ref[...], pw_ref, pwl_ref) * ps_ref[...]
            vn_c = vn_ref[tail, :]
            mixed = sgb_ref[...]
            for hd in range(N_HEADS_D):
                vm_hi, vm_lo = _split_bf16(jnp.where(lane // HEAD_D == hd, vn_c, 0.0))
                mixed = (mixed + jnp.dot(sgw_ref[hd], vm_hi, preferred_element_type=_F32)
                         + jnp.dot(sgw_ref[hd], vm_lo, preferred_element_type=_F32)
                         + jnp.dot(sgwl_ref[hd], vm_hi, preferred_element_type=_F32))
            mixf_ref[:, col(3)] = z_ref[tail, col(6)] * mixed
            x1_ref[tail, :] = x_ref[tail, :] + _dot_split(mixf_ref[...], wout_ref, woutl_ref)

    _store_route(x1_ref[...], (gffn_ref, rwh_ref, rw2_ref, rb_ref, lstrict_ref, ustrict_ref),
                 (h2_ref, route_ref, routet_ref, npad_ref))

    @pl.when(is_last)
    def _():
        end = HALO + tt
        sa_ref[0] = exta_ref[end - (CONV_A - 1):end, :]
        sp_ref[0] = extp_ref[end - POOL_STATE:end, :]
        sc_ref[0] = extc_ref[end - (CONV_C - 1):end, :]

    exta_ref[0:HALO, :] = exta_ref[tt:tt + HALO, :]
    extp_ref[0:HALO, :] = extp_ref[tt:tt + HALO, :]
    extc_ref[0:HALO, :] = extc_ref[tt:tt + HALO, :]


def _route_out_shapes(n_tiles):
    n_tok = n_tiles * TOK_TILE
    return [jax.ShapeDtypeStruct((n_tok, D_MODEL), _BF16),
            jax.ShapeDtypeStruct((n_tok, LANES), _F32),
            jax.ShapeDtypeStruct((n_tiles, SUBLANES, TOK_TILE), _F32),
            jax.ShapeDtypeStruct((n_tiles, 1, LANES), _F32)]


def _route_out_specs(tile_of):
    return [pl.BlockSpec((TOK_TILE, D_MODEL), lambda *g: (tile_of(*g), 0)),
            pl.BlockSpec((TOK_TILE, LANES), lambda *g: (tile_of(*g), 0)),
            pl.BlockSpec((1, SUBLANES, TOK_TILE), lambda *g: (tile_of(*g), 0, 0)),
            pl.BlockSpec((1, 1, LANES), lambda *g: (tile_of(*g), 0, 0))]


def _mixer_consts(lw, sgu_w, sgu_b):
    return [lw["g_mix"], lw["w_in"], lw["conv_a_w"], lw["pool_w_bd"], lw["pool_scale"], lw["conv_c_w"],
            lw["conv_c_b"], lw["ln_c_g"], lw["ln_c_b"], lw["ln_d_g"], lw["ln_d_b"], lw[sgu_w], lw[sgu_b], lw["w_out"],
            lw["g_ffn"], lw["router_w_hi"], lw["router_w_both"], lw["router_b"], lw["lstrict"], lw["ustrict"]]


def _prompt_mixer(x, bsz, lw, precise_tail):
    seq = x.shape[0] // bsz
    n_t = seq // TOK_TILE
    consts = _mixer_consts(lw, "sgu_w_tril", "sgu_bias_rows") + [lw["w_in_lo"], lw["pool_w_bd_lo"],
                                                                 lw["sgu_w_tril_lo"], lw["w_out_lo"]]
    tile_of = lambda b, t: b * n_t + t
    tile_spec = pl.BlockSpec((TOK_TILE, D_MODEL), lambda b, t: (tile_of(b, t), 0))

    def state_spec(rows):
        return pl.BlockSpec((1, rows, W_GROUP), lambda b, t: (b, 0, 0))

    return pl.pallas_call(
        functools.partial(_prompt_mixer_kernel, precise_tail, lw["layer"]),
        grid=(bsz, n_t),
        in_specs=[tile_spec] + [_const_spec(c.shape) for c in consts],
        out_specs=[tile_spec, state_spec(CONV_A - 1), state_spec(POOL_STATE), state_spec(CONV_C - 1)]
        + _route_out_specs(tile_of),
        out_shape=[jax.ShapeDtypeStruct((bsz * seq, D_MODEL), _F32),
                   jax.ShapeDtypeStruct((bsz, CONV_A - 1, W_GROUP), _F32),
                   jax.ShapeDtypeStruct((bsz, POOL_STATE, W_GROUP), _F32),
                   jax.ShapeDtypeStruct((bsz, CONV_C - 1, W_GROUP), _F32)] + _route_out_shapes(bsz * n_t),
        scratch_shapes=[pltpu.VMEM((TOK_TILE, IN_COLS), _F32),
                        pltpu.VMEM((HALO + TOK_TILE, W_GROUP), _F32),
                        pltpu.VMEM((HALO + TOK_TILE, W_GROUP), _F32),
                        pltpu.VMEM((HALO + TOK_TILE, W_GROUP), _F32),
                        pltpu.VMEM((TOK_TILE, W_GROUP), _BF16),
                        pltpu.VMEM((TOK_TILE, W_GROUP), _F32),
                        pltpu.VMEM((TOK_TILE, D_MODEL), _BF16),
                        pltpu.VMEM((CHUNK, D_MODEL), _F32),
                        pltpu.VMEM((CHUNK, W_GROUP), _F32)],
        compiler_params=pltpu.CompilerParams(dimension_semantics=("arbitrary", "arbitrary"),
                                             vmem_limit_bytes=VMEM_LIMIT),
        name="prompt_mixer",
    )(x, *consts)


def _sample_mixer_kernel(batch_major_in, layer, x_ref, sta_ref, stp_ref, stc_ref, gmix_ref, win_ref, caw_ref, pw_ref,
                         ps_ref, ccw_ref, ccb_ref, lncg_ref, lncb_ref, lndg_ref, lndb_ref, sgw_ref, sgb_ref, wout_ref,
                         gffn_ref, rwh_ref, rw2_ref, rb_ref, lstrict_ref, ustrict_ref,
                         x1_ref, nsa_ref, nsp_ref, nsc_ref, vrow_ref, h2_ref, route_ref, routet_ref, npad_ref,
                         xt_ref, z_ref, exta_ref, extp_ref, extc_ref, dpool_ref, vn_ref, mix_ref):
    (gmix_ref, ps_ref, ccb_ref, lncg_ref, lncb_ref, lndg_ref, lndb_ref, gffn_ref, rb_ref, caw_ref, ccw_ref, sgw_ref,
     sgb_ref) = _layer_views(layer, (gmix_ref, ps_ref, ccb_ref, lncg_ref, lncb_ref, lndg_ref, lndb_ref, gffn_ref, rb_ref),
                             (caw_ref, ccw_ref, sgw_ref, sgb_ref))
    nb = sta_ref.shape[0]
    n_tok = x1_ref.shape[0]
    n_t = n_tok // nb

    def col(k):
        return slice(k * W_GROUP, (k + 1) * W_GROUP)

    def slab(j, n=1):
        return slice(j * nb, (j + n) * nb)

    if batch_major_in:
        for tstep in range(n_t):
            xt_ref[slab(tstep), :] = x_ref[:, tstep * D_MODEL:(tstep + 1) * D_MODEL]
    else:
        xt_ref[...] = x_ref[...]

    h = _rmsnorm(xt_ref[...], gmix_ref[...]).astype(_BF16)
    z_ref[...] = jnp.dot(h, win_ref[...], preferred_element_type=_F32)

    for j in range(CONV_A - 1):
        exta_ref[slab(j), :] = sta_ref[:, col(j)]
    for j in range(POOL_STATE):
        extp_ref[slab(j), :] = stp_ref[:, col(j)]
    for j in range(CONV_C - 1):
        extc_ref[slab(j), :] = stc_ref[:, col(j)]
    for tstep in range(n_t):
        rows = slab(tstep)
        exta_ref[slab(CONV_A - 1 + tstep), :] = z_ref[rows, col(1)] * z_ref[rows, col(2)]
        extp_ref[slab(POOL_STATE + tstep), :] = z_ref[rows, col(3)]
        extc_ref[slab(CONV_C - 1 + tstep), :] = z_ref[rows, col(4)] * jax.nn.sigmoid(z_ref[rows, col(5)])

    win = _pool_windows((nb, W_GROUP))
    for tstep in range(n_t):
        rows = slab(tstep)
        conv_a = None
        for k in range(CONV_A):
            term = caw_ref[k:k + 1, :] * exta_ref[slab(tstep + k), :]
            conv_a = term if conv_a is None else conv_a + term
        mix_ref[rows, col(0)] = (z_ref[rows, col(0)] * conv_a).astype(_BF16)

        p_cur = extp_ref[slab(POOL_STATE + tstep), :]
        acc = p_cur
        for j in range(1, POOL_STATE + 1):
            acc = acc + jnp.where(win > j, extp_ref[slab(POOL_STATE + tstep - j), :], 0.0)
        cnt = jnp.minimum(PAST_LEN + tstep + 1, win).astype(_F32)
        dpool_ref[rows, :] = (acc / cnt - p_cur).astype(_BF16)

        conv_c = None
        for k in range(CONV_C):
            term = ccw_ref[k:k + 1, :] * extc_ref[slab(tstep + k), :]
            conv_c = term if conv_c is None else conv_c + term
        y_c = _layernorm(conv_c + ccb_ref[...], lncg_ref[...], lncb_ref[...])
        mix_ref[rows, col(2)] = _silu(y_c).astype(_BF16)

        v_n = _layernorm(z_ref[rows, col(7)], lndg_ref[...], lndb_ref[...])
        vn_ref[rows, :] = v_n
        vrow_ref[:, col(tstep)] = v_n

    y_p = jnp.dot(dpool_ref[...], pw_ref[...], preferred_element_type=_F32) * ps_ref[...]
    mix_ref[:, col(1)] = y_p.astype(_BF16)

    for i in range(n_t):
        mixed = sgb_ref[i:i + 1, :] + sgw_ref[i * n_t:i * n_t + 1, :] * vn_ref[slab(0), :]
        for j in range(1, i + 1):
            mixed = mixed + sgw_ref[i * n_t + j:i * n_t + j + 1, :] * vn_ref[slab(j), :]
        mix_ref[slab(i), col(3)] = (z_ref[slab(i), col(6)] * mixed).astype(_BF16)

    x1 = xt_ref[...] + jnp.dot(mix_ref[...], wout_ref[...], preferred_element_type=_F32)
    x1_ref[...] = x1
    _store_route(x1, (gffn_ref, rwh_ref, rw2_ref, rb_ref, lstrict_ref, ustrict_ref),
                 (h2_ref, route_ref, routet_ref, npad_ref))

    for j in range(CONV_A - 1):
        nsa_ref[:, col(j)] = exta_ref[slab(n_t + j), :]
    for j in range(POOL_STATE):
        nsp_ref[:, col(j)] = extp_ref[slab(n_t + j), :]
    for j in range(CONV_C - 1):
        nsc_ref[:, col(j)] = extc_ref[slab(n_t + j), :]


def _sample_mixer(x, st_a, st_p, st_c, lw, n_t, batch_major_in):
    n_seq = st_a.shape[0]
    nb = SAMPLE_SEQ_BLK
    n_blk = n_seq // nb
    n_tok = nb * n_t
    assert n_tok == TOK_TILE
    consts = _mixer_consts(lw, "sgu_w_rows", "sgu_b_rows")

    def seq_spec(width):
        return pl.BlockSpec((nb, width), lambda i: (i, 0))

    tok_spec = pl.BlockSpec((n_tok, D_MODEL), lambda i: (i, 0))
    state_widths = [(CONV_A - 1) * W_GROUP, POOL_STATE * W_GROUP, (CONV_C - 1) * W_GROUP]
    x_spec = seq_spec(n_t * D_MODEL) if batch_major_in else tok_spec
    return pl.pallas_call(
        functools.partial(_sample_mixer_kernel, batch_major_in, lw["layer"]),
        grid=(n_blk,),
        in_specs=[x_spec] + [seq_spec(w) for w in state_widths] + [_const_spec(c.shape) for c in consts],
        out_specs=[tok_spec] + [seq_spec(w) for w in state_widths] + [seq_spec(n_t * W_GROUP)]
        + _route_out_specs(lambda i: i),
        out_shape=[jax.ShapeDtypeStruct((n_blk * n_tok, D_MODEL), _F32)]
        + [jax.ShapeDtypeStruct((n_seq, w), _F32) for w in state_widths]
        + [jax.ShapeDtypeStruct((n_seq, n_t * W_GROUP), _F32)] + _route_out_shapes(n_blk),
        scratch_shapes=[pltpu.VMEM((n_tok, D_MODEL), _F32),
                        pltpu.VMEM((n_tok, IN_COLS), _F32),
                        pltpu.VMEM(((CONV_A - 1 + n_t) * nb, W_GROUP), _F32),
                        pltpu.VMEM(((POOL_STATE + n_t) * nb, W_GROUP), _F32),
                        pltpu.VMEM(((CONV_C - 1 + n_t) * nb, W_GROUP), _F32),
                        pltpu.VMEM((n_tok, W_GROUP), _BF16),
                        pltpu.VMEM((n_tok, W_GROUP), _F32),
                        pltpu.VMEM((n_tok, D_MODEL), _BF16)],
        compiler_params=pltpu.CompilerParams(dimension_semantics=("arbitrary",), vmem_limit_bytes=VMEM_LIMIT),
        name="sample_mixer",
    )(x, st_a, st_p, st_c, *consts)


def _plan_kernel(np_ref, ustrict_ref, dest_ref, tab_ref, used_ref, npx_ref, toff_ref, zc_ref):
    n_tiles = np_ref.shape[0]
    nt_pad = npx_ref.shape[0]
    zeros = jnp.zeros((nt_pad, LANES), _F32)
    npx_ref[...] = zeros
    toff_ref[...] = zeros
    zc_ref[...] = zeros
    npx_ref[0:n_tiles, :] = np_ref[...]
    np_all = npx_ref[...]

    tile_row = lax.broadcasted_iota(_I32, (nt_pad, 1), 0)
    n_real = jnp.sum(np_all, axis=-1, keepdims=True) * (1.0 / BF16_ROWS)
    n_zero = jnp.where(tile_row < n_tiles, N_CHUNK - n_real, 0.0)

    run = jnp.zeros((1, LANES), _F32)
    zrun = jnp.zeros((1, LANES), _F32)
    for i in range(n_tiles):
        toff_ref[i:i + 1, :] = run
        zc_ref[i:i + 1, :] = zrun
        run = run + npx_ref[i:i + 1, :]
        zrun = zrun + n_zero[i:i + 1, :]
    rows_e = run
    rows_pad = jnp.ceil(rows_e * (1.0 / GMM_TILE)) * GMM_TILE
    gap = (rows_pad - rows_e) * (1.0 / BF16_ROWS)

    def excl_lanes(v):
        return jnp.dot(v.astype(_BF16), ustrict_ref[...], preferred_element_type=_F32)

    gstart = excl_lanes(jnp.broadcast_to(rows_pad * (1.0 / GMM_TILE), (SUBLANES, LANES)))[0:1] * GMM_TILE
    gap_start = excl_lanes(jnp.broadcast_to(gap, (SUBLANES, LANES)))[0:1]
    gap_total = jnp.sum(gap, axis=-1, keepdims=True)
    rows_total = jnp.sum(rows_pad, axis=-1, keepdims=True)
    seg_start = excl_lanes(np_all * (1.0 / BF16_ROWS)) * BF16_ROWS
    delta = gstart + toff_ref[...] - seg_start

    chunk = lax.broadcasted_iota(_I32, (nt_pad, LANES), 1).astype(_F32)
    pos = chunk * BF16_ROWS
    q = zc_ref[...] + (chunk - n_real)
    real = pos
    gap_addr = q * BF16_ROWS
    for e in range(N_EXPERTS):
        ss = seg_start[:, e:e + 1]
        se = ss + np_all[:, e:e + 1]
        real = real + jnp.where(jnp.logical_and(ss <= pos, pos < se), delta[:, e:e + 1], 0.0)
        gs = gap_start[:, e:e + 1]
        ge = gs + gap[:, e:e + 1]
        base = gstart[:, e:e + 1] + rows_e[:, e:e + 1] - gs * BF16_ROWS
        gap_addr = gap_addr + jnp.where(jnp.logical_and(gs <= q, q < ge), base, 0.0)
    tail_addr = rows_total + (q - gap_total) * BF16_ROWS
    zero_addr = jnp.where(q < gap_total, gap_addr, tail_addr)
    dest = jnp.where(chunk < n_real, real, zero_addr)
    dest_ref[...] = dest[0:n_tiles, :].astype(_I32)
    used_ref[...] = jnp.broadcast_to(n_real * BF16_ROWS, (nt_pad, LANES))[0:n_tiles, :].astype(_I32)

    n_cols = tab_ref.shape[1]
    row_pos = lax.broadcasted_iota(_I32, (SUBLANES, n_cols), 1).astype(_F32) * GMM_TILE
    t_exp = jnp.zeros((SUBLANES, n_cols), _F32)
    t_val = jnp.zeros((SUBLANES, n_cols), _F32)
    for e in range(N_EXPERTS):
        gs = gstart[:, e:e + 1]
        t_exp = t_exp + jnp.where(gs + rows_pad[:, e:e + 1] <= row_pos, 1.0, 0.0)
        t_val = t_val + jnp.where(jnp.logical_and(gs <= row_pos, row_pos < gs + rows_e[:, e:e + 1]), 1.0, 0.0)
    t_exp = jnp.minimum(t_exp, N_EXPERTS - 1.0)
    n_valid = jnp.sum(t_val, axis=-1, keepdims=True)
    n_groups = jnp.ceil(n_valid * (1.0 / GMM_GROUP))
    s_idx = jnp.minimum(row_pos * (1.0 / GMM_TILE), n_groups - 1.0)
    sub = lax.broadcasted_iota(_I32, (SUBLANES, n_cols), 0)
    tab_ref[...] = jnp.where(sub == 0, t_exp, jnp.where(sub == 1, s_idx, 0.0)).astype(_I32)


def _plan(npad_all, lw, n_gmm_tiles):
    n_tiles = npad_all.shape[0]
    nt_pad = -(-n_tiles // SUBLANES) * SUBLANES
    n_cols = -(-n_gmm_tiles // LANES) * LANES
    out_shape = [jax.ShapeDtypeStruct((n_tiles, LANES), _I32), jax.ShapeDtypeStruct((SUBLANES, n_cols), _I32),
                 jax.ShapeDtypeStruct((n_tiles, LANES), _I32)]
    return pl.pallas_call(
        _plan_kernel,
        grid=(1,),
        in_specs=[_const_spec(npad_all.shape), _const_spec(lw["ustrict"].shape)],
        out_specs=[_const_spec(s.shape) for s in out_shape],
        out_shape=out_shape,
        scratch_shapes=[pltpu.VMEM((nt_pad, LANES), _F32)] * 3,
        name="moe_plan",
    )(npad_all, lw["ustrict"])


def _sort_kernel(n_prompt_tiles, dest_ref, used_ref, h2p_ref, h2s_ref, rtp_ref, rts_ref, xs_ref, buf_ref, sem_ref):
    i = pl.program_id(0)
    n = pl.num_programs(0)
    cur = lax.rem(i, 2)
    is_p = i < n_prompt_tiles
    h2 = jnp.where(is_p, h2p_ref[...], h2s_ref[...])
    rt = jnp.where(is_p, rtp_ref[0], rts_ref[0])
    s1 = rt[0:1, :]
    s2 = rt[1:2, :]

    def chunk_copy(tile, c, slot):
        dst = pl.multiple_of(dest_ref[tile * N_CHUNK + c], BF16_ROWS)
        return pltpu.make_async_copy(buf_ref.at[slot, pl.ds(c * BF16_ROWS, BF16_ROWS), :],
                                     xs_ref.at[pl.ds(dst, BF16_ROWS), :], sem_ref.at[slot])

    @pl.when(i < 2)
    def _():
        buf_ref[cur, SLOTS:SLOT_BUF, :] = jnp.zeros((SLOT_BUF - SLOTS, D_MODEL), _BF16)

    def sort_slots(lo, hi):
        slot_id = (lo + lax.broadcasted_iota(_I32, (hi - lo, TOK_TILE), 0)).astype(_F32)
        perm = jnp.where(jnp.logical_or(slot_id == s1, slot_id == s2), 1.0, 0.0).astype(_BF16)
        buf_ref[cur, lo:hi, :] = jnp.dot(perm, h2, preferred_element_type=_F32).astype(_BF16)

    def start_chunks(lo, hi):
        for c in range(lo // BF16_ROWS, hi // BF16_ROWS):
            chunk_copy(i, c, cur).start()

    start_chunks(SLOTS, SLOT_BUF)
    n_grp = SLOTS // SLOT_GRP
    for g in range(n_grp - 1):
        sort_slots(g * SLOT_GRP, (g + 1) * SLOT_GRP)
        start_chunks(g * SLOT_GRP, (g + 1) * SLOT_GRP)
    last_used = used_ref[i * LANES] > (n_grp - 1) * SLOT_GRP

    @pl.when(last_used)
    def _():
        sort_slots((n_grp - 1) * SLOT_GRP, SLOTS)

    @pl.when(jnp.logical_not(last_used))
    def _():
        buf_ref[cur, (n_grp - 1) * SLOT_GRP:SLOTS, :] = jnp.zeros((SLOT_GRP, D_MODEL), _BF16)

    start_chunks((n_grp - 1) * SLOT_GRP, SLOTS)

    @pl.when(i > 0)
    def _():
        for c in range(N_CHUNK):
            chunk_copy(i - 1, c, 1 - cur).wait()

    @pl.when(i == n - 1)
    def _():
        for c in range(N_CHUNK):
            chunk_copy(i, c, cur).wait()


def _sort(dest_flat, used_flat, h2p, h2s, rtp, rts):
    n_p = h2p.shape[0] // TOK_TILE
    n_s = h2s.shape[0] // TOK_TILE
    n_tiles = n_p + n_s
    p_idx = lambda i: jnp.minimum(i, n_p - 1)
    s_idx = lambda i: jnp.maximum(i - n_p, 0)
    return pl.pallas_call(
        functools.partial(_sort_kernel, n_p),
        grid_spec=pltpu.PrefetchScalarGridSpec(
            num_scalar_prefetch=2,
            grid=(n_tiles,),
            in_specs=[pl.BlockSpec((TOK_TILE, D_MODEL), lambda i, d, u: (p_idx(i), 0)),
                      pl.BlockSpec((TOK_TILE, D_MODEL), lambda i, d, u: (s_idx(i), 0)),
                      pl.BlockSpec((1, SUBLANES, TOK_TILE), lambda i, d, u: (p_idx(i), 0, 0)),
                      pl.BlockSpec((1, SUBLANES, TOK_TILE), lambda i, d, u: (s_idx(i), 0, 0))],
            out_specs=pl.BlockSpec(memory_space=pl.ANY),
            scratch_shapes=[pltpu.VMEM((2, SLOT_BUF, D_MODEL), _BF16), pltpu.SemaphoreType.DMA((2,))],
        ),
        out_shape=jax.ShapeDtypeStruct((n_tiles * SLOT_BUF, D_MODEL), _BF16),
        compiler_params=pltpu.CompilerParams(dimension_semantics=("arbitrary",), vmem_limit_bytes=VMEM_LIMIT),
        name="moe_sort",
    )(dest_flat, used_flat, h2p, h2s, rtp, rts)


def _gmm_kernel(texp_ref, sidx_ref, xs_ref, *refs):
    w_refs, ys_ref = refs[:-1], refs[-1]
    s = pl.program_id(0)

    @pl.when(sidx_ref[s] == s)
    def _():
        blk = GMM_TILE // GMM_ROW_SPLIT
        chains = [(j, slice(j * GMM_TILE + k * blk, j * GMM_TILE + (k + 1) * blk))
                  for j in range(GMM_GROUP) for k in range(GMM_ROW_SPLIT)]
        w_gu = [jnp.concatenate([w_refs[3 * j][0, 0].astype(_BF16), w_refs[3 * j + 1][0, 0].astype(_BF16)], axis=1)
                for j in range(GMM_GROUP)]
        w_d = [w_refs[3 * j + 2][0, 0].astype(_BF16) for j in range(GMM_GROUP)]
        gate_up = [jnp.dot(xs_ref[rows, :], w_gu[j], preferred_element_type=_F32) for j, rows in chains]
        for (j, rows), gu in zip(chains, gate_up):
            act = (_silu(gu[:, :D_FF_EXPERT]) * gu[:, D_FF_EXPERT:]).astype(_BF16)
            ys_ref[rows, :] = jnp.dot(act, w_d[j], preferred_element_type=_F32).astype(_BF16)


def _gmm(t_exp, s_idx, xs, layer, w_gate, w_up, w_down):
    n_steps = xs.shape[0] // (GMM_GROUP * GMM_TILE)
    group_spec = pl.BlockSpec((GMM_GROUP * GMM_TILE, D_MODEL), lambda s, te, si: (si[s], 0))
    w_specs, w_args = [], []
    for j in range(GMM_GROUP):
        expert = lambda s, te, si, j=j: (layer, te[GMM_GROUP * s + j], 0, 0)
        w_specs += [pl.BlockSpec((1, 1, D_MODEL, D_FF_EXPERT), expert),
                    pl.BlockSpec((1, 1, D_MODEL, D_FF_EXPERT), expert),
                    pl.BlockSpec((1, 1, D_FF_EXPERT, D_MODEL), expert)]
        w_args += [w_gate, w_up, w_down]
    return pl.pallas_call(
        _gmm_kernel,
        grid_spec=pltpu.PrefetchScalarGridSpec(
            num_scalar_prefetch=2,
            grid=(n_steps,),
            in_specs=[group_spec] + w_specs,
            out_specs=group_spec,
        ),
        out_shape=jax.ShapeDtypeStruct(xs.shape, _BF16),
        input_output_aliases={2: 0},
        compiler_params=pltpu.CompilerParams(dimension_semantics=("arbitrary",), vmem_limit_bytes=VMEM_LIMIT),
        name="moe_experts",
    )(t_exp, s_idx, xs, *w_args)


def _combine_kernel(n_prompt_tiles, final_norm, dest_ref, used_ref, x1p_ref, x1s_ref, rp_ref, rs_ref, gfin_ref, ys_ref,
                    outp_ref, outs_ref, ybuf_ref, sem_ref):
    i = pl.program_id(0)
    n = pl.num_programs(0)
    cur = lax.rem(i, 2)
    is_p = i < n_prompt_tiles

    def chunk_copy(tile, c, slot):
        src = pl.multiple_of(dest_ref[tile * N_CHUNK + c], BF16_ROWS)
        return pltpu.make_async_copy(ys_ref.at[pl.ds(src, BF16_ROWS), :],
                                     ybuf_ref.at[slot, pl.ds(c * BF16_ROWS, BF16_ROWS), :], sem_ref.at[slot])

    @pl.when(i == 0)
    def _():
        for c in range(N_CHUNK_REAL):
            chunk_copy(0, c, 0).start()

    @pl.when(i + 1 < n)
    def _():
        for c in range(N_CHUNK_REAL):
            chunk_copy(i + 1, c, 1 - cur).start()

    for c in range(N_CHUNK_REAL):
        chunk_copy(i, c, cur).wait()

    route = jnp.where(is_p, rp_ref[...], rs_ref[...])
    acc = jnp.where(is_p, x1p_ref[...], x1s_ref[...])
    s1 = route[:, 0:1]
    s2 = route[:, 1:2]
    w1 = route[:, 2:3]
    w2 = route[:, 3:4]
    def unperm_dot(lo, hi):
        slot_id = (lo + lax.broadcasted_iota(_I32, (TOK_TILE, hi - lo), 1)).astype(_F32)
        unperm = (jnp.where(slot_id == s1, w1, 0.0) + jnp.where(slot_id == s2, w2, 0.0)).astype(_BF16)
        return jnp.dot(unperm, ybuf_ref[cur, lo:hi, :], preferred_element_type=_F32)

    def finish(y):
        if final_norm:
            y = _rmsnorm(y, gfin_ref[...])

        @pl.when(is_p)
        def _():
            outp_ref[...] = y

        @pl.when(jnp.logical_not(is_p))
        def _():
            outs_ref[...] = y

    n_grp = SLOTS // SLOT_GRP
    for g in range(n_grp - 1):
        acc = acc + unperm_dot(g * SLOT_GRP, (g + 1) * SLOT_GRP)
    last_used = used_ref[i * LANES] > (n_grp - 1) * SLOT_GRP

    @pl.when(last_used)
    def _():
        finish(acc + unperm_dot((n_grp - 1) * SLOT_GRP, SLOTS))

    @pl.when(jnp.logical_not(last_used))
    def _():
        finish(acc)


def _combine(dest_flat, used_flat, x1p, x1s, rp, rs, g_fin, ys, final_norm):
    n_p = x1p.shape[0] // TOK_TILE
    n_s = x1s.shape[0] // TOK_TILE
    p_idx = lambda i, d, u: (jnp.minimum(i, n_p - 1), 0)
    s_idx = lambda i, d, u: (jnp.maximum(i - n_p, 0), 0)
    return pl.pallas_call(
        functools.partial(_combine_kernel, n_p, final_norm),
        grid_spec=pltpu.PrefetchScalarGridSpec(
            num_scalar_prefetch=2,
            grid=(n_p + n_s,),
            in_specs=[pl.BlockSpec((TOK_TILE, D_MODEL), p_idx), pl.BlockSpec((TOK_TILE, D_MODEL), s_idx),
                      pl.BlockSpec((TOK_TILE, LANES), p_idx), pl.BlockSpec((TOK_TILE, LANES), s_idx),
                      pl.BlockSpec(g_fin.shape, lambda i, d, u: (0, 0)),
                      pl.BlockSpec(memory_space=pl.ANY)],
            out_specs=[pl.BlockSpec((TOK_TILE, D_MODEL), p_idx), pl.BlockSpec((TOK_TILE, D_MODEL), s_idx)],
            scratch_shapes=[pltpu.VMEM((2, SLOTS, D_MODEL), _BF16), pltpu.SemaphoreType.DMA((2,))],
        ),
        out_shape=[jax.ShapeDtypeStruct(x1p.shape, _F32), jax.ShapeDtypeStruct(x1s.shape, _F32)],
        compiler_params=pltpu.CompilerParams(dimension_semantics=("arbitrary",), vmem_limit_bytes=VMEM_LIMIT),
        name="moe_combine",
    )(dest_flat, used_flat, x1p, x1s, rp, rs, g_fin, ys)


def _moe(x1p, x1s, routing_p, routing_s, lw, g_fin, final_norm):
    h2p, rp, rtp, npp = routing_p
    h2s, rs, rts, nps = routing_s
    n_tiles = npp.shape[0] + nps.shape[0]
    n_gmm = n_tiles * SLOT_BUF // GMM_TILE
    assert n_tiles * (SLOT_BUF - SLOTS) >= N_EXPERTS * (GMM_TILE - BF16_ROWS)
    npad_all = jnp.concatenate([npp, nps], axis=0).reshape(n_tiles, LANES)
    dest, tab, used = _plan(npad_all, lw, n_gmm)
    dest_flat = dest.reshape(-1)
    used_flat = used.reshape(-1)
    xs = _sort(dest_flat, used_flat, h2p, h2s, rtp, rts)
    ys = _gmm(tab[0], tab[1], xs, lw["layer"], *lw["expert_w"])
    return _combine(dest_flat, used_flat, x1p, x1s, rp, rs, g_fin, ys, final_norm)


def _shared_weights(g_mix, conv_a_w, pool_w, pool_scale, conv_c_w, conv_c_b, ln_c_g, ln_c_b, ln_d_g, ln_d_b, sgu_w,
                    sgu_b, g_ffn, router_group_w, router_group_b, router_expert_w, router_expert_b, expert_w_gate,
                    expert_w_up, expert_w_down, n_t_sample):
    depth = g_mix.shape[0]
    n_pool = len(POOL_WINDOWS)
    pool_bd = (pool_w[:, :, :, None, :] * jnp.eye(n_pool, dtype=_F32)[None, :, None, :, None]).reshape(
        depth, W_GROUP, W_GROUP)
    tril = jnp.tril(jnp.ones((CHUNK, CHUNK), dtype=bool))
    sgu_tril = jnp.where(tril, sgu_w, 0.0)
    w_small = sgu_tril[:, :, :n_t_sample, :n_t_sample]
    sgu_w_rows = jnp.repeat(jnp.transpose(w_small, (0, 2, 3, 1)).reshape(depth, n_t_sample * n_t_sample, N_HEADS_D),
                            HEAD_D, axis=2)
    sgu_bias_rows = jnp.repeat(jnp.swapaxes(sgu_b, 1, 2), HEAD_D, axis=2)
    n_route = N_EXPERTS + N_EXPERT_GROUPS
    router_w = jnp.pad(jnp.concatenate([router_expert_w, router_group_w], axis=2), ((0, 0), (0, 0), (0, LANES - n_route)))
    router_b = jnp.pad(jnp.concatenate([router_expert_b, router_group_b], axis=1), ((0, 0), (0, LANES - n_route)))
    return {
        "g_mix": g_mix, "conv_a_w": conv_a_w, "pool_scale": pool_scale, "conv_c_w": conv_c_w, "conv_c_b": conv_c_b,
        "ln_c_g": ln_c_g, "ln_c_b": ln_c_b, "ln_d_g": ln_d_g, "ln_d_b": ln_d_b, "g_ffn": g_ffn,
        "pool_bd": pool_bd, "sgu_tril": sgu_tril, "sgu_w_rows": sgu_w_rows, "sgu_bias_rows": sgu_bias_rows,
        "sgu_b_rows": sgu_bias_rows[:, :n_t_sample], "router_w": router_w, "router_b": router_b,
        "expert_w": (expert_w_gate, expert_w_up, expert_w_down),
        "lstrict": jnp.tril(jnp.ones((TOK_TILE, TOK_TILE), _F32), -1).astype(_BF16),
        "ustrict": jnp.triu(jnp.ones((LANES, LANES), _F32), 1).astype(_BF16),
    }


def _layer_weights(l, shared, w_in, w_out, precise_tail):
    router_w_hi, router_w_lo = _weight_split(shared["router_w"], l)
    sgu_shape = shared["sgu_tril"].shape[1:]
    if precise_tail:
        w_in_hi, w_in_lo = _weight_split(w_in, l)
        w_out_hi, w_out_lo = _weight_split(w_out, l)
        pool_hi, pool_lo = _weight_split(shared["pool_bd"], l)
        sgu_hi, sgu_lo = (s.reshape(sgu_shape) for s in
                          _weight_split(shared["sgu_tril"].reshape(-1, N_HEADS_D * CHUNK, CHUNK), l))
    else:
        w_in_hi = w_in_lo = w_in[l].astype(_BF16)
        w_out_hi = w_out_lo = w_out[l].astype(_BF16)
        pool_hi = pool_lo = shared["pool_bd"][l].astype(_BF16)
        sgu_hi = sgu_lo = shared["sgu_tril"][l].astype(_BF16)
    return dict(shared, layer=l, w_in=w_in_hi, w_in_lo=w_in_lo, w_out=w_out_hi, w_out_lo=w_out_lo, pool_w_bd=pool_hi,
                pool_w_bd_lo=pool_lo, sgu_w_tril=sgu_hi, sgu_w_tril_lo=sgu_lo, router_w_hi=router_w_hi,
                router_w_both=jnp.concatenate([router_w_hi, router_w_lo], axis=1))


def kernel(x_prompt, x_sample, state_conv_a, state_pool, state_conv_c, g_mix, w_in, conv_a_w, pool_w, pool_scale, conv_c_w, conv_c_b, ln_c_g, ln_c_b, ln_d_g, ln_d_b, sgu_w, sgu_b, w_out, g_ffn, router_group_w, router_group_b, router_expert_w, router_expert_b, expert_w_gate, expert_w_up, expert_w_down, g_final):
    depth = g_mix.shape[0]
    bsz, seq, _ = x_prompt.shape
    nb, n_t, _ = x_sample.shape
    g_fin = g_final.reshape(1, -1)

    xp = x_prompt.reshape(bsz * seq, D_MODEL)
    xs = x_sample.reshape(nb, n_t * D_MODEL)
    outs = {k: [] for k in ("sa_p", "sp_p", "sc_p", "sa_s", "sp_s", "sc_s", "v")}
    shared = _shared_weights(g_mix, conv_a_w, pool_w, pool_scale, conv_c_w, conv_c_b, ln_c_g, ln_c_b, ln_d_g, ln_d_b,
                             sgu_w, sgu_b, g_ffn, router_group_w, router_group_b, router_expert_w, router_expert_b,
                             expert_w_gate, expert_w_up, expert_w_down, n_t)
    for l in range(depth):
        precise_tail = l + 1 < depth
        lw = _layer_weights(l, shared, w_in, w_out, precise_tail)
        x1p, sa, sp, sc, *routing_p = _prompt_mixer(xp, bsz, lw, precise_tail)
        outs["sa_p"].append(sa)
        outs["sp_p"].append(sp)
        outs["sc_p"].append(sc)
        x1s, nsa, nsp, nsc, vrow, *routing_s = _sample_mixer(
            xs, state_conv_a[l].reshape(nb, -1), state_pool[l].reshape(nb, -1), state_conv_c[l].reshape(nb, -1),
            lw, n_t, batch_major_in=(l == 0))
        outs["sa_s"].append(nsa.reshape(nb, CONV_A - 1, W_GROUP))
        outs["sp_s"].append(nsp.reshape(nb, POOL_STATE, W_GROUP))
        outs["sc_s"].append(nsc.reshape(nb, CONV_C - 1, W_GROUP))
        outs["v"].append(vrow.reshape(nb, n_t, W_GROUP))
        xp, xs = _moe(x1p, x1s, routing_p, routing_s, lw, g_fin, final_norm=(l == depth - 1))

    y_prompt = xp.reshape(bsz, seq, D_MODEL)
    y_sample = jnp.transpose(xs.reshape(nb // SAMPLE_SEQ_BLK, n_t, SAMPLE_SEQ_BLK, D_MODEL),
                             (0, 2, 1, 3)).reshape(nb, n_t, D_MODEL)
    return (y_prompt, y_sample, jnp.stack(outs["sa_p"]), jnp.stack(outs["sp_p"]), jnp.stack(outs["sc_p"]),
            jnp.stack(outs["sa_s"]), jnp.stack(outs["sp_s"]), jnp.stack(outs["sc_s"]), jnp.stack(outs["v"]))
```

```python
import functools

import jax
import jax.numpy as jnp
from jax import lax
from jax.experimental import pallas as pl
from jax.experimental.pallas import tpu as pltpu

D_MODEL = 1024
W_GROUP = 256
IN_COLS = 8 * W_GROUP
CONV_A = 3
POOL_WINDOWS = (2, 4, 8, 16)
POOL_CH = W_GROUP // len(POOL_WINDOWS)
POOL_STATE = max(POOL_WINDOWS) - 1
CONV_C = 31
CHUNK = 128
N_HEADS_D = 4
HEAD_D = W_GROUP // N_HEADS_D
N_EXPERT_GROUPS = 4
EXPERTS_PER_GROUP = 8
N_EXPERTS = N_EXPERT_GROUPS * EXPERTS_PER_GROUP
TOP_K = 2
D_FF_EXPERT = 128
EPS = 1e-6
PAST_LEN = 16384

LANES = 128
SUBLANES = 8
BF16_ROWS = 16
HALO = 32
ROW_BLK = 64
MIX_STRIPE = 256
TOK_TILE = 512
SAMPLE_SEQ_BLK = 64
GMM_TILE = 512
GMM_GROUP = 2
GMM_ROW_SPLIT = 2
SLOTS = -(-(TOP_K * TOK_TILE + N_EXPERTS * (BF16_ROWS - 1)) // 256) * 256
N_CHUNK_REAL = SLOTS // BF16_ROWS
N_CHUNK = 128
SLOT_BUF = N_CHUNK * BF16_ROWS
SLOT_GRP = 256
VMEM_LIMIT = 56 * 1024 * 1024

_F32 = jnp.float32
_BF16 = jnp.bfloat16
_I32 = jnp.int32
_HI = lax.Precision.HIGHEST


def _rmsnorm(x, g):
    return x * lax.rsqrt(jnp.mean(x * x, axis=-1, keepdims=True) + EPS) * g


def _layernorm(x, g, b):
    mu = jnp.mean(x, axis=-1, keepdims=True)
    xc = x - mu
    var = jnp.mean(xc * xc, axis=-1, keepdims=True)
    return xc * lax.rsqrt(var + EPS) * g + b


def _silu(x):
    return x * jax.nn.sigmoid(x)


def _split_bf16(a):
    bits = lax.bitcast_convert_type(a, jnp.uint32)
    hi = lax.bitcast_convert_type(bits & jnp.uint32(0xFFFF0000), _F32)
    return hi.astype(_BF16), (a - hi).astype(_BF16)


def _dot_split(a, wh_ref, wl_ref):
    a_hi, a_lo = _split_bf16(a)
    n = a.shape[0]
    both = jnp.dot(jnp.concatenate([a_hi, a_lo], axis=0), wh_ref[...], preferred_element_type=_F32)
    return both[:n] + both[n:] + jnp.dot(a_hi, wl_ref[...], preferred_element_type=_F32)


def _weight_split_kernel(w_ref, hi_ref, lo_ref):
    w = w_ref[0]
    hi = w.astype(_BF16)
    hi_ref[...] = hi
    lo_ref[...] = (w - hi.astype(_F32)).astype(_BF16)


def _weight_split(w, layer):
    _, rows, cols = w.shape
    blk = min(rows, 256)
    spec = pl.BlockSpec((blk, cols), lambda i: (i, 0))
    return pl.pallas_call(
        _weight_split_kernel,
        grid=(rows // blk,),
        in_specs=[pl.BlockSpec((1, blk, cols), lambda i: (layer, i, 0))],
        out_specs=[spec, spec],
        out_shape=[jax.ShapeDtypeStruct((rows, cols), _BF16)] * 2,
        name="weight_split",
    )(w)


def _rows_back(x, r):
    return pltpu.roll(x, r, axis=0)


def _rows_ahead(x, r):
    return x if r == 0 else pltpu.roll(x, x.shape[0] - r, axis=0)


def _pool_windows(shape):
    lane = lax.broadcasted_iota(_I32, shape, 1)
    return jnp.left_shift(2, lane // POOL_CH)


def _const_spec(shape):
    nd = len(shape)
    return pl.BlockSpec(shape, lambda *_: (0,) * nd)


def _route_tile(x1, gffn_ref, rwh_ref, rw2_ref, rb_ref, lstrict_ref, ustrict_ref):
    h2 = _rmsnorm(x1, gffn_ref[...])
    h_hi, h_lo = _split_bf16(h2)
    hi_both = jnp.dot(h_hi, rw2_ref[...], preferred_element_type=_F32)
    logits = (hi_both[:, :LANES] + hi_both[:, LANES:]
              + jnp.dot(h_lo, rwh_ref[...], preferred_element_type=_F32)) + rb_ref[...]
    lane = lax.broadcasted_iota(_I32, logits.shape, 1)
    lane_f = lane.astype(_F32)
    neg = jnp.float32(-jnp.inf)
    big = jnp.float32(LANES)

    is_group = jnp.logical_and(lane >= N_EXPERTS, lane < N_EXPERTS + N_EXPERT_GROUPS)
    lg = jnp.where(is_group, logits, neg)
    g_max = jnp.max(lg, axis=-1, keepdims=True)
    g_idx = jnp.min(jnp.where(lg == g_max, lane_f, big), axis=-1, keepdims=True) - N_EXPERTS
    p_top = 1.0 / jnp.sum(jnp.exp(lg - g_max), axis=-1, keepdims=True)

    in_group = (lane // EXPERTS_PER_GROUP).astype(_F32) == g_idx
    le = jnp.where(jnp.logical_and(in_group, lane < N_EXPERTS), logits, neg)
    m1 = jnp.max(le, axis=-1, keepdims=True)
    i1 = jnp.min(jnp.where(le == m1, lane_f, big), axis=-1, keepdims=True)
    le2 = jnp.where(lane_f == i1, neg, le)
    m2 = jnp.max(le2, axis=-1, keepdims=True)
    i2 = jnp.min(jnp.where(le2 == m2, lane_f, big), axis=-1, keepdims=True)
    e2 = jnp.exp(m2 - m1)
    w1 = p_top / (1.0 + e2)
    w2 = p_top * e2 / (1.0 + e2)

    o1 = jnp.where(lane_f == i1, 1.0, 0.0)
    o2 = jnp.where(lane_f == i2, 1.0, 0.0)
    lane2 = lax.broadcasted_iota(_I32, (x1.shape[0], 2 * LANES), 1).astype(_F32)
    o12 = jnp.where(jnp.logical_or(lane2 == i1, lane2 == i2 + LANES), 1.0, 0.0).astype(_BF16)
    before = jnp.dot(lstrict_ref[...], o12, preferred_element_type=_F32)
    before1 = before[:, :LANES]
    before2 = before[:, LANES:]
    n1 = jnp.sum(o1, axis=0, keepdims=True)
    n2 = jnp.sum(o2, axis=0, keepdims=True)
    n_tiles16 = jnp.floor((n1 + n2 + (BF16_ROWS - 1)) * (1.0 / BF16_ROWS))
    npad = n_tiles16 * BF16_ROWS
    seg_start = jnp.dot(jnp.broadcast_to(n_tiles16, (SUBLANES, LANES)).astype(_BF16), ustrict_ref[...],
                        preferred_element_type=_F32)[0:1] * BF16_ROWS
    s1 = jnp.sum(o1 * (seg_start + before1), axis=-1, keepdims=True)
    s2 = jnp.sum(o2 * (seg_start + n1 + before2), axis=-1, keepdims=True)
    route = jnp.where(lane == 0, s1, jnp.where(lane == 1, s2, jnp.where(lane == 2, w1, jnp.where(lane == 3, w2, 0.0))))
    return h2.astype(_BF16), route, npad


def _store_route(x1, route_refs, out_refs):
    h2_ref, route_ref, routet_ref, npad_ref = out_refs
    h2, route, npad = _route_tile(x1, *route_refs)
    h2_ref[...] = h2
    route_ref[...] = route
    routet_ref[0] = jnp.transpose(route)[0:SUBLANES, :]
    npad_ref[0] = npad


def _layer_views(layer, row_refs, mat_refs):
    return [r.at[pl.ds(layer, 1)] for r in row_refs] + [r.at[layer] for r in mat_refs]


def _prompt_mixer_kernel(precise_tail, layer, x_ref, gmix_ref, win_ref, caw_ref, pw_ref, ps_ref, ccw_ref, ccb_ref,
                         lncg_ref, lncb_ref, lndg_ref, lndb_ref, sgw_ref, sgb_ref, wout_ref, gffn_ref, rwh_ref, rw2_ref,
                         rb_ref, lstrict_ref, ustrict_ref, winl_ref, pwl_ref, sgwl_ref, woutl_ref,
                         x1_ref, sa_ref, sp_ref, sc_ref, h2_ref, route_ref, routet_ref, npad_ref,
                         z_ref, exta_ref, extp_ref, extc_ref, dpool_ref, vn_ref, mix_ref, mixf_ref, dpoolf_ref):
    (gmix_ref, ps_ref, ccb_ref, lncg_ref, lncb_ref, lndg_ref, lndb_ref, gffn_ref, rb_ref, caw_ref, ccw_ref,
     sgb_ref) = _layer_views(layer, (gmix_ref, ps_ref, ccb_ref, lncg_ref, lncb_ref, lndg_ref, lndb_ref, gffn_ref, rb_ref),
                             (caw_ref, ccw_ref, sgb_ref))
    t = pl.program_id(1)
    n_t = pl.num_programs(1)
    tt = x_ref.shape[0]
    tail = slice(tt - CHUNK, tt)
    is_last = t == n_t - 1

    @pl.when(t == 0)
    def _():
        zeros = jnp.zeros((HALO, W_GROUP), _F32)
        exta_ref[0:HALO, :] = zeros
        extp_ref[0:HALO, :] = zeros
        extc_ref[0:HALO, :] = zeros

    def col(k):
        return slice(k * W_GROUP, (k + 1) * W_GROUP)

    stripes = [slice(s0, s0 + MIX_STRIPE) for s0 in range(0, tt, MIX_STRIPE)]
    for rows in stripes:
        h = _rmsnorm(x_ref[rows, :], gmix_ref[...])
        z_ref[rows, :] = jnp.dot(h.astype(_BF16), win_ref[...], preferred_element_type=_F32)
    if precise_tail:
        @pl.when(is_last)
        def _():
            z_ref[tail, :] = _dot_split(_rmsnorm(x_ref[tail, :], gmix_ref[...]), win_ref, winl_ref)

    win = _pool_windows((ROW_BLK, W_GROUP))
    row_iota = lax.broadcasted_iota(_I32, (ROW_BLK, W_GROUP), 0)
    low_group = lax.broadcasted_iota(_I32, (ROW_BLK + 2 * SUBLANES, LANES), 1) < POOL_CH
    lane = lax.broadcasted_iota(_I32, (CHUNK, W_GROUP), 1)

    def mix_row_block(r0):
        rows = slice(r0, r0 + ROW_BLK)
        ext_rows = slice(HALO + r0, HALO + r0 + ROW_BLK)
        exta_ref[ext_rows, :] = z_ref[rows, col(1)] * z_ref[rows, col(2)]
        extp_ref[ext_rows, :] = z_ref[rows, col(3)]
        extc_ref[ext_rows, :] = z_ref[rows, col(4)] * jax.nn.sigmoid(z_ref[rows, col(5)])

        ua = exta_ref[HALO + r0 - SUBLANES:HALO + r0 + ROW_BLK, :]
        conv_a = caw_ref[CONV_A - 1:CONV_A, :] * ua
        for k in range(CONV_A - 1):
            conv_a = conv_a + caw_ref[k:k + 1, :] * _rows_back(ua, CONV_A - 1 - k)
        y_a = z_ref[rows, col(0)] * conv_a[SUBLANES:, :]
        mix_ref[rows, col(0)] = y_a.astype(_BF16)

        pe = extp_ref[HALO + r0 - 2 * SUBLANES:HALO + r0 + ROW_BLK, :]
        s2 = pe + _rows_back(pe, 1)
        s4 = s2 + _rows_back(s2, 2)
        s4_hi = s4[:, LANES:]
        s8 = s4_hi + _rows_back(s4_hi, 4)
        s16 = s8 + _rows_back(s8, 8)
        sums = jnp.concatenate([jnp.where(low_group, s2[:, :LANES], s4[:, :LANES]), jnp.where(low_group, s8, s16)],
                               axis=1)[2 * SUBLANES:, :]
        pos = t * tt + r0 + row_iota
        cnt = jnp.minimum(pos + 1, win).astype(_F32)
        d_pool = sums / cnt - pe[2 * SUBLANES:, :]
        dpool_ref[rows, :] = d_pool.astype(_BF16)

        halves = []
        for hc in range(W_GROUP // LANES):
            lanes = slice(hc * LANES, (hc + 1) * LANES)
            xe = extc_ref[HALO + r0 - HALO:HALO + r0 + ROW_BLK, lanes]
            conv_c = None
            for r in range(SUBLANES):
                xr = _rows_ahead(xe, r)
                for a in range(HALO // SUBLANES + 1):
                    k = SUBLANES * a + r - (HALO - (CONV_C - 1))
                    if 0 <= k < CONV_C:
                        term = ccw_ref[k:k + 1, lanes] * xr[SUBLANES * a:SUBLANES * a + ROW_BLK, :]
                        conv_c = term if conv_c is None else conv_c + term
            halves.append(conv_c)
        y_c = _layernorm(jnp.concatenate(halves, axis=1) + ccb_ref[...], lncg_ref[...], lncb_ref[...])
        y_c = _silu(y_c)
        mix_ref[rows, col(2)] = y_c.astype(_BF16)

        vn_ref[rows, :] = _layernorm(z_ref[rows, col(7)], lndg_ref[...], lndb_ref[...])

        if precise_tail and r0 >= tt - CHUNK:
            tail_rows = slice(r0 - (tt - CHUNK), r0 - (tt - CHUNK) + ROW_BLK)
            mixf_ref[tail_rows, col(0)] = y_a
            mixf_ref[tail_rows, col(2)] = y_c
            dpoolf_ref[tail_rows, :] = d_pool

    for rows in stripes:
        for r0 in range(rows.start, rows.stop, ROW_BLK):
            mix_row_block(r0)

        y_p = jnp.dot(dpool_ref[rows, :], pw_ref[...], preferred_element_type=_F32) * ps_ref[...]
        mix_ref[rows, col(1)] = y_p.astype(_BF16)

        for c0 in range(rows.start, rows.stop, CHUNK):
            chunk = slice(c0, c0 + CHUNK)
            vn_c = vn_ref[chunk, :]
            mixed = sgb_ref[...]
            for hd in range(N_HEADS_D):
                vm = jnp.where(lane // HEAD_D == hd, vn_c, 0.0).astype(_BF16)
                mixed = mixed + jnp.dot(sgw_ref[hd], vm, preferred_element_type=_F32)
            mix_ref[chunk, col(3)] = (z_ref[chunk, col(6)] * mixed).astype(_BF16)

        x1_ref[rows, :] = x_ref[rows, :] + jnp.dot(mix_ref[rows, :], wout_ref[...], preferred_element_type=_F32)

    if precise_tail:
        @pl.when(is_last)
        def _():
            mixf_ref[:, col(1)] = _dot_split(dpoolf_ref[...], pw_ref, pwl_ref) * ps_ref[...]
            vn_c = vn_ref[tail, :]
            mixed = sgb_ref[...]
            for hd in range(N_HEADS_D):
                vm_hi, vm_lo = _split_bf16(jnp.where(lane // HEAD_D == hd, vn_c, 0.0))
                mixed = (mixed + jnp.dot(sgw_ref[hd], vm_hi, preferred_element_type=_F32)
                         + jnp.dot(sgw_ref[hd], vm_lo, preferred_element_type=_F32)
                         + jnp.dot(sgwl_ref[hd], vm_hi, preferred_element_type=_F32))
            mixf_ref[:, col(3)] = z_ref[tail, col(6)] * mixed
            x1_ref[tail, :] = x_ref[tail, :] + _dot_split(mixf_ref[...], wout_ref, woutl_ref)

    _store_route(x1_ref[...], (gffn_ref, rwh_ref, rw2_ref, rb_ref, lstrict_ref, ustrict_ref),
                 (h2_ref, route_ref, routet_ref, npad_ref))

    @pl.when(is_last)
    def _():
        end = HALO + tt
        sa_ref[0] = exta_ref[end - (CONV_A - 1):end, :]
        sp_ref[0] = extp_ref[end - POOL_STATE:end, :]
        sc_ref[0] = extc_ref[end - (CONV_C - 1):end, :]

    exta_ref[0:HALO, :] = exta_ref[tt:tt + HALO, :]
    extp_ref[0:HALO, :] = extp_ref[tt:tt + HALO, :]
    extc_ref[0:HALO, :] = extc_ref[tt:tt + HALO, :]


def _route_out_shapes(n_tiles):
    n_tok = n_tiles * TOK_TILE
    return [jax.ShapeDtypeStruct((n_tok, D_MODEL), _BF16),
            jax.ShapeDtypeStruct((n_tok, LANES), _F32),
            jax.ShapeDtypeStruct((n_tiles, SUBLANES, TOK_TILE), _F32),
            jax.ShapeDtypeStruct((n_tiles, 1, LANES), _F32)]


def _route_out_specs(tile_of):
    return [pl.BlockSpec((TOK_TILE, D_MODEL), lambda *g: (tile_of(*g), 0)),
            pl.BlockSpec((TOK_TILE, LANES), lambda *g: (tile_of(*g), 0)),
            pl.BlockSpec((1, SUBLANES, TOK_TILE), lambda *g: (tile_of(*g), 0, 0)),
            pl.BlockSpec((1, 1, LANES), lambda *g: (tile_of(*g), 0, 0))]


def _mixer_consts(lw, sgu_w, sgu_b):
    return [lw["g_mix"], lw["w_in"], lw["conv_a_w"], lw["pool_w_bd"], lw["pool_scale"], lw["conv_c_w"],
            lw["conv_c_b"], lw["ln_c_g"], lw["ln_c_b"], lw["ln_d_g"], lw["ln_d_b"], lw[sgu_w], lw[sgu_b], lw["w_out"],
            lw["g_ffn"], lw["router_w_hi"], lw["router_w_both"], lw["router_b"], lw["lstrict"], lw["ustrict"]]


def _prompt_mixer(x, bsz, lw, precise_tail):
    seq = x.shape[0] // bsz
    n_t = seq // TOK_TILE
    consts = _mixer_consts(lw, "sgu_w_tril", "sgu_bias_rows") + [lw["w_in_lo"], lw["pool_w_bd_lo"],
                                                                 lw["sgu_w_tril_lo"], lw["w_out_lo"]]
    tile_of = lambda b, t: b * n_t + t
    tile_spec = pl.BlockSpec((TOK_TILE, D_MODEL), lambda b, t: (tile_of(b, t), 0))

    def state_spec(rows):
        return pl.BlockSpec((1, rows, W_GROUP), lambda b, t: (b, 0, 0))

    return pl.pallas_call(
        functools.partial(_prompt_mixer_kernel, precise_tail, lw["layer"]),
        grid=(bsz, n_t),
        in_specs=[tile_spec] + [_const_spec(c.shape) for c in consts],
        out_specs=[tile_spec, state_spec(CONV_A - 1), state_spec(POOL_STATE), state_spec(CONV_C - 1)]
        + _route_out_specs(tile_of),
        out_shape=[jax.ShapeDtypeStruct((bsz * seq, D_MODEL), _F32),
                   jax.ShapeDtypeStruct((bsz, CONV_A - 1, W_GROUP), _F32),
                   jax.ShapeDtypeStruct((bsz, POOL_STATE, W_GROUP), _F32),
                   jax.ShapeDtypeStruct((bsz, CONV_C - 1, W_GROUP), _F32)] + _route_out_shapes(bsz * n_t),
        scratch_shapes=[pltpu.VMEM((TOK_TILE, IN_COLS), _F32),
                        pltpu.VMEM((HALO + TOK_TILE, W_GROUP), _F32),
                        pltpu.VMEM((HALO + TOK_TILE, W_GROUP), _F32),
                        pltpu.VMEM((HALO + TOK_TILE, W_GROUP), _F32),
                        pltpu.VMEM((TOK_TILE, W_GROUP), _BF16),
                        pltpu.VMEM((TOK_TILE, W_GROUP), _F32),
                        pltpu.VMEM((TOK_TILE, D_MODEL), _BF16),
                        pltpu.VMEM((CHUNK, D_MODEL), _F32),
                        pltpu.VMEM((CHUNK, W_GROUP), _F32)],
        compiler_params=pltpu.CompilerParams(dimension_semantics=("arbitrary", "arbitrary"),
                                             vmem_limit_bytes=VMEM_LIMIT),
        name="prompt_mixer",
    )(x, *consts)


def _sample_mixer_kernel(batch_major_in, layer, x_ref, sta_ref, stp_ref, stc_ref, gmix_ref, win_ref, caw_ref, pw_ref,
                         ps_ref, ccw_ref, ccb_ref, lncg_ref, lncb_ref, lndg_ref, lndb_ref, sgw_ref, sgb_ref, wout_ref,
                         gffn_ref, rwh_ref, rw2_ref, rb_ref, lstrict_ref, ustrict_ref,
                         x1_ref, nsa_ref, nsp_ref, nsc_ref, vrow_ref, h2_ref, route_ref, routet_ref, npad_ref,
                         xt_ref, z_ref, exta_ref, extp_ref, extc_ref, dpool_ref, vn_ref, mix_ref):
    (gmix_ref, ps_ref, ccb_ref, lncg_ref, lncb_ref, lndg_ref, lndb_ref, gffn_ref, rb_ref, caw_ref, ccw_ref, sgw_ref,
     sgb_ref) = _layer_views(layer, (gmix_ref, ps_ref, ccb_ref, lncg_ref, lncb_ref, lndg_ref, lndb_ref, gffn_ref, rb_ref),
                             (caw_ref, ccw_ref, sgw_ref, sgb_ref))
    nb = sta_ref.shape[0]
    n_tok = x1_ref.shape[0]
    n_t = n_tok // nb

    def col(k):
        return slice(k * W_GROUP, (k + 1) * W_GROUP)

    def slab(j, n=1):
        return slice(j * nb, (j + n) * nb)

    if batch_major_in:
        for tstep in range(n_t):
            xt_ref[slab(tstep), :] = x_ref[:, tstep * D_MODEL:(tstep + 1) * D_MODEL]
    else:
        xt_ref[...] = x_ref[...]

    h = _rmsnorm(xt_ref[...], gmix_ref[...]).astype(_BF16)
    z_ref[...] = jnp.dot(h, win_ref[...], preferred_element_type=_F32)

    for j in range(CONV_A - 1):
        exta_ref[slab(j), :] = sta_ref[:, col(j)]
    for j in range(POOL_STATE):
        extp_ref[slab(j), :] = stp_ref[:, col(j)]
    for j in range(CONV_C - 1):
        extc_ref[slab(j), :] = stc_ref[:, col(j)]
    for tstep in range(n_t):
        rows = slab(tstep)
        exta_ref[slab(CONV_A - 1 + tstep), :] = z_ref[rows, col(1)] * z_ref[rows, col(2)]
        extp_ref[slab(POOL_STATE + tstep), :] = z_ref[rows, col(3)]
        extc_ref[slab(CONV_C - 1 + tstep), :] = z_ref[rows, col(4)] * jax.nn.sigmoid(z_ref[rows, col(5)])

    win = _pool_windows((nb, W_GROUP))
    for tstep in range(n_t):
        rows = slab(tstep)
        conv_a = None
        for k in range(CONV_A):
            term = caw_ref[k:k + 1, :] * exta_ref[slab(tstep + k), :]
            conv_a = term if conv_a is None else conv_a + term
        mix_ref[rows, col(0)] = (z_ref[rows, col(0)] * conv_a).astype(_BF16)

        p_cur = extp_ref[slab(POOL_STATE + tstep), :]
        acc = p_cur
        for j in range(1, POOL_STATE + 1):
            acc = acc + jnp.where(win > j, extp_ref[slab(POOL_STATE + tstep - j), :], 0.0)
        cnt = jnp.minimum(PAST_LEN + tstep + 1, win).astype(_F32)
        dpool_ref[rows, :] = (acc / cnt - p_cur).astype(_BF16)

        conv_c = None
        for k in range(CONV_C):
            term = ccw_ref[k:k + 1, :] * extc_ref[slab(tstep + k), :]
            conv_c = term if conv_c is None else conv_c + term
        y_c = _layernorm(conv_c + ccb_ref[...], lncg_ref[...], lncb_ref[...])
        mix_ref[rows, col(2)] = _silu(y_c).astype(_BF16)

        v_n = _layernorm(z_ref[rows, col(7)], lndg_ref[...], lndb_ref[...])
        vn_ref[rows, :] = v_n
        vrow_ref[:, col(tstep)] = v_n

    y_p = jnp.dot(dpool_ref[...], pw_ref[...], preferred_element_type=_F32) * ps_ref[...]
    mix_ref[:, col(1)] = y_p.astype(_BF16)

    for i in range(n_t):
        mixed = sgb_ref[i:i + 1, :] + sgw_ref[i * n_t:i * n_t + 1, :] * vn_ref[slab(0), :]
        for j in range(1, i + 1):
            mixed = mixed + sgw_ref[i * n_t + j:i * n_t + j + 1, :] * vn_ref[slab(j), :]
        mix_ref[slab(i), col(3)] = (z_ref[slab(i), col(6)] * mixed).astype(_BF16)

    x1 = xt_ref[...] + jnp.dot(mix_ref[...], wout_ref[...], preferred_element_type=_F32)
    x1_ref[...] = x1
    _store_route(x1, (gffn_ref, rwh_ref, rw2_ref, rb_ref, lstrict_ref, ustrict_ref),
                 (h2_ref, route_ref, routet_ref, npad_ref))

    for j in range(CONV_A - 1):
        nsa_ref[:, col(j)] = exta_ref[slab(n_t + j), :]
    for j in range(POOL_STATE):
        nsp_ref[:, col(j)] = extp_ref[slab(n_t + j), :]
    for j in range(CONV_C - 1):
        nsc_ref[:, col(j)] = extc_ref[slab(n_t + j), :]


def _sample_mixer(x, st_a, st_p, st_c, lw, n_t, batch_major_in):
    n_seq = st_a.shape[0]
    nb = SAMPLE_SEQ_BLK
    n_blk = n_seq // nb
    n_tok = nb * n_t
    assert n_tok == TOK_TILE
    consts = _mixer_consts(lw, "sgu_w_rows", "sgu_b_rows")

    def seq_spec(width):
        return pl.BlockSpec((nb, width), lambda i: (i, 0))

    tok_spec = pl.BlockSpec((n_tok, D_MODEL), lambda i: (i, 0))
    state_widths = [(CONV_A - 1) * W_GROUP, POOL_STATE * W_GROUP, (CONV_C - 1) * W_GROUP]
    x_spec = seq_spec(n_t * D_MODEL) if batch_major_in else tok_spec
    return pl.pallas_call(
        functools.partial(_sample_mixer_kernel, batch_major_in, lw["layer"]),
        grid=(n_blk,),
        in_specs=[x_spec] + [seq_spec(w) for w in state_widths] + [_const_spec(c.shape) for c in consts],
        out_specs=[tok_spec] + [seq_spec(w) for w in state_widths] + [seq_spec(n_t * W_GROUP)]
        + _route_out_specs(lambda i: i),
        out_shape=[jax.ShapeDtypeStruct((n_blk * n_tok, D_MODEL), _F32)]
        + [jax.ShapeDtypeStruct((n_seq, w), _F32) for w in state_widths]
        + [jax.ShapeDtypeStruct((n_seq, n_t * W_GROUP), _F32)] + _route_out_shapes(n_blk),
        scratch_shapes=[pltpu.VMEM((n_tok, D_MODEL), _F32),
                        pltpu.VMEM((n_tok, IN_COLS), _F32),
                        pltpu.VMEM(((CONV_A - 1 + n_t) * nb, W_GROUP), _F32),
                        pltpu.VMEM(((POOL_STATE + n_t) * nb, W_GROUP), _F32),
                        pltpu.VMEM(((CONV_C - 1 + n_t) * nb, W_GROUP), _F32),
                        pltpu.VMEM((n_tok, W_GROUP), _BF16),
                        pltpu.VMEM((n_tok, W_GROUP), _F32),
                        pltpu.VMEM((n_tok, D_MODEL), _BF16)],
        compiler_params=pltpu.CompilerParams(dimension_semantics=("arbitrary",), vmem_limit_bytes=VMEM_LIMIT),
        name="sample_mixer",
    )(x, st_a, st_p, st_c, *consts)


def _plan_kernel(np_ref, ustrict_ref, dest_ref, tab_ref, used_ref, npx_ref, toff_ref, zc_ref):
    n_tiles = np_ref.shape[0]
    nt_pad = npx_ref.shape[0]
    zeros = jnp.zeros((nt_pad, LANES), _F32)
    npx_ref[...] = zeros
    toff_ref[...] = zeros
    zc_ref[...] = zeros
    npx_ref[0:n_tiles, :] = np_ref[...]
    np_all = npx_ref[...]

    tile_row = lax.broadcasted_iota(_I32, (nt_pad, 1), 0)
    n_real = jnp.sum(np_all, axis=-1, keepdims=True) * (1.0 / BF16_ROWS)
    n_zero = jnp.where(tile_row < n_tiles, N_CHUNK - n_real, 0.0)

    run = jnp.zeros((1, LANES), _F32)
    zrun = jnp.zeros((1, LANES), _F32)
    for i in range(n_tiles):
        toff_ref[i:i + 1, :] = run
        zc_ref[i:i + 1, :] = zrun
        run = run + npx_ref[i:i + 1, :]
        zrun = zrun + n_zero[i:i + 1, :]
    rows_e = run
    rows_pad = jnp.ceil(rows_e * (1.0 / GMM_TILE)) * GMM_TILE
    gap = (rows_pad - rows_e) * (1.0 / BF16_ROWS)

    def excl_lanes(v):
        return jnp.dot(v.astype(_BF16), ustrict_ref[...], preferred_element_type=_F32)

    gstart = excl_lanes(jnp.broadcast_to(rows_pad * (1.0 / GMM_TILE), (SUBLANES, LANES)))[0:1] * GMM_TILE
    gap_start = excl_lanes(jnp.broadcast_to(gap, (SUBLANES, LANES)))[0:1]
    gap_total = jnp.sum(gap, axis=-1, keepdims=True)
    rows_total = jnp.sum(rows_pad, axis=-1, keepdims=True)
    seg_start = excl_lanes(np_all * (1.0 / BF16_ROWS)) * BF16_ROWS
    delta = gstart + toff_ref[...] - seg_start

    chunk = lax.broadcasted_iota(_I32, (nt_pad, LANES), 1).astype(_F32)
    pos = chunk * BF16_ROWS
    q = zc_ref[...] + (chunk - n_real)
    real = pos
    gap_addr = q * BF16_ROWS
    for e in range(N_EXPERTS):
        ss = seg_start[:, e:e + 1]
        se = ss + np_all[:, e:e + 1]
        real = real + jnp.where(jnp.logical_and(ss <= pos, pos < se), delta[:, e:e + 1], 0.0)
        gs = gap_start[:, e:e + 1]
        ge = gs + gap[:, e:e + 1]
        base = gstart[:, e:e + 1] + rows_e[:, e:e + 1] - gs * BF16_ROWS
        gap_addr = gap_addr + jnp.where(jnp.logical_and(gs <= q, q < ge), base, 0.0)
    tail_addr = rows_total + (q - gap_total) * BF16_ROWS
    zero_addr = jnp.where(q < gap_total, gap_addr, tail_addr)
    dest = jnp.where(chunk < n_real, real, zero_addr)
    dest_ref[...] = dest[0:n_tiles, :].astype(_I32)
    used_ref[...] = jnp.broadcast_to(n_real * BF16_ROWS, (nt_pad, LANES))[0:n_tiles, :].astype(_I32)

    n_cols = tab_ref.shape[1]
    row_pos = lax.broadcasted_iota(_I32, (SUBLANES, n_cols), 1).astype(_F32) * GMM_TILE
    t_exp = jnp.zeros((SUBLANES, n_cols), _F32)
    t_val = jnp.zeros((SUBLANES, n_cols), _F32)
    for e in range(N_EXPERTS):
        gs = gstart[:, e:e + 1]
        t_exp = t_exp + jnp.where(gs + rows_pad[:, e:e + 1] <= row_pos, 1.0, 0.0)
        t_val = t_val + jnp.where(jnp.logical_and(gs <= row_pos, row_pos < gs + rows_e[:, e:e + 1]), 1.0, 0.0)
    t_exp = jnp.minimum(t_exp, N_EXPERTS - 1.0)
    n_valid = jnp.sum(t_val, axis=-1, keepdims=True)
    n_groups = jnp.ceil(n_valid * (1.0 / GMM_GROUP))
    s_idx = jnp.minimum(row_pos * (1.0 / GMM_TILE), n_groups - 1.0)
    sub = lax.broadcasted_iota(_I32, (SUBLANES, n_cols), 0)
    tab_ref[...] = jnp.where(sub == 0, t_exp, jnp.where(sub == 1, s_idx, 0.0)).astype(_I32)


def _plan(npad_all, lw, n_gmm_tiles):
    n_tiles = npad_all.shape[0]
    nt_pad = -(-n_tiles // SUBLANES) * SUBLANES
    n_cols = -(-n_gmm_tiles // LANES) * LANES
    out_shape = [jax.ShapeDtypeStruct((n_tiles, LANES), _I32), jax.ShapeDtypeStruct((SUBLANES, n_cols), _I32),
                 jax.ShapeDtypeStruct((n_tiles, LANES), _I32)]
    return pl.pallas_call(
        _plan_kernel,
        grid=(1,),
        in_specs=[_const_spec(npad_all.shape), _const_spec(lw["ustrict"].shape)],
        out_specs=[_const_spec(s.shape) for s in out_shape],
        out_shape=out_shape,
        scratch_shapes=[pltpu.VMEM((nt_pad, LANES), _F32)] * 3,
        name="moe_plan",
    )(npad_all, lw["ustrict"])


def _sort_kernel(n_prompt_tiles, dest_ref, used_ref, h2p_ref, h2s_ref, rtp_ref, rts_ref, xs_ref, buf_ref, sem_ref):
    i = pl.program_id(0)
    n = pl.num_programs(0)
    cur = lax.rem(i, 2)
    is_p = i < n_prompt_tiles
    h2 = jnp.where(is_p, h2p_ref[...], h2s_ref[...])
    rt = jnp.where(is_p, rtp_ref[0], rts_ref[0])
    s1 = rt[0:1, :]
    s2 = rt[1:2, :]

    def chunk_copy(tile, c, slot):
        dst = pl.multiple_of(dest_ref[tile * N_CHUNK + c], BF16_ROWS)
        return pltpu.make_async_copy(buf_ref.at[slot, pl.ds(c * BF16_ROWS, BF16_ROWS), :],
                                     xs_ref.at[pl.ds(dst, BF16_ROWS), :], sem_ref.at[slot])

    @pl.when(i < 2)
    def _():
        buf_ref[cur, SLOTS:SLOT_BUF, :] = jnp.zeros((SLOT_BUF - SLOTS, D_MODEL), _BF16)

    def sort_slots(lo, hi):
        slot_id = (lo + lax.broadcasted_iota(_I32, (hi - lo, TOK_TILE), 0)).astype(_F32)
        perm = jnp.where(jnp.logical_or(slot_id == s1, slot_id == s2), 1.0, 0.0).astype(_BF16)
        buf_ref[cur, lo:hi, :] = jnp.dot(perm, h2, preferred_element_type=_F32).astype(_BF16)

    def start_chunks(lo, hi):
        for c in range(lo // BF16_ROWS, hi // BF16_ROWS):
            chunk_copy(i, c, cur).start()

    start_chunks(SLOTS, SLOT_BUF)
    n_grp = SLOTS // SLOT_GRP
    for g in range(n_grp - 1):
        sort_slots(g * SLOT_GRP, (g + 1) * SLOT_GRP)
        start_chunks(g * SLOT_GRP, (g + 1) * SLOT_GRP)
    last_used = used_ref[i * LANES] > (n_grp - 1) * SLOT_GRP

    @pl.when(last_used)
    def _():
        sort_slots((n_grp - 1) * SLOT_GRP, SLOTS)

    @pl.when(jnp.logical_not(last_used))
    def _():
        buf_ref[cur, (n_grp - 1) * SLOT_GRP:SLOTS, :] = jnp.zeros((SLOT_GRP, D_MODEL), _BF16)

    start_chunks((n_grp - 1) * SLOT_GRP, SLOTS)

    @pl.when(i > 0)
    def _():
        for c in range(N_CHUNK):
            chunk_copy(i - 1, c, 1 - cur).wait()

    @pl.when(i == n - 1)
    def _():
        for c in range(N_CHUNK):
            chunk_copy(i, c, cur).wait()


def _sort(dest_flat, used_flat, h2p, h2s, rtp, rts):
    n_p = h2p.shape[0] // TOK_TILE
    n_s = h2s.shape[0] // TOK_TILE
    n_tiles = n_p + n_s
    p_idx = lambda i: jnp.minimum(i, n_p - 1)
    s_idx = lambda i: jnp.maximum(i - n_p, 0)
    return pl.pallas_call(
        functools.partial(_sort_kernel, n_p),
        grid_spec=pltpu.PrefetchScalarGridSpec(
            num_scalar_prefetch=2,
            grid=(n_tiles,),
            in_specs=[pl.BlockSpec((TOK_TILE, D_MODEL), lambda i, d, u: (p_idx(i), 0)),
                      pl.BlockSpec((TOK_TILE, D_MODEL), lambda i, d, u: (s_idx(i), 0)),
                      pl.BlockSpec((1, SUBLANES, TOK_TILE), lambda i, d, u: (p_idx(i), 0, 0)),
                      pl.BlockSpec((1, SUBLANES, TOK_TILE), lambda i, d, u: (s_idx(i), 0, 0))],
            out_specs=pl.BlockSpec(memory_space=pl.ANY),
            scratch_shapes=[pltpu.VMEM((2, SLOT_BUF, D_MODEL), _BF16), pltpu.SemaphoreType.DMA((2,))],
        ),
        out_shape=jax.ShapeDtypeStruct((n_tiles * SLOT_BUF, D_MODEL), _BF16),
        compiler_params=pltpu.CompilerParams(dimension_semantics=("arbitrary",), vmem_limit_bytes=VMEM_LIMIT),
        name="moe_sort",
    )(dest_flat, used_flat, h2p, h2s, rtp, rts)


def _gmm_kernel(texp_ref, sidx_ref, xs_ref, *refs):
    w_refs, ys_ref = refs[:-1], refs[-1]
    s = pl.program_id(0)

    @pl.when(sidx_ref[s] == s)
    def _():
        blk = GMM_TILE // GMM_ROW_SPLIT
        chains = [(j, slice(j * GMM_TILE + k * blk, j * GMM_TILE + (k + 1) * blk))
                  for j in range(GMM_GROUP) for k in range(GMM_ROW_SPLIT)]
        w_gu = [jnp.concatenate([w_refs[3 * j][0, 0].astype(_BF16), w_refs[3 * j + 1][0, 0].astype(_BF16)], axis=1)
                for j in range(GMM_GROUP)]
        w_d = [w_refs[3 * j + 2][0, 0].astype(_BF16) for j in range(GMM_GROUP)]
        gate_up = [jnp.dot(xs_ref[rows, :], w_gu[j], preferred_element_type=_F32) for j, rows in chains]
        for (j, rows), gu in zip(chains, gate_up):
            act = (_silu(gu[:, :D_FF_EXPERT]) * gu[:, D_FF_EXPERT:]).astype(_BF16)
            ys_ref[rows, :] = jnp.dot(act, w_d[j], preferred_element_type=_F32).astype(_BF16)


def _gmm(t_exp, s_idx, xs, layer, w_gate, w_up, w_down):
    n_steps = xs.shape[0] // (GMM_GROUP * GMM_TILE)
    group_spec = pl.BlockSpec((GMM_GROUP * GMM_TILE, D_MODEL), lambda s, te, si: (si[s], 0))
    w_specs, w_args = [], []
    for j in range(GMM_GROUP):
        expert = lambda s, te, si, j=j: (layer, te[GMM_GROUP * s + j], 0, 0)
        w_specs += [pl.BlockSpec((1, 1, D_MODEL, D_FF_EXPERT), expert),
                    pl.BlockSpec((1, 1, D_MODEL, D_FF_EXPERT), expert),
                    pl.BlockSpec((1, 1, D_FF_EXPERT, D_MODEL), expert)]
        w_args += [w_gate, w_up, w_down]
    return pl.pallas_call(
        _gmm_kernel,
        grid_spec=pltpu.PrefetchScalarGridSpec(
            num_scalar_prefetch=2,
            grid=(n_steps,),
            in_specs=[group_spec] + w_specs,
            out_specs=group_spec,
        ),
        out_shape=jax.ShapeDtypeStruct(xs.shape, _BF16),
        input_output_aliases={2: 0},
        compiler_params=pltpu.CompilerParams(dimension_semantics=("arbitrary",), vmem_limit_bytes=VMEM_LIMIT),
        name="moe_experts",
    )(t_exp, s_idx, xs, *w_args)


def _combine_kernel(n_prompt_tiles, final_norm, dest_ref, used_ref, x1p_ref, x1s_ref, rp_ref, rs_ref, gfin_ref, ys_ref,
                    outp_ref, outs_ref, ybuf_ref, sem_ref):
    i = pl.program_id(0)
    n = pl.num_programs(0)
    cur = lax.rem(i, 2)
    is_p = i < n_prompt_tiles

    def chunk_copy(tile, c, slot):
        src = pl.multiple_of(dest_ref[tile * N_CHUNK + c], BF16_ROWS)
        return pltpu.make_async_copy(ys_ref.at[pl.ds(src, BF16_ROWS), :],
                                     ybuf_ref.at[slot, pl.ds(c * BF16_ROWS, BF16_ROWS), :], sem_ref.at[slot])

    @pl.when(i == 0)
    def _():
        for c in range(N_CHUNK_REAL):
            chunk_copy(0, c, 0).start()

    @pl.when(i + 1 < n)
    def _():
        for c in range(N_CHUNK_REAL):
            chunk_copy(i + 1, c, 1 - cur).start()

    for c in range(N_CHUNK_REAL):
        chunk_copy(i, c, cur).wait()

    route = jnp.where(is_p, rp_ref[...], rs_ref[...])
    acc = jnp.where(is_p, x1p_ref[...], x1s_ref[...])
    s1 = route[:, 0:1]
    s2 = route[:, 1:2]
    w1 = route[:, 2:3]
    w2 = route[:, 3:4]
    def unperm_dot(lo, hi):
        slot_id = (lo + lax.broadcasted_iota(_I32, (TOK_TILE, hi - lo), 1)).astype(_F32)
        unperm = (jnp.where(slot_id == s1, w1, 0.0) + jnp.where(slot_id == s2, w2, 0.0)).astype(_BF16)
        return jnp.dot(unperm, ybuf_ref[cur, lo:hi, :], preferred_element_type=_F32)

    def finish(y):
        if final_norm:
            y = _rmsnorm(y, gfin_ref[...])

        @pl.when(is_p)
        def _():
            outp_ref[...] = y

        @pl.when(jnp.logical_not(is_p))
        def _():
            outs_ref[...] = y

    n_grp = SLOTS // SLOT_GRP
    for g in range(n_grp - 1):
        acc = acc + unperm_dot(g * SLOT_GRP, (g + 1) * SLOT_GRP)
    last_used = used_ref[i * LANES] > (n_grp - 1) * SLOT_GRP

    @pl.when(last_used)
    def _():
        finish(acc + unperm_dot((n_grp - 1) * SLOT_GRP, SLOTS))

    @pl.when(jnp.logical_not(last_used))
    def _():
        finish(acc)


def _combine(dest_flat, used_flat, x1p, x1s, rp, rs, g_fin, ys, final_norm):
    n_p = x1p.shape[0] // TOK_TILE
    n_s = x1s.shape[0] // TOK_TILE
    p_idx = lambda i, d, u: (jnp.minimum(i, n_p - 1), 0)
    s_idx = lambda i, d, u: (jnp.maximum(i - n_p, 0), 0)
    return pl.pallas_call(
        functools.partial(_combine_kernel, n_p, final_norm),
        grid_spec=pltpu.PrefetchScalarGridSpec(
            num_scalar_prefetch=2,
            grid=(n_p + n_s,),
            in_specs=[pl.BlockSpec((TOK_TILE, D_MODEL), p_idx), pl.BlockSpec((TOK_TILE, D_MODEL), s_idx),
                      pl.BlockSpec((TOK_TILE, LANES), p_idx), pl.BlockSpec((TOK_TILE, LANES), s_idx),
                      pl.BlockSpec(g_fin.shape, lambda i, d, u: (0, 0)),
                      pl.BlockSpec(memory_space=pl.ANY)],
            out_specs=[pl.BlockSpec((TOK_TILE, D_MODEL), p_idx), pl.BlockSpec((TOK_TILE, D_MODEL), s_idx)],
            scratch_shapes=[pltpu.VMEM((2, SLOTS, D_MODEL), _BF16), pltpu.SemaphoreType.DMA((2,))],
        ),
        out_shape=[jax.ShapeDtypeStruct(x1p.shape, _F32), jax.ShapeDtypeStruct(x1s.shape, _F32)],
        compiler_params=pltpu.CompilerParams(dimension_semantics=("arbitrary",), vmem_limit_bytes=VMEM_LIMIT),
        name="moe_combine",
    )(dest_flat, used_flat, x1p, x1s, rp, rs, g_fin, ys)


def _moe(x1p, x1s, routing_p, routing_s, lw, g_fin, final_norm):
    h2p, rp, rtp, npp = routing_p
    h2s, rs, rts, nps = routing_s
    n_tiles = npp.shape[0] + nps.shape[0]
    n_gmm = n_tiles * SLOT_BUF // GMM_TILE
    assert n_tiles * (SLOT_BUF - SLOTS) >= N_EXPERTS * (GMM_TILE - BF16_ROWS)
    npad_all = jnp.concatenate([npp, nps], axis=0).reshape(n_tiles, LANES)
    dest, tab, used = _plan(npad_all, lw, n_gmm)
    dest_flat = dest.reshape(-1)
    used_flat = used.reshape(-1)
    xs = _sort(dest_flat, used_flat, h2p, h2s, rtp, rts)
    ys = _gmm(tab[0], tab[1], xs, lw["layer"], *lw["expert_w"])
    return _combine(dest_flat, used_flat, x1p, x1s, rp, rs, g_fin, ys, final_norm)


def _shared_weights(g_mix, conv_a_w, pool_w, pool_scale, conv_c_w, conv_c_b, ln_c_g, ln_c_b, ln_d_g, ln_d_b, sgu_w,
                    sgu_b, g_ffn, router_group_w, router_group_b, router_expert_w, router_expert_b, expert_w_gate,
                    expert_w_up, expert_w_down, n_t_sample):
    depth = g_mix.shape[0]
    n_pool = len(POOL_WINDOWS)
    pool_bd = (pool_w[:, :, :, None, :] * jnp.eye(n_pool, dtype=_F32)[None, :, None, :, None]).reshape(
        depth, W_GROUP, W_GROUP)
    tril = jnp.tril(jnp.ones((CHUNK, CHUNK), dtype=bool))
    sgu_tril = jnp.where(tril, sgu_w, 0.0)
    w_small = sgu_tril[:, :, :n_t_sample, :n_t_sample]
    sgu_w_rows = jnp.repeat(jnp.transpose(w_small, (0, 2, 3, 1)).reshape(depth, n_t_sample * n_t_sample, N_HEADS_D),
                            HEAD_D, axis=2)
    sgu_bias_rows = jnp.repeat(jnp.swapaxes(sgu_b, 1, 2), HEAD_D, axis=2)
    n_route = N_EXPERTS + N_EXPERT_GROUPS
    router_w = jnp.pad(jnp.concatenate([router_expert_w, router_group_w], axis=2), ((0, 0), (0, 0), (0, LANES - n_route)))
    router_b = jnp.pad(jnp.concatenate([router_expert_b, router_group_b], axis=1), ((0, 0), (0, LANES - n_route)))
    return {
        "g_mix": g_mix, "conv_a_w": conv_a_w, "pool_scale": pool_scale, "conv_c_w": conv_c_w, "conv_c_b": conv_c_b,
        "ln_c_g": ln_c_g, "ln_c_b": ln_c_b, "ln_d_g": ln_d_g, "ln_d_b": ln_d_b, "g_ffn": g_ffn,
        "pool_bd": pool_bd, "sgu_tril": sgu_tril, "sgu_w_rows": sgu_w_rows, "sgu_bias_rows": sgu_bias_rows,
        "sgu_b_rows": sgu_bias_rows[:, :n_t_sample], "router_w": router_w, "router_b": router_b,
        "expert_w": (expert_w_gate, expert_w_up, expert_w_down),
        "lstrict": jnp.tril(jnp.ones((TOK_TILE, TOK_TILE), _F32), -1).astype(_BF16),
        "ustrict": jnp.triu(jnp.ones((LANES, LANES), _F32), 1).astype(_BF16),
    }


def _layer_weights(l, shared, w_in, w_out, precise_tail):
    router_w_hi, router_w_lo = _weight_split(shared["router_w"], l)
    sgu_shape = shared["sgu_tril"].shape[1:]
    if precise_tail:
        w_in_hi, w_in_lo = _weight_split(w_in, l)
        w_out_hi, w_out_lo = _weight_split(w_out, l)
        pool_hi, pool_lo = _weight_split(shared["pool_bd"], l)
        sgu_hi, sgu_lo = (s.reshape(sgu_shape) for s in
                          _weight_split(shared["sgu_tril"].reshape(-1, N_HEADS_D * CHUNK, CHUNK), l))
    else:
        w_in_hi = w_in_lo = w_in[l].astype(_BF16)
        w_out_hi = w_out_lo = w_out[l].astype(_BF16)
        pool_hi = pool_lo = shared["pool_bd"][l].astype(_BF16)
        sgu_hi = sgu_lo = shared["sgu_tril"][l].astype(_BF16)
    return dict(shared, layer=l, w_in=w_in_hi, w_in_lo=w_in_lo, w_out=w_out_hi, w_out_lo=w_out_lo, pool_w_bd=pool_hi,
                pool_w_bd_lo=pool_lo, sgu_w_tril=sgu_hi, sgu_w_tril_lo=sgu_lo, router_w_hi=router_w_hi,
                router_w_both=jnp.concatenate([router_w_hi, router_w_lo], axis=1))


def kernel(x_prompt, x_sample, state_conv_a, state_pool, state_conv_c, g_mix, w_in, conv_a_w, pool_w, pool_scale, conv_c_w, conv_c_b, ln_c_g, ln_c_b, ln_d_g, ln_d_b, sgu_w, sgu_b, w_out, g_ffn, router_group_w, router_group_b, router_expert_w, router_expert_b, expert_w_gate, expert_w_up, expert_w_down, g_final):
    depth = g_mix.shape[0]
    bsz, seq, _ = x_prompt.shape
    nb, n_t, _ = x_sample.shape
    g_fin = g_final.reshape(1, -1)

    xp = x_prompt.reshape(bsz * seq, D_MODEL)
    xs = x_sample.reshape(nb, n_t * D_MODEL)
    outs = {k: [] for k in ("sa_p", "sp_p", "sc_p", "sa_s", "sp_s", "sc_s", "v")}
    shared = _shared_weights(g_mix, conv_a_w, pool_w, pool_scale, conv_c_w, conv_c_b, ln_c_g, ln_c_b, ln_d_g, ln_d_b,
                             sgu_w, sgu_b, g_ffn, router_group_w, router_group_b, router_expert_w, router_expert_b,
                             expert_w_gate, expert_w_up, expert_w_down, n_t)
    for l in range(depth):
        precise_tail = l + 1 < depth
        lw = _layer_weights(l, shared, w_in, w_out, precise_tail)
        x1p, sa, sp, sc, *routing_p = _prompt_mixer(xp, bsz, lw, precise_tail)
        outs["sa_p"].append(sa)
        outs["sp_p"].append(sp)
        outs["sc_p"].append(sc)
        x1s, nsa, nsp, nsc, vrow, *routing_s = _sample_mixer(
            xs, state_conv_a[l].reshape(nb, -1), state_pool[l].reshape(nb, -1), state_conv_c[l].reshape(nb, -1),
            lw, n_t, batch_major_in=(l == 0))
        outs["sa_s"].append(nsa.reshape(nb, CONV_A - 1, W_GROUP))
        outs["sp_s"].append(nsp.reshape(nb, POOL_STATE, W_GROUP))
        outs["sc_s"].append(nsc.reshape(nb, CONV_C - 1, W_GROUP))
        outs["v"].append(vrow.reshape(nb, n_t, W_GROUP))
        xp, xs = _moe(x1p, x1s, routing_p, routing_s, lw, g_fin, final_norm=(l == depth - 1))

    y_prompt = xp.reshape(bsz, seq, D_MODEL)
    y_sample = jnp.transpose(xs.reshape(nb // SAMPLE_SEQ_BLK, n_t, SAMPLE_SEQ_BLK, D_MODEL),
                             (0, 2, 1, 3)).reshape(nb, n_t, D_MODEL)
    return (y_prompt, y_sample, jnp.stack(outs["sa_p"]), jnp.stack(outs["sp_p"]), jnp.stack(outs["sc_p"]),
            jnp.stack(outs["sa_s"]), jnp.stack(outs["sp_s"]), jnp.stack(outs["sc_s"]), jnp.stack(outs["v"]))
```

```python
import functools

import jax
import jax.numpy as jnp
from jax import lax
from jax.experimental import pallas as pl
from jax.experimental.pallas import tpu as pltpu

D_MODEL = 1024
W_GROUP = 256
IN_COLS = 8 * W_GROUP
CONV_A = 3
POOL_WINDOWS = (2, 4, 8, 16)
POOL_CH = W_GROUP // len(POOL_WINDOWS)
POOL_STATE = max(POOL_WINDOWS) - 1
CONV_C = 31
CHUNK = 128
N_HEADS_D = 4
HEAD_D = W_GROUP // N_HEADS_D
N_EXPERT_GROUPS = 4
EXPERTS_PER_GROUP = 8
N_EXPERTS = N_EXPERT_GROUPS * EXPERTS_PER_GROUP
TOP_K = 2
D_FF_EXPERT = 128
EPS = 1e-6
PAST_LEN = 16384

LANES = 128
SUBLANES = 8
BF16_ROWS = 16
HALO = 32
ROW_BLK = 64
MIX_STRIPE = 256
TOK_TILE = 512
SAMPLE_SEQ_BLK = 64
GMM_TILE = 512
GMM_GROUP = 4
GMM_ROW_SPLIT = 1
SLOTS = -(-(TOP_K * TOK_TILE + N_EXPERTS * (BF16_ROWS - 1)) // 256) * 256
N_CHUNK_REAL = SLOTS // BF16_ROWS
N_CHUNK = 128
SLOT_BUF = N_CHUNK * BF16_ROWS
SLOT_GRP = 256
VMEM_LIMIT = 56 * 1024 * 1024

_F32 = jnp.float32
_BF16 = jnp.bfloat16
_I32 = jnp.int32
_HI = lax.Precision.HIGHEST


def _rmsnorm(x, g):
    return x * lax.rsqrt(jnp.mean(x * x, axis=-1, keepdims=True) + EPS) * g


def _layernorm(x, g, b):
    mu = jnp.mean(x, axis=-1, keepdims=True)
    xc = x - mu
    var = jnp.mean(xc * xc, axis=-1, keepdims=True)
    return xc * lax.rsqrt(var + EPS) * g + b


def _silu(x):
    return x * jax.nn.sigmoid(x)


def _split_bf16(a):
    bits = lax.bitcast_convert_type(a, jnp.uint32)
    hi = lax.bitcast_convert_type(bits & jnp.uint32(0xFFFF0000), _F32)
    return hi.astype(_BF16), (a - hi).astype(_BF16)


def _dot_split(a, wh_ref, wl_ref):
    a_hi, a_lo = _split_bf16(a)
    n = a.shape[0]
    both = jnp.dot(jnp.concatenate([a_hi, a_lo], axis=0), wh_ref[...], preferred_element_type=_F32)
    return both[:n] + both[n:] + jnp.dot(a_hi, wl_ref[...], preferred_element_type=_F32)


def _weight_split_kernel(w_ref, hi_ref, lo_ref):
    w = w_ref[0]
    hi = w.astype(_BF16)
    hi_ref[...] = hi
    lo_ref[...] = (w - hi.astype(_F32)).astype(_BF16)


def _weight_split(w, layer):
    _, rows, cols = w.shape
    blk = min(rows, 256)
    spec = pl.BlockSpec((blk, cols), lambda i: (i, 0))
    return pl.pallas_call(
        _weight_split_kernel,
        grid=(rows // blk,),
        in_specs=[pl.BlockSpec((1, blk, cols), lambda i: (layer, i, 0))],
        out_specs=[spec, spec],
        out_shape=[jax.ShapeDtypeStruct((rows, cols), _BF16)] * 2,
        name="weight_split",
    )(w)


def _rows_back(x, r):
    return pltpu.roll(x, r, axis=0)


def _rows_ahead(x, r):
    return x if r == 0 else pltpu.roll(x, x.shape[0] - r, axis=0)


def _pool_windows(shape):
    lane = lax.broadcasted_iota(_I32, shape, 1)
    return jnp.left_shift(2, lane // POOL_CH)


def _const_spec(shape):
    nd = len(shape)
    return pl.BlockSpec(shape, lambda *_: (0,) * nd)


def _route_tile(x1, gffn_ref, rwh_ref, rw2_ref, rb_ref, lstrict_ref, ustrict_ref):
    h2 = _rmsnorm(x1, gffn_ref[...])
    h_hi, h_lo = _split_bf16(h2)
    hi_both = jnp.dot(h_hi, rw2_ref[...], preferred_element_type=_F32)
    logits = (hi_both[:, :LANES] + hi_both[:, LANES:]
              + jnp.dot(h_lo, rwh_ref[...], preferred_element_type=_F32)) + rb_ref[...]
    lane = lax.broadcasted_iota(_I32, logits.shape, 1)
    lane_f = lane.astype(_F32)
    neg = jnp.float32(-jnp.inf)
    big = jnp.float32(LANES)

    is_group = jnp.logical_and(lane >= N_EXPERTS, lane < N_EXPERTS + N_EXPERT_GROUPS)
    lg = jnp.where(is_group, logits, neg)
    g_max = jnp.max(lg, axis=-1, keepdims=True)
    g_idx = jnp.min(jnp.where(lg == g_max, lane_f, big), axis=-1, keepdims=True) - N_EXPERTS
    p_top = 1.0 / jnp.sum(jnp.exp(lg - g_max), axis=-1, keepdims=True)

    in_group = (lane // EXPERTS_PER_GROUP).astype(_F32) == g_idx
    le = jnp.where(jnp.logical_and(in_group, lane < N_EXPERTS), logits, neg)
    m1 = jnp.max(le, axis=-1, keepdims=True)
    i1 = jnp.min(jnp.where(le == m1, lane_f, big), axis=-1, keepdims=True)
    le2 = jnp.where(lane_f == i1, neg, le)
    m2 = jnp.max(le2, axis=-1, keepdims=True)
    i2 = jnp.min(jnp.where(le2 == m2, lane_f, big), axis=-1, keepdims=True)
    e2 = jnp.exp(m2 - m1)
    w1 = p_top / (1.0 + e2)
    w2 = p_top * e2 / (1.0 + e2)

    o1 = jnp.where(lane_f == i1, 1.0, 0.0)
    o2 = jnp.where(lane_f == i2, 1.0, 0.0)
    lane2 = lax.broadcasted_iota(_I32, (x1.shape[0], 2 * LANES), 1).astype(_F32)
    o12 = jnp.where(jnp.logical_or(lane2 == i1, lane2 == i2 + LANES), 1.0, 0.0).astype(_BF16)
    before = jnp.dot(lstrict_ref[...], o12, preferred_element_type=_F32)
    before1 = before[:, :LANES]
    before2 = before[:, LANES:]
    n1 = jnp.sum(o1, axis=0, keepdims=True)
    n2 = jnp.sum(o2, axis=0, keepdims=True)
    n_tiles16 = jnp.floor((n1 + n2 + (BF16_ROWS - 1)) * (1.0 / BF16_ROWS))
    npad = n_tiles16 * BF16_ROWS
    seg_start = jnp.dot(jnp.broadcast_to(n_tiles16, (SUBLANES, LANES)).astype(_BF16), ustrict_ref[...],
                        preferred_element_type=_F32)[0:1] * BF16_ROWS
    s1 = jnp.sum(o1 * (seg_start + before1), axis=-1, keepdims=True)
    s2 = jnp.sum(o2 * (seg_start + n1 + before2), axis=-1, keepdims=True)
    route = jnp.where(lane == 0, s1, jnp.where(lane == 1, s2, jnp.where(lane == 2, w1, jnp.where(lane == 3, w2, 0.0))))
    return h2.astype(_BF16), route, npad


def _store_route(x1, route_refs, out_refs):
    h2_ref, route_ref, routet_ref, npad_ref = out_refs
    h2, route, npad = _route_tile(x1, *route_refs)
    h2_ref[...] = h2
    route_ref[...] = route
    routet_ref[0] = jnp.transpose(route)[0:SUBLANES, :]
    npad_ref[0] = npad


def _layer_views(layer, row_refs, mat_refs):
    return [r.at[pl.ds(layer, 1)] for r in row_refs] + [r.at[layer] for r in mat_refs]


def _prompt_mixer_kernel(precise_tail, layer, x_ref, gmix_ref, win_ref, caw_ref, pw_ref, ps_ref, ccw_ref, ccb_ref,
                         lncg_ref, lncb_ref, lndg_ref, lndb_ref, sgw_ref, sgb_ref, wout_ref, gffn_ref, rwh_ref, rw2_ref,
                         rb_ref, lstrict_ref, ustrict_ref, winl_ref, pwl_ref, sgwl_ref, woutl_ref,
                         x1_ref, sa_ref, sp_ref, sc_ref, h2_ref, route_ref, routet_ref, npad_ref,
                         z_ref, exta_ref, extp_ref, extc_ref, dpool_ref, vn_ref, mix_ref, mixf_ref, dpoolf_ref):
    (gmix_ref, ps_ref, ccb_ref, lncg_ref, lncb_ref, lndg_ref, lndb_ref, gffn_ref, rb_ref, caw_ref, ccw_ref,
     sgb_ref) = _layer_views(layer, (gmix_ref, ps_ref, ccb_ref, lncg_ref, lncb_ref, lndg_ref, lndb_ref, gffn_ref, rb_ref),
                             (caw_ref, ccw_ref, sgb_ref))
    t = pl.program_id(1)
    n_t = pl.num_programs(1)
    tt = x_ref.shape[0]
    tail = slice(tt - CHUNK, tt)
    is_last = t == n_t - 1

    @pl.when(t == 0)
    def _():
        zeros = jnp.zeros((HALO, W_GROUP), _F32)
        exta_ref[0:HALO, :] = zeros
        extp_ref[0:HALO, :] = zeros
        extc_ref[0:HALO, :] = zeros

    def col(k):
        return slice(k * W_GROUP, (k + 1) * W_GROUP)

    stripes = [slice(s0, s0 + MIX_STRIPE) for s0 in range(0, tt, MIX_STRIPE)]
    for rows in stripes:
        h = _rmsnorm(x_ref[rows, :], gmix_ref[...])
        z_ref[rows, :] = jnp.dot(h.astype(_BF16), win_ref[...], preferred_element_type=_F32)
    if precise_tail:
        @pl.when(is_last)
        def _():
            z_ref[tail, :] = _dot_split(_rmsnorm(x_ref[tail, :], gmix_ref[...]), win_ref, winl_ref)

    win = _pool_windows((ROW_BLK, W_GROUP))
    row_iota = lax.broadcasted_iota(_I32, (ROW_BLK, W_GROUP), 0)
    low_group = lax.broadcasted_iota(_I32, (ROW_BLK + 2 * SUBLANES, LANES), 1) < POOL_CH
    lane = lax.broadcasted_iota(_I32, (CHUNK, W_GROUP), 1)

    def mix_row_block(r0):
        rows = slice(r0, r0 + ROW_BLK)
        ext_rows = slice(HALO + r0, HALO + r0 + ROW_BLK)
        exta_ref[ext_rows, :] = z_ref[rows, col(1)] * z_ref[rows, col(2)]
        extp_ref[ext_rows, :] = z_ref[rows, col(3)]
        extc_ref[ext_rows, :] = z_ref[rows, col(4)] * jax.nn.sigmoid(z_ref[rows, col(5)])

        ua = exta_ref[HALO + r0 - SUBLANES:HALO + r0 + ROW_BLK, :]
        conv_a = caw_ref[CONV_A - 1:CONV_A, :] * ua
        for k in range(CONV_A - 1):
            conv_a = conv_a + caw_ref[k:k + 1, :] * _rows_back(ua, CONV_A - 1 - k)
        y_a = z_ref[rows, col(0)] * conv_a[SUBLANES:, :]
        mix_ref[rows, col(0)] = y_a.astype(_BF16)

        pe = extp_ref[HALO + r0 - 2 * SUBLANES:HALO + r0 + ROW_BLK, :]
        s2 = pe + _rows_back(pe, 1)
        s4 = s2 + _rows_back(s2, 2)
        s4_hi = s4[:, LANES:]
        s8 = s4_hi + _rows_back(s4_hi, 4)
        s16 = s8 + _rows_back(s8, 8)
        sums = jnp.concatenate([jnp.where(low_group, s2[:, :LANES], s4[:, :LANES]), jnp.where(low_group, s8, s16)],
                               axis=1)[2 * SUBLANES:, :]
        pos = t * tt + r0 + row_iota
        cnt = jnp.minimum(pos + 1, win).astype(_F32)
        d_pool = sums / cnt - pe[2 * SUBLANES:, :]
        dpool_ref[rows, :] = d_pool.astype(_BF16)

        halves = []
        for hc in range(W_GROUP // LANES):
            lanes = slice(hc * LANES, (hc + 1) * LANES)
            xe = extc_ref[HALO + r0 - HALO:HALO + r0 + ROW_BLK, lanes]
            conv_c = None
            for r in range(SUBLANES):
                xr = _rows_ahead(xe, r)
                for a in range(HALO // SUBLANES + 1):
                    k = SUBLANES * a + r - (HALO - (CONV_C - 1))
                    if 0 <= k < CONV_C:
                        term = ccw_ref[k:k + 1, lanes] * xr[SUBLANES * a:SUBLANES * a + ROW_BLK, :]
                        conv_c = term if conv_c is None else conv_c + term
            halves.append(conv_c)
        y_c = _layernorm(jnp.concatenate(halves, axis=1) + ccb_ref[...], lncg_ref[...], lncb_ref[...])
        y_c = _silu(y_c)
        mix_ref[rows, col(2)] = y_c.astype(_BF16)

        vn_ref[rows, :] = _layernorm(z_ref[rows, col(7)], lndg_ref[...], lndb_ref[...])

        if precise_tail and r0 >= tt - CHUNK:
            tail_rows = slice(r0 - (tt - CHUNK), r0 - (tt - CHUNK) + ROW_BLK)
            mixf_ref[tail_rows, col(0)] = y_a
            mixf_ref[tail_rows, col(2)] = y_c
            dpoolf_ref[tail_rows, :] = d_pool

    for rows in stripes:
        for r0 in range(rows.start, rows.stop, ROW_BLK):
            mix_row_block(r0)

        y_p = jnp.dot(dpool_ref[rows, :], pw_ref[...], preferred_element_type=_F32) * ps_ref[...]
        mix_ref[rows, col(1)] = y_p.astype(_BF16)

        for c0 in range(rows.start, rows.stop, CHUNK):
            chunk = slice(c0, c0 + CHUNK)
            vn_c = vn_ref[chunk, :]
            mixed = sgb_ref[...]
            for hd in range(N_HEADS_D):
                vm = jnp.where(lane // HEAD_D == hd, vn_c, 0.0).astype(_BF16)
                mixed = mixed + jnp.dot(sgw_ref[hd], vm, preferred_element_type=_F32)
            mix_ref[chunk, col(3)] = (z_ref[chunk, col(6)] * mixed).astype(_BF16)

        x1_ref[rows, :] = x_ref[rows, :] + jnp.dot(mix_ref[rows, :], wout_ref[...], preferred_element_type=_F32)

    if precise_tail:
        @pl.when(is_last)
        def _():
            mixf_ref[:, col(1)] = _dot_split(dpoolf_ref[...], pw_ref, pwl_ref) * ps_ref[...]
            vn_c = vn_ref[tail, :]
            mixed = sgb_ref[...]
            for hd in range(N_HEADS_D):
                vm_hi, vm_lo = _split_bf16(jnp.where(lane // HEAD_D == hd, vn_c, 0.0))
                mixed = (mixed + jnp.dot(sgw_ref[hd], vm_hi, preferred_element_type=_F32)
                         + jnp.dot(sgw_ref[hd], vm_lo, preferred_element_type=_F32)
                         + jnp.dot(sgwl_ref[hd], vm_hi, preferred_element_type=_F32))
            mixf_ref[:, col(3)] = z_ref[tail, col(6)] * mixed
            x1_ref[tail, :] = x_ref[tail, :] + _dot_split(mixf_ref[...], wout_ref, woutl_ref)

    _store_route(x1_ref[...], (gffn_ref, rwh_ref, rw2_ref, rb_ref, lstrict_ref, ustrict_ref),
                 (h2_ref, route_ref, routet_ref, npad_ref))

    @pl.when(is_last)
    def _():
        end = HALO + tt
        sa_ref[0] = exta_ref[end - (CONV_A - 1):end, :]
        sp_ref[0] = extp_ref[end - POOL_STATE:end, :]
        sc_ref[0] = extc_ref[end - (CONV_C - 1):end, :]

    exta_ref[0:HALO, :] = exta_ref[tt:tt + HALO, :]
    extp_ref[0:HALO, :] = extp_ref[tt:tt + HALO, :]
    extc_ref[0:HALO, :] = extc_ref[tt:tt + HALO, :]


def _route_out_shapes(n_tiles):
    n_tok = n_tiles * TOK_TILE
    return [jax.ShapeDtypeStruct((n_tok, D_MODEL), _BF16),
            jax.ShapeDtypeStruct((n_tok, LANES), _F32),
            jax.ShapeDtypeStruct((n_tiles, SUBLANES, TOK_TILE), _F32),
            jax.ShapeDtypeStruct((n_tiles, 1, LANES), _F32)]


def _route_out_specs(tile_of):
    return [pl.BlockSpec((TOK_TILE, D_MODEL), lambda *g: (tile_of(*g), 0)),
            pl.BlockSpec((TOK_TILE, LANES), lambda *g: (tile_of(*g), 0)),
            pl.BlockSpec((1, SUBLANES, TOK_TILE), lambda *g: (tile_of(*g), 0, 0)),
            pl.BlockSpec((1, 1, LANES), lambda *g: (tile_of(*g), 0, 0))]


def _mixer_consts(lw, sgu_w, sgu_b):
    return [lw["g_mix"], lw["w_in"], lw["conv_a_w"], lw["pool_w_bd"], lw["pool_scale"], lw["conv_c_w"],
            lw["conv_c_b"], lw["ln_c_g"], lw["ln_c_b"], lw["ln_d_g"], lw["ln_d_b"], lw[sgu_w], lw[sgu_b], lw["w_out"],
            lw["g_ffn"], lw["router_w_hi"], lw["router_w_both"], lw["router_b"], lw["lstrict"], lw["ustrict"]]


def _prompt_mixer(x, bsz, lw, precise_tail):
    seq = x.shape[0] // bsz
    n_t = seq // TOK_TILE
    consts = _mixer_consts(lw, "sgu_w_tril", "sgu_bias_rows") + [lw["w_in_lo"], lw["pool_w_bd_lo"],
                                                                 lw["sgu_w_tril_lo"], lw["w_out_lo"]]
    tile_of = lambda b, t: b * n_t + t
    tile_spec = pl.BlockSpec((TOK_TILE, D_MODEL), lambda b, t: (tile_of(b, t), 0))

    def state_spec(rows):
        return pl.BlockSpec((1, rows, W_GROUP), lambda b, t: (b, 0, 0))

    return pl.pallas_call(
        functools.partial(_prompt_mixer_kernel, precise_tail, lw["layer"]),
        grid=(bsz, n_t),
        in_specs=[tile_spec] + [_const_spec(c.shape) for c in consts],
        out_specs=[tile_spec, state_spec(CONV_A - 1), state_spec(POOL_STATE), state_spec(CONV_C - 1)]
        + _route_out_specs(tile_of),
        out_shape=[jax.ShapeDtypeStruct((bsz * seq, D_MODEL), _F32),
                   jax.ShapeDtypeStruct((bsz, CONV_A - 1, W_GROUP), _F32),
                   jax.ShapeDtypeStruct((bsz, POOL_STATE, W_GROUP), _F32),
                   jax.ShapeDtypeStruct((bsz, CONV_C - 1, W_GROUP), _F32)] + _route_out_shapes(bsz * n_t),
        scratch_shapes=[pltpu.VMEM((TOK_TILE, IN_COLS), _F32),
                        pltpu.VMEM((HALO + TOK_TILE, W_GROUP), _F32),
                        pltpu.VMEM((HALO + TOK_TILE, W_GROUP), _F32),
                        pltpu.VMEM((HALO + TOK_TILE, W_GROUP), _F32),
                        pltpu.VMEM((TOK_TILE, W_GROUP), _BF16),
                        pltpu.VMEM((TOK_TILE, W_GROUP), _F32),
                        pltpu.VMEM((TOK_TILE, D_MODEL), _BF16),
                        pltpu.VMEM((CHUNK, D_MODEL), _F32),
                        pltpu.VMEM((CHUNK, W_GROUP), _F32)],
        compiler_params=pltpu.CompilerParams(dimension_semantics=("arbitrary", "arbitrary"),
                                             vmem_limit_bytes=VMEM_LIMIT),
        name="prompt_mixer",
    )(x, *consts)


def _sample_mixer_kernel(batch_major_in, layer, x_ref, sta_ref, stp_ref, stc_ref, gmix_ref, win_ref, caw_ref, pw_ref,
                         ps_ref, ccw_ref, ccb_ref, lncg_ref, lncb_ref, lndg_ref, lndb_ref, sgw_ref, sgb_ref, wout_ref,
                         gffn_ref, rwh_ref, rw2_ref, rb_ref, lstrict_ref, ustrict_ref,
                         x1_ref, nsa_ref, nsp_ref, nsc_ref, vrow_ref, h2_ref, route_ref, routet_ref, npad_ref,
                         xt_ref, z_ref, exta_ref, extp_ref, extc_ref, dpool_ref, vn_ref, mix_ref):
    (gmix_ref, ps_ref, ccb_ref, lncg_ref, lncb_ref, lndg_ref, lndb_ref, gffn_ref, rb_ref, caw_ref, ccw_ref, sgw_ref,
     sgb_ref) = _layer_views(layer, (gmix_ref, ps_ref, ccb_ref, lncg_ref, lncb_ref, lndg_ref, lndb_ref, gffn_ref, rb_ref),
                             (caw_ref, ccw_ref, sgw_ref, sgb_ref))
    nb = sta_ref.shape[0]
    n_tok = x1_ref.shape[0]
    n_t = n_tok // nb

    def col(k):
        return slice(k * W_GROUP, (k + 1) * W_GROUP)

    def slab(j, n=1):
        return slice(j * nb, (j + n) * nb)

    if batch_major_in:
        for tstep in range(n_t):
            xt_ref[slab(tstep), :] = x_ref[:, tstep * D_MODEL:(tstep + 1) * D_MODEL]
    else:
        xt_ref[...] = x_ref[...]

    h = _rmsnorm(xt_ref[...], gmix_ref[...]).astype(_BF16)
    z_ref[...] = jnp.dot(h, win_ref[...], preferred_element_type=_F32)

    for j in range(CONV_A - 1):
        exta_ref[slab(j), :] = sta_ref[:, col(j)]
    for j in range(POOL_STATE):
        extp_ref[slab(j), :] = stp_ref[:, col(j)]
    for j in range(CONV_C - 1):
        extc_ref[slab(j), :] = stc_ref[:, col(j)]
    for tstep in range(n_t):
        rows = slab(tstep)
        exta_ref[slab(CONV_A - 1 + tstep), :] = z_ref[rows, col(1)] * z_ref[rows, col(2)]
        extp_ref[slab(POOL_STATE + tstep), :] = z_ref[rows, col(3)]
        extc_ref[slab(CONV_C - 1 + tstep), :] = z_ref[rows, col(4)] * jax.nn.sigmoid(z_ref[rows, col(5)])

    win = _pool_windows((nb, W_GROUP))
    for tstep in range(n_t):
        rows = slab(tstep)
        conv_a = None
        for k in range(CONV_A):
            term = caw_ref[k:k + 1, :] * exta_ref[slab(tstep + k), :]
            conv_a = term if conv_a is None else conv_a + term
        mix_ref[rows, col(0)] = (z_ref[rows, col(0)] * conv_a).astype(_BF16)

        p_cur = extp_ref[slab(POOL_STATE + tstep), :]
        acc = p_cur
        for j in range(1, POOL_STATE + 1):
            acc = acc + jnp.where(win > j, extp_ref[slab(POOL_STATE + tstep - j), :], 0.0)
        cnt = jnp.minimum(PAST_LEN + tstep + 1, win).astype(_F32)
        dpool_ref[rows, :] = (acc / cnt - p_cur).astype(_BF16)

        conv_c = None
        for k in range(CONV_C):
            term = ccw_ref[k:k + 1, :] * extc_ref[slab(tstep + k), :]
            conv_c = term if conv_c is None else conv_c + term
        y_c = _layernorm(conv_c + ccb_ref[...], lncg_ref[...], lncb_ref[...])
        mix_ref[rows, col(2)] = _silu(y_c).astype(_BF16)

        v_n = _layernorm(z_ref[rows, col(7)], lndg_ref[...], lndb_ref[...])
        vn_ref[rows, :] = v_n
        vrow_ref[:, col(tstep)] = v_n

    y_p = jnp.dot(dpool_ref[...], pw_ref[...], preferred_element_type=_F32) * ps_ref[...]
    mix_ref[:, col(1)] = y_p.astype(_BF16)

    for i in range(n_t):
        mixed = sgb_ref[i:i + 1, :] + sgw_ref[i * n_t:i * n_t + 1, :] * vn_ref[slab(0), :]
        for j in range(1, i + 1):
            mixed = mixed + sgw_ref[i * n_t + j:i * n_t + j + 1, :] * vn_ref[slab(j), :]
        mix_ref[slab(i), col(3)] = (z_ref[slab(i), col(6)] * mixed).astype(_BF16)

    x1 = xt_ref[...] + jnp.dot(mix_ref[...], wout_ref[...], preferred_element_type=_F32)
    x1_ref[...] = x1
    _store_route(x1, (gffn_ref, rwh_ref, rw2_ref, rb_ref, lstrict_ref, ustrict_ref),
                 (h2_ref, route_ref, routet_ref, npad_ref))

    for j in range(CONV_A - 1):
        nsa_ref[:, col(j)] = exta_ref[slab(n_t + j), :]
    for j in range(POOL_STATE):
        nsp_ref[:, col(j)] = extp_ref[slab(n_t + j), :]
    for j in range(CONV_C - 1):
        nsc_ref[:, col(j)] = extc_ref[slab(n_t + j), :]


def _sample_mixer(x, st_a, st_p, st_c, lw, n_t, batch_major_in):
    n_seq = st_a.shape[0]
    nb = SAMPLE_SEQ_BLK
    n_blk = n_seq // nb
    n_tok = nb * n_t
    assert n_tok == TOK_TILE
    consts = _mixer_consts(lw, "sgu_w_rows", "sgu_b_rows")

    def seq_spec(width):
        return pl.BlockSpec((nb, width), lambda i: (i, 0))

    tok_spec = pl.BlockSpec((n_tok, D_MODEL), lambda i: (i, 0))
    state_widths = [(CONV_A - 1) * W_GROUP, POOL_STATE * W_GROUP, (CONV_C - 1) * W_GROUP]
    x_spec = seq_spec(n_t * D_MODEL) if batch_major_in else tok_spec
    return pl.pallas_call(
        functools.partial(_sample_mixer_kernel, batch_major_in, lw["layer"]),
        grid=(n_blk,),
        in_specs=[x_spec] + [seq_spec(w) for w in state_widths] + [_const_spec(c.shape) for c in consts],
        out_specs=[tok_spec] + [seq_spec(w) for w in state_widths] + [seq_spec(n_t * W_GROUP)]
        + _route_out_specs(lambda i: i),
        out_shape=[jax.ShapeDtypeStruct((n_blk * n_tok, D_MODEL), _F32)]
        + [jax.ShapeDtypeStruct((n_seq, w), _F32) for w in state_widths]
        + [jax.ShapeDtypeStruct((n_seq, n_t * W_GROUP), _F32)] + _route_out_shapes(n_blk),
        scratch_shapes=[pltpu.VMEM((n_tok, D_MODEL), _F32),
                        pltpu.VMEM((n_tok, IN_COLS), _F32),
                        pltpu.VMEM(((CONV_A - 1 + n_t) * nb, W_GROUP), _F32),
                        pltpu.VMEM(((POOL_STATE + n_t) * nb, W_GROUP), _F32),
                        pltpu.VMEM(((CONV_C - 1 + n_t) * nb, W_GROUP), _F32),
                        pltpu.VMEM((n_tok, W_GROUP), _BF16),
                        pltpu.VMEM((n_tok, W_GROUP), _F32),
                        pltpu.VMEM((n_tok, D_MODEL), _BF16)],
        compiler_params=pltpu.CompilerParams(dimension_semantics=("arbitrary",), vmem_limit_bytes=VMEM_LIMIT),
        name="sample_mixer",
    )(x, st_a, st_p, st_c, *consts)


def _plan_kernel(np_ref, ustrict_ref, dest_ref, tab_ref, used_ref, npx_ref, toff_ref, zc_ref):
    n_tiles = np_ref.shape[0]
    nt_pad = npx_ref.shape[0]
    zeros = jnp.zeros((nt_pad, LANES), _F32)
    npx_ref[...] = zeros
    toff_ref[...] = zeros
    zc_ref[...] = zeros
    npx_ref[0:n_tiles, :] = np_ref[...]
    np_all = npx_ref[...]

    tile_row = lax.broadcasted_iota(_I32, (nt_pad, 1), 0)
    n_real = jnp.sum(np_all, axis=-1, keepdims=True) * (1.0 / BF16_ROWS)
    n_zero = jnp.where(tile_row < n_tiles, N_CHUNK - n_real, 0.0)

    run = jnp.zeros((1, LANES), _F32)
    zrun = jnp.zeros((1, LANES), _F32)
    for i in range(n_tiles):
        toff_ref[i:i + 1, :] = run
        zc_ref[i:i + 1, :] = zrun
        run = run + npx_ref[i:i + 1, :]
        zrun = zrun + n_zero[i:i + 1, :]
    rows_e = run
    rows_pad = jnp.ceil(rows_e * (1.0 / GMM_TILE)) * GMM_TILE
    gap = (rows_pad - rows_e) * (1.0 / BF16_ROWS)

    def excl_lanes(v):
        return jnp.dot(v.astype(_BF16), ustrict_ref[...], preferred_element_type=_F32)

    gstart = excl_lanes(jnp.broadcast_to(rows_pad * (1.0 / GMM_TILE), (SUBLANES, LANES)))[0:1] * GMM_TILE
    gap_start = excl_lanes(jnp.broadcast_to(gap, (SUBLANES, LANES)))[0:1]
    gap_total = jnp.sum(gap, axis=-1, keepdims=True)
    rows_total = jnp.sum(rows_pad, axis=-1, keepdims=True)
    seg_start = excl_lanes(np_all * (1.0 / BF16_ROWS)) * BF16_ROWS
    delta = gstart + toff_ref[...] - seg_start

    chunk = lax.broadcasted_iota(_I32, (nt_pad, LANES), 1).astype(_F32)
    pos = chunk * BF16_ROWS
    q = zc_ref[...] + (chunk - n_real)
    real = pos
    gap_addr = q * BF16_ROWS
    for e in range(N_EXPERTS):
        ss = seg_start[:, e:e + 1]
        se = ss + np_all[:, e:e + 1]
        real = real + jnp.where(jnp.logical_and(ss <= pos, pos < se), delta[:, e:e + 1], 0.0)
        gs = gap_start[:, e:e + 1]
        ge = gs + gap[:, e:e + 1]
        base = gstart[:, e:e + 1] + rows_e[:, e:e + 1] - gs * BF16_ROWS
        gap_addr = gap_addr + jnp.where(jnp.logical_and(gs <= q, q < ge), base, 0.0)
    tail_addr = rows_total + (q - gap_total) * BF16_ROWS
    zero_addr = jnp.where(q < gap_total, gap_addr, tail_addr)
    dest = jnp.where(chunk < n_real, real, zero_addr)
    dest_ref[...] = dest[0:n_tiles, :].astype(_I32)
    used_ref[...] = jnp.broadcast_to(n_real * BF16_ROWS, (nt_pad, LANES))[0:n_tiles, :].astype(_I32)

    n_cols = tab_ref.shape[1]
    row_pos = lax.broadcasted_iota(_I32, (SUBLANES, n_cols), 1).astype(_F32) * GMM_TILE
    t_exp = jnp.zeros((SUBLANES, n_cols), _F32)
    t_val = jnp.zeros((SUBLANES, n_cols), _F32)
    for e in range(N_EXPERTS):
        gs = gstart[:, e:e + 1]
        t_exp = t_exp + jnp.where(gs + rows_pad[:, e:e + 1] <= row_pos, 1.0, 0.0)
        t_val = t_val + jnp.where(jnp.logical_and(gs <= row_pos, row_pos < gs + rows_e[:, e:e + 1]), 1.0, 0.0)
    t_exp = jnp.minimum(t_exp, N_EXPERTS - 1.0)
    n_valid = jnp.sum(t_val, axis=-1, keepdims=True)
    n_groups = jnp.ceil(n_valid * (1.0 / GMM_GROUP))
    s_idx = jnp.minimum(row_pos * (1.0 / GMM_TILE), n_groups - 1.0)
    sub = lax.broadcasted_iota(_I32, (SUBLANES, n_cols), 0)
    tab_ref[...] = jnp.where(sub == 0, t_exp, jnp.where(sub == 1, s_idx, 0.0)).astype(_I32)


def _plan(npad_all, lw, n_gmm_tiles):
    n_tiles = npad_all.shape[0]
    nt_pad = -(-n_tiles // SUBLANES) * SUBLANES
    n_cols = -(-n_gmm_tiles // LANES) * LANES
    out_shape = [jax.ShapeDtypeStruct((n_tiles, LANES), _I32), jax.ShapeDtypeStruct((SUBLANES, n_cols), _I32),
                 jax.ShapeDtypeStruct((n_tiles, LANES), _I32)]
    return pl.pallas_call(
        _plan_kernel,
        grid=(1,),
        in_specs=[_const_spec(npad_all.shape), _const_spec(lw["ustrict"].shape)],
        out_specs=[_const_spec(s.shape) for s in out_shape],
        out_shape=out_shape,
        scratch_shapes=[pltpu.VMEM((nt_pad, LANES), _F32)] * 3,
        name="moe_plan",
    )(npad_all, lw["ustrict"])


def _sort_kernel(n_prompt_tiles, dest_ref, used_ref, h2p_ref, h2s_ref, rtp_ref, rts_ref, xs_ref, buf_ref, sem_ref):
    i = pl.program_id(0)
    n = pl.num_programs(0)
    cur = lax.rem(i, 2)
    is_p = i < n_prompt_tiles
    h2 = jnp.where(is_p, h2p_ref[...], h2s_ref[...])
    rt = jnp.where(is_p, rtp_ref[0], rts_ref[0])
    s1 = rt[0:1, :]
    s2 = rt[1:2, :]

    def chunk_copy(tile, c, slot):
        dst = pl.multiple_of(dest_ref[tile * N_CHUNK + c], BF16_ROWS)
        return pltpu.make_async_copy(buf_ref.at[slot, pl.ds(c * BF16_ROWS, BF16_ROWS), :],
                                     xs_ref.at[pl.ds(dst, BF16_ROWS), :], sem_ref.at[slot])

    @pl.when(i < 2)
    def _():
        buf_ref[cur, SLOTS:SLOT_BUF, :] = jnp.zeros((SLOT_BUF - SLOTS, D_MODEL), _BF16)

    def sort_slots(lo, hi):
        slot_id = (lo + lax.broadcasted_iota(_I32, (hi - lo, TOK_TILE), 0)).astype(_F32)
        perm = jnp.where(jnp.logical_or(slot_id == s1, slot_id == s2), 1.0, 0.0).astype(_BF16)
        buf_ref[cur, lo:hi, :] = jnp.dot(perm, h2, preferred_element_type=_F32).astype(_BF16)

    def start_chunks(lo, hi):
        for c in range(lo // BF16_ROWS, hi // BF16_ROWS):
            chunk_copy(i, c, cur).start()

    start_chunks(SLOTS, SLOT_BUF)
    n_grp = SLOTS // SLOT_GRP
    for g in range(n_grp - 1):
        sort_slots(g * SLOT_GRP, (g + 1) * SLOT_GRP)
        start_chunks(g * SLOT_GRP, (g + 1) * SLOT_GRP)
    last_used = used_ref[i * LANES] > (n_grp - 1) * SLOT_GRP

    @pl.when(last_used)
    def _():
        sort_slots((n_grp - 1) * SLOT_GRP, SLOTS)

    @pl.when(jnp.logical_not(last_used))
    def _():
        buf_ref[cur, (n_grp - 1) * SLOT_GRP:SLOTS, :] = jnp.zeros((SLOT_GRP, D_MODEL), _BF16)

    start_chunks((n_grp - 1) * SLOT_GRP, SLOTS)

    @pl.when(i > 0)
    def _():
        for c in range(N_CHUNK):
            chunk_copy(i - 1, c, 1 - cur).wait()

    @pl.when(i == n - 1)
    def _():
        for c in range(N_CHUNK):
            chunk_copy(i, c, cur).wait()


def _sort(dest_flat, used_flat, h2p, h2s, rtp, rts):
    n_p = h2p.shape[0] // TOK_TILE
    n_s = h2s.shape[0] // TOK_TILE
    n_tiles = n_p + n_s
    p_idx = lambda i: jnp.minimum(i, n_p - 1)
    s_idx = lambda i: jnp.maximum(i - n_p, 0)
    return pl.pallas_call(
        functools.partial(_sort_kernel, n_p),
        grid_spec=pltpu.PrefetchScalarGridSpec(
            num_scalar_prefetch=2,
            grid=(n_tiles,),
            in_specs=[pl.BlockSpec((TOK_TILE, D_MODEL), lambda i, d, u: (p_idx(i), 0)),
                      pl.BlockSpec((TOK_TILE, D_MODEL), lambda i, d, u: (s_idx(i), 0)),
                      pl.BlockSpec((1, SUBLANES, TOK_TILE), lambda i, d, u: (p_idx(i), 0, 0)),
                      pl.BlockSpec((1, SUBLANES, TOK_TILE), lambda i, d, u: (s_idx(i), 0, 0))],
            out_specs=pl.BlockSpec(memory_space=pl.ANY),
            scratch_shapes=[pltpu.VMEM((2, SLOT_BUF, D_MODEL), _BF16), pltpu.SemaphoreType.DMA((2,))],
        ),
        out_shape=jax.ShapeDtypeStruct((n_tiles * SLOT_BUF, D_MODEL), _BF16),
        compiler_params=pltpu.CompilerParams(dimension_semantics=("arbitrary",), vmem_limit_bytes=VMEM_LIMIT),
        name="moe_sort",
    )(dest_flat, used_flat, h2p, h2s, rtp, rts)


def _gmm_kernel(texp_ref, sidx_ref, xs_ref, *refs):
    w_refs, ys_ref = refs[:-1], refs[-1]
    s = pl.program_id(0)

    @pl.when(sidx_ref[s] == s)
    def _():
        blk = GMM_TILE // GMM_ROW_SPLIT
        chains = [(j, slice(j * GMM_TILE + k * blk, j * GMM_TILE + (k + 1) * blk))
                  for j in range(GMM_GROUP) for k in range(GMM_ROW_SPLIT)]
        w_gu = [jnp.concatenate([w_refs[3 * j][0, 0].astype(_BF16), w_refs[3 * j + 1][0, 0].astype(_BF16)], axis=1)
                for j in range(GMM_GROUP)]
        w_d = [w_refs[3 * j + 2][0, 0].astype(_BF16) for j in range(GMM_GROUP)]
        gate_up = [jnp.dot(xs_ref[rows, :], w_gu[j], preferred_element_type=_F32) for j, rows in chains]
        for (j, rows), gu in zip(chains, gate_up):
            act = (_silu(gu[:, :D_FF_EXPERT]) * gu[:, D_FF_EXPERT:]).astype(_BF16)
            ys_ref[rows, :] = jnp.dot(act, w_d[j], preferred_element_type=_F32).astype(_BF16)


def _gmm(t_exp, s_idx, xs, layer, w_gate, w_up, w_down):
    n_steps = xs.shape[0] // (GMM_GROUP * GMM_TILE)
    group_spec = pl.BlockSpec((GMM_GROUP * GMM_TILE, D_MODEL), lambda s, te, si: (si[s], 0))
    w_specs, w_args = [], []
    for j in range(GMM_GROUP):
        expert = lambda s, te, si, j=j: (layer, te[GMM_GROUP * s + j], 0, 0)
        w_specs += [pl.BlockSpec((1, 1, D_MODEL, D_FF_EXPERT), expert),
                    pl.BlockSpec((1, 1, D_MODEL, D_FF_EXPERT), expert),
                    pl.BlockSpec((1, 1, D_FF_EXPERT, D_MODEL), expert)]
        w_args += [w_gate, w_up, w_down]
    return pl.pallas_call(
        _gmm_kernel,
        grid_spec=pltpu.PrefetchScalarGridSpec(
            num_scalar_prefetch=2,
            grid=(n_steps,),
            in_specs=[group_spec] + w_specs,
            out_specs=group_spec,
        ),
        out_shape=jax.ShapeDtypeStruct(xs.shape, _BF16),
        input_output_aliases={2: 0},
        compiler_params=pltpu.CompilerParams(dimension_semantics=("arbitrary",), vmem_limit_bytes=VMEM_LIMIT),
        name="moe_experts",
    )(t_exp, s_idx, xs, *w_args)


def _combine_kernel(n_prompt_tiles, final_norm, dest_ref, used_ref, x1p_ref, x1s_ref, rp_ref, rs_ref, gfin_ref, ys_ref,
                    outp_ref, outs_ref, ybuf_ref, sem_ref):
    i = pl.program_id(0)
    n = pl.num_programs(0)
    cur = lax.rem(i, 2)
    is_p = i < n_prompt_tiles

    def chunk_copy(tile, c, slot):
        src = pl.multiple_of(dest_ref[tile * N_CHUNK + c], BF16_ROWS)
        return pltpu.make_async_copy(ys_ref.at[pl.ds(src, BF16_ROWS), :],
                                     ybuf_ref.at[slot, pl.ds(c * BF16_ROWS, BF16_ROWS), :], sem_ref.at[slot])

    @pl.when(i == 0)
    def _():
        for c in range(N_CHUNK_REAL):
            chunk_copy(0, c, 0).start()

    @pl.when(i + 1 < n)
    def _():
        for c in range(N_CHUNK_REAL):
            chunk_copy(i + 1, c, 1 - cur).start()

    for c in range(N_CHUNK_REAL):
        chunk_copy(i, c, cur).wait()

    route = jnp.where(is_p, rp_ref[...], rs_ref[...])
    acc = jnp.where(is_p, x1p_ref[...], x1s_ref[...])
    s1 = route[:, 0:1]
    s2 = route[:, 1:2]
    w1 = route[:, 2:3]
    w2 = route[:, 3:4]
    def unperm_dot(lo, hi):
        slot_id = (lo + lax.broadcasted_iota(_I32, (TOK_TILE, hi - lo), 1)).astype(_F32)
        unperm = (jnp.where(slot_id == s1, w1, 0.0) + jnp.where(slot_id == s2, w2, 0.0)).astype(_BF16)
        return jnp.dot(unperm, ybuf_ref[cur, lo:hi, :], preferred_element_type=_F32)

    def finish(y):
        if final_norm:
            y = _rmsnorm(y, gfin_ref[...])

        @pl.when(is_p)
        def _():
            outp_ref[...] = y

        @pl.when(jnp.logical_not(is_p))
        def _():
            outs_ref[...] = y

    n_grp = SLOTS // SLOT_GRP
    for g in range(n_grp - 1):
        acc = acc + unperm_dot(g * SLOT_GRP, (g + 1) * SLOT_GRP)
    last_used = used_ref[i * LANES] > (n_grp - 1) * SLOT_GRP

    @pl.when(last_used)
    def _():
        finish(acc + unperm_dot((n_grp - 1) * SLOT_GRP, SLOTS))

    @pl.when(jnp.logical_not(last_used))
    def _():
        finish(acc)


def _combine(dest_flat, used_flat, x1p, x1s, rp, rs, g_fin, ys, final_norm):
    n_p = x1p.shape[0] // TOK_TILE
    n_s = x1s.shape[0] // TOK_TILE
    p_idx = lambda i, d, u: (jnp.minimum(i, n_p - 1), 0)
    s_idx = lambda i, d, u: (jnp.maximum(i - n_p, 0), 0)
    return pl.pallas_call(
        functools.partial(_combine_kernel, n_p, final_norm),
        grid_spec=pltpu.PrefetchScalarGridSpec(
            num_scalar_prefetch=2,
            grid=(n_p + n_s,),
            in_specs=[pl.BlockSpec((TOK_TILE, D_MODEL), p_idx), pl.BlockSpec((TOK_TILE, D_MODEL), s_idx),
                      pl.BlockSpec((TOK_TILE, LANES), p_idx), pl.BlockSpec((TOK_TILE, LANES), s_idx),
                      pl.BlockSpec(g_fin.shape, lambda i, d, u: (0, 0)),
                      pl.BlockSpec(memory_space=pl.ANY)],
            out_specs=[pl.BlockSpec((TOK_TILE, D_MODEL), p_idx), pl.BlockSpec((TOK_TILE, D_MODEL), s_idx)],
            scratch_shapes=[pltpu.VMEM((2, SLOTS, D_MODEL), _BF16), pltpu.SemaphoreType.DMA((2,))],
        ),
        out_shape=[jax.ShapeDtypeStruct(x1p.shape, _F32), jax.ShapeDtypeStruct(x1s.shape, _F32)],
        compiler_params=pltpu.CompilerParams(dimension_semantics=("arbitrary",), vmem_limit_bytes=VMEM_LIMIT),
        name="moe_combine",
    )(dest_flat, used_flat, x1p, x1s, rp, rs, g_fin, ys)


def _moe(x1p, x1s, routing_p, routing_s, lw, g_fin, final_norm):
    h2p, rp, rtp, npp = routing_p
    h2s, rs, rts, nps = routing_s
    n_tiles = npp.shape[0] + nps.shape[0]
    n_gmm = n_tiles * SLOT_BUF // GMM_TILE
    assert n_tiles * (SLOT_BUF - SLOTS) >= N_EXPERTS * (GMM_TILE - BF16_ROWS)
    npad_all = jnp.concatenate([npp, nps], axis=0).reshape(n_tiles, LANES)
    dest, tab, used = _plan(npad_all, lw, n_gmm)
    dest_flat = dest.reshape(-1)
    used_flat = used.reshape(-1)
    xs = _sort(dest_flat, used_flat, h2p, h2s, rtp, rts)
    ys = _gmm(tab[0], tab[1], xs, lw["layer"], *lw["expert_w"])
    return _combine(dest_flat, used_flat, x1p, x1s, rp, rs, g_fin, ys, final_norm)


def _shared_weights(g_mix, conv_a_w, pool_w, pool_scale, conv_c_w, conv_c_b, ln_c_g, ln_c_b, ln_d_g, ln_d_b, sgu_w,
                    sgu_b, g_ffn, router_group_w, router_group_b, router_expert_w, router_expert_b, expert_w_gate,
                    expert_w_up, expert_w_down, n_t_sample):
    depth = g_mix.shape[0]
    n_pool = len(POOL_WINDOWS)
    pool_bd = (pool_w[:, :, :, None, :] * jnp.eye(n_pool, dtype=_F32)[None, :, None, :, None]).reshape(
        depth, W_GROUP, W_GROUP)
    tril = jnp.tril(jnp.ones((CHUNK, CHUNK), dtype=bool))
    sgu_tril = jnp.where(tril, sgu_w, 0.0)
    w_small = sgu_tril[:, :, :n_t_sample, :n_t_sample]
    sgu_w_rows = jnp.repeat(jnp.transpose(w_small, (0, 2, 3, 1)).reshape(depth, n_t_sample * n_t_sample, N_HEADS_D),
                            HEAD_D, axis=2)
    sgu_bias_rows = jnp.repeat(jnp.swapaxes(sgu_b, 1, 2), HEAD_D, axis=2)
    n_route = N_EXPERTS + N_EXPERT_GROUPS
    router_w = jnp.pad(jnp.concatenate([router_expert_w, router_group_w], axis=2), ((0, 0), (0, 0), (0, LANES - n_route)))
    router_b = jnp.pad(jnp.concatenate([router_expert_b, router_group_b], axis=1), ((0, 0), (0, LANES - n_route)))
    return {
        "g_mix": g_mix, "conv_a_w": conv_a_w, "pool_scale": pool_scale, "conv_c_w": conv_c_w, "conv_c_b": conv_c_b,
        "ln_c_g": ln_c_g, "ln_c_b": ln_c_b, "ln_d_g": ln_d_g, "ln_d_b": ln_d_b, "g_ffn": g_ffn,
        "pool_bd": pool_bd, "sgu_tril": sgu_tril, "sgu_w_rows": sgu_w_rows, "sgu_bias_rows": sgu_bias_rows,
        "sgu_b_rows": sgu_bias_rows[:, :n_t_sample], "router_w": router_w, "router_b": router_b,
        "expert_w": (expert_w_gate, expert_w_up, expert_w_down),
        "lstrict": jnp.tril(jnp.ones((TOK_TILE, TOK_TILE), _F32), -1).astype(_BF16),
        "ustrict": jnp.triu(jnp.ones((LANES, LANES), _F32), 1).astype(_BF16),
    }


def _layer_weights(l, shared, w_in, w_out, precise_tail):
    router_w_hi, router_w_lo = _weight_split(shared["router_w"], l)
    sgu_shape = shared["sgu_tril"].shape[1:]
    if precise_tail:
        w_in_hi, w_in_lo = _weight_split(w_in, l)
        w_out_hi, w_out_lo = _weight_split(w_out, l)
        pool_hi, pool_lo = _weight_split(shared["pool_bd"], l)
        sgu_hi, sgu_lo = (s.reshape(sgu_shape) for s in
                          _weight_split(shared["sgu_tril"].reshape(-1, N_HEADS_D * CHUNK, CHUNK), l))
    else:
        w_in_hi = w_in_lo = w_in[l].astype(_BF16)
        w_out_hi = w_out_lo = w_out[l].astype(_BF16)
        pool_hi = pool_lo = shared["pool_bd"][l].astype(_BF16)
        sgu_hi = sgu_lo = shared["sgu_tril"][l].astype(_BF16)
    return dict(shared, layer=l, w_in=w_in_hi, w_in_lo=w_in_lo, w_out=w_out_hi, w_out_lo=w_out_lo, pool_w_bd=pool_hi,
                pool_w_bd_lo=pool_lo, sgu_w_tril=sgu_hi, sgu_w_tril_lo=sgu_lo, router_w_hi=router_w_hi,
                router_w_both=jnp.concatenate([router_w_hi, router_w_lo], axis=1))


def kernel(x_prompt, x_sample, state_conv_a, state_pool, state_conv_c, g_mix, w_in, conv_a_w, pool_w, pool_scale, conv_c_w, conv_c_b, ln_c_g, ln_c_b, ln_d_g, ln_d_b, sgu_w, sgu_b, w_out, g_ffn, router_group_w, router_group_b, router_expert_w, router_expert_b, expert_w_gate, expert_w_up, expert_w_down, g_final):
    depth = g_mix.shape[0]
    bsz, seq, _ = x_prompt.shape
    nb, n_t, _ = x_sample.shape
    g_fin = g_final.reshape(1, -1)

    xp = x_prompt.reshape(bsz * seq, D_MODEL)
    xs = x_sample.reshape(nb, n_t * D_MODEL)
    outs = {k: [] for k in ("sa_p", "sp_p", "sc_p", "sa_s", "sp_s", "sc_s", "v")}
    shared = _shared_weights(g_mix, conv_a_w, pool_w, pool_scale, conv_c_w, conv_c_b, ln_c_g, ln_c_b, ln_d_g, ln_d_b,
                             sgu_w, sgu_b, g_ffn, router_group_w, router_group_b, router_expert_w, router_expert_b,
                             expert_w_gate, expert_w_up, expert_w_down, n_t)
    for l in range(depth):
        precise_tail = l + 1 < depth
        lw = _layer_weights(l, shared, w_in, w_out, precise_tail)
        x1p, sa, sp, sc, *routing_p = _prompt_mixer(xp, bsz, lw, precise_tail)
        outs["sa_p"].append(sa)
        outs["sp_p"].append(sp)
        outs["sc_p"].append(sc)
        x1s, nsa, nsp, nsc, vrow, *routing_s = _sample_mixer(
            xs, state_conv_a[l].reshape(nb, -1), state_pool[l].reshape(nb, -1), state_conv_c[l].reshape(nb, -1),
            lw, n_t, batch_major_in=(l == 0))
        outs["sa_s"].append(nsa.reshape(nb, CONV_A - 1, W_GROUP))
        outs["sp_s"].append(nsp.reshape(nb, POOL_STATE, W_GROUP))
        outs["sc_s"].append(nsc.reshape(nb, CONV_C - 1, W_GROUP))
        outs["v"].append(vrow.reshape(nb, n_t, W_GROUP))
        xp, xs = _moe(x1p, x1s, routing_p, routing_s, lw, g_fin, final_norm=(l == depth - 1))

    y_prompt = xp.reshape(bsz, seq, D_MODEL)
    y_sample = jnp.transpose(xs.reshape(nb // SAMPLE_SEQ_BLK, n_t, SAMPLE_SEQ_BLK, D_MODEL),
                             (0, 2, 1, 3)).reshape(nb, n_t, D_MODEL)
    return (y_prompt, y_sample, jnp.stack(outs["sa_p"]), jnp.stack(outs["sp_p"]), jnp.stack(outs["sc_p"]),
            jnp.stack(outs["sa_s"]), jnp.stack(outs["sp_s"]), jnp.stack(outs["sc_s"]), jnp.stack(outs["v"]))
```

```python
import functools

import jax
import jax.numpy as jnp
from jax import lax
from jax.experimental import pallas as pl
from jax.experimental.pallas import tpu as pltpu

D_MODEL = 1024
W_GROUP = 256
IN_COLS = 8 * W_GROUP
CONV_A = 3
POOL_WINDOWS = (2, 4, 8, 16)
POOL_CH = W_GROUP // len(POOL_WINDOWS)
POOL_STATE = max(POOL_WINDOWS) - 1
CONV_C = 31
CHUNK = 128
N_HEADS_D = 4
HEAD_D = W_GROUP // N_HEADS_D
N_EXPERT_GROUPS = 4
EXPERTS_PER_GROUP = 8
N_EXPERTS = N_EXPERT_GROUPS * EXPERTS_PER_GROUP
TOP_K = 2
D_FF_EXPERT = 128
EPS = 1e-6
PAST_LEN = 16384

LANES = 128
SUBLANES = 8
BF16_ROWS = 16
HALO = 32
ROW_BLK = 64
MIX_STRIPE = 256
TOK_TILE = 512
SAMPLE_SEQ_BLK = 64
GMM_TILE = 512
GMM_GROUP = 2
GMM_ROW_SPLIT = 2
SLOTS = -(-(TOP_K * TOK_TILE + N_EXPERTS * (BF16_ROWS - 1)) // 256) * 256
N_CHUNK_REAL = SLOTS // BF16_ROWS
N_CHUNK = 128
SLOT_BUF = N_CHUNK * BF16_ROWS
SLOT_GRP = 256
VMEM_LIMIT = 56 * 1024 * 1024

_F32 = jnp.float32
_BF16 = jnp.bfloat16
_I32 = jnp.int32


def _rmsnorm(x, g):
    return x * lax.rsqrt(jnp.mean(x * x, axis=-1, keepdims=True) + EPS) * g


def _layernorm(x, g, b):
    mu = jnp.mean(x, axis=-1, keepdims=True)
    xc = x - mu
    var = jnp.mean(xc * xc, axis=-1, keepdims=True)
    return xc * lax.rsqrt(var + EPS) * g + b


def _silu(x):
    return x * jax.nn.sigmoid(x)


def _split_bf16(a):
    bits = lax.bitcast_convert_type(a, jnp.uint32)
    hi = lax.bitcast_convert_type(bits & jnp.uint32(0xFFFF0000), _F32)
    return hi.astype(_BF16), (a - hi).astype(_BF16)


def _dot_split(a, wh_ref, wl_ref):
    a_hi, a_lo = _split_bf16(a)
    n = a.shape[0]
    both = jnp.dot(jnp.concatenate([a_hi, a_lo], axis=0), wh_ref[...], preferred_element_type=_F32)
    return both[:n] + both[n:] + jnp.dot(a_hi, wl_ref[...], preferred_element_type=_F32)


def _weight_split_kernel(w_ref, hi_ref, lo_ref):
    w = w_ref[0]
    hi = w.astype(_BF16)
    hi_ref[...] = hi
    lo_ref[...] = (w - hi.astype(_F32)).astype(_BF16)


def _weight_split(w, layer):
    _, rows, cols = w.shape
    blk = min(rows, 256)
    spec = pl.BlockSpec((blk, cols), lambda i: (i, 0))
    return pl.pallas_call(
        _weight_split_kernel,
        grid=(rows // blk,),
        in_specs=[pl.BlockSpec((1, blk, cols), lambda i: (layer, i, 0))],
        out_specs=[spec, spec],
        out_shape=[jax.ShapeDtypeStruct((rows, cols), _BF16)] * 2,
        name="weight_split",
    )(w)


def _rows_back(x, r):
    return pltpu.roll(x, r, axis=0)


def _rows_ahead(x, r):
    return x if r == 0 else pltpu.roll(x, x.shape[0] - r, axis=0)


def _pool_windows(shape):
    lane = lax.broadcasted_iota(_I32, shape, 1)
    return jnp.left_shift(2, lane // POOL_CH)


def _const_spec(shape):
    nd = len(shape)
    return pl.BlockSpec(shape, lambda *_: (0,) * nd)


def _route_tile(x1, gffn_ref, rwh_ref, rw2_ref, rb_ref, lstrict_ref, ustrict_ref):
    h2 = _rmsnorm(x1, gffn_ref[...])
    h_hi, h_lo = _split_bf16(h2)
    hi_both = jnp.dot(h_hi, rw2_ref[...], preferred_element_type=_F32)
    logits = (hi_both[:, :LANES] + hi_both[:, LANES:]
              + jnp.dot(h_lo, rwh_ref[...], preferred_element_type=_F32)) + rb_ref[...]
    lane = lax.broadcasted_iota(_I32, logits.shape, 1)
    lane_f = lane.astype(_F32)
    neg = jnp.float32(-jnp.inf)
    big = jnp.float32(LANES)

    is_group = jnp.logical_and(lane >= N_EXPERTS, lane < N_EXPERTS + N_EXPERT_GROUPS)
    lg = jnp.where(is_group, logits, neg)
    g_max = jnp.max(lg, axis=-1, keepdims=True)
    g_idx = jnp.min(jnp.where(lg == g_max, lane_f, big), axis=-1, keepdims=True) - N_EXPERTS
    p_top = 1.0 / jnp.sum(jnp.exp(lg - g_max), axis=-1, keepdims=True)

    in_group = (lane // EXPERTS_PER_GROUP).astype(_F32) == g_idx
    le = jnp.where(jnp.logical_and(in_group, lane < N_EXPERTS), logits, neg)
    m1 = jnp.max(le, axis=-1, keepdims=True)
    i1 = jnp.min(jnp.where(le == m1, lane_f, big), axis=-1, keepdims=True)
    le2 = jnp.where(lane_f == i1, neg, le)
    m2 = jnp.max(le2, axis=-1, keepdims=True)
    i2 = jnp.min(jnp.where(le2 == m2, lane_f, big), axis=-1, keepdims=True)
    e2 = jnp.exp(m2 - m1)
    w1 = p_top / (1.0 + e2)
    w2 = p_top * e2 / (1.0 + e2)

    o1 = jnp.where(lane_f == i1, 1.0, 0.0)
    o2 = jnp.where(lane_f == i2, 1.0, 0.0)
    lane2 = lax.broadcasted_iota(_I32, (x1.shape[0], 2 * LANES), 1).astype(_F32)
    o12 = jnp.where(jnp.logical_or(lane2 == i1, lane2 == i2 + LANES), 1.0, 0.0).astype(_BF16)
    before = jnp.dot(lstrict_ref[...], o12, preferred_element_type=_F32)
    before1 = before[:, :LANES]
    before2 = before[:, LANES:]
    n1 = jnp.sum(o1, axis=0, keepdims=True)
    n2 = jnp.sum(o2, axis=0, keepdims=True)
    n_tiles16 = jnp.floor((n1 + n2 + (BF16_ROWS - 1)) * (1.0 / BF16_ROWS))
    npad = n_tiles16 * BF16_ROWS
    seg_start = jnp.dot(jnp.broadcast_to(n_tiles16, (SUBLANES, LANES)).astype(_BF16), ustrict_ref[...],
                        preferred_element_type=_F32)[0:1] * BF16_ROWS
    s1 = jnp.sum(o1 * (seg_start + before1), axis=-1, keepdims=True)
    s2 = jnp.sum(o2 * (seg_start + n1 + before2), axis=-1, keepdims=True)
    route = jnp.where(lane == 0, s1, jnp.where(lane == 1, s2, jnp.where(lane == 2, w1, jnp.where(lane == 3, w2, 0.0))))
    return h2.astype(_BF16), route, npad


def _store_route(x1, route_refs, out_refs):
    h2_ref, route_ref, routet_ref, npad_ref = out_refs
    h2, route, npad = _route_tile(x1, *route_refs)
    h2_ref[...] = h2
    route_ref[...] = route
    routet_ref[0] = jnp.transpose(route)[0:SUBLANES, :]
    npad_ref[0] = npad


def _layer_views(layer, row_refs, mat_refs):
    return [r.at[pl.ds(layer, 1)] for r in row_refs] + [r.at[layer] for r in mat_refs]


def _prompt_mixer_kernel(precise_tail, layer, x_ref, gmix_ref, win_ref, caw_ref, pw_ref, ps_ref, ccw_ref, ccb_ref,
                         lncg_ref, lncb_ref, lndg_ref, lndb_ref, sgw_ref, sgb_ref, wout_ref, gffn_ref, rwh_ref, rw2_ref,
                         rb_ref, lstrict_ref, ustrict_ref, winl_ref, pwl_ref, sgwl_ref, woutl_ref,
                         x1_ref, sa_ref, sp_ref, sc_ref, h2_ref, route_ref, routet_ref, npad_ref,
                         z_ref, exta_ref, extp_ref, extc_ref, dpool_ref, vn_ref, mix_ref, mixf_ref, dpoolf_ref):
    (gmix_ref, ps_ref, ccb_ref, lncg_ref, lncb_ref, lndg_ref, lndb_ref, gffn_ref, rb_ref, caw_ref, ccw_ref,
     sgb_ref) = _layer_views(layer, (gmix_ref, ps_ref, ccb_ref, lncg_ref, lncb_ref, lndg_ref, lndb_ref, gffn_ref, rb_ref),
                             (caw_ref, ccw_ref, sgb_ref))
    t = pl.program_id(1)
    n_t = pl.num_programs(1)
    tt = x_ref.shape[0]
    tail = slice(tt - CHUNK, tt)
    is_last = t == n_t - 1

    @pl.when(t == 0)
    def _():
        zeros = jnp.zeros((HALO, W_GROUP), _F32)
        exta_ref[0:HALO, :] = zeros
        extp_ref[0:HALO, :] = zeros
        extc_ref[0:HALO, :] = zeros

    def col(k):
        return slice(k * W_GROUP, (k + 1) * W_GROUP)

    stripes = [slice(s0, s0 + MIX_STRIPE) for s0 in range(0, tt, MIX_STRIPE)]
    for rows in stripes:
        h = _rmsnorm(x_ref[rows, :], gmix_ref[...])
        z_ref[rows, :] = jnp.dot(h.astype(_BF16), win_ref[...], preferred_element_type=_F32)

    win = _pool_windows((ROW_BLK, W_GROUP))
    row_iota = lax.broadcasted_iota(_I32, (ROW_BLK, W_GROUP), 0)
    low_group = lax.broadcasted_iota(_I32, (ROW_BLK + 2 * SUBLANES, LANES), 1) < POOL_CH
    lane = lax.broadcasted_iota(_I32, (CHUNK, W_GROUP), 1)

    def mix_row_block(r0):
        rows = slice(r0, r0 + ROW_BLK)
        ext_rows = slice(HALO + r0, HALO + r0 + ROW_BLK)
        exta_ref[ext_rows, :] = z_ref[rows, col(1)] * z_ref[rows, col(2)]
        extp_ref[ext_rows, :] = z_ref[rows, col(3)]
        extc_ref[ext_rows, :] = z_ref[rows, col(4)] * jax.nn.sigmoid(z_ref[rows, col(5)])

        ua = exta_ref[HALO + r0 - SUBLANES:HALO + r0 + ROW_BLK, :]
        conv_a = caw_ref[CONV_A - 1:CONV_A, :] * ua
        for k in range(CONV_A - 1):
            conv_a = conv_a + caw_ref[k:k + 1, :] * _rows_back(ua, CONV_A - 1 - k)
        y_a = z_ref[rows, col(0)] * conv_a[SUBLANES:, :]
        mix_ref[rows, col(0)] = y_a.astype(_BF16)

        pe = extp_ref[HALO + r0 - 2 * SUBLANES:HALO + r0 + ROW_BLK, :]
        s2 = pe + _rows_back(pe, 1)
        s4 = s2 + _rows_back(s2, 2)
        s4_hi = s4[:, LANES:]
        s8 = s4_hi + _rows_back(s4_hi, 4)
        s16 = s8 + _rows_back(s8, 8)
        sums = jnp.concatenate([jnp.where(low_group, s2[:, :LANES], s4[:, :LANES]), jnp.where(low_group, s8, s16)],
                               axis=1)[2 * SUBLANES:, :]
        pos = t * tt + r0 + row_iota
        cnt = jnp.minimum(pos + 1, win).astype(_F32)
        d_pool = sums / cnt - pe[2 * SUBLANES:, :]
        dpool_ref[rows, :] = d_pool.astype(_BF16)

        halves = []
        for hc in range(W_GROUP // LANES):
            lanes = slice(hc * LANES, (hc + 1) * LANES)
            xe = extc_ref[HALO + r0 - HALO:HALO + r0 + ROW_BLK, lanes]
            conv_c = None
            for r in range(SUBLANES):
                xr = _rows_ahead(xe, r)
                for a in range(HALO // SUBLANES + 1):
                    k = SUBLANES * a + r - (HALO - (CONV_C - 1))
                    if 0 <= k < CONV_C:
                        term = ccw_ref[k:k + 1, lanes] * xr[SUBLANES * a:SUBLANES * a + ROW_BLK, :]
                        conv_c = term if conv_c is None else conv_c + term
            halves.append(conv_c)
        y_c = _layernorm(jnp.concatenate(halves, axis=1) + ccb_ref[...], lncg_ref[...], lncb_ref[...])
        y_c = _silu(y_c)
        mix_ref[rows, col(2)] = y_c.astype(_BF16)

        vn_ref[rows, :] = _layernorm(z_ref[rows, col(7)], lndg_ref[...], lndb_ref[...])

        if precise_tail and r0 >= tt - CHUNK:
            tail_rows = slice(r0 - (tt - CHUNK), r0 - (tt - CHUNK) + ROW_BLK)
            mixf_ref[tail_rows, col(0)] = y_a
            mixf_ref[tail_rows, col(2)] = y_c
            dpoolf_ref[tail_rows, :] = d_pool

    for rows in stripes:
        for r0 in range(rows.start, rows.stop, ROW_BLK):
            mix_row_block(r0)

        y_p = jnp.dot(dpool_ref[rows, :], pw_ref[...], preferred_element_type=_F32) * ps_ref[...]
        mix_ref[rows, col(1)] = y_p.astype(_BF16)

        for c0 in range(rows.start, rows.stop, CHUNK):
            chunk = slice(c0, c0 + CHUNK)
            vn_c = vn_ref[chunk, :]
            mixed = sgb_ref[...]
            for hd in range(N_HEADS_D):
                vm = jnp.where(lane // HEAD_D == hd, vn_c, 0.0).astype(_BF16)
                mixed = mixed + jnp.dot(sgw_ref[hd], vm, preferred_element_type=_F32)
            mix_ref[chunk, col(3)] = (z_ref[chunk, col(6)] * mixed).astype(_BF16)

        x1_ref[rows, :] = x_ref[rows, :] + jnp.dot(mix_ref[rows, :], wout_ref[...], preferred_element_type=_F32)

    if precise_tail:
        @pl.when(is_last)
        def _():
            z_ref[tail, :] = _dot_split(_rmsnorm(x_ref[tail, :], gmix_ref[...]), win_ref, winl_ref)
            for r0 in range(tt - CHUNK, tt, ROW_BLK):
                mix_row_block(r0)
            mixf_ref[:, col(1)] = _dot_split(dpoolf_ref[...], pw_ref, pwl_ref) * ps_ref[...]
            vn_c = vn_ref[tail, :]
            mixed = sgb_ref[...]
            for hd in range(N_HEADS_D):
                vm_hi, vm_lo = _split_bf16(jnp.where(lane // HEAD_D == hd, vn_c, 0.0))
                mixed = (mixed + jnp.dot(sgw_ref[hd], vm_hi, preferred_element_type=_F32)
                         + jnp.dot(sgw_ref[hd], vm_lo, preferred_element_type=_F32)
                         + jnp.dot(sgwl_ref[hd], vm_hi, preferred_element_type=_F32))
            mixf_ref[:, col(3)] = z_ref[tail, col(6)] * mixed
            x1_ref[tail, :] = x_ref[tail, :] + _dot_split(mixf_ref[...], wout_ref, woutl_ref)

    _store_route(x1_ref[...], (gffn_ref, rwh_ref, rw2_ref, rb_ref, lstrict_ref, ustrict_ref),
                 (h2_ref, route_ref, routet_ref, npad_ref))

    @pl.when(is_last)
    def _():
        end = HALO + tt
        sa_ref[0] = exta_ref[end - (CONV_A - 1):end, :]
        sp_ref[0] = extp_ref[end - POOL_STATE:end, :]
        sc_ref[0] = extc_ref[end - (CONV_C - 1):end, :]

    exta_ref[0:HALO, :] = exta_ref[tt:tt + HALO, :]
    extp_ref[0:HALO, :] = extp_ref[tt:tt + HALO, :]
    extc_ref[0:HALO, :] = extc_ref[tt:tt + HALO, :]


def _route_out_shapes(n_tiles):
    n_tok = n_tiles * TOK_TILE
    return [jax.ShapeDtypeStruct((n_tok, D_MODEL), _BF16),
            jax.ShapeDtypeStruct((n_tok, LANES), _F32),
            jax.ShapeDtypeStruct((n_tiles, SUBLANES, TOK_TILE), _F32),
            jax.ShapeDtypeStruct((n_tiles, 1, LANES), _F32)]


def _route_out_specs(tile_of):
    return [pl.BlockSpec((TOK_TILE, D_MODEL), lambda *g: (tile_of(*g), 0)),
            pl.BlockSpec((TOK_TILE, LANES), lambda *g: (tile_of(*g), 0)),
            pl.BlockSpec((1, SUBLANES, TOK_TILE), lambda *g: (tile_of(*g), 0, 0)),
            pl.BlockSpec((1, 1, LANES), lambda *g: (tile_of(*g), 0, 0))]


def _mixer_consts(lw, sgu_w, sgu_b):
    return [lw["g_mix"], lw["w_in"], lw["conv_a_w"], lw["pool_w_bd"], lw["pool_scale"], lw["conv_c_w"],
            lw["conv_c_b"], lw["ln_c_g"], lw["ln_c_b"], lw["ln_d_g"], lw["ln_d_b"], lw[sgu_w], lw[sgu_b], lw["w_out"],
            lw["g_ffn"], lw["router_w_hi"], lw["router_w_both"], lw["router_b"], lw["lstrict"], lw["ustrict"]]


def _prompt_mixer(x, bsz, lw, precise_tail):
    seq = x.shape[0] // bsz
    n_t = seq // TOK_TILE
    consts = _mixer_consts(lw, "sgu_w_tril", "sgu_bias_rows") + [lw["w_in_lo"], lw["pool_w_bd_lo"],
                                                                 lw["sgu_w_tril_lo"], lw["w_out_lo"]]
    tile_of = lambda b, t: b * n_t + t
    tile_spec = pl.BlockSpec((TOK_TILE, D_MODEL), lambda b, t: (tile_of(b, t), 0))

    def state_spec(rows):
        return pl.BlockSpec((1, rows, W_GROUP), lambda b, t: (b, 0, 0))

    return pl.pallas_call(
        functools.partial(_prompt_mixer_kernel, precise_tail, lw["layer"]),
        grid=(bsz, n_t),
        in_specs=[tile_spec] + [_const_spec(c.shape) for c in consts],
        out_specs=[tile_spec, state_spec(CONV_A - 1), state_spec(POOL_STATE), state_spec(CONV_C - 1)]
        + _route_out_specs(tile_of),
        out_shape=[jax.ShapeDtypeStruct((bsz * seq, D_MODEL), _F32),
                   jax.ShapeDtypeStruct((bsz, CONV_A - 1, W_GROUP), _F32),
                   jax.ShapeDtypeStruct((bsz, POOL_STATE, W_GROUP), _F32),
                   jax.ShapeDtypeStruct((bsz, CONV_C - 1, W_GROUP), _F32)] + _route_out_shapes(bsz * n_t),
        scratch_shapes=[pltpu.VMEM((TOK_TILE, IN_COLS), _F32),
                        pltpu.VMEM((HALO + TOK_TILE, W_GROUP), _F32),
                        pltpu.VMEM((HALO + TOK_TILE, W_GROUP), _F32),
                        pltpu.VMEM((HALO + TOK_TILE, W_GROUP), _F32),
                        pltpu.VMEM((TOK_TILE, W_GROUP), _BF16),
                        pltpu.VMEM((TOK_TILE, W_GROUP), _F32),
                        pltpu.VMEM((TOK_TILE, D_MODEL), _BF16),
                        pltpu.VMEM((CHUNK, D_MODEL), _F32),
                        pltpu.VMEM((CHUNK, W_GROUP), _F32)],
        compiler_params=pltpu.CompilerParams(dimension_semantics=("arbitrary", "arbitrary"),
                                             vmem_limit_bytes=VMEM_LIMIT),
        name="prompt_mixer",
    )(x, *consts)


def _sample_mixer_kernel(batch_major_in, layer, x_ref, sta_ref, stp_ref, stc_ref, gmix_ref, win_ref, caw_ref, pw_ref,
                         ps_ref, ccw_ref, ccb_ref, lncg_ref, lncb_ref, lndg_ref, lndb_ref, sgw_ref, sgb_ref, wout_ref,
                         gffn_ref, rwh_ref, rw2_ref, rb_ref, lstrict_ref, ustrict_ref,
                         x1_ref, nsa_ref, nsp_ref, nsc_ref, vrow_ref, h2_ref, route_ref, routet_ref, npad_ref,
                         xt_ref, z_ref, exta_ref, extp_ref, extc_ref, dpool_ref, vn_ref, mix_ref):
    (gmix_ref, ps_ref, ccb_ref, lncg_ref, lncb_ref, lndg_ref, lndb_ref, gffn_ref, rb_ref, caw_ref, ccw_ref, sgw_ref,
     sgb_ref) = _layer_views(layer, (gmix_ref, ps_ref, ccb_ref, lncg_ref, lncb_ref, lndg_ref, lndb_ref, gffn_ref, rb_ref),
                             (caw_ref, ccw_ref, sgw_ref, sgb_ref))
    nb = sta_ref.shape[1]
    n_tok = x1_ref.shape[0]
    n_t = n_tok // nb

    def col(k):
        return slice(k * W_GROUP, (k + 1) * W_GROUP)

    def slab(j, n=1):
        return slice(j * nb, (j + n) * nb)

    if batch_major_in:
        for tstep in range(n_t):
            xt_ref[slab(tstep), :] = x_ref[:, tstep * D_MODEL:(tstep + 1) * D_MODEL]
    else:
        xt_ref[...] = x_ref[...]

    h = _rmsnorm(xt_ref[...], gmix_ref[...]).astype(_BF16)
    z_ref[...] = jnp.dot(h, win_ref[...], preferred_element_type=_F32)

    for j in range(CONV_A - 1):
        exta_ref[slab(j), :] = sta_ref[0, :, j, :]
    for j in range(POOL_STATE):
        extp_ref[slab(j), :] = stp_ref[0, :, j, :]
    for j in range(CONV_C - 1):
        extc_ref[slab(j), :] = stc_ref[0, :, j, :]
    for tstep in range(n_t):
        rows = slab(tstep)
        exta_ref[slab(CONV_A - 1 + tstep), :] = z_ref[rows, col(1)] * z_ref[rows, col(2)]
        extp_ref[slab(POOL_STATE + tstep), :] = z_ref[rows, col(3)]
        extc_ref[slab(CONV_C - 1 + tstep), :] = z_ref[rows, col(4)] * jax.nn.sigmoid(z_ref[rows, col(5)])

    win = _pool_windows((nb, W_GROUP))
    for tstep in range(n_t):
        rows = slab(tstep)
        conv_a = None
        for k in range(CONV_A):
            term = caw_ref[k:k + 1, :] * exta_ref[slab(tstep + k), :]
            conv_a = term if conv_a is None else conv_a + term
        mix_ref[rows, col(0)] = (z_ref[rows, col(0)] * conv_a).astype(_BF16)

        p_cur = extp_ref[slab(POOL_STATE + tstep), :]
        acc = p_cur
        for j in range(1, POOL_STATE + 1):
            acc = acc + jnp.where(win > j, extp_ref[slab(POOL_STATE + tstep - j), :], 0.0)
        cnt = jnp.minimum(PAST_LEN + tstep + 1, win).astype(_F32)
        dpool_ref[rows, :] = (acc / cnt - p_cur).astype(_BF16)

        conv_c = None
        for k in range(CONV_C):
            term = ccw_ref[k:k + 1, :] * extc_ref[slab(tstep + k), :]
            conv_c = term if conv_c is None else conv_c + term
        y_c = _layernorm(conv_c + ccb_ref[...], lncg_ref[...], lncb_ref[...])
        mix_ref[rows, col(2)] = _silu(y_c).astype(_BF16)

        v_n = _layernorm(z_ref[rows, col(7)], lndg_ref[...], lndb_ref[...])
        vn_ref[rows, :] = v_n
        vrow_ref[:, tstep, :] = v_n

    y_p = jnp.dot(dpool_ref[...], pw_ref[...], preferred_element_type=_F32) * ps_ref[...]
    mix_ref[:, col(1)] = y_p.astype(_BF16)

    for i in range(n_t):
        mixed = sgb_ref[i:i + 1, :] + sgw_ref[i * n_t:i * n_t + 1, :] * vn_ref[slab(0), :]
        for j in range(1, i + 1):
            mixed = mixed + sgw_ref[i * n_t + j:i * n_t + j + 1, :] * vn_ref[slab(j), :]
        mix_ref[slab(i), col(3)] = (z_ref[slab(i), col(6)] * mixed).astype(_BF16)

    x1 = xt_ref[...] + jnp.dot(mix_ref[...], wout_ref[...], preferred_element_type=_F32)
    x1_ref[...] = x1
    _store_route(x1, (gffn_ref, rwh_ref, rw2_ref, rb_ref, lstrict_ref, ustrict_ref),
                 (h2_ref, route_ref, routet_ref, npad_ref))

    for j in range(CONV_A - 1):
        nsa_ref[:, j, :] = exta_ref[slab(n_t + j), :]
    for j in range(POOL_STATE):
        nsp_ref[:, j, :] = extp_ref[slab(n_t + j), :]
    for j in range(CONV_C - 1):
        nsc_ref[:, j, :] = extc_ref[slab(n_t + j), :]


def _sample_mixer(x, st_a, st_p, st_c, lw, n_t, batch_major_in):
    n_seq = st_a.shape[1]
    nb = SAMPLE_SEQ_BLK
    n_blk = n_seq // nb
    n_tok = nb * n_t
    assert n_tok == TOK_TILE
    consts = _mixer_consts(lw, "sgu_w_rows", "sgu_b_rows")

    def seq_spec(width):
        return pl.BlockSpec((nb, width), lambda i: (i, 0))

    def rows_spec(rows):
        return pl.BlockSpec((nb, rows, W_GROUP), lambda i: (i, 0, 0))

    def state_spec(rows):
        return pl.BlockSpec((1, nb, rows, W_GROUP), lambda i: (lw["layer"], i, 0, 0))

    tok_spec = pl.BlockSpec((n_tok, D_MODEL), lambda i: (i, 0))
    out_rows = [CONV_A - 1, POOL_STATE, CONV_C - 1, n_t]
    x_spec = seq_spec(n_t * D_MODEL) if batch_major_in else tok_spec
    return pl.pallas_call(
        functools.partial(_sample_mixer_kernel, batch_major_in, lw["layer"]),
        grid=(n_blk,),
        in_specs=[x_spec] + [state_spec(r) for r in out_rows[:3]] + [_const_spec(c.shape) for c in consts],
        out_specs=[tok_spec] + [rows_spec(r) for r in out_rows] + _route_out_specs(lambda i: i),
        out_shape=[jax.ShapeDtypeStruct((n_blk * n_tok, D_MODEL), _F32)]
        + [jax.ShapeDtypeStruct((n_seq, r, W_GROUP), _F32) for r in out_rows] + _route_out_shapes(n_blk),
        scratch_shapes=[pltpu.VMEM((n_tok, D_MODEL), _F32),
                        pltpu.VMEM((n_tok, IN_COLS), _F32),
                        pltpu.VMEM(((CONV_A - 1 + n_t) * nb, W_GROUP), _F32),
                        pltpu.VMEM(((POOL_STATE + n_t) * nb, W_GROUP), _F32),
                        pltpu.VMEM(((CONV_C - 1 + n_t) * nb, W_GROUP), _F32),
                        pltpu.VMEM((n_tok, W_GROUP), _BF16),
                        pltpu.VMEM((n_tok, W_GROUP), _F32),
                        pltpu.VMEM((n_tok, D_MODEL), _BF16)],
        compiler_params=pltpu.CompilerParams(dimension_semantics=("arbitrary",), vmem_limit_bytes=VMEM_LIMIT),
        name="sample_mixer",
    )(x, st_a, st_p, st_c, *consts)


def _plan_kernel(np_ref, ustrict_ref, dest_ref, tab_ref, used_ref, npx_ref, toff_ref, zc_ref):
    n_tiles = np_ref.shape[0]
    nt_pad = npx_ref.shape[0]
    zeros = jnp.zeros((nt_pad, LANES), _F32)
    npx_ref[...] = zeros
    toff_ref[...] = zeros
    zc_ref[...] = zeros
    npx_ref[0:n_tiles, :] = np_ref[...]
    np_all = npx_ref[...]

    tile_row = lax.broadcasted_iota(_I32, (nt_pad, 1), 0)
    n_real = jnp.sum(np_all, axis=-1, keepdims=True) * (1.0 / BF16_ROWS)
    n_zero = jnp.where(tile_row < n_tiles, N_CHUNK - n_real, 0.0)

    run = jnp.zeros((1, LANES), _F32)
    zrun = jnp.zeros((1, LANES), _F32)
    for i in range(n_tiles):
        toff_ref[i:i + 1, :] = run
        zc_ref[i:i + 1, :] = zrun
        run = run + npx_ref[i:i + 1, :]
        zrun = zrun + n_zero[i:i + 1, :]
    rows_e = run
    rows_pad = jnp.ceil(rows_e * (1.0 / GMM_TILE)) * GMM_TILE
    gap = (rows_pad - rows_e) * (1.0 / BF16_ROWS)

    def excl_lanes(v):
        return jnp.dot(v.astype(_BF16), ustrict_ref[...], preferred_element_type=_F32)

    gstart = excl_lanes(jnp.broadcast_to(rows_pad * (1.0 / GMM_TILE), (SUBLANES, LANES)))[0:1] * GMM_TILE
    gap_start = excl_lanes(jnp.broadcast_to(gap, (SUBLANES, LANES)))[0:1]
    gap_total = jnp.sum(gap, axis=-1, keepdims=True)
    rows_total = jnp.sum(rows_pad, axis=-1, keepdims=True)
    seg_start = excl_lanes(np_all * (1.0 / BF16_ROWS)) * BF16_ROWS
    delta = gstart + toff_ref[...] - seg_start

    chunk = lax.broadcasted_iota(_I32, (nt_pad, LANES), 1).astype(_F32)
    pos = chunk * BF16_ROWS
    q = zc_ref[...] + (chunk - n_real)
    real = pos
    gap_addr = q * BF16_ROWS
    for e in range(N_EXPERTS):
        ss = seg_start[:, e:e + 1]
        se = ss + np_all[:, e:e + 1]
        real = real + jnp.where(jnp.logical_and(ss <= pos, pos < se), delta[:, e:e + 1], 0.0)
        gs = gap_start[:, e:e + 1]
        ge = gs + gap[:, e:e + 1]
        base = gstart[:, e:e + 1] + rows_e[:, e:e + 1] - gs * BF16_ROWS
        gap_addr = gap_addr + jnp.where(jnp.logical_and(gs <= q, q < ge), base, 0.0)
    tail_addr = rows_total + (q - gap_total) * BF16_ROWS
    zero_addr = jnp.where(q < gap_total, gap_addr, tail_addr)
    dest = jnp.where(chunk < n_real, real, zero_addr)
    dest_ref[...] = dest[0:n_tiles, :].astype(_I32)
    used_ref[...] = jnp.broadcast_to(n_real * BF16_ROWS, (nt_pad, LANES))[0:n_tiles, :].astype(_I32)

    n_cols = tab_ref.shape[1]
    row_pos = lax.broadcasted_iota(_I32, (SUBLANES, n_cols), 1).astype(_F32) * GMM_TILE
    t_exp = jnp.zeros((SUBLANES, n_cols), _F32)
    t_val = jnp.zeros((SUBLANES, n_cols), _F32)
    for e in range(N_EXPERTS):
        gs = gstart[:, e:e + 1]
        t_exp = t_exp + jnp.where(gs + rows_pad[:, e:e + 1] <= row_pos, 1.0, 0.0)
        t_val = t_val + jnp.where(jnp.logical_and(gs <= row_pos, row_pos < gs + rows_e[:, e:e + 1]), 1.0, 0.0)
    t_exp = jnp.minimum(t_exp, N_EXPERTS - 1.0)
    n_valid = jnp.sum(t_val, axis=-1, keepdims=True)
    n_groups = jnp.ceil(n_valid * (1.0 / GMM_GROUP))
    s_idx = jnp.minimum(row_pos * (1.0 / GMM_TILE), n_groups - 1.0)
    sub = lax.broadcasted_iota(_I32, (SUBLANES, n_cols), 0)
    tab_ref[...] = jnp.where(sub == 0, t_exp, jnp.where(sub == 1, s_idx, 0.0)).astype(_I32)


def _plan(npad_all, lw, n_gmm_tiles):
    n_tiles = npad_all.shape[0]
    nt_pad = -(-n_tiles // SUBLANES) * SUBLANES
    n_cols = -(-n_gmm_tiles // LANES) * LANES
    out_shape = [jax.ShapeDtypeStruct((n_tiles, LANES), _I32), jax.ShapeDtypeStruct((SUBLANES, n_cols), _I32),
                 jax.ShapeDtypeStruct((n_tiles, LANES), _I32)]
    return pl.pallas_call(
        _plan_kernel,
        grid=(1,),
        in_specs=[_const_spec(npad_all.shape), _const_spec(lw["ustrict"].shape)],
        out_specs=[_const_spec(s.shape) for s in out_shape],
        out_shape=out_shape,
        scratch_shapes=[pltpu.VMEM((nt_pad, LANES), _F32)] * 3,
        name="moe_plan",
    )(npad_all, lw["ustrict"])


def _sort_kernel(n_prompt_tiles, dest_ref, used_ref, h2p_ref, h2s_ref, rtp_ref, rts_ref, xs_ref, buf_ref, sem_ref):
    i = pl.program_id(0)
    n = pl.num_programs(0)
    cur = lax.rem(i, 2)
    is_p = i < n_prompt_tiles
    h2 = jnp.where(is_p, h2p_ref[...], h2s_ref[...])
    rt = jnp.where(is_p, rtp_ref[0], rts_ref[0])
    s1 = rt[0:1, :]
    s2 = rt[1:2, :]

    def chunk_copy(tile, c, slot):
        dst = pl.multiple_of(dest_ref[tile * N_CHUNK + c], BF16_ROWS)
        return pltpu.make_async_copy(buf_ref.at[slot, pl.ds(c * BF16_ROWS, BF16_ROWS), :],
                                     xs_ref.at[pl.ds(dst, BF16_ROWS), :], sem_ref.at[slot])

    @pl.when(i < 2)
    def _():
        buf_ref[cur, SLOTS:SLOT_BUF, :] = jnp.zeros((SLOT_BUF - SLOTS, D_MODEL), _BF16)

    def sort_slots(lo, hi):
        slot_id = (lo + lax.broadcasted_iota(_I32, (hi - lo, TOK_TILE), 0)).astype(_F32)
        perm = jnp.where(jnp.logical_or(slot_id == s1, slot_id == s2), 1.0, 0.0).astype(_BF16)
        buf_ref[cur, lo:hi, :] = jnp.dot(perm, h2, preferred_element_type=_F32).astype(_BF16)

    def start_chunks(lo, hi):
        for c in range(lo // BF16_ROWS, hi // BF16_ROWS):
            chunk_copy(i, c, cur).start()

    start_chunks(SLOTS, SLOT_BUF)
    n_grp = SLOTS // SLOT_GRP
    for g in range(n_grp - 1):
        sort_slots(g * SLOT_GRP, (g + 1) * SLOT_GRP)
        start_chunks(g * SLOT_GRP, (g + 1) * SLOT_GRP)
    last_used = used_ref[i * LANES] > (n_grp - 1) * SLOT_GRP

    @pl.when(last_used)
    def _():
        sort_slots((n_grp - 1) * SLOT_GRP, SLOTS)

    @pl.when(jnp.logical_not(last_used))
    def _():
        buf_ref[cur, (n_grp - 1) * SLOT_GRP:SLOTS, :] = jnp.zeros((SLOT_GRP, D_MODEL), _BF16)

    start_chunks((n_grp - 1) * SLOT_GRP, SLOTS)

    @pl.when(i > 0)
    def _():
        for c in range(N_CHUNK):
            chunk_copy(i - 1, c, 1 - cur).wait()

    @pl.when(i == n - 1)
    def _():
        for c in range(N_CHUNK):
            chunk_copy(i, c, cur).wait()


def _sort(dest_flat, used_flat, h2p, h2s, rtp, rts):
    n_p = h2p.shape[0] // TOK_TILE
    n_s = h2s.shape[0] // TOK_TILE
    n_tiles = n_p + n_s
    p_idx = lambda i: jnp.minimum(i, n_p - 1)
    s_idx = lambda i: jnp.maximum(i - n_p, 0)
    return pl.pallas_call(
        functools.partial(_sort_kernel, n_p),
        grid_spec=pltpu.PrefetchScalarGridSpec(
            num_scalar_prefetch=2,
            grid=(n_tiles,),
            in_specs=[pl.BlockSpec((TOK_TILE, D_MODEL), lambda i, d, u: (p_idx(i), 0)),
                      pl.BlockSpec((TOK_TILE, D_MODEL), lambda i, d, u: (s_idx(i), 0)),
                      pl.BlockSpec((1, SUBLANES, TOK_TILE), lambda i, d, u: (p_idx(i), 0, 0)),
                      pl.BlockSpec((1, SUBLANES, TOK_TILE), lambda i, d, u: (s_idx(i), 0, 0))],
            out_specs=pl.BlockSpec(memory_space=pl.ANY),
            scratch_shapes=[pltpu.VMEM((2, SLOT_BUF, D_MODEL), _BF16), pltpu.SemaphoreType.DMA((2,))],
        ),
        out_shape=jax.ShapeDtypeStruct((n_tiles * SLOT_BUF, D_MODEL), _BF16),
        compiler_params=pltpu.CompilerParams(dimension_semantics=("arbitrary",), vmem_limit_bytes=VMEM_LIMIT),
        name="moe_sort",
    )(dest_flat, used_flat, h2p, h2s, rtp, rts)


def _gmm_kernel(texp_ref, sidx_ref, xs_ref, *refs):
    w_refs, ys_ref = refs[:-1], refs[-1]
    s = pl.program_id(0)

    @pl.when(sidx_ref[s] == s)
    def _():
        blk = GMM_TILE // GMM_ROW_SPLIT
        chains = [(j, slice(j * GMM_TILE + k * blk, j * GMM_TILE + (k + 1) * blk))
                  for j in range(GMM_GROUP) for k in range(GMM_ROW_SPLIT)]
        w_gu = [jnp.concatenate([w_refs[3 * j][0, 0].astype(_BF16), w_refs[3 * j + 1][0, 0].astype(_BF16)], axis=1)
                for j in range(GMM_GROUP)]
        w_d = [w_refs[3 * j + 2][0, 0].astype(_BF16) for j in range(GMM_GROUP)]
        gate_up = [jnp.dot(xs_ref[rows, :], w_gu[j], preferred_element_type=_F32) for j, rows in chains]
        for (j, rows), gu in zip(chains, gate_up):
            act = (_silu(gu[:, :D_FF_EXPERT]) * gu[:, D_FF_EXPERT:]).astype(_BF16)
            ys_ref[rows, :] = jnp.dot(act, w_d[j], preferred_element_type=_F32).astype(_BF16)


def _gmm(t_exp, s_idx, xs, layer, w_gate, w_up, w_down):
    n_steps = xs.shape[0] // (GMM_GROUP * GMM_TILE)
    group_spec = pl.BlockSpec((GMM_GROUP * GMM_TILE, D_MODEL), lambda s, te, si: (si[s], 0))
    w_specs, w_args = [], []
    for j in range(GMM_GROUP):
        expert = lambda s, te, si, j=j: (layer, te[GMM_GROUP * s + j], 0, 0)
        w_specs += [pl.BlockSpec((1, 1, D_MODEL, D_FF_EXPERT), expert),
                    pl.BlockSpec((1, 1, D_MODEL, D_FF_EXPERT), expert),
                    pl.BlockSpec((1, 1, D_FF_EXPERT, D_MODEL), expert)]
        w_args += [w_gate, w_up, w_down]
    return pl.pallas_call(
        _gmm_kernel,
        grid_spec=pltpu.PrefetchScalarGridSpec(
            num_scalar_prefetch=2,
            grid=(n_steps,),
            in_specs=[group_spec] + w_specs,
            out_specs=group_spec,
        ),
        out_shape=jax.ShapeDtypeStruct(xs.shape, _BF16),
        input_output_aliases={2: 0},
        compiler_params=pltpu.CompilerParams(dimension_semantics=("arbitrary",), vmem_limit_bytes=VMEM_LIMIT),
        name="moe_experts",
    )(t_exp, s_idx, xs, *w_args)


def _combine_kernel(n_prompt_tiles, final_norm, dest_ref, used_ref, x1p_ref, x1s_ref, rp_ref, rs_ref, gfin_ref, ys_ref,
                    outp_ref, outs_ref, ybuf_ref, sem_ref):
    i = pl.program_id(0)
    n = pl.num_programs(0)
    cur = lax.rem(i, 2)
    is_p = i < n_prompt_tiles

    def chunk_copy(tile, c, slot):
        src = pl.multiple_of(dest_ref[tile * N_CHUNK + c], BF16_ROWS)
        return pltpu.make_async_copy(ys_ref.at[pl.ds(src, BF16_ROWS), :],
                                     ybuf_ref.at[slot, pl.ds(c * BF16_ROWS, BF16_ROWS), :], sem_ref.at[slot])

    @pl.when(i == 0)
    def _():
        for c in range(N_CHUNK_REAL):
            chunk_copy(0, c, 0).start()

    @pl.when(i + 1 < n)
    def _():
        for c in range(N_CHUNK_REAL):
            chunk_copy(i + 1, c, 1 - cur).start()

    for c in range(N_CHUNK_REAL):
        chunk_copy(i, c, cur).wait()

    route = jnp.where(is_p, rp_ref[...], rs_ref[...])
    acc = jnp.where(is_p, x1p_ref[...], x1s_ref[...])
    s1 = route[:, 0:1]
    s2 = route[:, 1:2]
    w1 = route[:, 2:3]
    w2 = route[:, 3:4]
    def unperm_dot(lo, hi):
        slot_id = (lo + lax.broadcasted_iota(_I32, (TOK_TILE, hi - lo), 1)).astype(_F32)
        unperm = jnp.where(slot_id == s1, w1, jnp.where(slot_id == s2, w2, 0.0)).astype(_BF16)
        return jnp.dot(unperm, ybuf_ref[cur, lo:hi, :], preferred_element_type=_F32)

    def finish(y):
        if final_norm:
            y = _rmsnorm(y, gfin_ref[...])

        @pl.when(is_p)
        def _():
            outp_ref[...] = y

        @pl.when(jnp.logical_not(is_p))
        def _():
            if final_norm:
                nb = outs_ref.shape[0]
                for tstep in range(TOK_TILE // nb):
                    outs_ref[:, tstep * D_MODEL:(tstep + 1) * D_MODEL] = y[tstep * nb:(tstep + 1) * nb, :]
            else:
                outs_ref[...] = y

    n_grp = SLOTS // SLOT_GRP
    for g in range(n_grp - 1):
        acc = acc + unperm_dot(g * SLOT_GRP, (g + 1) * SLOT_GRP)
    last_used = used_ref[i * LANES] > (n_grp - 1) * SLOT_GRP

    @pl.when(last_used)
    def _():
        finish(acc + unperm_dot((n_grp - 1) * SLOT_GRP, SLOTS))

    @pl.when(jnp.logical_not(last_used))
    def _():
        finish(acc)


def _combine(dest_flat, used_flat, x1p, x1s, rp, rs, g_fin, ys, final_norm):
    n_p = x1p.shape[0] // TOK_TILE
    n_s = x1s.shape[0] // TOK_TILE
    p_idx = lambda i, d, u: (jnp.minimum(i, n_p - 1), 0)
    s_idx = lambda i, d, u: (jnp.maximum(i - n_p, 0), 0)
    s_rows, s_cols = (SAMPLE_SEQ_BLK, TOK_TILE // SAMPLE_SEQ_BLK * D_MODEL) if final_norm else (TOK_TILE, D_MODEL)
    return pl.pallas_call(
        functools.partial(_combine_kernel, n_p, final_norm),
        grid_spec=pltpu.PrefetchScalarGridSpec(
            num_scalar_prefetch=2,
            grid=(n_p + n_s,),
            in_specs=[pl.BlockSpec((TOK_TILE, D_MODEL), p_idx), pl.BlockSpec((TOK_TILE, D_MODEL), s_idx),
                      pl.BlockSpec((TOK_TILE, LANES), p_idx), pl.BlockSpec((TOK_TILE, LANES), s_idx),
                      pl.BlockSpec(g_fin.shape, lambda i, d, u: (0, 0)),
                      pl.BlockSpec(memory_space=pl.ANY)],
            out_specs=[pl.BlockSpec((TOK_TILE, D_MODEL), p_idx), pl.BlockSpec((s_rows, s_cols), s_idx)],
            scratch_shapes=[pltpu.VMEM((2, SLOTS, D_MODEL), _BF16), pltpu.SemaphoreType.DMA((2,))],
        ),
        out_shape=[jax.ShapeDtypeStruct(x1p.shape, _F32), jax.ShapeDtypeStruct((n_s * s_rows, s_cols), _F32)],
        compiler_params=pltpu.CompilerParams(dimension_semantics=("arbitrary",), vmem_limit_bytes=VMEM_LIMIT),
        name="moe_combine",
    )(dest_flat, used_flat, x1p, x1s, rp, rs, g_fin, ys)


def _moe(x1p, x1s, routing_p, routing_s, lw, g_fin, final_norm):
    h2p, rp, rtp, npp = routing_p
    h2s, rs, rts, nps = routing_s
    n_tiles = npp.shape[0] + nps.shape[0]
    n_gmm = n_tiles * SLOT_BUF // GMM_TILE
    assert n_tiles * (SLOT_BUF - SLOTS) >= N_EXPERTS * (GMM_TILE - BF16_ROWS)
    npad_all = jnp.concatenate([npp, nps], axis=0).reshape(n_tiles, LANES)
    dest, tab, used = _plan(npad_all, lw, n_gmm)
    dest_flat = dest.reshape(-1)
    used_flat = used.reshape(-1)
    xs = _sort(dest_flat, used_flat, h2p, h2s, rtp, rts)
    ys = _gmm(tab[0], tab[1], xs, lw["layer"], *lw["expert_w"])
    return _combine(dest_flat, used_flat, x1p, x1s, rp, rs, g_fin, ys, final_norm)


def _shared_weights(g_mix, conv_a_w, pool_w, pool_scale, conv_c_w, conv_c_b, ln_c_g, ln_c_b, ln_d_g, ln_d_b, sgu_w,
                    sgu_b, g_ffn, router_group_w, router_group_b, router_expert_w, router_expert_b, expert_w_gate,
                    expert_w_up, expert_w_down, n_t_sample):
    depth = g_mix.shape[0]
    n_pool = len(POOL_WINDOWS)
    pool_bd = (pool_w[:, :, :, None, :] * jnp.eye(n_pool, dtype=_F32)[None, :, None, :, None]).reshape(
        depth, W_GROUP, W_GROUP)
    tril = jnp.tril(jnp.ones((CHUNK, CHUNK), dtype=bool))
    sgu_tril = jnp.where(tril, sgu_w, 0.0)
    w_small = sgu_tril[:, :, :n_t_sample, :n_t_sample]
    sgu_w_rows = jnp.repeat(jnp.transpose(w_small, (0, 2, 3, 1)).reshape(depth, n_t_sample * n_t_sample, N_HEADS_D),
                            HEAD_D, axis=2)
    sgu_bias_rows = jnp.repeat(jnp.swapaxes(sgu_b, 1, 2), HEAD_D, axis=2)
    n_route = N_EXPERTS + N_EXPERT_GROUPS
    router_w = jnp.pad(jnp.concatenate([router_expert_w, router_group_w], axis=2), ((0, 0), (0, 0), (0, LANES - n_route)))
    router_b = jnp.pad(jnp.concatenate([router_expert_b, router_group_b], axis=1), ((0, 0), (0, LANES - n_route)))
    return {
        "g_mix": g_mix, "conv_a_w": conv_a_w, "pool_scale": pool_scale, "conv_c_w": conv_c_w, "conv_c_b": conv_c_b,
        "ln_c_g": ln_c_g, "ln_c_b": ln_c_b, "ln_d_g": ln_d_g, "ln_d_b": ln_d_b, "g_ffn": g_ffn,
        "pool_bd": pool_bd, "sgu_tril": sgu_tril, "sgu_w_rows": sgu_w_rows, "sgu_bias_rows": sgu_bias_rows,
        "sgu_b_rows": sgu_bias_rows[:, :n_t_sample], "router_w": router_w, "router_b": router_b,
        "expert_w": (expert_w_gate, expert_w_up, expert_w_down),
        "lstrict": jnp.tril(jnp.ones((TOK_TILE, TOK_TILE), _F32), -1).astype(_BF16),
        "ustrict": jnp.triu(jnp.ones((LANES, LANES), _F32), 1).astype(_BF16),
    }


def _layer_weights(l, shared, w_in, w_out, precise_tail):
    router_w_hi, router_w_lo = _weight_split(shared["router_w"], l)
    sgu_shape = shared["sgu_tril"].shape[1:]
    if precise_tail:
        w_in_hi, w_in_lo = _weight_split(w_in, l)
        w_out_hi, w_out_lo = _weight_split(w_out, l)
        pool_hi, pool_lo = _weight_split(shared["pool_bd"], l)
        sgu_hi, sgu_lo = (s.reshape(sgu_shape) for s in
                          _weight_split(shared["sgu_tril"].reshape(-1, N_HEADS_D * CHUNK, CHUNK), l))
    else:
        w_in_hi = w_in_lo = w_in[l].astype(_BF16)
        w_out_hi = w_out_lo = w_out[l].astype(_BF16)
        pool_hi = pool_lo = shared["pool_bd"][l].astype(_BF16)
        sgu_hi = sgu_lo = shared["sgu_tril"][l].astype(_BF16)
    return dict(shared, layer=l, w_in=w_in_hi, w_in_lo=w_in_lo, w_out=w_out_hi, w_out_lo=w_out_lo, pool_w_bd=pool_hi,
                pool_w_bd_lo=pool_lo, sgu_w_tril=sgu_hi, sgu_w_tril_lo=sgu_lo, router_w_hi=router_w_hi,
                router_w_both=jnp.concatenate([router_w_hi, router_w_lo], axis=1))


def kernel(x_prompt, x_sample, state_conv_a, state_pool, state_conv_c, g_mix, w_in, conv_a_w, pool_w, pool_scale, conv_c_w, conv_c_b, ln_c_g, ln_c_b, ln_d_g, ln_d_b, sgu_w, sgu_b, w_out, g_ffn, router_group_w, router_group_b, router_expert_w, router_expert_b, expert_w_gate, expert_w_up, expert_w_down, g_final):
    depth = g_mix.shape[0]
    bsz, seq, _ = x_prompt.shape
    nb, n_t, _ = x_sample.shape
    g_fin = g_final.reshape(1, -1)

    xp = x_prompt.reshape(bsz * seq, D_MODEL)
    xs = x_sample.reshape(nb, n_t * D_MODEL)
    outs = {k: [] for k in ("sa_p", "sp_p", "sc_p", "sa_s", "sp_s", "sc_s", "v")}
    shared = _shared_weights(g_mix, conv_a_w, pool_w, pool_scale, conv_c_w, conv_c_b, ln_c_g, ln_c_b, ln_d_g, ln_d_b,
                             sgu_w, sgu_b, g_ffn, router_group_w, router_group_b, router_expert_w, router_expert_b,
                             expert_w_gate, expert_w_up, expert_w_down, n_t)
    for l in range(depth):
        precise_tail = l + 1 < depth
        lw = _layer_weights(l, shared, w_in, w_out, precise_tail)
        x1p, sa, sp, sc, *routing_p = _prompt_mixer(xp, bsz, lw, precise_tail)
        outs["sa_p"].append(sa)
        outs["sp_p"].append(sp)
        outs["sc_p"].append(sc)
        x1s, nsa, nsp, nsc, vrow, *routing_s = _sample_mixer(xs, state_conv_a, state_pool, state_conv_c, lw, n_t,
                                                             batch_major_in=(l == 0))
        outs["sa_s"].append(nsa)
        outs["sp_s"].append(nsp)
        outs["sc_s"].append(nsc)
        outs["v"].append(vrow)
        xp, xs = _moe(x1p, x1s, routing_p, routing_s, lw, g_fin, final_norm=(l == depth - 1))

    y_prompt = xp.reshape(bsz, seq, D_MODEL)
    y_sample = xs.reshape(nb, n_t, D_MODEL)
    return (y_prompt, y_sample, jnp.stack(outs["sa_p"]), jnp.stack(outs["sp_p"]), jnp.stack(outs["sc_p"]),
            jnp.stack(outs["sa_s"]), jnp.stack(outs["sp_s"]), jnp.stack(outs["sc_s"]), jnp.stack(outs["v"]))
```

```python
import functools

import jax
import jax.numpy as jnp
from jax import lax
from jax.experimental import pallas as pl
from jax.experimental.pallas import tpu as pltpu

D_MODEL = 1024
W_GROUP = 256
IN_COLS = 8 * W_GROUP
CONV_A = 3
POOL_WINDOWS = (2, 4, 8, 16)
POOL_CH = W_GROUP // len(POOL_WINDOWS)
POOL_STATE = max(POOL_WINDOWS) - 1
CONV_C = 31
CHUNK = 128
N_HEADS_D = 4
HEAD_D = W_GROUP // N_HEADS_D
N_EXPERT_GROUPS = 4
EXPERTS_PER_GROUP = 8
N_EXPERTS = N_EXPERT_GROUPS * EXPERTS_PER_GROUP
TOP_K = 2
D_FF_EXPERT = 128
EPS = 1e-6
PAST_LEN = 16384

LANES = 128
SUBLANES = 8
BF16_ROWS = 16
HALO = 32
ROW_BLK = 64
MIX_STRIPE = 256
TOK_TILE = 512
SAMPLE_SEQ_BLK = 64
GMM_TILE = 512
GMM_GROUP = 2
GMM_ROW_SPLIT = 2
SLOTS = -(-(TOP_K * TOK_TILE + N_EXPERTS * (BF16_ROWS - 1)) // 256) * 256
N_CHUNK_REAL = SLOTS // BF16_ROWS
N_CHUNK = 128
SLOT_BUF = N_CHUNK * BF16_ROWS
SLOT_GRP = 256
VMEM_LIMIT = 56 * 1024 * 1024

_F32 = jnp.float32
_BF16 = jnp.bfloat16
_I32 = jnp.int32


def _rmsnorm(x, g):
    return x * lax.rsqrt(jnp.mean(x * x, axis=-1, keepdims=True) + EPS) * g


def _layernorm(x, g, b):
    mu = jnp.mean(x, axis=-1, keepdims=True)
    xc = x - mu
    var = jnp.mean(xc * xc, axis=-1, keepdims=True)
    return xc * lax.rsqrt(var + EPS) * g + b


def _silu(x):
    return x * jax.nn.sigmoid(x)


def _split_bf16(a):
    bits = lax.bitcast_convert_type(a, jnp.uint32)
    hi = lax.bitcast_convert_type(bits & jnp.uint32(0xFFFF0000), _F32)
    return hi.astype(_BF16), (a - hi).astype(_BF16)


def _dot_split(a, wh_ref, wl_ref):
    a_hi, a_lo = _split_bf16(a)
    n = a.shape[0]
    both = jnp.dot(jnp.concatenate([a_hi, a_lo], axis=0), wh_ref[...], preferred_element_type=_F32)
    return both[:n] + both[n:] + jnp.dot(a_hi, wl_ref[...], preferred_element_type=_F32)


def _weight_split_kernel(w_ref, hi_ref, lo_ref):
    w = w_ref[0]
    hi = w.astype(_BF16)
    hi_ref[...] = hi
    lo_ref[...] = (w - hi.astype(_F32)).astype(_BF16)


def _weight_split(w, layer):
    _, rows, cols = w.shape
    blk = min(rows, 256)
    spec = pl.BlockSpec((blk, cols), lambda i: (i, 0))
    return pl.pallas_call(
        _weight_split_kernel,
        grid=(rows // blk,),
        in_specs=[pl.BlockSpec((1, blk, cols), lambda i: (layer, i, 0))],
        out_specs=[spec, spec],
        out_shape=[jax.ShapeDtypeStruct((rows, cols), _BF16)] * 2,
        name="weight_split",
    )(w)


def _rows_back(x, r):
    return pltpu.roll(x, r, axis=0)


def _rows_ahead(x, r):
    return x if r == 0 else pltpu.roll(x, x.shape[0] - r, axis=0)


def _pool_windows(shape):
    lane = lax.broadcasted_iota(_I32, shape, 1)
    return jnp.left_shift(2, lane // POOL_CH)


def _const_spec(shape):
    nd = len(shape)
    return pl.BlockSpec(shape, lambda *_: (0,) * nd)


def _route_tile(x1, gffn_ref, rwh_ref, rw2_ref, rb_ref, lstrict_ref, ustrict_ref):
    h2 = _rmsnorm(x1, gffn_ref[...])
    h_hi, h_lo = _split_bf16(h2)
    hi_both = jnp.dot(h_hi, rw2_ref[...], preferred_element_type=_F32)
    logits = (hi_both[:, :LANES] + hi_both[:, LANES:]
              + jnp.dot(h_lo, rwh_ref[...], preferred_element_type=_F32)) + rb_ref[...]
    lane = lax.broadcasted_iota(_I32, logits.shape, 1)
    lane_f = lane.astype(_F32)
    neg = jnp.float32(-jnp.inf)
    big = jnp.float32(LANES)

    is_group = jnp.logical_and(lane >= N_EXPERTS, lane < N_EXPERTS + N_EXPERT_GROUPS)
    lg = jnp.where(is_group, logits, neg)
    g_max = jnp.max(lg, axis=-1, keepdims=True)
    g_idx = jnp.min(jnp.where(lg == g_max, lane_f, big), axis=-1, keepdims=True) - N_EXPERTS
    p_top = 1.0 / jnp.sum(jnp.exp(lg - g_max), axis=-1, keepdims=True)

    in_group = (lane // EXPERTS_PER_GROUP).astype(_F32) == g_idx
    le = jnp.where(jnp.logical_and(in_group, lane < N_EXPERTS), logits, neg)
    m1 = jnp.max(le, axis=-1, keepdims=True)
    i1 = jnp.min(jnp.where(le == m1, lane_f, big), axis=-1, keepdims=True)
    le2 = jnp.where(lane_f == i1, neg, le)
    m2 = jnp.max(le2, axis=-1, keepdims=True)
    i2 = jnp.min(jnp.where(le2 == m2, lane_f, big), axis=-1, keepdims=True)
    e2 = jnp.exp(m2 - m1)
    w1 = p_top / (1.0 + e2)
    w2 = p_top * e2 / (1.0 + e2)

    o1 = jnp.where(lane_f == i1, 1.0, 0.0)
    o2 = jnp.where(lane_f == i2, 1.0, 0.0)
    lane2 = lax.broadcasted_iota(_I32, (x1.shape[0], 2 * LANES), 1).astype(_F32)
    o12 = jnp.where(jnp.logical_or(lane2 == i1, lane2 == i2 + LANES), 1.0, 0.0).astype(_BF16)
    before = jnp.dot(lstrict_ref[...], o12, preferred_element_type=_F32)
    before1 = before[:, :LANES]
    before2 = before[:, LANES:]
    n1 = jnp.sum(o1, axis=0, keepdims=True)
    n2 = jnp.sum(o2, axis=0, keepdims=True)
    n_tiles16 = jnp.floor((n1 + n2 + (BF16_ROWS - 1)) * (1.0 / BF16_ROWS))
    npad = n_tiles16 * BF16_ROWS
    seg_start = jnp.dot(jnp.broadcast_to(n_tiles16, (SUBLANES, LANES)).astype(_BF16), ustrict_ref[...],
                        preferred_element_type=_F32)[0:1] * BF16_ROWS
    s1 = jnp.sum(o1 * (seg_start + before1), axis=-1, keepdims=True)
    s2 = jnp.sum(o2 * (seg_start + n1 + before2), axis=-1, keepdims=True)
    route = jnp.where(lane == 0, s1, jnp.where(lane == 1, s2, jnp.where(lane == 2, w1, jnp.where(lane == 3, w2, 0.0))))
    return h2.astype(_BF16), route, npad


def _store_route(x1, route_refs, out_refs):
    h2_ref, route_ref, routet_ref, npad_ref = out_refs
    h2, route, npad = _route_tile(x1, *route_refs)
    h2_ref[...] = h2
    route_ref[...] = route
    routet_ref[0] = jnp.transpose(route)[0:SUBLANES, :]
    npad_ref[0] = npad


def _layer_views(layer, row_refs, mat_refs):
    return [r.at[pl.ds(layer, 1)] for r in row_refs] + [r.at[layer] for r in mat_refs]


def _prompt_mixer_kernel(precise_tail, layer, x_ref, gmix_ref, win_ref, caw_ref, pw_ref, ps_ref, ccw_ref, ccb_ref,
                         lncg_ref, lncb_ref, lndg_ref, lndb_ref, sgw_ref, sgb_ref, wout_ref, gffn_ref, rwh_ref, rw2_ref,
                         rb_ref, lstrict_ref, ustrict_ref, winl_ref, pwl_ref, sgwl_ref, woutl_ref,
                         x1_ref, sa_ref, sp_ref, sc_ref, h2_ref, route_ref, routet_ref, npad_ref,
                         z_ref, exta_ref, extp_ref, extc_ref, dpool_ref, vn_ref, mix_ref, mixf_ref, dpoolf_ref):
    (gmix_ref, ps_ref, ccb_ref, lncg_ref, lncb_ref, lndg_ref, lndb_ref, gffn_ref, rb_ref, caw_ref, ccw_ref,
     sgb_ref) = _layer_views(layer, (gmix_ref, ps_ref, ccb_ref, lncg_ref, lncb_ref, lndg_ref, lndb_ref, gffn_ref, rb_ref),
                             (caw_ref, ccw_ref, sgb_ref))
    t = pl.program_id(1)
    n_t = pl.num_programs(1)
    tt = x_ref.shape[0]
    tail = slice(tt - CHUNK, tt)
    is_last = t == n_t - 1

    @pl.when(t == 0)
    def _():
        zeros = jnp.zeros((HALO, W_GROUP), _F32)
        exta_ref[0:HALO, :] = zeros
        extp_ref[0:HALO, :] = zeros
        extc_ref[0:HALO, :] = zeros

    def col(k):
        return slice(k * W_GROUP, (k + 1) * W_GROUP)

    stripes = [slice(s0, s0 + MIX_STRIPE) for s0 in range(0, tt, MIX_STRIPE)]
    for rows in stripes:
        h = _rmsnorm(x_ref[rows, :], gmix_ref[...])
        z_ref[rows, :] = jnp.dot(h.astype(_BF16), win_ref[...], preferred_element_type=_F32)

    win = _pool_windows((ROW_BLK, W_GROUP))
    row_iota = lax.broadcasted_iota(_I32, (ROW_BLK, W_GROUP), 0)
    low_group = lax.broadcasted_iota(_I32, (ROW_BLK + 2 * SUBLANES, LANES), 1) < POOL_CH
    lane = lax.broadcasted_iota(_I32, (CHUNK, W_GROUP), 1)

    def mix_row_block(r0):
        rows = slice(r0, r0 + ROW_BLK)
        ext_rows = slice(HALO + r0, HALO + r0 + ROW_BLK)
        exta_ref[ext_rows, :] = z_ref[rows, col(1)] * z_ref[rows, col(2)]
        extp_ref[ext_rows, :] = z_ref[rows, col(3)]
        extc_ref[ext_rows, :] = z_ref[rows, col(4)] * jax.nn.sigmoid(z_ref[rows, col(5)])

        ua = exta_ref[HALO + r0 - SUBLANES:HALO + r0 + ROW_BLK, :]
        conv_a = caw_ref[CONV_A - 1:CONV_A, :] * ua
        for k in range(CONV_A - 1):
            conv_a = conv_a + caw_ref[k:k + 1, :] * _rows_back(ua, CONV_A - 1 - k)
        y_a = z_ref[rows, col(0)] * conv_a[SUBLANES:, :]
        mix_ref[rows, col(0)] = y_a.astype(_BF16)

        pe = extp_ref[HALO + r0 - 2 * SUBLANES:HALO + r0 + ROW_BLK, :]
        s2 = pe + _rows_back(pe, 1)
        s4 = s2 + _rows_back(s2, 2)
        s4_hi = s4[:, LANES:]
        s8 = s4_hi + _rows_back(s4_hi, 4)
        s16 = s8 + _rows_back(s8, 8)
        sums = jnp.concatenate([jnp.where(low_group, s2[:, :LANES], s4[:, :LANES]), jnp.where(low_group, s8, s16)],
                               axis=1)[2 * SUBLANES:, :]
        pos = t * tt + r0 + row_iota
        cnt = jnp.minimum(pos + 1, win).astype(_F32)
        d_pool = sums / cnt - pe[2 * SUBLANES:, :]
        dpool_ref[rows, :] = d_pool.astype(_BF16)

        halves = []
        for hc in range(W_GROUP // LANES):
            lanes = slice(hc * LANES, (hc + 1) * LANES)
            xe = extc_ref[HALO + r0 - HALO:HALO + r0 + ROW_BLK, lanes]
            conv_c = None
            for r in range(SUBLANES):
                xr = _rows_ahead(xe, r)
                for a in range(HALO // SUBLANES + 1):
                    k = SUBLANES * a + r - (HALO - (CONV_C - 1))
                    if 0 <= k < CONV_C:
                        term = ccw_ref[k:k + 1, lanes] * xr[SUBLANES * a:SUBLANES * a + ROW_BLK, :]
                        conv_c = term if conv_c is None else conv_c + term
            halves.append(conv_c)
        y_c = _layernorm(jnp.concatenate(halves, axis=1) + ccb_ref[...], lncg_ref[...], lncb_ref[...])
        y_c = _silu(y_c)
        mix_ref[rows, col(2)] = y_c.astype(_BF16)

        vn_ref[rows, :] = _layernorm(z_ref[rows, col(7)], lndg_ref[...], lndb_ref[...])

        if precise_tail and r0 >= tt - CHUNK:
            tail_rows = slice(r0 - (tt - CHUNK), r0 - (tt - CHUNK) + ROW_BLK)
            mixf_ref[tail_rows, col(0)] = y_a
            mixf_ref[tail_rows, col(2)] = y_c
            dpoolf_ref[tail_rows, :] = d_pool

    for rows in stripes:
        for r0 in range(rows.start, rows.stop, ROW_BLK):
            mix_row_block(r0)

        y_p = jnp.dot(dpool_ref[rows, :], pw_ref[...], preferred_element_type=_F32) * ps_ref[...]
        mix_ref[rows, col(1)] = y_p.astype(_BF16)

        for c0 in range(rows.start, rows.stop, CHUNK):
            chunk = slice(c0, c0 + CHUNK)
            vn_c = vn_ref[chunk, :]
            mixed = sgb_ref[...]
            for hd in range(N_HEADS_D):
                vm = jnp.where(lane // HEAD_D == hd, vn_c, 0.0).astype(_BF16)
                mixed = mixed + jnp.dot(sgw_ref[hd], vm, preferred_element_type=_F32)
            mix_ref[chunk, col(3)] = (z_ref[chunk, col(6)] * mixed).astype(_BF16)

        x1_ref[rows, :] = x_ref[rows, :] + jnp.dot(mix_ref[rows, :], wout_ref[...], preferred_element_type=_F32)

    if precise_tail:
        @pl.when(is_last)
        def _():
            z_ref[tail, :] = _dot_split(_rmsnorm(x_ref[tail, :], gmix_ref[...]), win_ref, winl_ref)
            for r0 in range(tt - CHUNK, tt, ROW_BLK):
                mix_row_block(r0)
            mixf_ref[:, col(1)] = _dot_split(dpoolf_ref[...], pw_ref, pwl_ref) * ps_ref[...]
            vn_c = vn_ref[tail, :]
            mixed = sgb_ref[...]
            for hd in range(N_HEADS_D):
                vm_hi, vm_lo = _split_bf16(jnp.where(lane // HEAD_D == hd, vn_c, 0.0))
                mixed = (mixed + jnp.dot(sgw_ref[hd], vm_hi, preferred_element_type=_F32)
                         + jnp.dot(sgw_ref[hd], vm_lo, preferred_element_type=_F32)
                         + jnp.dot(sgwl_ref[hd], vm_hi, preferred_element_type=_F32))
            mixf_ref[:, col(3)] = z_ref[tail, col(6)] * mixed
            x1_ref[tail, :] = x_ref[tail, :] + _dot_split(mixf_ref[...], wout_ref, woutl_ref)

    _store_route(x1_ref[...], (gffn_ref, rwh_ref, rw2_ref, rb_ref, lstrict_ref, ustrict_ref),
                 (h2_ref, route_ref, routet_ref, npad_ref))

    @pl.when(is_last)
    def _():
        end = HALO + tt
        sa_ref[0] = exta_ref[end - (CONV_A - 1):end, :]
        sp_ref[0] = extp_ref[end - POOL_STATE:end, :]
        sc_ref[0] = extc_ref[end - (CONV_C - 1):end, :]

    exta_ref[0:HALO, :] = exta_ref[tt:tt + HALO, :]
    extp_ref[0:HALO, :] = extp_ref[tt:tt + HALO, :]
    extc_ref[0:HALO, :] = extc_ref[tt:tt + HALO, :]


def _route_out_shapes(n_tiles):
    n_tok = n_tiles * TOK_TILE
    return [jax.ShapeDtypeStruct((n_tok, D_MODEL), _BF16),
            jax.ShapeDtypeStruct((n_tok, LANES), _F32),
            jax.ShapeDtypeStruct((n_tiles, SUBLANES, TOK_TILE), _F32),
            jax.ShapeDtypeStruct((n_tiles, 1, LANES), _F32)]


def _route_out_specs(tile_of):
    return [pl.BlockSpec((TOK_TILE, D_MODEL), lambda *g: (tile_of(*g), 0)),
            pl.BlockSpec((TOK_TILE, LANES), lambda *g: (tile_of(*g), 0)),
            pl.BlockSpec((1, SUBLANES, TOK_TILE), lambda *g: (tile_of(*g), 0, 0)),
            pl.BlockSpec((1, 1, LANES), lambda *g: (tile_of(*g), 0, 0))]


def _mixer_consts(lw, sgu_w, sgu_b):
    return [lw["g_mix"], lw["w_in"], lw["conv_a_w"], lw["pool_w_bd"], lw["pool_scale"], lw["conv_c_w"],
            lw["conv_c_b"], lw["ln_c_g"], lw["ln_c_b"], lw["ln_d_g"], lw["ln_d_b"], lw[sgu_w], lw[sgu_b], lw["w_out"],
            lw["g_ffn"], lw["router_w_hi"], lw["router_w_both"], lw["router_b"], lw["lstrict"], lw["ustrict"]]


def _prompt_mixer(x, bsz, lw, precise_tail):
    seq = x.shape[0] // bsz
    n_t = seq // TOK_TILE
    consts = _mixer_consts(lw, "sgu_w_tril", "sgu_bias_rows") + [lw["w_in_lo"], lw["pool_w_bd_lo"],
                                                                 lw["sgu_w_tril_lo"], lw["w_out_lo"]]
    tile_of = lambda b, t: b * n_t + t
    tile_spec = pl.BlockSpec((TOK_TILE, D_MODEL), lambda b, t: (tile_of(b, t), 0))

    def state_spec(rows):
        return pl.BlockSpec((1, rows, W_GROUP), lambda b, t: (b, 0, 0))

    return pl.pallas_call(
        functools.partial(_prompt_mixer_kernel, precise_tail, lw["layer"]),
        grid=(bsz, n_t),
        in_specs=[tile_spec] + [_const_spec(c.shape) for c in consts],
        out_specs=[tile_spec, state_spec(CONV_A - 1), state_spec(POOL_STATE), state_spec(CONV_C - 1)]
        + _route_out_specs(tile_of),
        out_shape=[jax.ShapeDtypeStruct((bsz * seq, D_MODEL), _F32),
                   jax.ShapeDtypeStruct((bsz, CONV_A - 1, W_GROUP), _F32),
                   jax.ShapeDtypeStruct((bsz, POOL_STATE, W_GROUP), _F32),
                   jax.ShapeDtypeStruct((bsz, CONV_C - 1, W_GROUP), _F32)] + _route_out_shapes(bsz * n_t),
        scratch_shapes=[pltpu.VMEM((TOK_TILE, IN_COLS), _F32),
                        pltpu.VMEM((HALO + TOK_TILE, W_GROUP), _F32),
                        pltpu.VMEM((HALO + TOK_TILE, W_GROUP), _F32),
                        pltpu.VMEM((HALO + TOK_TILE, W_GROUP), _F32),
                        pltpu.VMEM((TOK_TILE, W_GROUP), _BF16),
                        pltpu.VMEM((TOK_TILE, W_GROUP), _F32),
                        pltpu.VMEM((TOK_TILE, D_MODEL), _BF16),
                        pltpu.VMEM((CHUNK, D_MODEL), _F32),
                        pltpu.VMEM((CHUNK, W_GROUP), _F32)],
        compiler_params=pltpu.CompilerParams(dimension_semantics=("arbitrary", "arbitrary"),
                                             vmem_limit_bytes=VMEM_LIMIT),
        name="prompt_mixer",
    )(x, *consts)


def _sample_mixer_kernel(batch_major_in, layer, x_ref, sta_ref, stp_ref, stc_ref, gmix_ref, win_ref, caw_ref, pw_ref,
                         ps_ref, ccw_ref, ccb_ref, lncg_ref, lncb_ref, lndg_ref, lndb_ref, sgw_ref, sgb_ref, wout_ref,
                         gffn_ref, rwh_ref, rw2_ref, rb_ref, lstrict_ref, ustrict_ref,
                         x1_ref, nsa_ref, nsp_ref, nsc_ref, vrow_ref, h2_ref, route_ref, routet_ref, npad_ref,
                         xt_ref, z_ref, exta_ref, extp_ref, extc_ref, dpool_ref, vn_ref, mix_ref):
    (gmix_ref, ps_ref, ccb_ref, lncg_ref, lncb_ref, lndg_ref, lndb_ref, gffn_ref, rb_ref, caw_ref, ccw_ref, sgw_ref,
     sgb_ref) = _layer_views(layer, (gmix_ref, ps_ref, ccb_ref, lncg_ref, lncb_ref, lndg_ref, lndb_ref, gffn_ref, rb_ref),
                             (caw_ref, ccw_ref, sgw_ref, sgb_ref))
    nb = sta_ref.shape[1]
    n_tok = x1_ref.shape[0]
    n_t = n_tok // nb

    def col(k):
        return slice(k * W_GROUP, (k + 1) * W_GROUP)

    def slab(j, n=1):
        return slice(j * nb, (j + n) * nb)

    if batch_major_in:
        for tstep in range(n_t):
            xt_ref[slab(tstep), :] = x_ref[:, tstep, :]
    else:
        xt_ref[...] = x_ref[...]

    h = _rmsnorm(xt_ref[...], gmix_ref[...]).astype(_BF16)
    z_ref[...] = jnp.dot(h, win_ref[...], preferred_element_type=_F32)

    for j in range(CONV_A - 1):
        exta_ref[slab(j), :] = sta_ref[0, :, j, :]
    for j in range(POOL_STATE):
        extp_ref[slab(j), :] = stp_ref[0, :, j, :]
    for j in range(CONV_C - 1):
        extc_ref[slab(j), :] = stc_ref[0, :, j, :]
    for tstep in range(n_t):
        rows = slab(tstep)
        exta_ref[slab(CONV_A - 1 + tstep), :] = z_ref[rows, col(1)] * z_ref[rows, col(2)]
        extp_ref[slab(POOL_STATE + tstep), :] = z_ref[rows, col(3)]
        extc_ref[slab(CONV_C - 1 + tstep), :] = z_ref[rows, col(4)] * jax.nn.sigmoid(z_ref[rows, col(5)])

    win = _pool_windows((nb, W_GROUP))
    for tstep in range(n_t):
        rows = slab(tstep)
        conv_a = None
        for k in range(CONV_A):
            term = caw_ref[k:k + 1, :] * exta_ref[slab(tstep + k), :]
            conv_a = term if conv_a is None else conv_a + term
        mix_ref[rows, col(0)] = (z_ref[rows, col(0)] * conv_a).astype(_BF16)

        p_cur = extp_ref[slab(POOL_STATE + tstep), :]
        acc = p_cur
        for j in range(1, POOL_STATE + 1):
            acc = acc + jnp.where(win > j, extp_ref[slab(POOL_STATE + tstep - j), :], 0.0)
        cnt = jnp.minimum(PAST_LEN + tstep + 1, win).astype(_F32)
        dpool_ref[rows, :] = (acc / cnt - p_cur).astype(_BF16)

        conv_c = None
        for k in range(CONV_C):
            term = ccw_ref[k:k + 1, :] * extc_ref[slab(tstep + k), :]
            conv_c = term if conv_c is None else conv_c + term
        y_c = _layernorm(conv_c + ccb_ref[...], lncg_ref[...], lncb_ref[...])
        mix_ref[rows, col(2)] = _silu(y_c).astype(_BF16)

        v_n = _layernorm(z_ref[rows, col(7)], lndg_ref[...], lndb_ref[...])
        vn_ref[rows, :] = v_n
        vrow_ref[:, tstep, :] = v_n

    y_p = jnp.dot(dpool_ref[...], pw_ref[...], preferred_element_type=_F32) * ps_ref[...]
    mix_ref[:, col(1)] = y_p.astype(_BF16)

    for i in range(n_t):
        mixed = sgb_ref[i:i + 1, :] + sgw_ref[i * n_t:i * n_t + 1, :] * vn_ref[slab(0), :]
        for j in range(1, i + 1):
            mixed = mixed + sgw_ref[i * n_t + j:i * n_t + j + 1, :] * vn_ref[slab(j), :]
        mix_ref[slab(i), col(3)] = (z_ref[slab(i), col(6)] * mixed).astype(_BF16)

    x1 = xt_ref[...] + jnp.dot(mix_ref[...], wout_ref[...], preferred_element_type=_F32)
    x1_ref[...] = x1
    _store_route(x1, (gffn_ref, rwh_ref, rw2_ref, rb_ref, lstrict_ref, ustrict_ref),
                 (h2_ref, route_ref, routet_ref, npad_ref))

    for j in range(CONV_A - 1):
        nsa_ref[:, j, :] = exta_ref[slab(n_t + j), :]
    for j in range(POOL_STATE):
        nsp_ref[:, j, :] = extp_ref[slab(n_t + j), :]
    for j in range(CONV_C - 1):
        nsc_ref[:, j, :] = extc_ref[slab(n_t + j), :]


def _sample_mixer(x, st_a, st_p, st_c, lw, n_t, batch_major_in):
    n_seq = st_a.shape[1]
    nb = SAMPLE_SEQ_BLK
    n_blk = n_seq // nb
    n_tok = nb * n_t
    assert n_tok == TOK_TILE
    consts = _mixer_consts(lw, "sgu_w_rows", "sgu_b_rows")

    def rows_spec(rows):
        return pl.BlockSpec((nb, rows, W_GROUP), lambda i: (i, 0, 0))

    def state_spec(rows):
        return pl.BlockSpec((1, nb, rows, W_GROUP), lambda i: (lw["layer"], i, 0, 0))

    tok_spec = pl.BlockSpec((n_tok, D_MODEL), lambda i: (i, 0))
    out_rows = [CONV_A - 1, POOL_STATE, CONV_C - 1, n_t]
    x_spec = pl.BlockSpec((nb, n_t, D_MODEL), lambda i: (i, 0, 0)) if batch_major_in else tok_spec
    return pl.pallas_call(
        functools.partial(_sample_mixer_kernel, batch_major_in, lw["layer"]),
        grid=(n_blk,),
        in_specs=[x_spec] + [state_spec(r) for r in out_rows[:3]] + [_const_spec(c.shape) for c in consts],
        out_specs=[tok_spec] + [rows_spec(r) for r in out_rows] + _route_out_specs(lambda i: i),
        out_shape=[jax.ShapeDtypeStruct((n_blk * n_tok, D_MODEL), _F32)]
        + [jax.ShapeDtypeStruct((n_seq, r, W_GROUP), _F32) for r in out_rows] + _route_out_shapes(n_blk),
        scratch_shapes=[pltpu.VMEM((n_tok, D_MODEL), _F32),
                        pltpu.VMEM((n_tok, IN_COLS), _F32),
                        pltpu.VMEM(((CONV_A - 1 + n_t) * nb, W_GROUP), _F32),
                        pltpu.VMEM(((POOL_STATE + n_t) * nb, W_GROUP), _F32),
                        pltpu.VMEM(((CONV_C - 1 + n_t) * nb, W_GROUP), _F32),
                        pltpu.VMEM((n_tok, W_GROUP), _BF16),
                        pltpu.VMEM((n_tok, W_GROUP), _F32),
                        pltpu.VMEM((n_tok, D_MODEL), _BF16)],
        compiler_params=pltpu.CompilerParams(dimension_semantics=("arbitrary",), vmem_limit_bytes=VMEM_LIMIT),
        name="sample_mixer",
    )(x, st_a, st_p, st_c, *consts)


def _plan_kernel(np_ref, ustrict_ref, dest_ref, tab_ref, used_ref, npx_ref, toff_ref, zc_ref):
    n_tiles = np_ref.shape[0]
    nt_pad = npx_ref.shape[0]
    zeros = jnp.zeros((nt_pad, LANES), _F32)
    npx_ref[...] = zeros
    toff_ref[...] = zeros
    zc_ref[...] = zeros
    npx_ref[0:n_tiles, :] = np_ref[...]
    np_all = npx_ref[...]

    tile_row = lax.broadcasted_iota(_I32, (nt_pad, 1), 0)
    n_real = jnp.sum(np_all, axis=-1, keepdims=True) * (1.0 / BF16_ROWS)
    n_zero = jnp.where(tile_row < n_tiles, N_CHUNK - n_real, 0.0)

    run = jnp.zeros((1, LANES), _F32)
    zrun = jnp.zeros((1, LANES), _F32)
    for i in range(n_tiles):
        toff_ref[i:i + 1, :] = run
        zc_ref[i:i + 1, :] = zrun
        run = run + npx_ref[i:i + 1, :]
        zrun = zrun + n_zero[i:i + 1, :]
    rows_e = run
    rows_pad = jnp.ceil(rows_e * (1.0 / GMM_TILE)) * GMM_TILE
    gap = (rows_pad - rows_e) * (1.0 / BF16_ROWS)

    def excl_lanes(v):
        return jnp.dot(v.astype(_BF16), ustrict_ref[...], preferred_element_type=_F32)

    gstart = excl_lanes(jnp.broadcast_to(rows_pad * (1.0 / GMM_TILE), (SUBLANES, LANES)))[0:1] * GMM_TILE
    gap_start = excl_lanes(jnp.broadcast_to(gap, (SUBLANES, LANES)))[0:1]
    gap_total = jnp.sum(gap, axis=-1, keepdims=True)
    rows_total = jnp.sum(rows_pad, axis=-1, keepdims=True)
    seg_start = excl_lanes(np_all * (1.0 / BF16_ROWS)) * BF16_ROWS
    delta = gstart + toff_ref[...] - seg_start

    chunk = lax.broadcasted_iota(_I32, (nt_pad, LANES), 1).astype(_F32)
    pos = chunk * BF16_ROWS
    q = zc_ref[...] + (chunk - n_real)
    real = pos
    gap_addr = q * BF16_ROWS
    for e in range(N_EXPERTS):
        ss = seg_start[:, e:e + 1]
        se = ss + np_all[:, e:e + 1]
        real = real + jnp.where(jnp.logical_and(ss <= pos, pos < se), delta[:, e:e + 1], 0.0)
        gs = gap_start[:, e:e + 1]
        ge = gs + gap[:, e:e + 1]
        base = gstart[:, e:e + 1] + rows_e[:, e:e + 1] - gs * BF16_ROWS
        gap_addr = gap_addr + jnp.where(jnp.logical_and(gs <= q, q < ge), base, 0.0)
    tail_addr = rows_total + (q - gap_total) * BF16_ROWS
    zero_addr = jnp.where(q < gap_total, gap_addr, tail_addr)
    dest = jnp.where(chunk < n_real, real, zero_addr)
    dest_ref[...] = dest[0:n_tiles, :].astype(_I32)
    used_ref[...] = jnp.broadcast_to(n_real * BF16_ROWS, (nt_pad, LANES))[0:n_tiles, :].astype(_I32)

    n_cols = tab_ref.shape[1]
    row_pos = lax.broadcasted_iota(_I32, (SUBLANES, n_cols), 1).astype(_F32) * GMM_TILE
    t_exp = jnp.zeros((SUBLANES, n_cols), _F32)
    t_val = jnp.zeros((SUBLANES, n_cols), _F32)
    for e in range(N_EXPERTS):
        gs = gstart[:, e:e + 1]
        t_exp = t_exp + jnp.where(gs + rows_pad[:, e:e + 1] <= row_pos, 1.0, 0.0)
        t_val = t_val + jnp.where(jnp.logical_and(gs <= row_pos, row_pos < gs + rows_e[:, e:e + 1]), 1.0, 0.0)
    t_exp = jnp.minimum(t_exp, N_EXPERTS - 1.0)
    n_valid = jnp.sum(t_val, axis=-1, keepdims=True)
    n_groups = jnp.ceil(n_valid * (1.0 / GMM_GROUP))
    s_idx = jnp.minimum(row_pos * (1.0 / GMM_TILE), n_groups - 1.0)
    sub = lax.broadcasted_iota(_I32, (SUBLANES, n_cols), 0)
    tab_ref[...] = jnp.where(sub == 0, t_exp, jnp.where(sub == 1, s_idx, 0.0)).astype(_I32)


def _plan(npad_all, lw, n_gmm_tiles):
    n_tiles = npad_all.shape[0]
    nt_pad = -(-n_tiles // SUBLANES) * SUBLANES
    n_cols = -(-n_gmm_tiles // LANES) * LANES
    out_shape = [jax.ShapeDtypeStruct((n_tiles, LANES), _I32), jax.ShapeDtypeStruct((SUBLANES, n_cols), _I32),
                 jax.ShapeDtypeStruct((n_tiles, LANES), _I32)]
    return pl.pallas_call(
        _plan_kernel,
        grid=(1,),
        in_specs=[_const_spec(npad_all.shape), _const_spec(lw["ustrict"].shape)],
        out_specs=[_const_spec(s.shape) for s in out_shape],
        out_shape=out_shape,
        scratch_shapes=[pltpu.VMEM((nt_pad, LANES), _F32)] * 3,
        name="moe_plan",
    )(npad_all, lw["ustrict"])


def _sort_kernel(n_prompt_tiles, dest_ref, used_ref, h2p_ref, h2s_ref, rtp_ref, rts_ref, xs_ref, buf_ref, sem_ref):
    i = pl.program_id(0)
    n = pl.num_programs(0)
    cur = lax.rem(i, 2)
    is_p = i < n_prompt_tiles
    h2 = jnp.where(is_p, h2p_ref[...], h2s_ref[...])
    rt = jnp.where(is_p, rtp_ref[0], rts_ref[0])
    s1 = rt[0:1, :]
    s2 = rt[1:2, :]

    def chunk_copy(tile, c, slot):
        dst = pl.multiple_of(dest_ref[tile * N_CHUNK + c], BF16_ROWS)
        return pltpu.make_async_copy(buf_ref.at[slot, pl.ds(c * BF16_ROWS, BF16_ROWS), :],
                                     xs_ref.at[pl.ds(dst, BF16_ROWS), :], sem_ref.at[slot])

    @pl.when(i < 2)
    def _():
        buf_ref[cur, SLOTS:SLOT_BUF, :] = jnp.zeros((SLOT_BUF - SLOTS, D_MODEL), _BF16)

    def sort_slots(lo, hi):
        slot_id = (lo + lax.broadcasted_iota(_I32, (hi - lo, TOK_TILE), 0)).astype(_F32)
        perm = jnp.where(jnp.logical_or(slot_id == s1, slot_id == s2), 1.0, 0.0).astype(_BF16)
        buf_ref[cur, lo:hi, :] = jnp.dot(perm, h2, preferred_element_type=_F32).astype(_BF16)

    def start_chunks(lo, hi):
        for c in range(lo // BF16_ROWS, hi // BF16_ROWS):
            chunk_copy(i, c, cur).start()

    start_chunks(SLOTS, SLOT_BUF)
    n_grp = SLOTS // SLOT_GRP
    for g in range(n_grp - 1):
        sort_slots(g * SLOT_GRP, (g + 1) * SLOT_GRP)
        start_chunks(g * SLOT_GRP, (g + 1) * SLOT_GRP)
    last_used = used_ref[i * LANES] > (n_grp - 1) * SLOT_GRP

    @pl.when(last_used)
    def _():
        sort_slots((n_grp - 1) * SLOT_GRP, SLOTS)

    @pl.when(jnp.logical_not(last_used))
    def _():
        buf_ref[cur, (n_grp - 1) * SLOT_GRP:SLOTS, :] = jnp.zeros((SLOT_GRP, D_MODEL), _BF16)

    start_chunks((n_grp - 1) * SLOT_GRP, SLOTS)

    @pl.when(i > 0)
    def _():
        for c in range(N_CHUNK):
            chunk_copy(i - 1, c, 1 - cur).wait()

    @pl.when(i == n - 1)
    def _():
        for c in range(N_CHUNK):
            chunk_copy(i, c, cur).wait()


def _sort(dest_flat, used_flat, h2p, h2s, rtp, rts):
    n_p = h2p.shape[0] // TOK_TILE
    n_s = h2s.shape[0] // TOK_TILE
    n_tiles = n_p + n_s
    p_idx = lambda i: jnp.minimum(i, n_p - 1)
    s_idx = lambda i: jnp.maximum(i - n_p, 0)
    return pl.pallas_call(
        functools.partial(_sort_kernel, n_p),
        grid_spec=pltpu.PrefetchScalarGridSpec(
            num_scalar_prefetch=2,
            grid=(n_tiles,),
            in_specs=[pl.BlockSpec((TOK_TILE, D_MODEL), lambda i, d, u: (p_idx(i), 0)),
                      pl.BlockSpec((TOK_TILE, D_MODEL), lambda i, d, u: (s_idx(i), 0)),
                      pl.BlockSpec((1, SUBLANES, TOK_TILE), lambda i, d, u: (p_idx(i), 0, 0)),
                      pl.BlockSpec((1, SUBLANES, TOK_TILE), lambda i, d, u: (s_idx(i), 0, 0))],
            out_specs=pl.BlockSpec(memory_space=pl.ANY),
            scratch_shapes=[pltpu.VMEM((2, SLOT_BUF, D_MODEL), _BF16), pltpu.SemaphoreType.DMA((2,))],
        ),
        out_shape=jax.ShapeDtypeStruct((n_tiles * SLOT_BUF, D_MODEL), _BF16),
        compiler_params=pltpu.CompilerParams(dimension_semantics=("arbitrary",), vmem_limit_bytes=VMEM_LIMIT),
        name="moe_sort",
    )(dest_flat, used_flat, h2p, h2s, rtp, rts)


def _gmm_kernel(texp_ref, sidx_ref, xs_ref, *refs):
    w_refs, ys_ref = refs[:-1], refs[-1]
    s = pl.program_id(0)

    @pl.when(sidx_ref[s] == s)
    def _():
        blk = GMM_TILE // GMM_ROW_SPLIT
        chains = [(j, slice(j * GMM_TILE + k * blk, j * GMM_TILE + (k + 1) * blk))
                  for j in range(GMM_GROUP) for k in range(GMM_ROW_SPLIT)]
        w_gu = [jnp.concatenate([w_refs[3 * j][0, 0].astype(_BF16), w_refs[3 * j + 1][0, 0].astype(_BF16)], axis=1)
                for j in range(GMM_GROUP)]
        w_d = [w_refs[3 * j + 2][0, 0].astype(_BF16) for j in range(GMM_GROUP)]
        gate_up = [jnp.dot(xs_ref[rows, :], w_gu[j], preferred_element_type=_F32) for j, rows in chains]
        for (j, rows), gu in zip(chains, gate_up):
            act = (_silu(gu[:, :D_FF_EXPERT]) * gu[:, D_FF_EXPERT:]).astype(_BF16)
            ys_ref[rows, :] = jnp.dot(act, w_d[j], preferred_element_type=_F32).astype(_BF16)


def _gmm(t_exp, s_idx, xs, layer, w_gate, w_up, w_down):
    n_steps = xs.shape[0] // (GMM_GROUP * GMM_TILE)
    group_spec = pl.BlockSpec((GMM_GROUP * GMM_TILE, D_MODEL), lambda s, te, si: (si[s], 0))
    w_specs, w_args = [], []
    for j in range(GMM_GROUP):
        expert = lambda s, te, si, j=j: (layer, te[GMM_GROUP * s + j], 0, 0)
        w_specs += [pl.BlockSpec((1, 1, D_MODEL, D_FF_EXPERT), expert),
                    pl.BlockSpec((1, 1, D_MODEL, D_FF_EXPERT), expert),
                    pl.BlockSpec((1, 1, D_FF_EXPERT, D_MODEL), expert)]
        w_args += [w_gate, w_up, w_down]
    return pl.pallas_call(
        _gmm_kernel,
        grid_spec=pltpu.PrefetchScalarGridSpec(
            num_scalar_prefetch=2,
            grid=(n_steps,),
            in_specs=[group_spec] + w_specs,
            out_specs=group_spec,
        ),
        out_shape=jax.ShapeDtypeStruct(xs.shape, _BF16),
        input_output_aliases={2: 0},
        compiler_params=pltpu.CompilerParams(dimension_semantics=("arbitrary",), vmem_limit_bytes=VMEM_LIMIT),
        name="moe_experts",
    )(t_exp, s_idx, xs, *w_args)


def _combine_kernel(n_prompt_tiles, final_norm, dest_ref, used_ref, x1p_ref, x1s_ref, rp_ref, rs_ref, gfin_ref, ys_ref,
                    outp_ref, outs_ref, ybuf_ref, sem_ref):
    i = pl.program_id(0)
    n = pl.num_programs(0)
    cur = lax.rem(i, 2)
    is_p = i < n_prompt_tiles

    def chunk_copy(tile, c, slot):
        src = pl.multiple_of(dest_ref[tile * N_CHUNK + c], BF16_ROWS)
        return pltpu.make_async_copy(ys_ref.at[pl.ds(src, BF16_ROWS), :],
                                     ybuf_ref.at[slot, pl.ds(c * BF16_ROWS, BF16_ROWS), :], sem_ref.at[slot])

    @pl.when(i == 0)
    def _():
        for c in range(N_CHUNK_REAL):
            chunk_copy(0, c, 0).start()

    @pl.when(i + 1 < n)
    def _():
        for c in range(N_CHUNK_REAL):
            chunk_copy(i + 1, c, 1 - cur).start()

    for c in range(N_CHUNK_REAL):
        chunk_copy(i, c, cur).wait()

    route = jnp.where(is_p, rp_ref[...], rs_ref[...])
    acc = jnp.where(is_p, x1p_ref[...], x1s_ref[...])
    s1 = route[:, 0:1]
    s2 = route[:, 1:2]
    w1 = route[:, 2:3]
    w2 = route[:, 3:4]
    def unperm_dot(lo, hi):
        slot_id = (lo + lax.broadcasted_iota(_I32, (TOK_TILE, hi - lo), 1)).astype(_F32)
        unperm = jnp.where(slot_id == s1, w1, jnp.where(slot_id == s2, w2, 0.0)).astype(_BF16)
        return jnp.dot(unperm, ybuf_ref[cur, lo:hi, :], preferred_element_type=_F32)

    def finish(y):
        if final_norm:
            y = _rmsnorm(y, gfin_ref[...])

        @pl.when(is_p)
        def _():
            outp_ref[...] = y

        @pl.when(jnp.logical_not(is_p))
        def _():
            if final_norm:
                nb = outs_ref.shape[0]
                for tstep in range(TOK_TILE // nb):
                    outs_ref[:, tstep, :] = y[tstep * nb:(tstep + 1) * nb, :]
            else:
                outs_ref[...] = y

    n_grp = SLOTS // SLOT_GRP
    for g in range(n_grp - 1):
        acc = acc + unperm_dot(g * SLOT_GRP, (g + 1) * SLOT_GRP)
    last_used = used_ref[i * LANES] > (n_grp - 1) * SLOT_GRP

    @pl.when(last_used)
    def _():
        finish(acc + unperm_dot((n_grp - 1) * SLOT_GRP, SLOTS))

    @pl.when(jnp.logical_not(last_used))
    def _():
        finish(acc)


def _combine(dest_flat, used_flat, x1p, x1s, rp, rs, g_fin, ys, final_norm):
    n_p = x1p.shape[0] // TOK_TILE
    n_s = x1s.shape[0] // TOK_TILE
    p_idx = lambda i, d, u: (jnp.minimum(i, n_p - 1), 0)
    s_idx = lambda i, d, u: (jnp.maximum(i - n_p, 0), 0)
    if final_norm:
        s_shape = (n_s * SAMPLE_SEQ_BLK, TOK_TILE // SAMPLE_SEQ_BLK, D_MODEL)
        s_out_spec = pl.BlockSpec((SAMPLE_SEQ_BLK,) + s_shape[1:], lambda i, d, u: (jnp.maximum(i - n_p, 0), 0, 0))
    else:
        s_shape, s_out_spec = x1s.shape, pl.BlockSpec((TOK_TILE, D_MODEL), s_idx)
    return pl.pallas_call(
        functools.partial(_combine_kernel, n_p, final_norm),
        grid_spec=pltpu.PrefetchScalarGridSpec(
            num_scalar_prefetch=2,
            grid=(n_p + n_s,),
            in_specs=[pl.BlockSpec((TOK_TILE, D_MODEL), p_idx), pl.BlockSpec((TOK_TILE, D_MODEL), s_idx),
                      pl.BlockSpec((TOK_TILE, LANES), p_idx), pl.BlockSpec((TOK_TILE, LANES), s_idx),
                      pl.BlockSpec(g_fin.shape, lambda i, d, u: (0, 0)),
                      pl.BlockSpec(memory_space=pl.ANY)],
            out_specs=[pl.BlockSpec((TOK_TILE, D_MODEL), p_idx), s_out_spec],
            scratch_shapes=[pltpu.VMEM((2, SLOTS, D_MODEL), _BF16), pltpu.SemaphoreType.DMA((2,))],
        ),
        out_shape=[jax.ShapeDtypeStruct(x1p.shape, _F32), jax.ShapeDtypeStruct(s_shape, _F32)],
        compiler_params=pltpu.CompilerParams(dimension_semantics=("arbitrary",), vmem_limit_bytes=VMEM_LIMIT),
        name="moe_combine",
    )(dest_flat, used_flat, x1p, x1s, rp, rs, g_fin, ys)


def _moe(x1p, x1s, routing_p, routing_s, lw, g_fin, final_norm):
    h2p, rp, rtp, npp = routing_p
    h2s, rs, rts, nps = routing_s
    n_tiles = npp.shape[0] + nps.shape[0]
    n_gmm = n_tiles * SLOT_BUF // GMM_TILE
    assert n_tiles * (SLOT_BUF - SLOTS) >= N_EXPERTS * (GMM_TILE - BF16_ROWS)
    npad_all = jnp.concatenate([npp, nps], axis=0).reshape(n_tiles, LANES)
    dest, tab, used = _plan(npad_all, lw, n_gmm)
    dest_flat = dest.reshape(-1)
    used_flat = used.reshape(-1)
    xs = _sort(dest_flat, used_flat, h2p, h2s, rtp, rts)
    ys = _gmm(tab[0], tab[1], xs, lw["layer"], *lw["expert_w"])
    return _combine(dest_flat, used_flat, x1p, x1s, rp, rs, g_fin, ys, final_norm)


def _shared_weights(g_mix, conv_a_w, pool_w, pool_scale, conv_c_w, conv_c_b, ln_c_g, ln_c_b, ln_d_g, ln_d_b, sgu_w,
                    sgu_b, g_ffn, router_group_w, router_group_b, router_expert_w, router_expert_b, expert_w_gate,
                    expert_w_up, expert_w_down, n_t_sample):
    depth = g_mix.shape[0]
    n_pool = len(POOL_WINDOWS)
    pool_bd = (pool_w[:, :, :, None, :] * jnp.eye(n_pool, dtype=_F32)[None, :, None, :, None]).reshape(
        depth, W_GROUP, W_GROUP)
    tril = jnp.tril(jnp.ones((CHUNK, CHUNK), dtype=bool))
    sgu_tril = jnp.where(tril, sgu_w, 0.0)
    w_small = sgu_tril[:, :, :n_t_sample, :n_t_sample]
    sgu_w_rows = jnp.repeat(jnp.transpose(w_small, (0, 2, 3, 1)).reshape(depth, n_t_sample * n_t_sample, N_HEADS_D),
                            HEAD_D, axis=2)
    sgu_bias_rows = jnp.repeat(jnp.swapaxes(sgu_b, 1, 2), HEAD_D, axis=2)
    n_route = N_EXPERTS + N_EXPERT_GROUPS
    router_w = jnp.pad(jnp.concatenate([router_expert_w, router_group_w], axis=2), ((0, 0), (0, 0), (0, LANES - n_route)))
    router_b = jnp.pad(jnp.concatenate([router_expert_b, router_group_b], axis=1), ((0, 0), (0, LANES - n_route)))
    return {
        "g_mix": g_mix, "conv_a_w": conv_a_w, "pool_scale": pool_scale, "conv_c_w": conv_c_w, "conv_c_b": conv_c_b,
        "ln_c_g": ln_c_g, "ln_c_b": ln_c_b, "ln_d_g": ln_d_g, "ln_d_b": ln_d_b, "g_ffn": g_ffn,
        "pool_bd": pool_bd, "sgu_tril": sgu_tril, "sgu_w_rows": sgu_w_rows, "sgu_bias_rows": sgu_bias_rows,
        "sgu_b_rows": sgu_bias_rows[:, :n_t_sample], "router_w": router_w, "router_b": router_b,
        "expert_w": (expert_w_gate, expert_w_up, expert_w_down),
        "lstrict": jnp.tril(jnp.ones((TOK_TILE, TOK_TILE), _F32), -1).astype(_BF16),
        "ustrict": jnp.triu(jnp.ones((LANES, LANES), _F32), 1).astype(_BF16),
    }


def _layer_weights(l, shared, w_in, w_out, precise_tail):
    router_w_hi, router_w_lo = _weight_split(shared["router_w"], l)
    sgu_shape = shared["sgu_tril"].shape[1:]
    if precise_tail:
        w_in_hi, w_in_lo = _weight_split(w_in, l)
        w_out_hi, w_out_lo = _weight_split(w_out, l)
        pool_hi, pool_lo = _weight_split(shared["pool_bd"], l)
        sgu_hi, sgu_lo = (s.reshape(sgu_shape) for s in
                          _weight_split(shared["sgu_tril"].reshape(-1, N_HEADS_D * CHUNK, CHUNK), l))
    else:
        w_in_hi = w_in_lo = w_in[l].astype(_BF16)
        w_out_hi = w_out_lo = w_out[l].astype(_BF16)
        pool_hi = pool_lo = shared["pool_bd"][l].astype(_BF16)
        sgu_hi = sgu_lo = shared["sgu_tril"][l].astype(_BF16)
    return dict(shared, layer=l, w_in=w_in_hi, w_in_lo=w_in_lo, w_out=w_out_hi, w_out_lo=w_out_lo, pool_w_bd=pool_hi,
                pool_w_bd_lo=pool_lo, sgu_w_tril=sgu_hi, sgu_w_tril_lo=sgu_lo, router_w_hi=router_w_hi,
                router_w_both=jnp.concatenate([router_w_hi, router_w_lo], axis=1))


def kernel(x_prompt, x_sample, state_conv_a, state_pool, state_conv_c, g_mix, w_in, conv_a_w, pool_w, pool_scale, conv_c_w, conv_c_b, ln_c_g, ln_c_b, ln_d_g, ln_d_b, sgu_w, sgu_b, w_out, g_ffn, router_group_w, router_group_b, router_expert_w, router_expert_b, expert_w_gate, expert_w_up, expert_w_down, g_final):
    depth = g_mix.shape[0]
    bsz, seq, _ = x_prompt.shape
    nb, n_t, _ = x_sample.shape
    g_fin = g_final.reshape(1, -1)

    xp = x_prompt.reshape(bsz * seq, D_MODEL)
    xs = x_sample
    outs = {k: [] for k in ("sa_p", "sp_p", "sc_p", "sa_s", "sp_s", "sc_s", "v")}
    shared = _shared_weights(g_mix, conv_a_w, pool_w, pool_scale, conv_c_w, conv_c_b, ln_c_g, ln_c_b, ln_d_g, ln_d_b,
                             sgu_w, sgu_b, g_ffn, router_group_w, router_group_b, router_expert_w, router_expert_b,
                             expert_w_gate, expert_w_up, expert_w_down, n_t)
    for l in range(depth):
        precise_tail = l + 1 < depth
        lw = _layer_weights(l, shared, w_in, w_out, precise_tail)
        x1p, sa, sp, sc, *routing_p = _prompt_mixer(xp, bsz, lw, precise_tail)
        outs["sa_p"].append(sa)
        outs["sp_p"].append(sp)
        outs["sc_p"].append(sc)
        x1s, nsa, nsp, nsc, vrow, *routing_s = _sample_mixer(xs, state_conv_a, state_pool, state_conv_c, lw, n_t,
                                                             batch_major_in=(l == 0))
        outs["sa_s"].append(nsa)
        outs["sp_s"].append(nsp)
        outs["sc_s"].append(nsc)
        outs["v"].append(vrow)
        xp, xs = _moe(x1p, x1s, routing_p, routing_s, lw, g_fin, final_norm=(l == depth - 1))

    y_prompt = xp.reshape(bsz, seq, D_MODEL)
    y_sample = xs
    return (y_prompt, y_sample, jnp.stack(outs["sa_p"]), jnp.stack(outs["sp_p"]), jnp.stack(outs["sc_p"]),
            jnp.stack(outs["sa_s"]), jnp.stack(outs["sp_s"]), jnp.stack(outs["sc_s"]), jnp.stack(outs["v"]))
```
